```python
import jax, jax.numpy as jnp
from jax import lax
import numpy as np

D_MODEL = 2048
BATCH = 4
SEQ = 2048
DEPTH = 1
DEC_BATCH = 128
DEC_SEQ = 1
PAST_LEN = 16384
PAGE_SIZE = 128

RWKV_HEAD = 64
RWKV_WIDTH = D_MODEL // 2
RWKV_HEADS = RWKV_WIDTH // RWKV_HEAD
DECAY_LORA = 64
AAA_LORA = 64
GATE_LORA = 160
P_RWKV = 3 * RWKV_WIDTH + DECAY_LORA + AAA_LORA + GATE_LORA
RWKV_GN_EPS = 64e-5
RET_HEADS = 8
RET_DK = 128
RET_DV = 128
RET_WIDTH_QK = RET_HEADS * RET_DK
RET_WIDTH_V = RET_HEADS * RET_DV
RET_CHUNK = 128
ROPE_BASE = 10000.0
RET_GN_EPS = 1e-5
N_IN = P_RWKV + 2 * RET_WIDTH_QK + 2 * RET_WIDTH_V + 2 * D_MODEL
D_FF = 5632
CONV_W = 3
LN_EPS = 1e-5
ALPHA = (2.0 * DEPTH) ** 0.25
BETA = (8.0 * DEPTH) ** -0.25

kernel_name = "hybrid_rwkv7_retention_convffn_step"


def _split(t, sizes):
    return jnp.split(t, [int(s) for s in np.cumsum(sizes)[:-1]], axis=-1)


def _layer_norm(x, w, b):
    xf = x.astype(jnp.float32)
    mu = xf.mean(-1, keepdims=True)
    var = jnp.mean(jnp.square(xf - mu), -1, keepdims=True)
    return ((xf - mu) * lax.rsqrt(var + LN_EPS) * w + b).astype(x.dtype)


def _head_norm(x, w, b, eps):
    xf = x.astype(jnp.float32)
    mu = xf.mean(-1, keepdims=True)
    var = jnp.mean(jnp.square(xf - mu), -1, keepdims=True)
    y = ((xf - mu) * lax.rsqrt(var + eps)).reshape(x.shape[0], x.shape[1], -1)
    return y * w + b


def _rope(t, pos):
    half = t.shape[-1] // 2
    inv = ROPE_BASE ** (-jnp.arange(half, dtype=jnp.float32) / half)
    ang = pos[:, None] * inv[None, :]
    cos = jnp.cos(ang)[None, :, None, :]
    sin = jnp.sin(ang)[None, :, None, :]
    t1, t2 = t[..., :half], t[..., half:]
    return jnp.concatenate([t1 * cos - t2 * sin, t1 * sin + t2 * cos], axis=-1)


def _rwkv7(z, shift_prev, s0, mu, w0, w_decay_up, a0, w_aaa_up, w_gate_up, k_k, k_a, r_k, lnx_w, lnx_b):
    B, L, _ = z.shape
    H, N = RWKV_HEADS, RWKV_HEAD
    prev = jnp.concatenate([shift_prev[:, None].astype(z.dtype), z[:, :-1]], axis=1)
    zs = z + mu * (prev - z)
    r, k, v, wd, ad, gd = _split(zs, (RWKV_WIDTH, RWKV_WIDTH, RWKV_WIDTH, DECAY_LORA, AAA_LORA, GATE_LORA))
    w = -jax.nn.softplus(-(w0 + jnp.tanh(wd) @ w_decay_up)) - 0.5
    decay = jnp.exp(-jnp.exp(w.astype(jnp.float32)))
    a = jax.nn.sigmoid(a0 + ad @ w_aaa_up)
    g = jax.nn.sigmoid(gd) @ w_gate_up
    kk = (k * k_k).astype(jnp.float32).reshape(B, L, H, N)
    kk = kk / jnp.maximum(jnp.linalg.norm(kk, axis=-1, keepdims=True), 1e-12)
    k = k * (1.0 + (a - 1.0) * k_a)
    hd = lambda t: t.astype(jnp.float32).reshape(B, L, H, N)
    rh, kh, vh, ah, wh = hd(r), hd(k), hd(v), hd(a), hd(decay)
    tm = lambda t: t.transpose(1, 0, 2, 3)

    def step(S, inp):
        r_t, w_t, k_t, v_t, kk_t, a_t = inp
        sa = jnp.einsum('bhvk,bhk->bhv', S, -kk_t)
        S = (S * w_t[:, :, None, :] + sa[..., None] * (kk_t * a_t)[:, :, None, :]
             + v_t[..., None] * k_t[:, :, None, :])
        return S, jnp.einsum('bhvk,bhk->bhv', S, r_t)

    S, o = lax.scan(step, s0.astype(jnp.float32), (tm(rh), tm(wh), tm(kh), tm(vh), tm(kk), tm(ah)))
    o = _head_norm(o.transpose(1, 0, 2, 3), lnx_w, lnx_b, RWKV_GN_EPS)
    bonus = (jnp.sum(rh * kh * r_k, axis=-1, keepdims=True) * vh).reshape(B, L, -1)
    y = (o + bonus) * g
    return y, S, z[:, -1]


def _retention(q, k, v, s0, pos0):
    B, L, _ = q.shape
    H = RET_HEADS
    pos = pos0 + jnp.arange(L, dtype=jnp.float32)
    qh = _rope(q.astype(jnp.float32).reshape(B, L, H, RET_DK), pos)
    kh = _rope(k.astype(jnp.float32).reshape(B, L, H, RET_DK), pos) * (RET_DK ** -0.5)
    vh = v.astype(jnp.float32).reshape(B, L, H, RET_DV)
    C = RET_CHUNK if L % RET_CHUNK == 0 else L
    n = L // C
    blocks = lambda t: t.reshape(B, n, C, H, t.shape[-1]).transpose(1, 0, 3, 2, 4)
    log_g = jnp.log1p(-jnp.exp2(-5.0 - jnp.arange(H, dtype=jnp.float32)))
    i = jnp.arange(C, dtype=jnp.float32)
    rel = i[:, None] - i[None, :]
    intra = jnp.where(rel >= 0, jnp.exp(log_g[:, None, None] * jnp.maximum(rel, 0.0)), 0.0)
    q_decay = jnp.exp(log_g[:, None] * (i + 1.0))
    k_decay = jnp.exp(log_g[:, None] * (C - 1.0 - i))
    blk_decay = jnp.exp(log_g * C)

    def step(S, blk):
        qb, kb, vb = blk
        scores = jnp.einsum('bhid,bhjd->bhij', qb, kb) * intra
        o = (jnp.einsum('bhij,bhjv->bhiv', scores, vb)
             + jnp.einsum('bhid,bhdv->bhiv', qb, S) * q_decay[:, :, None])
        S = S * blk_decay[:, None, None] + jnp.einsum('bhjd,bhjv->bhdv', kb * k_decay[:, :, None], vb)
        return S, o

    S, o = lax.scan(step, s0.astype(jnp.float32), (blocks(qh), blocks(kh), blocks(vh)))
    o = o.transpose(1, 0, 3, 2, 4).reshape(B, L, H, RET_DV)
    return o, S


def _conv_ffn(u, conv_prev, w_up, conv_w, conv_b, w_down):
    h = u @ w_up
    L = h.shape[1]
    hp = jnp.concatenate([conv_prev.astype(h.dtype), h], axis=1)
    y = conv_b + sum(conv_w[j] * hp[:, j:j + L] for j in range(CONV_W))
    a, b = jnp.split(y, 2, axis=-1)
    return (jax.nn.silu(a) * b) @ w_down, hp[:, -(CONV_W - 1):]


def _layer(x, c, s_wkv, s_shift, s_ret, s_conv, pos0, lw):
    (w_ada, b_ada, w_in, shift_mu, w0, w_decay_up, a0, w_aaa_up, w_gate_up, k_k, k_a, r_k,
     lnx_w, lnx_b, ret_gn_w, ret_gn_b, w_branch_a, w_branch_b, w_out, ln1_w, ln1_b,
     w_up, conv_w, conv_b, w_down, ln2_w, ln2_b) = lw
    ada = (c @ w_ada + b_ada)[:, None, :]
    sh1, sc1, g1, sh2, sc2, g2 = jnp.split(ada, 6, axis=-1)
    u = x * (1.0 + sc1) + sh1
    proj = u @ w_in
    z_rwkv, q, k, v, ret_g, gate_a, gate_b = _split(
        proj, (P_RWKV, RET_WIDTH_QK, RET_WIDTH_QK, RET_WIDTH_V, RET_WIDTH_V, D_MODEL, D_MODEL))
    ya, wkv_new, shift_new = _rwkv7(z_rwkv, s_shift, s_wkv, shift_mu, w0, w_decay_up, a0, w_aaa_up,
                                    w_gate_up, k_k, k_a, r_k, lnx_w, lnx_b)
    ob, ret_new = _retention(q, k, v, s_ret, pos0)
    yb = _head_norm(ob, ret_gn_w, ret_gn_b, RET_GN_EPS) * jax.nn.silu(ret_g)
    merged = (jax.nn.sigmoid(gate_a) * (ya.astype(x.dtype) @ w_branch_a)
              + jax.nn.sigmoid(gate_b) * (yb.astype(x.dtype) @ w_branch_b))
    x1 = _layer_norm(ALPHA * x + g1 * (merged @ w_out), ln1_w, ln1_b)
    u2 = x1 * (1.0 + sc2) + sh2
    f, conv_new = _conv_ffn(u2, s_conv, w_up, conv_w, conv_b, w_down)
    x2 = _layer_norm(ALPHA * x1 + g2 * f, ln2_w, ln2_b)
    return x2, wkv_new, shift_new, ret_new, conv_new


def setup_inputs(seed: int = 0) -> dict:
    key = jax.random.key(seed)
    ks = iter(jax.random.split(key, 40))
    nrm = lambda shape, s: jax.random.normal(next(ks), shape, jnp.float32) * s
    D, F2 = D_MODEL, 2 * D_FF
    return {
        "x_prompt": nrm((BATCH, SEQ, D), 1.0),
        "x_sample": nrm((DEC_BATCH, DEC_SEQ, D), 1.0),
        "c_prompt": nrm((BATCH, D), 1.0),
        "c_sample": nrm((DEC_BATCH, D), 1.0),
        "state_wkv": nrm((DEPTH, DEC_BATCH, RWKV_HEADS, RWKV_HEAD, RWKV_HEAD), 0.1),
        "state_shift": nrm((DEPTH, DEC_BATCH, P_RWKV), 1.0),
        "state_ret": nrm((DEPTH, DEC_BATCH, RET_HEADS, RET_DK, RET_DV), 0.1),
        "state_conv": nrm((DEPTH, DEC_BATCH, CONV_W - 1, F2), 1.0),
        "w_ada": nrm((DEPTH, D, 6 * D), 0.1 * D ** -0.5),
        "b_ada": nrm((DEPTH, 6 * D), 0.01),
        "w_in": nrm((DEPTH, D, N_IN), D ** -0.5),
        "shift_mu": jax.random.uniform(next(ks), (DEPTH, P_RWKV), jnp.float32),
        "w0": jax.random.uniform(next(ks), (DEPTH, RWKV_WIDTH), jnp.float32, -6.0, -1.0),
        "w_decay_up": nrm((DEPTH, DECAY_LORA, RWKV_WIDTH), 0.1),
        "a0": nrm((DEPTH, RWKV_WIDTH), 0.1),
        "w_aaa_up": nrm((DEPTH, AAA_LORA, RWKV_WIDTH), 0.1),
        "w_gate_up": nrm((DEPTH, GATE_LORA, RWKV_WIDTH), GATE_LORA ** -0.5),
        "k_k": 0.85 + nrm((DEPTH, RWKV_WIDTH), 0.05),
        "k_a": 1.0 + nrm((DEPTH, RWKV_WIDTH), 0.05),
        "r_k": nrm((DEPTH, RWKV_HEADS, RWKV_HEAD), 0.1),
        "lnx_w": 1.0 + nrm((DEPTH, RWKV_WIDTH), 0.05),
        "lnx_b": nrm((DEPTH, RWKV_WIDTH), 0.01),
        "ret_gn_w": 1.0 + nrm((DEPTH, RET_WIDTH_V), 0.05),
        "ret_gn_b": nrm((DEPTH, RET_WIDTH_V), 0.01),
        "w_branch_a": nrm((DEPTH, RWKV_WIDTH, D), RWKV_WIDTH ** -0.5),
        "w_branch_b": nrm((DEPTH, RET_WIDTH_V, D), RET_WIDTH_V ** -0.5),
        "w_out": nrm((DEPTH, D, D), BETA * D ** -0.5),
        "ln1_w": 1.0 + nrm((DEPTH, D), 0.05),
        "ln1_b": nrm((DEPTH, D), 0.01),
        "w_up": nrm((DEPTH, D, F2), D ** -0.5),
        "conv_w": nrm((DEPTH, CONV_W, F2), CONV_W ** -0.5),
        "conv_b": nrm((DEPTH, F2), 0.01),
        "w_down": nrm((DEPTH, D_FF, D), BETA * D_FF ** -0.5),
        "ln2_w": 1.0 + nrm((DEPTH, D), 0.05),
        "ln2_b": nrm((DEPTH, D), 0.01),
    }


def reference(x_prompt, x_sample, c_prompt, c_sample, state_wkv, state_shift, state_ret, state_conv,
              w_ada, b_ada, w_in, shift_mu, w0, w_decay_up, a0, w_aaa_up, w_gate_up, k_k, k_a, r_k,
              lnx_w, lnx_b, ret_gn_w, ret_gn_b, w_branch_a, w_branch_b, w_out, ln1_w, ln1_b,
              w_up, conv_w, conv_b, w_down, ln2_w, ln2_b):
    B = x_prompt.shape[0]
    f32 = jnp.float32
    xp, xs = x_prompt, x_sample
    wkv_p, shift_p, ret_p, conv_p = [], [], [], []
    wkv_s, shift_s, ret_s, conv_s = [], [], [], []
    for l in range(DEPTH):
        lw = (w_ada[l], b_ada[l], w_in[l], shift_mu[l], w0[l], w_decay_up[l], a0[l], w_aaa_up[l],
              w_gate_up[l], k_k[l], k_a[l], r_k[l], lnx_w[l], lnx_b[l], ret_gn_w[l], ret_gn_b[l],
              w_branch_a[l], w_branch_b[l], w_out[l], ln1_w[l], ln1_b[l], w_up[l], conv_w[l],
              conv_b[l], w_down[l], ln2_w[l], ln2_b[l])
        xp, a1, a2, a3, a4 = _layer(
            xp, c_prompt,
            jnp.zeros((B, RWKV_HEADS, RWKV_HEAD, RWKV_HEAD), f32),
            jnp.zeros((B, P_RWKV), f32),
            jnp.zeros((B, RET_HEADS, RET_DK, RET_DV), f32),
            jnp.zeros((B, CONV_W - 1, 2 * D_FF), f32),
            0, lw)
        xs, b1, b2, b3, b4 = _layer(xs, c_sample, state_wkv[l], state_shift[l], state_ret[l],
                                    state_conv[l], PAST_LEN, lw)
        wkv_p.append(a1); shift_p.append(a2); ret_p.append(a3); conv_p.append(a4)
        wkv_s.append(b1); shift_s.append(b2); ret_s.append(b3); conv_s.append(b4)
    st = lambda lst, ref: jnp.stack(lst, axis=0).astype(ref.dtype)
    return (xp, xs,
            st(wkv_p, state_wkv), st(shift_p, state_shift), st(ret_p, state_ret), st(conv_p, state_conv),
            st(wkv_s, state_wkv), st(shift_s, state_shift), st(ret_s, state_ret), st(conv_s, state_conv))
```

```python
import functools
import math

import jax
import jax.numpy as jnp
from jax import lax
from jax.experimental import pallas as pl
from jax.experimental.pallas import tpu as pltpu

F32 = jnp.float32
BF16 = jnp.bfloat16

PAST_LEN = 16384
ROPE_BASE = 10000.0
RWKV_GN_EPS = 64e-5
RET_GN_EPS = 1e-5
LN_EPS = 1e-5
RET_CHUNK = 128

LANES = 128
SUBLANES = 8
VMEM_LIMIT_BYTES = 56 * 1024 * 1024

RWKV_CHUNK = 64


def _params(*sem):
    return pltpu.CompilerParams(dimension_semantics=sem, vmem_limit_bytes=VMEM_LIMIT_BYTES)


def _dot(a, b):
    return jnp.dot(a, b, preferred_element_type=F32)


def _dot_nt(a, b):
    return lax.dot_general(a, b, (((1,), (1,)), ((), ())), preferred_element_type=F32)


def _dot_tn(a, b):
    return lax.dot_general(a, b, (((0,), (0,)), ((), ())), preferred_element_type=F32)


def _bf(x):
    return x.astype(BF16)


def _split2(x):
    hi = x.astype(BF16)
    lo = (x - hi.astype(F32)).astype(BF16)
    return hi, lo


def _dot3(a, b, dot):
    ah, al = _split2(a)
    bh, bl = _split2(b)
    return dot(ah, bh) + (dot(ah, bl) + dot(al, bh))


def _dot_exact_lhs(a_bf, b):
    b1 = b.astype(BF16)
    r1 = b - b1.astype(F32)
    b2 = r1.astype(BF16)
    b3 = (r1 - b2.astype(F32)).astype(BF16)
    return _dot(a_bf, b1) + (_dot(a_bf, b2) + _dot(a_bf, b3))


def _layer_norm_rows(t, w, b):
    mu = jnp.mean(t, axis=-1, keepdims=True)
    d = t - mu
    var = jnp.mean(d * d, axis=-1, keepdims=True)
    return d * lax.rsqrt(var + LN_EPS) * w + b


def _pick_tile(n, candidates):
    for c in candidates:
        if n % c == 0:
            return c
    return n


def _mm_bias_kernel(x_ref, w_ref, b_ref, o_ref):
    o_ref[...] = _dot(x_ref[...], w_ref[...]) + b_ref[...]


def _mm_bias(x_bf, w_bf, b_row):
    m, k = x_bf.shape
    n = w_bf.shape[1]
    tn = _pick_tile(n, (1536, 1024, 512, 256, 128))
    return pl.pallas_call(
        _mm_bias_kernel,
        grid=(n // tn,),
        in_specs=[pl.BlockSpec((m, k), lambda j: (0, 0)),
                  pl.BlockSpec((k, tn), lambda j: (0, j)),
                  pl.BlockSpec((1, tn), lambda j: (0, j))],
        out_specs=pl.BlockSpec((m, tn), lambda j: (0, j)),
        out_shape=jax.ShapeDtypeStruct((m, n), F32),
        compiler_params=_params("arbitrary"),
    )(x_bf, w_bf, b_row)


def _modmm_kernel(x_ref, sc_ref, sh_ref, w_ref, o_ref, u_ref):
    @pl.when(pl.program_id(1) == 0)
    def _():
        u_ref[...] = _bf(x_ref[...] * (1.0 + sc_ref[0]) + sh_ref[0])

    o_ref[...] = _dot(u_ref[...], w_ref[...])


def _modmm(x, sc, sh, w_bf, tm, tpg):
    m, d = x.shape
    n = w_bf.shape[1]
    tn = _pick_tile(n, (512, 256, 128))
    r = sc.shape[1]
    mod_spec = pl.BlockSpec((1, r, d), lambda i, j: (i // tpg, 0, 0))
    return pl.pallas_call(
        _modmm_kernel,
        grid=(m // tm, n // tn),
        in_specs=[pl.BlockSpec((tm, d), lambda i, j: (i, 0)), mod_spec, mod_spec,
                  pl.BlockSpec((d, tn), lambda i, j: (0, j))],
        out_specs=pl.BlockSpec((tm, tn), lambda i, j: (i, j)),
        out_shape=jax.ShapeDtypeStruct((m, n), F32),
        scratch_shapes=[pltpu.VMEM((tm, d), BF16)],
        compiler_params=_params("arbitrary", "arbitrary"),
    )(x, sc, sh, w_bf)


_PV_MU_R, _PV_MU_K, _PV_MU_V, _PV_W0, _PV_A0, _PV_KK, _PV_KA, _PV_RK, _PV_LNW, _PV_LNB = range(10)
_PV_ROWS = 16


def _head_half_mask(shape):
    return lax.broadcasted_iota(jnp.int32, shape, 1) < (LANES // 2)


def _half_sum(x, h0):
    s0 = jnp.sum(jnp.where(h0, x, 0.0), axis=-1, keepdims=True)
    s1 = jnp.sum(jnp.where(h0, 0.0, x), axis=-1, keepdims=True)
    return jnp.where(h0, s0, s1)


def _rwkv_tokens(zr, zk, zv, zl, pr, pk, pv_, plr, pvec, mul, wd, wa, wg):
    row = lambda i: pvec[i:i + 1, :]
    r = zr + row(_PV_MU_R) * (pr - zr)
    k = zk + row(_PV_MU_K) * (pk - zk)
    v = zv + row(_PV_MU_V) * (pv_ - zv)
    ls = zl + mul * (plr - zl)
    wl = _dot(_bf(jnp.tanh(ls)), wd)
    al = _dot(_bf(ls), wa)
    g = _dot(_bf(jax.nn.sigmoid(ls)), wg)
    w = -jax.nn.softplus(-(row(_PV_W0) + wl)) - 0.5
    logd = -jnp.exp(w)
    a = jax.nn.sigmoid(row(_PV_A0) + al)
    h0 = _head_half_mask(zr.shape)
    kkr = k * row(_PV_KK)
    nrm = jnp.sqrt(_half_sum(kkr * kkr, h0))
    kk = kkr / jnp.maximum(nrm, 1e-12)
    kp = k * (1.0 + (a - 1.0) * row(_PV_KA))
    bonus = _half_sum(r * kp * row(_PV_RK), h0) * v
    return r, kp, v, kk, a, logd, g, bonus


def _rwkv_finish(o, bonus, g, pvec):
    h0 = _head_half_mask(o.shape)
    inv_n = 1.0 / (LANES // 2)
    mu = _half_sum(o, h0) * inv_n
    d = o - mu
    var = _half_sum(d * d, h0) * inv_n
    on = d * lax.rsqrt(var + RWKV_GN_EPS)
    return (on * pvec[_PV_LNW:_PV_LNW + 1, :] + pvec[_PV_LNB:_PV_LNB + 1, :] + bonus) * g


def _shift_rows(z, first_row):
    rolled = pltpu.roll(z, 1, 0)
    rowid = lax.broadcasted_iota(jnp.int32, z.shape, 0)
    return jnp.where(rowid == 0, first_row, rolled)


def _rwkv_seq_kernel(zr_ref, zk_ref, zv_ref, zl_ref, fr_ref, fk_ref, fv_ref, fl_ref,
                     pvec_ref, mul_ref, wd_ref, wa_ref, wg_ref, tri_ref,
                     ya_ref, wkv_ref, s_ref, cr_ref, ck_ref, cv_ref, cl_ref):
    i = pl.program_id(2)
    tc = zr_ref.shape[0]
    c = RWKV_CHUNK
    half = LANES // 2

    @pl.when(i == 0)
    def _():
        s_ref[...] = jnp.zeros_like(s_ref)
        cr_ref[0:1, :] = fr_ref[0]
        ck_ref[0:1, :] = fk_ref[0]
        cv_ref[0:1, :] = fv_ref[0]
        cl_ref[0:1, :] = fl_ref[0]

    zr, zk, zv, zl = zr_ref[...], zk_ref[...], zv_ref[...], zl_ref[...]
    pvec = pvec_ref[...]
    r, kp, v, kk, a, logd, g, bonus = _rwkv_tokens(
        zr, zk, zv, zl,
        _shift_rows(zr, cr_ref[0:1, :]), _shift_rows(zk, ck_ref[0:1, :]),
        _shift_rows(zv, cv_ref[0:1, :]), _shift_rows(zl, cl_ref[0:1, :]),
        pvec, mul_ref[...], wd_ref[...], wa_ref[...], wg_ref[...])
    cr_ref[0:1, :] = zr[tc - 1:tc, :]
    ck_ref[0:1, :] = zk[tc - 1:tc, :]
    cv_ref[0:1, :] = zv[tc - 1:tc, :]
    cl_ref[0:1, :] = zl[tc - 1:tc, :]

    alpha = -kk
    beta = kk * a
    cum_incl = _dot_exact_lhs(tri_ref[...], logd)
    cum_excl = cum_incl - logd

    h0 = _head_half_mask((c, LANES))

    def stack(x):
        return jnp.concatenate([jnp.where(h0, x, 0.0), jnp.where(h0, 0.0, x)], axis=0)

    rr = lax.broadcasted_iota(jnp.int32, (2 * c, 2 * c), 0)
    cc = lax.broadcasted_iota(jnp.int32, (2 * c, 2 * c), 1)
    same = (rr >= c) == (cc >= c)
    strict = same & (cc < rr)
    incl = same & (cc <= rr)
    eye = rr == cc
    zeros_blk = jnp.zeros((2 * c, LANES), F32)

    outs = []
    for ci in range(tc // c):
        sl = slice(ci * c, (ci + 1) * c)
        li, le = cum_incl[sl], cum_excl[sl]
        tot = li[c - 1:c, :]
        e_in = jnp.exp(li)
        e_neg = jnp.exp(-li)
        e_hat = jnp.exp(tot - li)
        r_t = stack(r[sl] * e_in)
        a_t = stack(alpha[sl] * jnp.exp(le))
        b_t = stack(beta[sl] * e_neg)
        k_t = stack(kp[sl] * e_neg)
        b_h = stack(beta[sl] * e_hat)
        k_h = stack(kp[sl] * e_hat)
        v_s = stack(v[sl])
        w_c = jnp.exp(tot)

        amat = _dot_nt(_bf(jnp.concatenate([a_t, r_t], axis=0)),
                       _bf(jnp.concatenate([b_t, k_t], axis=0)))
        p = jnp.where(strict, amat[:2 * c, :2 * c], 0.0)
        a_ak = jnp.where(strict, amat[:2 * c, 2 * c:], 0.0)
        a_rb = jnp.where(incl, amat[2 * c:, :2 * c], 0.0)
        a_rk = jnp.where(incl, amat[2 * c:, 2 * c:], 0.0)

        v_bf = _bf(v_s)
        x = jnp.concatenate([a_t, _dot(_bf(a_ak), v_bf)], axis=1)
        nsteps = int(math.log2(c))
        for it in range(nsteps):
            p_bf = _bf(p)
            x = x + _dot(p_bf, _bf(x))
            if it + 1 < nsteps:
                p = _dot(p_bf, p_bf)

        rhs = jnp.concatenate([x, jnp.concatenate([zeros_blk, v_s], axis=1)], axis=0)
        y = _dot(_bf(jnp.concatenate([a_rb, a_rk], axis=1)), _bf(rhs))
        r_hat = r_t + y[:, :LANES]
        o_v = y[:, LANES:]
        r_pair = r_hat[:c] + r_hat[c:]
        o_pair = o_v[:c] + o_v[c:]

        z = _dot_tn(_bf(x), _bf(b_h))
        g_t = jnp.where(eye, w_c, 0.0) + z[:LANES]
        h_t = z[LANES:] + _dot_tn(v_bf, _bf(k_h))

        s0 = s_ref[...]
        outs.append(_dot3(r_pair, s0, _dot_nt) + o_pair)
        s_ref[...] = _dot3(s0, g_t, _dot) + h_t

    o = jnp.concatenate(outs, axis=0) if len(outs) > 1 else outs[0]
    ya_ref[...] = _bf(_rwkv_finish(o, bonus, g, pvec))

    @pl.when(i == pl.num_programs(2) - 1)
    def _():
        s = s_ref[...]
        wkv_ref[0, 0] = s[:half, :half]
        wkv_ref[0, 1] = s[half:, half:]


def _rwkv_seq(proj, nb, seq, lay, first, pvec, mul, wd, wa, wg):
    rw, lslot = lay["rw"], lay["lslot"]
    npair = rw // LANES
    tc = _pick_tile(seq, (256, 128, 64))
    nt = seq // tc
    c = RWKV_CHUNK
    t_idx = jnp.arange(tc)
    tri = ((t_idx[:, None] // c == t_idx[None, :] // c) & (t_idx[None, :] <= t_idx[:, None])).astype(BF16)
    fr, fk, fv, fl = first
    col = lambda off: pl.BlockSpec((tc, LANES), lambda b, p, i, off=off: (b * nt + i, off + p))
    fcol = lambda off: pl.BlockSpec((1, 1, LANES), lambda b, p, i, off=off: (b, 0, off + p))
    return pl.pallas_call(
        _rwkv_seq_kernel,
        grid=(nb, npair, nt),
        in_specs=[col(0), col(npair), col(2 * npair),
                  pl.BlockSpec((tc, lslot), lambda b, p, i: (b * nt + i, lay["off_l"] // lslot)),
                  fcol(0), fcol(npair), fcol(2 * npair),
                  pl.BlockSpec((1, 1, lslot), lambda b, p, i: (b, 0, 0)),
                  pl.BlockSpec((_PV_ROWS, LANES), lambda b, p, i: (0, p)),
                  pl.BlockSpec((1, lslot), lambda b, p, i: (0, 0)),
                  pl.BlockSpec((lslot, LANES), lambda b, p, i: (0, p)),
                  pl.BlockSpec((lslot, LANES), lambda b, p, i: (0, p)),
                  pl.BlockSpec((lslot, LANES), lambda b, p, i: (0, p)),
                  pl.BlockSpec((tc, tc), lambda b, p, i: (0, 0))],
        out_specs=[pl.BlockSpec((tc, LANES), lambda b, p, i: (b * nt + i, p)),
                   pl.BlockSpec((1, 2, LANES // 2, LANES // 2), lambda b, p, i: (b, p, 0, 0))],
        out_shape=[jax.ShapeDtypeStruct((nb * seq, rw), BF16),
                   jax.ShapeDtypeStruct((nb, 2 * npair, LANES // 2, LANES // 2), F32)],
        scratch_shapes=[pltpu.VMEM((LANES, LANES), F32),
                        pltpu.VMEM((SUBLANES, LANES), F32), pltpu.VMEM((SUBLANES, LANES), F32),
                        pltpu.VMEM((SUBLANES, LANES), F32), pltpu.VMEM((SUBLANES, lslot), F32)],
        compiler_params=_params("arbitrary", "arbitrary", "arbitrary"),
    )(proj, proj, proj, proj, fr, fk, fv, fl, pvec, mul, wd, wa, wg, tri)


_STEP_UNROLL = 8


def _rwkv_step_kernel(zr_ref, zk_ref, zv_ref, zl_ref, pr_ref, pk_ref, pv_ref, plr_ref,
                      pvec_ref, mul_ref, wd_ref, wa_ref, wg_ref, s_ref,
                      ya_ref, snew_ref, w_s, kk_s, be_s, kp_s, r_s, v_s, o_s):
    bb = zr_ref.shape[0]
    half = LANES // 2
    pvec = pvec_ref[...]
    r, kp, v, kk, a, logd, g, bonus = _rwkv_tokens(
        zr_ref[...], zk_ref[...], zv_ref[...], zl_ref[...],
        pr_ref[...], pk_ref[...], pv_ref[...], plr_ref[...],
        pvec, mul_ref[...], wd_ref[...], wa_ref[...], wg_ref[...])
    w_s[...] = jnp.exp(logd)
    kk_s[...] = kk
    be_s[...] = kk * a
    kp_s[...] = kp
    r_s[...] = r
    v_s[...] = v
    eye = (lax.broadcasted_iota(jnp.int32, (half, half), 0)
           == lax.broadcasted_iota(jnp.int32, (half, half), 1))

    def body(blk, carry):
        for u in range(_STEP_UNROLL):
            b = blk * _STEP_UNROLL + u
            for e in range(2):
                ls = slice(e * half, (e + 1) * half)
                row = lambda ref: ref[pl.ds(b, 1), ls]
                s = s_ref[b, e]
                sa = jnp.sum(s * (-row(kk_s)), axis=-1, keepdims=True)
                v_col = jnp.sum(jnp.where(eye, row(v_s), 0.0), axis=-1, keepdims=True)
                s_new = s * row(w_s) + sa * row(be_s) + v_col * row(kp_s)
                snew_ref[b, e] = s_new
                o_col = jnp.sum(s_new * row(r_s), axis=-1, keepdims=True)
                o_s[pl.ds(b, 1), ls] = jnp.sum(jnp.where(eye, o_col, 0.0), axis=0, keepdims=True)
        return carry

    lax.fori_loop(0, bb // _STEP_UNROLL, body, 0)
    ya_ref[...] = _bf(_rwkv_finish(o_s[...], bonus, g, pvec))


def _rwkv_step(proj, nb, lay, prev, pvec, mul, wd, wa, wg, s_wkv):
    rw, lslot = lay["rw"], lay["lslot"]
    npair = rw // LANES
    bb = _pick_tile(nb, (32, 16, 8))
    pm, plr = prev
    half = LANES // 2
    col = lambda off: pl.BlockSpec((bb, LANES), lambda p, j, off=off: (j, off + p))
    wspec = pl.BlockSpec((lslot, LANES), lambda p, j: (0, p))
    sspec = pl.BlockSpec((bb, 2, half, half), lambda p, j: (j, p, 0, 0))
    return pl.pallas_call(
        _rwkv_step_kernel,
        grid=(npair, nb // bb),
        in_specs=[col(0), col(npair), col(2 * npair),
                  pl.BlockSpec((bb, lslot), lambda p, j: (j, lay["off_l"] // lslot)),
                  col(0), col(npair), col(2 * npair),
                  pl.BlockSpec((bb, lslot), lambda p, j: (j, 0)),
                  pl.BlockSpec((_PV_ROWS, LANES), lambda p, j: (0, p)),
                  pl.BlockSpec((1, lslot), lambda p, j: (0, 0)),
                  wspec, wspec, wspec, sspec],
        out_specs=[pl.BlockSpec((bb, LANES), lambda p, j: (j, p)), sspec],
        out_shape=[jax.ShapeDtypeStruct((nb, rw), BF16),
                   jax.ShapeDtypeStruct(s_wkv.shape, F32)],
        scratch_shapes=[pltpu.VMEM((bb, LANES), F32) for _ in range(7)],
        compiler_params=_params("arbitrary", "arbitrary"),
    )(proj, proj, proj, proj, pm, pm, pm, plr, pvec, mul, wd, wa, wg, s_wkv)


def _rope_rows(t, cos2, sin2):
    return t * cos2 + pltpu.roll(t, LANES // 2, 1) * sin2


def _head_norm_rows(o, eps):
    mu = jnp.mean(o, axis=-1, keepdims=True)
    d = o - mu
    var = jnp.mean(d * d, axis=-1, keepdims=True)
    return d * lax.rsqrt(var + eps)


def _ret_seq_kernel(q_ref, k_ref, v_ref, g_ref, cos_ref, sin_ref, intra_ref, qd_ref, kd_ref,
                    blk_ref, gnw_ref, gnb_ref, yb_ref, ret_ref, s_ref):
    i = pl.program_id(1)
    nh = s_ref.shape[0]
    dk = s_ref.shape[1]

    @pl.when(i == 0)
    def _():
        s_ref[...] = jnp.zeros_like(s_ref)

    cos2, sin2 = cos_ref[...], sin_ref[...]
    for h in range(nh):
        hs = slice(h * LANES, (h + 1) * LANES)
        qh = _rope_rows(q_ref[:, hs], cos2, sin2)
        kh = _rope_rows(k_ref[:, hs], cos2, sin2) * (dk ** -0.5)
        vb = _bf(v_ref[:, hs])
        qb = _bf(qh)
        scores = _dot_nt(qb, _bf(kh)) * intra_ref[h]
        s0 = s_ref[h]
        o = _dot(_bf(scores), vb) + _dot(qb, _bf(s0)) * qd_ref[h]
        s_ref[h] = s0 * blk_ref[h] + _dot_tn(_bf(kh * kd_ref[h]), vb)
        on = _head_norm_rows(o, RET_GN_EPS)
        yb_ref[:, hs] = _bf((on * gnw_ref[:, hs] + gnb_ref[:, hs]) * jax.nn.silu(g_ref[:, hs]))

    @pl.when(i == pl.num_programs(1) - 1)
    def _():
        ret_ref[0] = s_ref[...]


def _ret_tables(nh, c):
    log_g = jnp.log1p(-jnp.exp2(-5.0 - jnp.arange(nh, dtype=F32)))
    i = jnp.arange(c, dtype=F32)
    rel = i[:, None] - i[None, :]
    intra = jnp.where(rel >= 0, jnp.exp(log_g[:, None, None] * jnp.maximum(rel, 0.0)), 0.0)
    q_decay = jnp.exp(log_g[:, None] * (i + 1.0))
    k_decay = jnp.exp(log_g[:, None] * (c - 1.0 - i))
    blk_decay = jnp.exp(log_g * c)
    return intra, q_decay, k_decay, blk_decay


def _rope_tables(pos, dk):
    half = dk // 2
    inv = ROPE_BASE ** (-jnp.arange(half, dtype=F32) / half)
    ang = pos[:, None] * inv[None, :]
    cos, sin = jnp.cos(ang), jnp.sin(ang)
    return jnp.concatenate([cos, cos], axis=-1), jnp.concatenate([-sin, sin], axis=-1)


def _ret_seq(proj, nb, seq, lay, gnw, gnb):
    qk, rv, nh = lay["qk"], lay["rv"], lay["ret_heads"]
    dk, dv = qk // nh, rv // nh
    assert dk == LANES and dv == LANES
    c = RET_CHUNK if seq % RET_CHUNK == 0 else seq
    assert c % SUBLANES == 0
    nt = seq // c
    intra, qd, kd, blk = _ret_tables(nh, c)
    qd = jnp.broadcast_to(qd[:, :, None], (nh, c, dv))
    kd = jnp.broadcast_to(kd[:, :, None], (nh, c, dk))
    blk = jnp.broadcast_to(blk[:, None, None], (nh, 1, dv))
    cos2, sin2 = _rope_tables(jnp.arange(seq, dtype=F32), dk)
    seg = lambda off, w: pl.BlockSpec((c, w), lambda b, i, off=off, w=w: (b * nt + i, off // w))
    full3 = lambda a: pl.BlockSpec(a.shape, lambda b, i: (0, 0, 0))
    return pl.pallas_call(
        _ret_seq_kernel,
        grid=(nb, nt),
        in_specs=[seg(lay["off_q"], qk), seg(lay["off_kr"], qk), seg(lay["off_vr"], rv),
                  seg(lay["off_rg"], rv),
                  pl.BlockSpec((c, dk), lambda b, i: (i, 0)), pl.BlockSpec((c, dk), lambda b, i: (i, 0)),
                  full3(intra), full3(qd), full3(kd), full3(blk),
                  pl.BlockSpec((1, rv), lambda b, i: (0, 0)), pl.BlockSpec((1, rv), lambda b, i: (0, 0))],
        out_specs=[pl.BlockSpec((c, rv), lambda b, i: (b * nt + i, 0)),
                   pl.BlockSpec((1, nh, dk, dv), lambda b, i: (b, 0, 0, 0))],
        out_shape=[jax.ShapeDtypeStruct((nb * seq, rv), BF16),
                   jax.ShapeDtypeStruct((nb, nh, dk, dv), F32)],
        scratch_shapes=[pltpu.VMEM((nh, dk, dv), F32)],
        compiler_params=_params("arbitrary", "arbitrary"),
    )(proj, proj, proj, proj, cos2, sin2, intra, qd, kd, blk, gnw, gnb)


_RC_INTRA, _RC_QD, _RC_KD, _RC_BLK = range(4)


def _ret_step_kernel(q_ref, k_ref, v_ref, g_ref, cos_ref, sin_ref, rc_ref, gnw_ref, gnb_ref, s_ref,
                     yb_ref, snew_ref, q_s, k_s, o_s):
    bb = q_ref.shape[0]
    nh = s_ref.shape[1]
    dk = s_ref.shape[2]
    cos2, sin2 = cos_ref[...], sin_ref[...]
    for h in range(nh):
        hs = slice(h * LANES, (h + 1) * LANES)
        q_s[:, hs] = _rope_rows(q_ref[:, hs], cos2, sin2)
        k_s[:, hs] = _rope_rows(k_ref[:, hs], cos2, sin2) * (dk ** -0.5)
    eye = (lax.broadcasted_iota(jnp.int32, (LANES, LANES), 0)
           == lax.broadcasted_iota(jnp.int32, (LANES, LANES), 1))
    for b in range(bb):
        for h in range(nh):
            hs = slice(h * LANES, (h + 1) * LANES)
            rc = lambda j: rc_ref[h, j:j + 1, :]
            q_row = q_s[b:b + 1, hs]
            k_row = k_s[b:b + 1, hs]
            v_row = v_ref[b:b + 1, hs]
            s0 = s_ref[b, h]
            q_col = jnp.sum(jnp.where(eye, q_row, 0.0), axis=-1, keepdims=True)
            k_col = jnp.sum(jnp.where(eye, k_row, 0.0), axis=-1, keepdims=True)
            score = jnp.sum(q_row * k_row, axis=-1, keepdims=True) * rc(_RC_INTRA)
            o_row = score * v_row + jnp.sum(s0 * q_col, axis=0, keepdims=True) * rc(_RC_QD)
            snew_ref[b, h] = s0 * rc(_RC_BLK) + (k_col * rc(_RC_KD)) * v_row
            o_s[b:b + 1, hs] = o_row
    for h in range(nh):
        hs = slice(h * LANES, (h + 1) * LANES)
        on = _head_norm_rows(o_s[:, hs], RET_GN_EPS)
        yb_ref[:, hs] = _bf((on * gnw_ref[:, hs] + gnb_ref[:, hs]) * jax.nn.silu(g_ref[:, hs]))


def _ret_step(proj, nb, lay, gnw, gnb, s_ret, pos0):
    qk, rv, nh = lay["qk"], lay["rv"], lay["ret_heads"]
    dk, dv = qk // nh, rv // nh
    assert dk == LANES and dv == LANES
    bb = SUBLANES
    intra, qd, kd, blk = _ret_tables(nh, 1)
    rc = jnp.stack([intra[:, 0, 0], qd[:, 0], kd[:, 0], blk], axis=1)
    rc = jnp.pad(rc, ((0, 0), (0, SUBLANES - 4)))
    rc = jnp.broadcast_to(rc[:, :, None], (nh, SUBLANES, LANES))
    cos2, sin2 = _rope_tables(jnp.asarray([pos0], dtype=F32), dk)
    seg = lambda off, w: pl.BlockSpec((bb, w), lambda j, off=off, w=w: (j, off // w))
    sspec = pl.BlockSpec((bb, nh, dk, dv), lambda j: (j, 0, 0, 0))
    row = lambda w: pl.BlockSpec((1, w), lambda j: (0, 0))
    return pl.pallas_call(
        _ret_step_kernel,
        grid=(nb // bb,),
        in_specs=[seg(lay["off_q"], qk), seg(lay["off_kr"], qk), seg(lay["off_vr"], rv),
                  seg(lay["off_rg"], rv), row(dk), row(dk),
                  pl.BlockSpec(rc.shape, lambda j: (0, 0, 0)), row(rv), row(rv), sspec],
        out_specs=[pl.BlockSpec((bb, rv), lambda j: (j, 0)), sspec],
        out_shape=[jax.ShapeDtypeStruct((nb, rv), BF16), jax.ShapeDtypeStruct(s_ret.shape, F32)],
        scratch_shapes=[pltpu.VMEM((bb, qk), F32), pltpu.VMEM((bb, qk), F32), pltpu.VMEM((bb, rv), F32)],
        compiler_params=_params("arbitrary"),
    )(proj, proj, proj, proj, cos2, sin2, rc, gnw, gnb, s_ret)


def _merge_kernel(alpha, ng, *refs):
    ya_ref, yb_ref = refs[0], refs[1]
    ga_refs = refs[2:2 + ng]
    gb_refs = refs[2 + ng:2 + 2 * ng]
    (x_ref, g1_ref, sc2_ref, sh2_ref, wa_ref, wb_ref, wo_ref, lnw_ref, lnb_ref,
     x1_ref, u2_ref) = refs[2 + 2 * ng:]
    cat = lambda rs: jnp.concatenate([r[...] for r in rs], axis=1) if ng > 1 else rs[0][...]
    merged = (jax.nn.sigmoid(cat(ga_refs)) * _dot(ya_ref[...], wa_ref[...])
              + jax.nn.sigmoid(cat(gb_refs)) * _dot(yb_ref[...], wb_ref[...]))
    t = alpha * x_ref[...] + g1_ref[0] * _dot(_bf(merged), wo_ref[...])
    x1 = _layer_norm_rows(t, lnw_ref[...], lnb_ref[...])
    x1_ref[...] = x1
    u2_ref[...] = _bf(x1 * (1.0 + sc2_ref[0]) + sh2_ref[0])


def _merge(ya, yb, proj, x, g1, sc2, sh2, wba, wbb, wout, lnw, lnb, lay, alpha, tm, tpg):
    m, d = x.shape
    rw, rv = ya.shape[1], yb.shape[1]
    gw = math.gcd(lay["off_ga"], d)
    ng = d // gw
    r = g1.shape[1]
    rowt = lambda w: pl.BlockSpec((tm, w), lambda i: (i, 0))
    gate = lambda off, q: pl.BlockSpec((tm, gw), lambda i, off=off, q=q: (i, off // gw + q))
    mod = pl.BlockSpec((1, r, d), lambda i: (i // tpg, 0, 0))
    const = lambda a: pl.BlockSpec(a.shape, lambda i: (0, 0), pipeline_mode=pl.Buffered(1))
    return pl.pallas_call(
        functools.partial(_merge_kernel, alpha, ng),
        grid=(m // tm,),
        in_specs=[rowt(rw), rowt(rv)]
                 + [gate(lay["off_ga"], q) for q in range(ng)]
                 + [gate(lay["off_gb"], q) for q in range(ng)]
                 + [rowt(d), mod, mod, mod, const(wba), const(wbb), const(wout), const(lnw), const(lnb)],
        out_specs=[rowt(d), rowt(d)],
        out_shape=[jax.ShapeDtypeStruct((m, d), F32), jax.ShapeDtypeStruct((m, d), BF16)],
        compiler_params=_params("arbitrary"),
    )(ya, yb, *([proj] * (2 * ng)), x, g1, sc2, sh2, wba, wbb, wout, lnw, lnb)


def _ffn_up_seq_kernel(tps, u_ref, wa_ref, wb_ref, cwa_ref, cwb_ref, ia_ref, ib_ref,
                       act_ref, ta_ref, tb_ref, ha_s, hb_s, ca_s, cb_s):
    i, j = pl.program_id(0), pl.program_id(1)
    tm = u_ref.shape[0]
    u = u_ref[...]
    halves = []
    for w_ref, cw_ref, init_ref, tail_ref, h_s, c_s in (
            (wa_ref, cwa_ref, ia_ref, ta_ref, ha_s, ca_s),
            (wb_ref, cwb_ref, ib_ref, tb_ref, hb_s, cb_s)):
        h = _dot(u, w_ref[...])

        @pl.when(i % tps == 0)
        def _():
            h_s[0:SUBLANES, :] = init_ref[0]

        @pl.when(i % tps != 0)
        def _():
            h_s[0:SUBLANES, :] = c_s[j]

        h_s[SUBLANES:SUBLANES + tm, :] = h
        tail = h_s[tm:tm + SUBLANES, :]
        c_s[j] = tail
        tail_ref[0] = tail
        cw = cw_ref[...]
        halves.append(cw[3:4, :] + cw[0:1, :] * h_s[SUBLANES - 2:SUBLANES - 2 + tm, :]
                      + cw[1:2, :] * h_s[SUBLANES - 1:SUBLANES - 1 + tm, :] + cw[2:3, :] * h)
    act_ref[...] = _bf(jax.nn.silu(halves[0]) * halves[1])


def _ffn_up_step_kernel(u_ref, wa_ref, wb_ref, cwa_ref, cwb_ref, p1a_ref, p2a_ref, p1b_ref, p2b_ref,
                        act_ref, ha_ref, hb_ref):
    u = u_ref[...]
    halves = []
    for w_ref, cw_ref, p1_ref, p2_ref, h_ref in ((wa_ref, cwa_ref, p1a_ref, p2a_ref, ha_ref),
                                                 (wb_ref, cwb_ref, p1b_ref, p2b_ref, hb_ref)):
        h = _dot(u, w_ref[...])
        h_ref[...] = h
        cw = cw_ref[...]
        halves.append(cw[3:4, :] + cw[0:1, :] * p2_ref[...] + cw[1:2, :] * p1_ref[...] + cw[2:3, :] * h)
    act_ref[...] = _bf(jax.nn.silu(halves[0]) * halves[1])


def _ffn_up(u2, w_up_bf, cwt, nb, seq, s_conv, tm):
    m, d = u2.shape
    f2 = w_up_bf.shape[1]
    f = f2 // 2
    tn = _pick_tile(f, (512, 256, 128))
    nj = f // tn
    w_a = pl.BlockSpec((d, tn), lambda i, j: (0, j))
    w_b = pl.BlockSpec((d, tn), lambda i, j: (0, nj + j))
    cw_a = pl.BlockSpec((4, tn), lambda i, j: (0, j))
    cw_b = pl.BlockSpec((4, tn), lambda i, j: (0, nj + j))
    if seq == 1:
        p_a = pl.BlockSpec((tm, tn), lambda i, j: (i, j))
        p_b = pl.BlockSpec((tm, tn), lambda i, j: (i, nj + j))
        prev1, prev2 = s_conv[:, 1, :], s_conv[:, 0, :]
        act, h_a, h_b = pl.pallas_call(
            _ffn_up_step_kernel,
            grid=(m // tm, nj),
            in_specs=[pl.BlockSpec((tm, d), lambda i, j: (i, 0)), w_a, w_b, cw_a, cw_b, p_a, p_a, p_b, p_b],
            out_specs=[pl.BlockSpec((tm, tn), lambda i, j: (i, j))] * 3,
            out_shape=[jax.ShapeDtypeStruct((m, f), BF16), jax.ShapeDtypeStruct((m, f), F32),
                       jax.ShapeDtypeStruct((m, f), F32)],
            compiler_params=_params("arbitrary", "arbitrary"),
        )(u2, w_up_bf, w_up_bf, cwt, cwt, prev1, prev2, prev1, prev2)
        conv_new = jnp.stack([prev1, jnp.concatenate([h_a, h_b], axis=-1)], axis=1)
        return act, conv_new
    assert seq % tm == 0 and tm >= SUBLANES
    tps = seq // tm
    init = jnp.pad(s_conv, ((0, 0), (SUBLANES - 2, 0), (0, 0)))
    i_a = pl.BlockSpec((1, SUBLANES, tn), lambda i, j: (i // tps, 0, j))
    i_b = pl.BlockSpec((1, SUBLANES, tn), lambda i, j: (i // tps, 0, nj + j))
    t_o = pl.BlockSpec((1, SUBLANES, tn), lambda i, j: (i // tps, 0, j))
    act, t_a, t_b = pl.pallas_call(
        functools.partial(_ffn_up_seq_kernel, tps),
        grid=(m // tm, nj),
        in_specs=[pl.BlockSpec((tm, d), lambda i, j: (i, 0)), w_a, w_b, cw_a, cw_b, i_a, i_b],
        out_specs=[pl.BlockSpec((tm, tn), lambda i, j: (i, j)), t_o, t_o],
        out_shape=[jax.ShapeDtypeStruct((m, f), BF16), jax.ShapeDtypeStruct((nb, SUBLANES, f), F32),
                   jax.ShapeDtypeStruct((nb, SUBLANES, f), F32)],
        scratch_shapes=[pltpu.VMEM((tm + SUBLANES, tn), F32), pltpu.VMEM((tm + SUBLANES, tn), F32),
                        pltpu.VMEM((nj, SUBLANES, tn), F32), pltpu.VMEM((nj, SUBLANES, tn), F32)],
        compiler_params=_params("arbitrary", "arbitrary"),
    )(u2, w_up_bf, w_up_bf, cwt, cwt, init, init)
    conv_new = jnp.concatenate([t_a[:, SUBLANES - 2:, :], t_b[:, SUBLANES - 2:, :]], axis=-1)
    return act, conv_new


def _ffn_down_kernel(alpha, act_ref, w_ref, x1_ref, g2_ref, lnw_ref, lnb_ref, o_ref, acc_ref):
    k = pl.program_id(1)

    @pl.when(k == 0)
    def _():
        acc_ref[...] = jnp.zeros_like(acc_ref)

    acc_ref[...] += _dot(act_ref[...], w_ref[...])

    @pl.when(k == pl.num_programs(1) - 1)
    def _():
        t = alpha * x1_ref[...] + g2_ref[0] * acc_ref[...]
        o_ref[...] = _layer_norm_rows(t, lnw_ref[...], lnb_ref[...])


def _ffn_down(act, w_down_bf, x1, g2, lnw, lnb, alpha, tm, tpg):
    m, f = act.shape
    d = w_down_bf.shape[1]
    tk = _pick_tile(f, (512, 256, 128))
    r = g2.shape[1]
    return pl.pallas_call(
        functools.partial(_ffn_down_kernel, alpha),
        grid=(m // tm, f // tk),
        in_specs=[pl.BlockSpec((tm, tk), lambda i, k: (i, k)),
                  pl.BlockSpec((tk, d), lambda i, k: (k, 0)),
                  pl.BlockSpec((tm, d), lambda i, k: (i, 0)),
                  pl.BlockSpec((1, r, d), lambda i, k: (i // tpg, 0, 0)),
                  pl.BlockSpec((1, d), lambda i, k: (0, 0)),
                  pl.BlockSpec((1, d), lambda i, k: (0, 0))],
        out_specs=pl.BlockSpec((tm, d), lambda i, k: (i, 0)),
        out_shape=jax.ShapeDtypeStruct((m, d), F32),
        scratch_shapes=[pltpu.VMEM((tm, d), F32)],
        compiler_params=_params("arbitrary", "arbitrary"),
    )(act, w_down_bf, x1, g2, lnw, lnb)


def _layout(d, rw, qk, rv, lora_w, ret_heads):
    lslot = next(s for s in (128, 256, 512, 1024, 2048) if s >= lora_w)
    off_q = 3 * rw
    off_kr = off_q + qk
    off_vr = off_kr + qk
    off_rg = off_vr + rv
    off_ga = off_rg + rv
    off_gb = off_ga + d
    off_l = off_gb + d
    assert rw % LANES == 0 and off_l % lslot == 0
    assert off_q % qk == 0 and off_kr % qk == 0 and off_vr % rv == 0 and off_rg % rv == 0
    return dict(d=d, rw=rw, qk=qk, rv=rv, lslot=lslot, lora_w=lora_w, ret_heads=ret_heads,
                off_q=off_q, off_kr=off_kr, off_vr=off_vr, off_rg=off_rg, off_ga=off_ga,
                off_gb=off_gb, off_l=off_l, nt=off_l + lslot)


def _prep_weights(lay, w_in, shift_mu, w0, w_decay_up, a0, w_aaa_up, w_gate_up, k_k, k_a, r_k,
                  lnx_w, lnx_b):
    rw, lslot, lora_w = lay["rw"], lay["lslot"], lay["lora_w"]
    p_rwkv = 3 * rw + lora_w
    d = w_in.shape[0]
    w_perm = jnp.concatenate(
        [w_in[:, :3 * rw], w_in[:, p_rwkv:], w_in[:, 3 * rw:p_rwkv],
         jnp.zeros((d, lslot - lora_w), w_in.dtype)], axis=1).astype(BF16)
    dl, al, gl = w_decay_up.shape[0], w_aaa_up.shape[0], w_gate_up.shape[0]
    pad_rows = lambda w, lo: jnp.pad(w, ((lo, lslot - lo - w.shape[0]), (0, 0))).astype(BF16)
    wd = pad_rows(w_decay_up, 0)
    wa = pad_rows(w_aaa_up, dl)
    wg = pad_rows(w_gate_up, dl + al)
    rows = [shift_mu[:rw], shift_mu[rw:2 * rw], shift_mu[2 * rw:3 * rw], w0, a0, k_k, k_a,
            r_k.reshape(-1), lnx_w, lnx_b]
    pvec = jnp.pad(jnp.stack(rows, axis=0), ((0, _PV_ROWS - len(rows)), (0, 0)))
    mul = jnp.pad(shift_mu[3 * rw:], (0, lslot - lora_w))[None, :]
    return w_perm, wd, wa, wg, pvec, mul


def _run_layer(x2d, nb, seq, ada, states, pos0, lay, wts, alpha):
    d, rw, lslot, lora_w = lay["d"], lay["rw"], lay["lslot"], lay["lora_w"]
    (w_perm, wd, wa, wg, pvec, mul, gnw, gnb, wba, wbb, wout, ln1w, ln1b,
     w_up, cwt, w_down, ln2w, ln2b) = wts
    m = nb * seq
    sh1, sc1, g1, sh2, sc2, g2 = jnp.split(ada, 6, axis=-1)
    if seq == 1:
        tm, tpg = m, 1
        shape_mod = lambda t: t[None]
    else:
        tm = _pick_tile(seq, (512, 256, 128, 64, 32, 16, 8))
        tpg = seq // tm
        shape_mod = lambda t: t[:, None, :]
    sh1, sc1, g1, sh2, sc2, g2 = map(shape_mod, (sh1, sc1, g1, sh2, sc2, g2))
    tm_in = _pick_tile(seq, (1024, 512, 256, 128, 64, 32, 16, 8)) if seq > 1 else m
    proj = _modmm(x2d, sc1, sh1, w_perm, tm_in, (seq // tm_in) if seq > 1 else 1)

    s_wkv, s_shift, s_ret, s_conv = states
    shift_main = s_shift[:, :3 * rw]
    shift_lora = jnp.pad(s_shift[:, 3 * rw:], ((0, 0), (0, lslot - lora_w)))
    if seq == 1:
        ya, wkv_new = _rwkv_step(proj, nb, lay, (shift_main, shift_lora), pvec, mul, wd, wa, wg, s_wkv)
        yb, ret_new = _ret_step(proj, nb, lay, gnw, gnb, s_ret, pos0)
    else:
        first = (shift_main[:, None, :], shift_main[:, None, :], shift_main[:, None, :],
                 shift_lora[:, None, :])
        ya, wkv_new = _rwkv_seq(proj, nb, seq, lay, first, pvec, mul, wd, wa, wg)
        yb, ret_new = _ret_seq(proj, nb, seq, lay, gnw, gnb)
    last = proj.reshape(nb, seq, lay["nt"])[:, -1, :]
    shift_new = jnp.concatenate([last[:, :3 * rw], last[:, lay["off_l"]:lay["off_l"] + lora_w]], axis=-1)

    tm_merge = min(tm, 256)
    x1, u2 = _merge(ya, yb, proj, x2d, g1, sc2, sh2, wba, wbb, wout, ln1w, ln1b, lay, alpha,
                    tm_merge, (seq // tm_merge) if seq > 1 else 1)
    act, conv_new = _ffn_up(u2, w_up, cwt, nb, seq, s_conv, tm)
    x2 = _ffn_down(act, w_down, x1, g2, ln2w, ln2b, alpha, tm, tpg)
    return x2, wkv_new, shift_new, ret_new, conv_new


def kernel(x_prompt, x_sample, c_prompt, c_sample, state_wkv, state_shift, state_ret, state_conv, w_ada, b_ada, w_in, shift_mu, w0, w_decay_up, a0, w_aaa_up, w_gate_up, k_k, k_a, r_k, lnx_w, lnx_b, ret_gn_w, ret_gn_b, w_branch_a, w_branch_b, w_out, ln1_w, ln1_b, w_up, conv_w, conv_b, w_down, ln2_w, ln2_b):
    depth = w_ada.shape[0]
    nbp, seq_p, d = x_prompt.shape
    nbs, seq_s, _ = x_sample.shape
    assert seq_s == 1
    rw = k_k.shape[-1]
    ret_heads, dk, dv = state_ret.shape[2:]
    lora_w = w_decay_up.shape[1] + w_aaa_up.shape[1] + w_gate_up.shape[1]
    lay = _layout(d, rw, ret_heads * dk, ret_heads * dv, lora_w, ret_heads)
    heads, hn = r_k.shape[1:]
    assert hn == LANES // 2 and heads * hn == rw
    alpha = (2.0 * depth) ** 0.25
    f2 = w_up.shape[-1]

    xp = x_prompt.reshape(nbp * seq_p, d)
    xs = x_sample.reshape(nbs * seq_s, d)
    c_all = jnp.concatenate([c_prompt, c_sample], axis=0)
    pad = (-c_all.shape[0]) % SUBLANES
    c_all = _bf(jnp.pad(c_all, ((0, pad), (0, 0))))

    outs_p, outs_s = [], []
    for l in range(depth):
        w_perm, wd, wa, wg, pvec, mul = _prep_weights(
            lay, w_in[l], shift_mu[l], w0[l], w_decay_up[l], a0[l], w_aaa_up[l], w_gate_up[l],
            k_k[l], k_a[l], r_k[l], lnx_w[l], lnx_b[l])
        cwt = jnp.concatenate([conv_w[l], conv_b[l][None, :]], axis=0)
        wts = (w_perm, wd, wa, wg, pvec, mul, ret_gn_w[l][None, :], ret_gn_b[l][None, :],
               _bf(w_branch_a[l]), _bf(w_branch_b[l]), _bf(w_out[l]), ln1_w[l][None, :], ln1_b[l][None, :],
               _bf(w_up[l]), cwt, _bf(w_down[l]), ln2_w[l][None, :], ln2_b[l][None, :])
        ada = _mm_bias(c_all, _bf(w_ada[l]), b_ada[l][None, :])
        zero_states = (None, jnp.zeros((nbp, state_shift.shape[-1]), F32), None,
                       jnp.zeros((nbp, state_conv.shape[2], f2), F32))
        xp, *st_p = _run_layer(xp, nbp, seq_p, ada[:nbp], zero_states, 0.0, lay, wts, alpha)
        xs, *st_s = _run_layer(xs, nbs, seq_s, ada[nbp:nbp + nbs],
                               (state_wkv[l], state_shift[l], state_ret[l], state_conv[l]),
                               float(PAST_LEN), lay, wts, alpha)
        outs_p.append(st_p)
        outs_s.append(st_s)

    stack = lambda lst, j, ref: jnp.stack([s[j] for s in lst], axis=0).astype(ref.dtype)
    refs = (state_wkv, state_shift, state_ret, state_conv)
    return (xp.reshape(x_prompt.shape), xs.reshape(x_sample.shape),
            *[stack(outs_p, j, refs[j]) for j in range(4)],
            *[stack(outs_s, j, refs[j]) for j in range(4)])
```

```python
import functools
import math

import jax
import jax.numpy as jnp
from jax import lax
from jax.experimental import pallas as pl
from jax.experimental.pallas import tpu as pltpu

F32 = jnp.float32
BF16 = jnp.bfloat16

PAST_LEN = 16384
ROPE_BASE = 10000.0
RWKV_GN_EPS = 64e-5
RET_GN_EPS = 1e-5
LN_EPS = 1e-5
RET_CHUNK = 128

LANES = 128
SUBLANES = 8
MXU_WIDTH = 256
VMEM_LIMIT_BYTES = 56 * 1024 * 1024

RWKV_CHUNK = 64
RWKV_PAIRS_PER_STEP = 4


def _params(*sem):
    return pltpu.CompilerParams(dimension_semantics=sem, vmem_limit_bytes=VMEM_LIMIT_BYTES)


def _dot(a, b):
    return jnp.dot(a, b, preferred_element_type=F32)


def _dot_nt(a, b):
    return lax.dot_general(a, b, (((1,), (1,)), ((), ())), preferred_element_type=F32)


def _dot_tn(a, b):
    return lax.dot_general(a, b, (((0,), (0,)), ((), ())), preferred_element_type=F32)


def _bf(x):
    return x.astype(BF16)


def _split2(x):
    hi = x.astype(BF16)
    lo = (x - hi.astype(F32)).astype(BF16)
    return hi, lo


def _dot3_nn(a, b):
    ah, al = _split2(a)
    bh, bl = _split2(b)
    return _dot(jnp.concatenate([ah, ah, al], axis=1), jnp.concatenate([bh, bl, bh], axis=0))


def _dot3_nt(a, b):
    ah, al = _split2(a)
    bh, bl = _split2(b)
    return _dot_nt(jnp.concatenate([ah, ah, al], axis=1), jnp.concatenate([bh, bl, bh], axis=1))


def _dot_exact_lhs(a_bf, b):
    b1 = b.astype(BF16)
    r1 = b - b1.astype(F32)
    b2 = r1.astype(BF16)
    b3 = (r1 - b2.astype(F32)).astype(BF16)
    return _dot(jnp.concatenate([a_bf, a_bf, a_bf], axis=1), jnp.concatenate([b1, b2, b3], axis=0))


def _layer_norm_rows(t, w, b):
    mu = jnp.mean(t, axis=-1, keepdims=True)
    d = t - mu
    var = jnp.mean(d * d, axis=-1, keepdims=True)
    return d * lax.rsqrt(var + LN_EPS) * w + b


def _pick_tile(n, candidates):
    for c in candidates:
        if n % c == 0:
            return c
    return n


def _mm_bias_kernel(x_ref, w_ref, b_ref, o_ref):
    o_ref[...] = _dot(x_ref[...], w_ref[...]) + b_ref[...]


def _mm_bias(x_bf, w_bf, b_row):
    m, k = x_bf.shape
    n = w_bf.shape[1]
    tn = _pick_tile(n, (1536, 1024, 512, 256, 128))
    return pl.pallas_call(
        _mm_bias_kernel,
        grid=(n // tn,),
        in_specs=[pl.BlockSpec((m, k), lambda j: (0, 0)),
                  pl.BlockSpec((k, tn), lambda j: (0, j)),
                  pl.BlockSpec((1, tn), lambda j: (0, j))],
        out_specs=pl.BlockSpec((m, tn), lambda j: (0, j)),
        out_shape=jax.ShapeDtypeStruct((m, n), F32),
        compiler_params=_params("arbitrary"),
    )(x_bf, w_bf, b_row)


def _modmm_kernel(x_ref, sc_ref, sh_ref, w_ref, o_ref, u_ref):
    @pl.when(pl.program_id(1) == 0)
    def _():
        u_ref[...] = _bf(x_ref[...] * (1.0 + sc_ref[0]) + sh_ref[0])

    o_ref[...] = _dot(u_ref[...], w_ref[...])


def _modmm(x, sc, sh, w_bf, tm, tpg):
    m, d = x.shape
    n = w_bf.shape[1]
    tn = _pick_tile(n, (512, 256, 128))
    r = sc.shape[1]
    mod_spec = pl.BlockSpec((1, r, d), lambda i, j: (i // tpg, 0, 0))
    return pl.pallas_call(
        _modmm_kernel,
        grid=(m // tm, n // tn),
        in_specs=[pl.BlockSpec((tm, d), lambda i, j: (i, 0)), mod_spec, mod_spec,
                  pl.BlockSpec((d, tn), lambda i, j: (0, j))],
        out_specs=pl.BlockSpec((tm, tn), lambda i, j: (i, j)),
        out_shape=jax.ShapeDtypeStruct((m, n), F32),
        scratch_shapes=[pltpu.VMEM((tm, d), BF16)],
        compiler_params=_params("arbitrary", "arbitrary"),
    )(x, sc, sh, w_bf)


_PV_MU_R, _PV_MU_K, _PV_MU_V, _PV_W0, _PV_A0, _PV_KK, _PV_KA, _PV_RK, _PV_LNW, _PV_LNB = range(10)
_PV_ROWS = 16


def _head_half_mask(shape):
    return lax.broadcasted_iota(jnp.int32, shape, 1) < (LANES // 2)


def _head_sums(x):
    h0 = _head_half_mask((x.shape[0], LANES))
    parts = []
    for p in range(x.shape[1] // LANES):
        xs = x[:, p * LANES:(p + 1) * LANES]
        s0 = jnp.sum(jnp.where(h0, xs, 0.0), axis=-1, keepdims=True)
        s1 = jnp.sum(jnp.where(h0, 0.0, xs), axis=-1, keepdims=True)
        parts.append(jnp.where(h0, s0, s1))
    return parts[0] if len(parts) == 1 else jnp.concatenate(parts, axis=1)


def _rwkv_tokens(zr, zk, zv, zl, pr, pk, pv_, plr, pvec, mul, wd, wa, wg):
    row = lambda i: pvec[i:i + 1, :]
    r = zr + row(_PV_MU_R) * (pr - zr)
    k = zk + row(_PV_MU_K) * (pk - zk)
    v = zv + row(_PV_MU_V) * (pv_ - zv)
    ls = zl + mul * (plr - zl)
    wl = _dot(_bf(jnp.tanh(ls)), wd)
    al = _dot(_bf(ls), wa)
    g = _dot(_bf(jax.nn.sigmoid(ls)), wg)
    w = -jax.nn.softplus(-(row(_PV_W0) + wl)) - 0.5
    logd = -jnp.exp(w)
    a = jax.nn.sigmoid(row(_PV_A0) + al)
    kkr = k * row(_PV_KK)
    nrm = jnp.sqrt(_head_sums(kkr * kkr))
    kk = kkr / jnp.maximum(nrm, 1e-12)
    kp = k * (1.0 + (a - 1.0) * row(_PV_KA))
    bonus = _head_sums(r * kp * row(_PV_RK)) * v
    return r, kp, v, kk, a, logd, g, bonus


def _rwkv_finish(o, bonus, g, pvec):
    inv_n = 1.0 / (LANES // 2)
    mu = _head_sums(o) * inv_n
    d = o - mu
    var = _head_sums(d * d) * inv_n
    on = d * lax.rsqrt(var + RWKV_GN_EPS)
    return (on * pvec[_PV_LNW:_PV_LNW + 1, :] + pvec[_PV_LNB:_PV_LNB + 1, :] + bonus) * g


def _shift_rows(z, first_row):
    rolled = pltpu.roll(z, 1, 0)
    rowid = lax.broadcasted_iota(jnp.int32, z.shape, 0)
    return jnp.where(rowid == 0, first_row, rolled)


def _rwkv_seq_kernel(zr_ref, zk_ref, zv_ref, zl_ref, fr_ref, fk_ref, fv_ref, fl_ref,
                     pvec_ref, mul_ref, wd_ref, wa_ref, wg_ref, tri_ref,
                     ya_ref, wkv_ref, s_ref, cr_ref, ck_ref, cv_ref, cl_ref):
    i = pl.program_id(2)
    tc, width = zr_ref.shape
    npp = width // LANES
    c = RWKV_CHUNK
    nch = tc // c
    half = LANES // 2

    @pl.when(i == 0)
    def _():
        s_ref[...] = jnp.zeros_like(s_ref)
        cr_ref[0:1, :] = fr_ref[0]
        ck_ref[0:1, :] = fk_ref[0]
        cv_ref[0:1, :] = fv_ref[0]
        cl_ref[0:1, :] = fl_ref[0]

    zr, zk, zv, zl = zr_ref[...], zk_ref[...], zv_ref[...], zl_ref[...]
    pvec = pvec_ref[...]
    r, kp, v, kk, a, logd, g, bonus = _rwkv_tokens(
        zr, zk, zv, zl,
        _shift_rows(zr, cr_ref[0:1, :]), _shift_rows(zk, ck_ref[0:1, :]),
        _shift_rows(zv, cv_ref[0:1, :]), _shift_rows(zl, cl_ref[0:1, :]),
        pvec, mul_ref[...], wd_ref[...], wa_ref[...], wg_ref[...])
    cr_ref[0:1, :] = zr[tc - 1:tc, :]
    ck_ref[0:1, :] = zk[tc - 1:tc, :]
    cv_ref[0:1, :] = zv[tc - 1:tc, :]
    cl_ref[0:1, :] = zl[tc - 1:tc, :]

    alpha = -kk
    beta = kk * a
    cum_incl = _dot_exact_lhs(tri_ref[...], logd)
    cum_excl = cum_incl - logd
    tot_rows = [cum_incl[(ci + 1) * c - 1:(ci + 1) * c, :] for ci in range(nch)]
    tot = jnp.concatenate([jnp.broadcast_to(t, (c, width)) for t in tot_rows], axis=0)
    e_neg = jnp.exp(-cum_incl)
    e_hat = jnp.exp(tot - cum_incl)
    r_t = r * jnp.exp(cum_incl)
    a_t = alpha * jnp.exp(cum_excl)
    b_t = beta * e_neg
    k_t = kp * e_neg
    b_h = beta * e_hat
    k_h = kp * e_hat

    h0 = _head_half_mask((c, LANES))

    def stack(x, p, ci):
        xb = x[ci * c:(ci + 1) * c, p * LANES:(p + 1) * LANES]
        return jnp.concatenate([jnp.where(h0, xb, 0.0), jnp.where(h0, 0.0, xb)], axis=0)

    rr = lax.broadcasted_iota(jnp.int32, (2 * c, 2 * c), 0)
    cc = lax.broadcasted_iota(jnp.int32, (2 * c, 2 * c), 1)
    same = (rr >= c) == (cc >= c)
    strict = same & (cc < rr)
    incl = same & (cc <= rr)
    eye = rr == cc
    zeros_blk = jnp.zeros((2 * c, LANES), BF16)

    probs = [(p, ci) for p in range(npp) for ci in range(nch)]
    a_s = {q: stack(a_t, *q) for q in probs}
    r_s = {q: stack(r_t, *q) for q in probs}
    v_bf = {q: _bf(stack(v, *q)) for q in probs}

    pmat, a_ak, a_r = {}, {}, {}
    for q in probs:
        amat = _dot_nt(_bf(jnp.concatenate([a_s[q], r_s[q]], axis=0)),
                       _bf(jnp.concatenate([stack(b_t, *q), stack(k_t, *q)], axis=0)))
        pmat[q] = jnp.where(strict, amat[:2 * c, :2 * c], 0.0)
        a_ak[q] = _bf(jnp.where(strict, amat[:2 * c, 2 * c:], 0.0))
        a_r[q] = _bf(jnp.concatenate([jnp.where(incl, amat[2 * c:, :2 * c], 0.0),
                                      jnp.where(incl, amat[2 * c:, 2 * c:], 0.0)], axis=1))

    x = {q: jnp.concatenate([a_s[q], _dot(a_ak[q], v_bf[q])], axis=1) for q in probs}

    nsteps = int(math.log2(c))
    for it in range(nsteps):
        for q in probs:
            p_bf = _bf(pmat[q])
            x[q] = x[q] + _dot(p_bf, _bf(x[q]))
            if it + 1 < nsteps:
                pmat[q] = _dot(p_bf, p_bf)

    r_pair, o_pair, g_t, h_t = {}, {}, {}, {}
    for q in probs:
        p, ci = q
        x_bf = _bf(x[q])
        rhs = jnp.concatenate([x_bf, jnp.concatenate([zeros_blk, v_bf[q]], axis=1)], axis=0)
        y = _dot(a_r[q], rhs)
        r_hat = r_s[q] + y[:, :LANES]
        r_pair[q] = r_hat[:c] + r_hat[c:]
        o_pair[q] = y[:c, LANES:] + y[c:, LANES:]
        z = _dot_tn(x_bf, _bf(stack(b_h, *q)))
        w_c = jnp.exp(tot_rows[ci][:, p * LANES:(p + 1) * LANES])
        g_t[q] = jnp.where(eye, w_c, 0.0) + z[:LANES]
        h_t[q] = z[LANES:] + _dot_tn(v_bf[q], _bf(stack(k_h, *q)))

    outs = {}
    for ci in range(nch):
        for p in range(npp):
            q = (p, ci)
            s0 = s_ref[p]
            outs[q] = _dot3_nt(r_pair[q], s0) + o_pair[q]
            s_ref[p] = _dot3_nn(s0, g_t[q]) + h_t[q]

    cols = [jnp.concatenate([outs[(p, ci)] for ci in range(nch)], axis=0) if nch > 1 else outs[(p, 0)]
            for p in range(npp)]
    o = jnp.concatenate(cols, axis=1) if npp > 1 else cols[0]
    ya_ref[...] = _bf(_rwkv_finish(o, bonus, g, pvec))

    @pl.when(i == pl.num_programs(2) - 1)
    def _():
        for p in range(npp):
            s = s_ref[p]
            wkv_ref[0, 2 * p] = s[:half, :half]
            wkv_ref[0, 2 * p + 1] = s[half:, half:]


def _rwkv_seq(proj, nb, seq, lay, first, pvec, mul, wd, wa, wg):
    rw, lslot = lay["rw"], lay["lslot"]
    npair = rw // LANES
    npp = RWKV_PAIRS_PER_STEP if npair % RWKV_PAIRS_PER_STEP == 0 else 1
    ngrp = npair // npp
    width = npp * LANES
    c = RWKV_CHUNK
    tc = _pick_tile(seq, (2 * c, c))
    nt = seq // tc
    t_idx = jnp.arange(tc)
    tri = ((t_idx[:, None] // c == t_idx[None, :] // c) & (t_idx[None, :] <= t_idx[:, None])).astype(BF16)
    fr, fk, fv, fl = first
    col = lambda s: pl.BlockSpec((tc, width), lambda b, gp, i, s=s: (b * nt + i, s * ngrp + gp))
    fcol = lambda s: pl.BlockSpec((1, 1, width), lambda b, gp, i, s=s: (b, 0, s * ngrp + gp))
    wspec = pl.BlockSpec((lslot, width), lambda b, gp, i: (0, gp))
    return pl.pallas_call(
        _rwkv_seq_kernel,
        grid=(nb, ngrp, nt),
        in_specs=[col(0), col(1), col(2),
                  pl.BlockSpec((tc, lslot), lambda b, gp, i: (b * nt + i, lay["off_l"] // lslot)),
                  fcol(0), fcol(1), fcol(2),
                  pl.BlockSpec((1, 1, lslot), lambda b, gp, i: (b, 0, 0)),
                  pl.BlockSpec((_PV_ROWS, width), lambda b, gp, i: (0, gp)),
                  pl.BlockSpec((1, lslot), lambda b, gp, i: (0, 0)),
                  wspec, wspec, wspec,
                  pl.BlockSpec((tc, tc), lambda b, gp, i: (0, 0))],
        out_specs=[pl.BlockSpec((tc, width), lambda b, gp, i: (b * nt + i, gp)),
                   pl.BlockSpec((1, 2 * npp, LANES // 2, LANES // 2), lambda b, gp, i: (b, gp, 0, 0))],
        out_shape=[jax.ShapeDtypeStruct((nb * seq, rw), BF16),
                   jax.ShapeDtypeStruct((nb, 2 * npair, LANES // 2, LANES // 2), F32)],
        scratch_shapes=[pltpu.VMEM((npp, LANES, LANES), F32),
                        pltpu.VMEM((SUBLANES, width), F32), pltpu.VMEM((SUBLANES, width), F32),
                        pltpu.VMEM((SUBLANES, width), F32), pltpu.VMEM((SUBLANES, lslot), F32)],
        compiler_params=_params("arbitrary", "arbitrary", "arbitrary"),
    )(proj, proj, proj, proj, fr, fk, fv, fl, pvec, mul, wd, wa, wg, tri)


_STEP_GROUP = 8


def _rwkv_step_kernel(zr_ref, zk_ref, zv_ref, zl_ref, pr_ref, pk_ref, pv_ref, plr_ref,
                      pvec_ref, mul_ref, wd_ref, wa_ref, wg_ref, s_ref,
                      ya_ref, snew_ref, sa_s, sw_s):
    bb = zr_ref.shape[0]
    half = LANES // 2
    pvec = pvec_ref[...]
    r, kp, v, kk, a, logd, g, bonus = _rwkv_tokens(
        zr_ref[...], zk_ref[...], zv_ref[...], zl_ref[...],
        pr_ref[...], pk_ref[...], pv_ref[...], plr_ref[...],
        pvec, mul_ref[...], wd_ref[...], wa_ref[...], wg_ref[...])
    w = jnp.exp(logd)
    beta = kk * a
    nkk = -kk
    wr = w * r
    eye = (lax.broadcasted_iota(jnp.int32, (half, half), 0)
           == lax.broadcasted_iota(jnp.int32, (half, half), 1))
    to_row = lambda col: jnp.sum(jnp.where(eye, col, 0.0), axis=0, keepdims=True)

    for g0 in range(0, bb, _STEP_GROUP):
        probs = [(b, e) for b in range(g0, min(g0 + _STEP_GROUP, bb)) for e in range(2)]
        row = lambda x, q: x[q[0]:q[0] + 1, q[1] * half:(q[1] + 1) * half]
        sa, sw, v_col = {}, {}, {}
        for q in probs:
            s = s_ref[q[0], q[1]]
            sa[q] = jnp.sum(s * row(nkk, q), axis=-1, keepdims=True)
            sw[q] = jnp.sum(s * row(wr, q), axis=-1, keepdims=True)
            v_col[q] = jnp.sum(jnp.where(eye, row(v, q), 0.0), axis=-1, keepdims=True)
        for q in probs:
            b, e = q
            s = s_ref[b, e]
            snew_ref[b, e] = s * row(w, q) + sa[q] * row(beta, q) + v_col[q] * row(kp, q)
            sa_s[b:b + 1, e * half:(e + 1) * half] = to_row(sa[q])
            sw_s[b:b + 1, e * half:(e + 1) * half] = to_row(sw[q])

    o = sw_s[...] + sa_s[...] * _head_sums(beta * r) + v * _head_sums(kp * r)
    ya_ref[...] = _bf(_rwkv_finish(o, bonus, g, pvec))


def _rwkv_step(proj, nb, lay, prev, pvec, mul, wd, wa, wg, s_wkv):
    rw, lslot = lay["rw"], lay["lslot"]
    npair = rw // LANES
    bb = _pick_tile(nb, (32, 16, 8))
    pm, plr = prev
    half = LANES // 2
    col = lambda off: pl.BlockSpec((bb, LANES), lambda p, j, off=off: (j, off + p))
    wspec = pl.BlockSpec((lslot, LANES), lambda p, j: (0, p))
    sspec = pl.BlockSpec((bb, 2, half, half), lambda p, j: (j, p, 0, 0))
    return pl.pallas_call(
        _rwkv_step_kernel,
        grid=(npair, nb // bb),
        in_specs=[col(0), col(npair), col(2 * npair),
                  pl.BlockSpec((bb, lslot), lambda p, j: (j, lay["off_l"] // lslot)),
                  col(0), col(npair), col(2 * npair),
                  pl.BlockSpec((bb, lslot), lambda p, j: (j, 0)),
                  pl.BlockSpec((_PV_ROWS, LANES), lambda p, j: (0, p)),
                  pl.BlockSpec((1, lslot), lambda p, j: (0, 0)),
                  wspec, wspec, wspec, sspec],
        out_specs=[pl.BlockSpec((bb, LANES), lambda p, j: (j, p)), sspec],
        out_shape=[jax.ShapeDtypeStruct((nb, rw), BF16),
                   jax.ShapeDtypeStruct(s_wkv.shape, F32)],
        scratch_shapes=[pltpu.VMEM((bb, LANES), F32), pltpu.VMEM((bb, LANES), F32)],
        compiler_params=_params("arbitrary", "arbitrary"),
    )(proj, proj, proj, proj, pm, pm, pm, plr, pvec, mul, wd, wa, wg, s_wkv)


def _rope_rows(t, cos2, sin2):
    return t * cos2 + pltpu.roll(t, LANES // 2, 1) * sin2


def _head_norm_rows(o, eps):
    mu = jnp.mean(o, axis=-1, keepdims=True)
    d = o - mu
    var = jnp.mean(d * d, axis=-1, keepdims=True)
    return d * lax.rsqrt(var + eps)


def _ret_seq_kernel(q_ref, k_ref, v_ref, g_ref, cos_ref, sin_ref, intra_ref, qd_ref, kd_ref,
                    blk_ref, gnw_ref, gnb_ref, yb_ref, ret_ref, s_ref):
    i = pl.program_id(1)
    nh = s_ref.shape[0]
    dk = s_ref.shape[1]

    @pl.when(i == 0)
    def _():
        s_ref[...] = jnp.zeros_like(s_ref)

    cos2, sin2 = cos_ref[...], sin_ref[...]
    for h in range(nh):
        hs = slice(h * LANES, (h + 1) * LANES)
        qh = _rope_rows(q_ref[:, hs], cos2, sin2)
        kh = _rope_rows(k_ref[:, hs], cos2, sin2) * (dk ** -0.5)
        vb = _bf(v_ref[:, hs])
        qb = _bf(qh)
        scores = _dot_nt(qb, _bf(kh)) * intra_ref[h]
        s0 = s_ref[h]
        o = _dot(_bf(scores), vb) + _dot(qb, _bf(s0)) * qd_ref[h]
        s_ref[h] = s0 * blk_ref[h] + _dot_tn(_bf(kh * kd_ref[h]), vb)
        on = _head_norm_rows(o, RET_GN_EPS)
        yb_ref[:, hs] = _bf((on * gnw_ref[:, hs] + gnb_ref[:, hs]) * jax.nn.silu(g_ref[:, hs]))

    @pl.when(i == pl.num_programs(1) - 1)
    def _():
        ret_ref[0] = s_ref[...]


def _ret_tables(nh, c):
    log_g = jnp.log1p(-jnp.exp2(-5.0 - jnp.arange(nh, dtype=F32)))
    i = jnp.arange(c, dtype=F32)
    rel = i[:, None] - i[None, :]
    intra = jnp.where(rel >= 0, jnp.exp(log_g[:, None, None] * jnp.maximum(rel, 0.0)), 0.0)
    q_decay = jnp.exp(log_g[:, None] * (i + 1.0))
    k_decay = jnp.exp(log_g[:, None] * (c - 1.0 - i))
    blk_decay = jnp.exp(log_g * c)
    return intra, q_decay, k_decay, blk_decay


def _rope_tables(pos, dk):
    half = dk // 2
    inv = ROPE_BASE ** (-jnp.arange(half, dtype=F32) / half)
    ang = pos[:, None] * inv[None, :]
    cos, sin = jnp.cos(ang), jnp.sin(ang)
    return jnp.concatenate([cos, cos], axis=-1), jnp.concatenate([-sin, sin], axis=-1)


def _ret_seq(proj, nb, seq, lay, gnw, gnb):
    qk, rv, nh = lay["qk"], lay["rv"], lay["ret_heads"]
    dk, dv = qk // nh, rv // nh
    assert dk == LANES and dv == LANES
    c = RET_CHUNK if seq % RET_CHUNK == 0 else seq
    assert c % SUBLANES == 0
    nt = seq // c
    intra, qd, kd, blk = _ret_tables(nh, c)
    qd = jnp.broadcast_to(qd[:, :, None], (nh, c, dv))
    kd = jnp.broadcast_to(kd[:, :, None], (nh, c, dk))
    blk = jnp.broadcast_to(blk[:, None, None], (nh, 1, dv))
    cos2, sin2 = _rope_tables(jnp.arange(seq, dtype=F32), dk)
    seg = lambda off, w: pl.BlockSpec((c, w), lambda b, i, off=off, w=w: (b * nt + i, off // w))
    full3 = lambda a: pl.BlockSpec(a.shape, lambda b, i: (0, 0, 0))
    return pl.pallas_call(
        _ret_seq_kernel,
        grid=(nb, nt),
        in_specs=[seg(lay["off_q"], qk), seg(lay["off_kr"], qk), seg(lay["off_vr"], rv),
                  seg(lay["off_rg"], rv),
                  pl.BlockSpec((c, dk), lambda b, i: (i, 0)), pl.BlockSpec((c, dk), lambda b, i: (i, 0)),
                  full3(intra), full3(qd), full3(kd), full3(blk),
                  pl.BlockSpec((1, rv), lambda b, i: (0, 0)), pl.BlockSpec((1, rv), lambda b, i: (0, 0))],
        out_specs=[pl.BlockSpec((c, rv), lambda b, i: (b * nt + i, 0)),
                   pl.BlockSpec((1, nh, dk, dv), lambda b, i: (b, 0, 0, 0))],
        out_shape=[jax.ShapeDtypeStruct((nb * seq, rv), BF16),
                   jax.ShapeDtypeStruct((nb, nh, dk, dv), F32)],
        scratch_shapes=[pltpu.VMEM((nh, dk, dv), F32)],
        compiler_params=_params("arbitrary", "arbitrary"),
    )(proj, proj, proj, proj, cos2, sin2, intra, qd, kd, blk, gnw, gnb)


_RC_INTRA, _RC_QD, _RC_KD, _RC_BLK = range(4)


def _ret_step_kernel(q_ref, k_ref, v_ref, g_ref, cos_ref, sin_ref, rc_ref, gnw_ref, gnb_ref, s_ref,
                     yb_ref, snew_ref, q_s, k_s, o_s):
    bb = q_ref.shape[0]
    nh = s_ref.shape[1]
    dk = s_ref.shape[2]
    cos2, sin2 = cos_ref[...], sin_ref[...]
    for h in range(nh):
        hs = slice(h * LANES, (h + 1) * LANES)
        q_s[:, hs] = _rope_rows(q_ref[:, hs], cos2, sin2)
        k_s[:, hs] = _rope_rows(k_ref[:, hs], cos2, sin2) * (dk ** -0.5)
    eye = (lax.broadcasted_iota(jnp.int32, (LANES, LANES), 0)
           == lax.broadcasted_iota(jnp.int32, (LANES, LANES), 1))
    for b in range(bb):
        for h in range(nh):
            hs = slice(h * LANES, (h + 1) * LANES)
            rc = lambda j: rc_ref[h, j:j + 1, :]
            q_row = q_s[b:b + 1, hs]
            k_row = k_s[b:b + 1, hs]
            v_row = v_ref[b:b + 1, hs]
            s0 = s_ref[b, h]
            q_col = jnp.sum(jnp.where(eye, q_row, 0.0), axis=-1, keepdims=True)
            k_col = jnp.sum(jnp.where(eye, k_row, 0.0), axis=-1, keepdims=True)
            score = jnp.sum(q_row * k_row, axis=-1, keepdims=True) * rc(_RC_INTRA)
            o_row = score * v_row + jnp.sum(s0 * q_col, axis=0, keepdims=True) * rc(_RC_QD)
            snew_ref[b, h] = s0 * rc(_RC_BLK) + (k_col * rc(_RC_KD)) * v_row
            o_s[b:b + 1, hs] = o_row
    for h in range(nh):
        hs = slice(h * LANES, (h + 1) * LANES)
        on = _head_norm_rows(o_s[:, hs], RET_GN_EPS)
        yb_ref[:, hs] = _bf((on * gnw_ref[:, hs] + gnb_ref[:, hs]) * jax.nn.silu(g_ref[:, hs]))


def _ret_step(proj, nb, lay, gnw, gnb, s_ret, pos0):
    qk, rv, nh = lay["qk"], lay["rv"], lay["ret_heads"]
    dk, dv = qk // nh, rv // nh
    assert dk == LANES and dv == LANES
    bb = SUBLANES
    intra, qd, kd, blk = _ret_tables(nh, 1)
    rc = jnp.stack([intra[:, 0, 0], qd[:, 0], kd[:, 0], blk], axis=1)
    rc = jnp.pad(rc, ((0, 0), (0, SUBLANES - 4)))
    rc = jnp.broadcast_to(rc[:, :, None], (nh, SUBLANES, LANES))
    cos2, sin2 = _rope_tables(jnp.asarray([pos0], dtype=F32), dk)
    seg = lambda off, w: pl.BlockSpec((bb, w), lambda j, off=off, w=w: (j, off // w))
    sspec = pl.BlockSpec((bb, nh, dk, dv), lambda j: (j, 0, 0, 0))
    row = lambda w: pl.BlockSpec((1, w), lambda j: (0, 0))
    return pl.pallas_call(
        _ret_step_kernel,
        grid=(nb // bb,),
        in_specs=[seg(lay["off_q"], qk), seg(lay["off_kr"], qk), seg(lay["off_vr"], rv),
                  seg(lay["off_rg"], rv), row(dk), row(dk),
                  pl.BlockSpec(rc.shape, lambda j: (0, 0, 0)), row(rv), row(rv), sspec],
        out_specs=[pl.BlockSpec((bb, rv), lambda j: (j, 0)), sspec],
        out_shape=[jax.ShapeDtypeStruct((nb, rv), BF16), jax.ShapeDtypeStruct(s_ret.shape, F32)],
        scratch_shapes=[pltpu.VMEM((bb, qk), F32), pltpu.VMEM((bb, qk), F32), pltpu.VMEM((bb, rv), F32)],
        compiler_params=_params("arbitrary"),
    )(proj, proj, proj, proj, cos2, sin2, rc, gnw, gnb, s_ret)


def _merge_kernel(alpha, ng, *refs):
    ya_ref, yb_ref = refs[0], refs[1]
    ga_refs = refs[2:2 + ng]
    gb_refs = refs[2 + ng:2 + 2 * ng]
    (x_ref, g1_ref, sc2_ref, sh2_ref, wa_ref, wb_ref, wo_ref, lnw_ref, lnb_ref,
     x1_ref, u2_ref) = refs[2 + 2 * ng:]
    cat = lambda rs: jnp.concatenate([r[...] for r in rs], axis=1) if ng > 1 else rs[0][...]
    merged = (jax.nn.sigmoid(cat(ga_refs)) * _dot(ya_ref[...], wa_ref[...])
              + jax.nn.sigmoid(cat(gb_refs)) * _dot(yb_ref[...], wb_ref[...]))
    t = alpha * x_ref[...] + g1_ref[0] * _dot(_bf(merged), wo_ref[...])
    x1 = _layer_norm_rows(t, lnw_ref[...], lnb_ref[...])
    x1_ref[...] = x1
    u2_ref[...] = _bf(x1 * (1.0 + sc2_ref[0]) + sh2_ref[0])


def _merge(ya, yb, proj, x, g1, sc2, sh2, wba, wbb, wout, lnw, lnb, lay, alpha, tm, tpg):
    m, d = x.shape
    rw, rv = ya.shape[1], yb.shape[1]
    gw = math.gcd(lay["off_ga"], d)
    ng = d // gw
    r = g1.shape[1]
    rowt = lambda w: pl.BlockSpec((tm, w), lambda i: (i, 0))
    gate = lambda off, q: pl.BlockSpec((tm, gw), lambda i, off=off, q=q: (i, off // gw + q))
    mod = pl.BlockSpec((1, r, d), lambda i: (i // tpg, 0, 0))
    const = lambda a: pl.BlockSpec(a.shape, lambda i: (0, 0), pipeline_mode=pl.Buffered(1))
    return pl.pallas_call(
        functools.partial(_merge_kernel, alpha, ng),
        grid=(m // tm,),
        in_specs=[rowt(rw), rowt(rv)]
                 + [gate(lay["off_ga"], q) for q in range(ng)]
                 + [gate(lay["off_gb"], q) for q in range(ng)]
                 + [rowt(d), mod, mod, mod, const(wba), const(wbb), const(wout), const(lnw), const(lnb)],
        out_specs=[rowt(d), rowt(d)],
        out_shape=[jax.ShapeDtypeStruct((m, d), F32), jax.ShapeDtypeStruct((m, d), BF16)],
        compiler_params=_params("arbitrary"),
    )(ya, yb, *([proj] * (2 * ng)), x, g1, sc2, sh2, wba, wbb, wout, lnw, lnb)


def _ffn_up_seq_kernel(tps, u_ref, wa_ref, wb_ref, cwa_ref, cwb_ref, ia_ref, ib_ref,
                       act_ref, ta_ref, tb_ref, ha_s, hb_s, ca_s, cb_s):
    i, j = pl.program_id(0), pl.program_id(1)
    tm = u_ref.shape[0]
    tn = act_ref.shape[1]
    u = u_ref[...]
    first = i % tps == 0
    last = i % tps == tps - 1
    for c0 in range(0, tn, MXU_WIDTH):
        cs = slice(c0, min(c0 + MXU_WIDTH, tn))
        halves = []
        for w_ref, cw_ref, init_ref, tail_ref, h_s, c_s in (
                (wa_ref, cwa_ref, ia_ref, ta_ref, ha_s, ca_s),
                (wb_ref, cwb_ref, ib_ref, tb_ref, hb_s, cb_s)):
            h = _dot(u, w_ref[:, cs])

            @pl.when(first)
            def _():
                h_s[0:SUBLANES, cs] = init_ref[0, :, cs]

            @pl.when(jnp.logical_not(first))
            def _():
                h_s[0:SUBLANES, cs] = c_s[j, :, cs]

            h_s[SUBLANES:SUBLANES + tm, cs] = h
            tail = h[tm - SUBLANES:tm, :]
            c_s[j, :, cs] = tail

            @pl.when(last)
            def _():
                tail_ref[i // tps, j, :, cs] = tail

            cw = cw_ref[:, cs]
            halves.append(cw[3:4, :] + cw[0:1, :] * h_s[SUBLANES - 2:SUBLANES - 2 + tm, cs]
                          + cw[1:2, :] * h_s[SUBLANES - 1:SUBLANES - 1 + tm, cs] + cw[2:3, :] * h)
        act_ref[:, cs] = _bf(jax.nn.silu(halves[0]) * halves[1])


def _ffn_up_step_kernel(u_ref, wa_ref, wb_ref, cwa_ref, cwb_ref, p1a_ref, p2a_ref, p1b_ref, p2b_ref,
                        act_ref, ha_ref, hb_ref):
    u = u_ref[...]
    halves = []
    for w_ref, cw_ref, p1_ref, p2_ref, h_ref in ((wa_ref, cwa_ref, p1a_ref, p2a_ref, ha_ref),
                                                 (wb_ref, cwb_ref, p1b_ref, p2b_ref, hb_ref)):
        h = _dot(u, w_ref[...])
        h_ref[...] = h
        cw = cw_ref[...]
        halves.append(cw[3:4, :] + cw[0:1, :] * p2_ref[...] + cw[1:2, :] * p1_ref[...] + cw[2:3, :] * h)
    act_ref[...] = _bf(jax.nn.silu(halves[0]) * halves[1])


def _ffn_up(u2, w_up_bf, cwt, nb, seq, s_conv, tm):
    m, d = u2.shape
    f2 = w_up_bf.shape[1]
    f = f2 // 2
    tn = _pick_tile(f, (512, 256, 128))
    nj = f // tn
    w_a = pl.BlockSpec((d, tn), lambda i, j: (0, j))
    w_b = pl.BlockSpec((d, tn), lambda i, j: (0, nj + j))
    cw_a = pl.BlockSpec((4, tn), lambda i, j: (0, j))
    cw_b = pl.BlockSpec((4, tn), lambda i, j: (0, nj + j))
    if seq == 1:
        p_a = pl.BlockSpec((tm, tn), lambda i, j: (i, j))
        p_b = pl.BlockSpec((tm, tn), lambda i, j: (i, nj + j))
        prev1, prev2 = s_conv[:, 1, :], s_conv[:, 0, :]
        act, h_a, h_b = pl.pallas_call(
            _ffn_up_step_kernel,
            grid=(m // tm, nj),
            in_specs=[pl.BlockSpec((tm, d), lambda i, j: (i, 0)), w_a, w_b, cw_a, cw_b, p_a, p_a, p_b, p_b],
            out_specs=[pl.BlockSpec((tm, tn), lambda i, j: (i, j))] * 3,
            out_shape=[jax.ShapeDtypeStruct((m, f), BF16), jax.ShapeDtypeStruct((m, f), F32),
                       jax.ShapeDtypeStruct((m, f), F32)],
            compiler_params=_params("arbitrary", "arbitrary"),
        )(u2, w_up_bf, w_up_bf, cwt, cwt, prev1, prev2, prev1, prev2)
        conv_new = jnp.stack([prev1, jnp.concatenate([h_a, h_b], axis=-1)], axis=1)
        return act, conv_new
    assert seq % tm == 0 and tm >= SUBLANES
    tps = seq // tm
    init = jnp.pad(s_conv, ((0, 0), (SUBLANES - 2, 0), (0, 0)))
    i_a = pl.BlockSpec((1, SUBLANES, tn), lambda i, j: (i // tps, 0, j))
    i_b = pl.BlockSpec((1, SUBLANES, tn), lambda i, j: (i // tps, 0, nj + j))
    t_o = pl.BlockSpec((nb, nj, SUBLANES, tn), lambda i, j: (0, 0, 0, 0))
    tails = jax.ShapeDtypeStruct((nb, nj, SUBLANES, tn), F32)
    act, t_a, t_b = pl.pallas_call(
        functools.partial(_ffn_up_seq_kernel, tps),
        grid=(m // tm, nj),
        in_specs=[pl.BlockSpec((tm, d), lambda i, j: (i, 0)), w_a, w_b, cw_a, cw_b, i_a, i_b],
        out_specs=[pl.BlockSpec((tm, tn), lambda i, j: (i, j)), t_o, t_o],
        out_shape=[jax.ShapeDtypeStruct((m, f), BF16), tails, tails],
        scratch_shapes=[pltpu.VMEM((tm + SUBLANES, tn), F32), pltpu.VMEM((tm + SUBLANES, tn), F32),
                        pltpu.VMEM((nj, SUBLANES, tn), F32), pltpu.VMEM((nj, SUBLANES, tn), F32)],
        compiler_params=_params("arbitrary", "arbitrary"),
    )(u2, w_up_bf, w_up_bf, cwt, cwt, init, init)
    rows = lambda t: t[:, :, SUBLANES - 2:, :].transpose(0, 2, 1, 3).reshape(nb, 2, f)
    conv_new = jnp.concatenate([rows(t_a), rows(t_b)], axis=-1)
    return act, conv_new


def _ffn_down_kernel(alpha, act_ref, w_ref, x1_ref, g2_ref, lnw_ref, lnb_ref, o_ref, acc_ref):
    k = pl.program_id(1)

    @pl.when(k == 0)
    def _():
        acc_ref[...] = jnp.zeros_like(acc_ref)

    acc_ref[...] += _dot(act_ref[...], w_ref[...])

    @pl.when(k == pl.num_programs(1) - 1)
    def _():
        t = alpha * x1_ref[...] + g2_ref[0] * acc_ref[...]
        o_ref[...] = _layer_norm_rows(t, lnw_ref[...], lnb_ref[...])


def _ffn_down(act, w_down_bf, x1, g2, lnw, lnb, alpha, tm, tpg):
    m, f = act.shape
    d = w_down_bf.shape[1]
    tk = _pick_tile(f, (512, 256, 128))
    r = g2.shape[1]
    return pl.pallas_call(
        functools.partial(_ffn_down_kernel, alpha),
        grid=(m // tm, f // tk),
        in_specs=[pl.BlockSpec((tm, tk), lambda i, k: (i, k)),
                  pl.BlockSpec((tk, d), lambda i, k: (k, 0)),
                  pl.BlockSpec((tm, d), lambda i, k: (i, 0)),
                  pl.BlockSpec((1, r, d), lambda i, k: (i // tpg, 0, 0)),
                  pl.BlockSpec((1, d), lambda i, k: (0, 0)),
                  pl.BlockSpec((1, d), lambda i, k: (0, 0))],
        out_specs=pl.BlockSpec((tm, d), lambda i, k: (i, 0)),
        out_shape=jax.ShapeDtypeStruct((m, d), F32),
        scratch_shapes=[pltpu.VMEM((tm, d), F32)],
        compiler_params=_params("arbitrary", "arbitrary"),
    )(act, w_down_bf, x1, g2, lnw, lnb)


def _layout(d, rw, qk, rv, lora_w, ret_heads):
    lslot = next(s for s in (128, 256, 512, 1024, 2048) if s >= lora_w)
    off_q = 3 * rw
    off_kr = off_q + qk
    off_vr = off_kr + qk
    off_rg = off_vr + rv
    off_ga = off_rg + rv
    off_gb = off_ga + d
    off_l = off_gb + d
    assert rw % LANES == 0 and off_l % lslot == 0
    assert off_q % qk == 0 and off_kr % qk == 0 and off_vr % rv == 0 and off_rg % rv == 0
    return dict(d=d, rw=rw, qk=qk, rv=rv, lslot=lslot, lora_w=lora_w, ret_heads=ret_heads,
                off_q=off_q, off_kr=off_kr, off_vr=off_vr, off_rg=off_rg, off_ga=off_ga,
                off_gb=off_gb, off_l=off_l, nt=off_l + lslot)


def _prep_weights(lay, w_in, shift_mu, w0, w_decay_up, a0, w_aaa_up, w_gate_up, k_k, k_a, r_k,
                  lnx_w, lnx_b):
    rw, lslot, lora_w = lay["rw"], lay["lslot"], lay["lora_w"]
    p_rwkv = 3 * rw + lora_w
    d = w_in.shape[0]
    w_perm = jnp.concatenate(
        [w_in[:, :3 * rw], w_in[:, p_rwkv:], w_in[:, 3 * rw:p_rwkv],
         jnp.zeros((d, lslot - lora_w), w_in.dtype)], axis=1).astype(BF16)
    dl, al, gl = w_decay_up.shape[0], w_aaa_up.shape[0], w_gate_up.shape[0]
    pad_rows = lambda w, lo: jnp.pad(w, ((lo, lslot - lo - w.shape[0]), (0, 0))).astype(BF16)
    wd = pad_rows(w_decay_up, 0)
    wa = pad_rows(w_aaa_up, dl)
    wg = pad_rows(w_gate_up, dl + al)
    rows = [shift_mu[:rw], shift_mu[rw:2 * rw], shift_mu[2 * rw:3 * rw], w0, a0, k_k, k_a,
            r_k.reshape(-1), lnx_w, lnx_b]
    pvec = jnp.pad(jnp.stack(rows, axis=0), ((0, _PV_ROWS - len(rows)), (0, 0)))
    mul = jnp.pad(shift_mu[3 * rw:], (0, lslot - lora_w))[None, :]
    return w_perm, wd, wa, wg, pvec, mul


def _run_layer(x2d, nb, seq, ada, states, pos0, lay, wts, alpha):
    d, rw, lslot, lora_w = lay["d"], lay["rw"], lay["lslot"], lay["lora_w"]
    (w_perm, wd, wa, wg, pvec, mul, gnw, gnb, wba, wbb, wout, ln1w, ln1b,
     w_up, cwt, w_down, ln2w, ln2b) = wts
    m = nb * seq
    sh1, sc1, g1, sh2, sc2, g2 = jnp.split(ada, 6, axis=-1)
    if seq == 1:
        tm, tpg = m, 1
        shape_mod = lambda t: t[None]
    else:
        tm = _pick_tile(seq, (512, 256, 128, 64, 32, 16, 8))
        tpg = seq // tm
        shape_mod = lambda t: t[:, None, :]
    sh1, sc1, g1, sh2, sc2, g2 = map(shape_mod, (sh1, sc1, g1, sh2, sc2, g2))
    tm_in = _pick_tile(seq, (1024, 512, 256, 128, 64, 32, 16, 8)) if seq > 1 else m
    proj = _modmm(x2d, sc1, sh1, w_perm, tm_in, (seq // tm_in) if seq > 1 else 1)

    s_wkv, s_shift, s_ret, s_conv = states
    shift_main = s_shift[:, :3 * rw]
    shift_lora = jnp.pad(s_shift[:, 3 * rw:], ((0, 0), (0, lslot - lora_w)))
    if seq == 1:
        ya, wkv_new = _rwkv_step(proj, nb, lay, (shift_main, shift_lora), pvec, mul, wd, wa, wg, s_wkv)
        yb, ret_new = _ret_step(proj, nb, lay, gnw, gnb, s_ret, pos0)
    else:
        first = (shift_main[:, None, :], shift_main[:, None, :], shift_main[:, None, :],
                 shift_lora[:, None, :])
        ya, wkv_new = _rwkv_seq(proj, nb, seq, lay, first, pvec, mul, wd, wa, wg)
        yb, ret_new = _ret_seq(proj, nb, seq, lay, gnw, gnb)
    last = proj.reshape(nb, seq, lay["nt"])[:, -1, :]
    shift_new = jnp.concatenate([last[:, :3 * rw], last[:, lay["off_l"]:lay["off_l"] + lora_w]], axis=-1)

    tm_merge = min(tm, 256)
    x1, u2 = _merge(ya, yb, proj, x2d, g1, sc2, sh2, wba, wbb, wout, ln1w, ln1b, lay, alpha,
                    tm_merge, (seq // tm_merge) if seq > 1 else 1)
    tm_up = _pick_tile(seq, (1024, 512, 256, 128, 64, 32, 16, 8)) if seq > 1 else m
    act, conv_new = _ffn_up(u2, w_up, cwt, nb, seq, s_conv, tm_up)
    x2 = _ffn_down(act, w_down, x1, g2, ln2w, ln2b, alpha, tm, tpg)
    return x2, wkv_new, shift_new, ret_new, conv_new


def kernel(x_prompt, x_sample, c_prompt, c_sample, state_wkv, state_shift, state_ret, state_conv, w_ada, b_ada, w_in, shift_mu, w0, w_decay_up, a0, w_aaa_up, w_gate_up, k_k, k_a, r_k, lnx_w, lnx_b, ret_gn_w, ret_gn_b, w_branch_a, w_branch_b, w_out, ln1_w, ln1_b, w_up, conv_w, conv_b, w_down, ln2_w, ln2_b):
    depth = w_ada.shape[0]
    nbp, seq_p, d = x_prompt.shape
    nbs, seq_s, _ = x_sample.shape
    assert seq_s == 1
    rw = k_k.shape[-1]
    ret_heads, dk, dv = state_ret.shape[2:]
    lora_w = w_decay_up.shape[1] + w_aaa_up.shape[1] + w_gate_up.shape[1]
    lay = _layout(d, rw, ret_heads * dk, ret_heads * dv, lora_w, ret_heads)
    heads, hn = r_k.shape[1:]
    assert hn == LANES // 2 and heads * hn == rw
    alpha = (2.0 * depth) ** 0.25
    f2 = w_up.shape[-1]

    xp = x_prompt.reshape(nbp * seq_p, d)
    xs = x_sample.reshape(nbs * seq_s, d)
    c_all = jnp.concatenate([c_prompt, c_sample], axis=0)
    pad = (-c_all.shape[0]) % SUBLANES
    c_all = _bf(jnp.pad(c_all, ((0, pad), (0, 0))))

    outs_p, outs_s = [], []
    for l in range(depth):
        w_perm, wd, wa, wg, pvec, mul = _prep_weights(
            lay, w_in[l], shift_mu[l], w0[l], w_decay_up[l], a0[l], w_aaa_up[l], w_gate_up[l],
            k_k[l], k_a[l], r_k[l], lnx_w[l], lnx_b[l])
        cwt = jnp.concatenate([conv_w[l], conv_b[l][None, :]], axis=0)
        wts = (w_perm, wd, wa, wg, pvec, mul, ret_gn_w[l][None, :], ret_gn_b[l][None, :],
               _bf(w_branch_a[l]), _bf(w_branch_b[l]), _bf(w_out[l]), ln1_w[l][None, :], ln1_b[l][None, :],
               _bf(w_up[l]), cwt, _bf(w_down[l]), ln2_w[l][None, :], ln2_b[l][None, :])
        ada = _mm_bias(c_all, _bf(w_ada[l]), b_ada[l][None, :])
        zero_states = (None, jnp.zeros((nbp, state_shift.shape[-1]), F32), None,
                       jnp.zeros((nbp, state_conv.shape[2], f2), F32))
        xp, *st_p = _run_layer(xp, nbp, seq_p, ada[:nbp], zero_states, 0.0, lay, wts, alpha)
        xs, *st_s = _run_layer(xs, nbs, seq_s, ada[nbp:nbp + nbs],
                               (state_wkv[l], state_shift[l], state_ret[l], state_conv[l]),
                               float(PAST_LEN), lay, wts, alpha)
        outs_p.append(st_p)
        outs_s.append(st_s)

    stack = lambda lst, j, ref: jnp.stack([s[j] for s in lst], axis=0).astype(ref.dtype)
    refs = (state_wkv, state_shift, state_ret, state_conv)
    return (xp.reshape(x_prompt.shape), xs.reshape(x_sample.shape),
            *[stack(outs_p, j, refs[j]) for j in range(4)],
            *[stack(outs_s, j, refs[j]) for j in range(4)])
```

```python
import functools
import math

import jax
import jax.numpy as jnp
from jax import lax
from jax.experimental import pallas as pl
from jax.experimental.pallas import tpu as pltpu

F32 = jnp.float32
BF16 = jnp.bfloat16

PAST_LEN = 16384
ROPE_BASE = 10000.0
RWKV_GN_EPS = 64e-5
RET_GN_EPS = 1e-5
LN_EPS = 1e-5
RET_CHUNK = 128

LANES = 128
SUBLANES = 8
MXU_WIDTH = 256
VMEM_LIMIT_BYTES = 56 * 1024 * 1024

RWKV_CHUNK = 64
RWKV_PAIRS_PER_STEP = 4


def _params(*sem):
    return pltpu.CompilerParams(dimension_semantics=sem, vmem_limit_bytes=VMEM_LIMIT_BYTES)


def _dot(a, b):
    return jnp.dot(a, b, preferred_element_type=F32)


def _dot_nt(a, b):
    return lax.dot_general(a, b, (((1,), (1,)), ((), ())), preferred_element_type=F32)


def _dot_tn(a, b):
    return lax.dot_general(a, b, (((0,), (0,)), ((), ())), preferred_element_type=F32)


def _bf(x):
    return x.astype(BF16)


def _split2(x):
    hi = x.astype(BF16)
    lo = (x - hi.astype(F32)).astype(BF16)
    return hi, lo


def _dot3_nn(a, b):
    ah, al = _split2(a)
    bh, bl = _split2(b)
    return _dot(jnp.concatenate([ah, ah, al], axis=1), jnp.concatenate([bh, bl, bh], axis=0))


def _dot3_nt(a, b):
    ah, al = _split2(a)
    bh, bl = _split2(b)
    return _dot_nt(jnp.concatenate([ah, ah, al], axis=1), jnp.concatenate([bh, bl, bh], axis=1))


def _dot_exact_lhs(a_bf, b):
    b1 = b.astype(BF16)
    r1 = b - b1.astype(F32)
    b2 = r1.astype(BF16)
    b3 = (r1 - b2.astype(F32)).astype(BF16)
    return _dot(jnp.concatenate([a_bf, a_bf, a_bf], axis=1), jnp.concatenate([b1, b2, b3], axis=0))


def _layer_norm_rows(t, w, b):
    mu = jnp.mean(t, axis=-1, keepdims=True)
    d = t - mu
    var = jnp.mean(d * d, axis=-1, keepdims=True)
    return d * lax.rsqrt(var + LN_EPS) * w + b


def _pick_tile(n, candidates):
    for c in candidates:
        if n % c == 0:
            return c
    return n


def _mm_bias_kernel(x_ref, w_ref, b_ref, o_ref):
    o_ref[...] = _dot(x_ref[...], _bf(w_ref[...])) + b_ref[...]


def _mm_bias(x_bf, w, b_row):
    m, k = x_bf.shape
    n = w.shape[1]
    tn = _pick_tile(n, (1024, 512, 256, 128))
    return pl.pallas_call(
        _mm_bias_kernel,
        grid=(n // tn,),
        in_specs=[pl.BlockSpec((m, k), lambda j: (0, 0)),
                  pl.BlockSpec((k, tn), lambda j: (0, j)),
                  pl.BlockSpec((1, tn), lambda j: (0, j))],
        out_specs=pl.BlockSpec((m, tn), lambda j: (0, j)),
        out_shape=jax.ShapeDtypeStruct((m, n), F32),
        compiler_params=_params("arbitrary"),
    )(x_bf, w, b_row)


def _modmm_kernel(x_ref, sc_ref, sh_ref, w_ref, o_ref, u_ref):
    @pl.when(pl.program_id(1) == 0)
    def _():
        u_ref[...] = _bf(x_ref[...] * (1.0 + sc_ref[0]) + sh_ref[0])

    o_ref[...] = _dot(u_ref[...], w_ref[...])


def _modmm(x, sc, sh, w_bf, tm, tpg):
    m, d = x.shape
    n = w_bf.shape[1]
    tn = _pick_tile(n, (512, 256, 128))
    r = sc.shape[1]
    mod_spec = pl.BlockSpec((1, r, d), lambda i, j: (i // tpg, 0, 0))
    return pl.pallas_call(
        _modmm_kernel,
        grid=(m // tm, n // tn),
        in_specs=[pl.BlockSpec((tm, d), lambda i, j: (i, 0)), mod_spec, mod_spec,
                  pl.BlockSpec((d, tn), lambda i, j: (0, j))],
        out_specs=pl.BlockSpec((tm, tn), lambda i, j: (i, j)),
        out_shape=jax.ShapeDtypeStruct((m, n), F32),
        scratch_shapes=[pltpu.VMEM((tm, d), BF16)],
        compiler_params=_params("arbitrary", "arbitrary"),
    )(x, sc, sh, w_bf)


_PV_MU_R, _PV_MU_K, _PV_MU_V, _PV_W0, _PV_A0, _PV_KK, _PV_KA, _PV_RK, _PV_LNW, _PV_LNB = range(10)
_PV_ROWS = 16


def _head_half_mask(shape):
    return lax.broadcasted_iota(jnp.int32, shape, 1) < (LANES // 2)


def _head_sums(x):
    h0 = _head_half_mask((x.shape[0], LANES))
    parts = []
    for p in range(x.shape[1] // LANES):
        xs = x[:, p * LANES:(p + 1) * LANES]
        s0 = jnp.sum(jnp.where(h0, xs, 0.0), axis=-1, keepdims=True)
        s1 = jnp.sum(jnp.where(h0, 0.0, xs), axis=-1, keepdims=True)
        parts.append(jnp.where(h0, s0, s1))
    return parts[0] if len(parts) == 1 else jnp.concatenate(parts, axis=1)


def _rwkv_tokens(zr, zk, zv, zl, pr, pk, pv_, plr, pvec, mul, wd, wa, wg):
    row = lambda i: pvec[i:i + 1, :]
    r = zr + row(_PV_MU_R) * (pr - zr)
    k = zk + row(_PV_MU_K) * (pk - zk)
    v = zv + row(_PV_MU_V) * (pv_ - zv)
    ls = zl + mul * (plr - zl)
    wl = _dot(_bf(jnp.tanh(ls)), wd)
    al = _dot(_bf(ls), wa)
    g = _dot(_bf(jax.nn.sigmoid(ls)), wg)
    w = -jax.nn.softplus(-(row(_PV_W0) + wl)) - 0.5
    logd = -jnp.exp(w)
    a = jax.nn.sigmoid(row(_PV_A0) + al)
    kkr = k * row(_PV_KK)
    nrm = jnp.sqrt(_head_sums(kkr * kkr))
    kk = kkr / jnp.maximum(nrm, 1e-12)
    kp = k * (1.0 + (a - 1.0) * row(_PV_KA))
    bonus = _head_sums(r * kp * row(_PV_RK)) * v
    return r, kp, v, kk, a, logd, g, bonus


def _rwkv_finish(o, bonus, g, pvec):
    inv_n = 1.0 / (LANES // 2)
    mu = _head_sums(o) * inv_n
    d = o - mu
    var = _head_sums(d * d) * inv_n
    on = d * lax.rsqrt(var + RWKV_GN_EPS)
    return (on * pvec[_PV_LNW:_PV_LNW + 1, :] + pvec[_PV_LNB:_PV_LNB + 1, :] + bonus) * g


def _shift_rows(z, first_row):
    rolled = pltpu.roll(z, 1, 0)
    rowid = lax.broadcasted_iota(jnp.int32, z.shape, 0)
    return jnp.where(rowid == 0, first_row, rolled)


def _rwkv_seq_kernel(zr_ref, zk_ref, zv_ref, zl_ref, fr_ref, fk_ref, fv_ref, fl_ref,
                     pvec_ref, mul_ref, wd_ref, wa_ref, wg_ref, tri_ref,
                     ya_ref, wkv_ref, s_ref, cr_ref, ck_ref, cv_ref, cl_ref):
    i = pl.program_id(2)
    tc, width = zr_ref.shape
    npp = width // LANES
    c = RWKV_CHUNK
    nch = tc // c
    half = LANES // 2

    @pl.when(i == 0)
    def _():
        s_ref[...] = jnp.zeros_like(s_ref)
        cr_ref[0:1, :] = fr_ref[0]
        ck_ref[0:1, :] = fk_ref[0]
        cv_ref[0:1, :] = fv_ref[0]
        cl_ref[0:1, :] = fl_ref[0]

    zr, zk, zv, zl = zr_ref[...], zk_ref[...], zv_ref[...], zl_ref[...]
    pvec = pvec_ref[...]
    r, kp, v, kk, a, logd, g, bonus = _rwkv_tokens(
        zr, zk, zv, zl,
        _shift_rows(zr, cr_ref[0:1, :]), _shift_rows(zk, ck_ref[0:1, :]),
        _shift_rows(zv, cv_ref[0:1, :]), _shift_rows(zl, cl_ref[0:1, :]),
        pvec, mul_ref[...], wd_ref[...], wa_ref[...], wg_ref[...])
    cr_ref[0:1, :] = zr[tc - 1:tc, :]
    ck_ref[0:1, :] = zk[tc - 1:tc, :]
    cv_ref[0:1, :] = zv[tc - 1:tc, :]
    cl_ref[0:1, :] = zl[tc - 1:tc, :]

    alpha = -kk
    beta = kk * a
    cum_incl = _dot_exact_lhs(tri_ref[...], logd)
    cum_excl = cum_incl - logd
    tot_rows = [cum_incl[(ci + 1) * c - 1:(ci + 1) * c, :] for ci in range(nch)]
    tot = jnp.concatenate([jnp.broadcast_to(t, (c, width)) for t in tot_rows], axis=0)
    e_neg = jnp.exp(-cum_incl)
    e_hat = jnp.exp(tot - cum_incl)
    r_t = r * jnp.exp(cum_incl)
    a_t = alpha * jnp.exp(cum_excl)
    b_t = beta * e_neg
    k_t = kp * e_neg
    b_h = beta * e_hat
    k_h = kp * e_hat

    h0 = _head_half_mask((c, LANES))

    def stack(x, p, ci):
        xb = x[ci * c:(ci + 1) * c, p * LANES:(p + 1) * LANES]
        return jnp.concatenate([jnp.where(h0, xb, 0.0), jnp.where(h0, 0.0, xb)], axis=0)

    rr = lax.broadcasted_iota(jnp.int32, (2 * c, 2 * c), 0)
    cc = lax.broadcasted_iota(jnp.int32, (2 * c, 2 * c), 1)
    same = (rr >= c) == (cc >= c)
    strict = same & (cc < rr)
    incl = same & (cc <= rr)
    eye = rr == cc
    zeros_blk = jnp.zeros((2 * c, LANES), BF16)

    probs = [(p, ci) for p in range(npp) for ci in range(nch)]
    a_s = {q: stack(a_t, *q) for q in probs}
    r_s = {q: stack(r_t, *q) for q in probs}
    v_bf = {q: _bf(stack(v, *q)) for q in probs}

    pmat, a_ak, a_r = {}, {}, {}
    for q in probs:
        amat = _dot_nt(_bf(jnp.concatenate([a_s[q], r_s[q]], axis=0)),
                       _bf(jnp.concatenate([stack(b_t, *q), stack(k_t, *q)], axis=0)))
        pmat[q] = jnp.where(strict, amat[:2 * c, :2 * c], 0.0)
        a_ak[q] = _bf(jnp.where(strict, amat[:2 * c, 2 * c:], 0.0))
        a_r[q] = _bf(jnp.concatenate([jnp.where(incl, amat[2 * c:, :2 * c], 0.0),
                                      jnp.where(incl, amat[2 * c:, 2 * c:], 0.0)], axis=1))

    x = {q: jnp.concatenate([a_s[q], _dot(a_ak[q], v_bf[q])], axis=1) for q in probs}

    nsteps = int(math.log2(c))
    for it in range(nsteps):
        for q in probs:
            p_bf = _bf(pmat[q])
            x[q] = x[q] + _dot(p_bf, _bf(x[q]))
            if it + 1 < nsteps:
                pmat[q] = _dot(p_bf, p_bf)

    r_pair, o_pair, g_t, h_t = {}, {}, {}, {}
    for q in probs:
        p, ci = q
        x_bf = _bf(x[q])
        rhs = jnp.concatenate([x_bf, jnp.concatenate([zeros_blk, v_bf[q]], axis=1)], axis=0)
        y = _dot(a_r[q], rhs)
        r_hat = r_s[q] + y[:, :LANES]
        r_pair[q] = r_hat[:c] + r_hat[c:]
        o_pair[q] = y[:c, LANES:] + y[c:, LANES:]
        z = _dot_tn(x_bf, _bf(stack(b_h, *q)))
        w_c = jnp.exp(tot_rows[ci][:, p * LANES:(p + 1) * LANES])
        g_t[q] = jnp.where(eye, w_c, 0.0) + z[:LANES]
        h_t[q] = z[LANES:] + _dot_tn(v_bf[q], _bf(stack(k_h, *q)))

    outs = {}
    for ci in range(nch):
        for p in range(npp):
            q = (p, ci)
            s0 = s_ref[p]
            outs[q] = _dot3_nt(r_pair[q], s0) + o_pair[q]
            s_ref[p] = _dot3_nn(s0, g_t[q]) + h_t[q]

    cols = [jnp.concatenate([outs[(p, ci)] for ci in range(nch)], axis=0) if nch > 1 else outs[(p, 0)]
            for p in range(npp)]
    o = jnp.concatenate(cols, axis=1) if npp > 1 else cols[0]
    ya_ref[...] = _bf(_rwkv_finish(o, bonus, g, pvec))

    @pl.when(i == pl.num_programs(2) - 1)
    def _():
        for p in range(npp):
            s = s_ref[p]
            wkv_ref[0, 2 * p] = s[:half, :half]
            wkv_ref[0, 2 * p + 1] = s[half:, half:]


def _rwkv_seq(proj, nb, seq, lay, first, pvec, mul, wd, wa, wg):
    rw, lslot = lay["rw"], lay["lslot"]
    npair = rw // LANES
    npp = RWKV_PAIRS_PER_STEP if npair % RWKV_PAIRS_PER_STEP == 0 else 1
    ngrp = npair // npp
    width = npp * LANES
    c = RWKV_CHUNK
    tc = _pick_tile(seq, (2 * c, c))
    nt = seq // tc
    t_idx = jnp.arange(tc)
    tri = ((t_idx[:, None] // c == t_idx[None, :] // c) & (t_idx[None, :] <= t_idx[:, None])).astype(BF16)
    fr, fk, fv, fl = first
    col = lambda s: pl.BlockSpec((tc, width), lambda b, gp, i, s=s: (b * nt + i, s * ngrp + gp))
    fcol = lambda s: pl.BlockSpec((1, 1, width), lambda b, gp, i, s=s: (b, 0, s * ngrp + gp))
    wspec = pl.BlockSpec((lslot, width), lambda b, gp, i: (0, gp))
    return pl.pallas_call(
        _rwkv_seq_kernel,
        grid=(nb, ngrp, nt),
        in_specs=[col(0), col(1), col(2),
                  pl.BlockSpec((tc, lslot), lambda b, gp, i: (b * nt + i, lay["off_l"] // lslot)),
                  fcol(0), fcol(1), fcol(2),
                  pl.BlockSpec((1, 1, lslot), lambda b, gp, i: (b, 0, 0)),
                  pl.BlockSpec((_PV_ROWS, width), lambda b, gp, i: (0, gp)),
                  pl.BlockSpec((1, lslot), lambda b, gp, i: (0, 0)),
                  wspec, wspec, wspec,
                  pl.BlockSpec((tc, tc), lambda b, gp, i: (0, 0))],
        out_specs=[pl.BlockSpec((tc, width), lambda b, gp, i: (b * nt + i, gp)),
                   pl.BlockSpec((1, 2 * npp, LANES // 2, LANES // 2), lambda b, gp, i: (b, gp, 0, 0))],
        out_shape=[jax.ShapeDtypeStruct((nb * seq, rw), BF16),
                   jax.ShapeDtypeStruct((nb, 2 * npair, LANES // 2, LANES // 2), F32)],
        scratch_shapes=[pltpu.VMEM((npp, LANES, LANES), F32),
                        pltpu.VMEM((SUBLANES, width), F32), pltpu.VMEM((SUBLANES, width), F32),
                        pltpu.VMEM((SUBLANES, width), F32), pltpu.VMEM((SUBLANES, lslot), F32)],
        compiler_params=_params("arbitrary", "arbitrary", "arbitrary"),
    )(proj, proj, proj, proj, fr, fk, fv, fl, pvec, mul, wd, wa, wg, tri)


_STEP_GROUP = 8


def _rwkv_step_kernel(zr_ref, zk_ref, zv_ref, zl_ref, pr_ref, pk_ref, pv_ref, plr_ref,
                      pvec_ref, mul_ref, wd_ref, wa_ref, wg_ref, s_ref,
                      ya_ref, snew_ref, sa_s, sw_s):
    bb = zr_ref.shape[0]
    half = LANES // 2
    pvec = pvec_ref[...]
    r, kp, v, kk, a, logd, g, bonus = _rwkv_tokens(
        zr_ref[...], zk_ref[...], zv_ref[...], zl_ref[...],
        pr_ref[...], pk_ref[...], pv_ref[...], plr_ref[...],
        pvec, mul_ref[...], wd_ref[...], wa_ref[...], wg_ref[...])
    w = jnp.exp(logd)
    beta = kk * a
    nkk = -kk
    wr = w * r
    rid = lax.broadcasted_iota(jnp.int32, (2 * SUBLANES, half), 0)

    for g0 in range(0, bb, _STEP_GROUP):
        probs = [(b, e) for b in range(g0, min(g0 + _STEP_GROUP, bb)) for e in range(2)]
        row = lambda x, q: x[q[0]:q[0] + 1, q[1] * half:(q[1] + 1) * half]
        prod = {}
        for q in probs:
            sh, sl = _split2(s_ref[q[0], q[1]])
            qh, ql = _split2(jnp.where(rid == 0, row(nkk, q), row(wr, q)))
            hi = _dot_nt(jnp.concatenate([qh, ql], axis=0), sh)
            prod[q] = hi[:2 * SUBLANES] + hi[2 * SUBLANES:] + _dot_nt(qh, sl)
        for q in probs:
            b, e = q
            sa_row, sw_row = prod[q][0:1], prod[q][1:2]
            lh, ll = _split2(jnp.where(rid == 0, sa_row, jnp.where(rid == 1, row(v, q), 0.0)))
            rh, rl = _split2(jnp.where(rid == 0, row(beta, q), jnp.where(rid == 1, row(kp, q), 0.0)))
            upd = (_dot_tn(jnp.concatenate([lh, ll], axis=0), jnp.concatenate([rh, rh], axis=0))
                   + _dot_tn(lh, rl))
            snew_ref[b, e] = s_ref[b, e] * row(w, q) + upd
            sa_s[b:b + 1, e * half:(e + 1) * half] = sa_row
            sw_s[b:b + 1, e * half:(e + 1) * half] = sw_row

    o = sw_s[...] + sa_s[...] * _head_sums(beta * r) + v * _head_sums(kp * r)
    ya_ref[...] = _bf(_rwkv_finish(o, bonus, g, pvec))


def _rwkv_step(proj, nb, lay, prev, pvec, mul, wd, wa, wg, s_wkv):
    rw, lslot = lay["rw"], lay["lslot"]
    npair = rw // LANES
    bb = _pick_tile(nb, (32, 16, 8))
    pm, plr = prev
    half = LANES // 2
    col = lambda off: pl.BlockSpec((bb, LANES), lambda p, j, off=off: (j, off + p))
    wspec = pl.BlockSpec((lslot, LANES), lambda p, j: (0, p))
    sspec = pl.BlockSpec((bb, 2, half, half), lambda p, j: (j, p, 0, 0))
    return pl.pallas_call(
        _rwkv_step_kernel,
        grid=(npair, nb // bb),
        in_specs=[col(0), col(npair), col(2 * npair),
                  pl.BlockSpec((bb, lslot), lambda p, j: (j, lay["off_l"] // lslot)),
                  col(0), col(npair), col(2 * npair),
                  pl.BlockSpec((bb, lslot), lambda p, j: (j, 0)),
                  pl.BlockSpec((_PV_ROWS, LANES), lambda p, j: (0, p)),
                  pl.BlockSpec((1, lslot), lambda p, j: (0, 0)),
                  wspec, wspec, wspec, sspec],
        out_specs=[pl.BlockSpec((bb, LANES), lambda p, j: (j, p)), sspec],
        out_shape=[jax.ShapeDtypeStruct((nb, rw), BF16),
                   jax.ShapeDtypeStruct(s_wkv.shape, F32)],
        scratch_shapes=[pltpu.VMEM((bb, LANES), F32), pltpu.VMEM((bb, LANES), F32)],
        compiler_params=_params("arbitrary", "arbitrary"),
    )(proj, proj, proj, proj, pm, pm, pm, plr, pvec, mul, wd, wa, wg, s_wkv)


def _rope_rows(t, cos2, sin2):
    return t * cos2 + pltpu.roll(t, LANES // 2, 1) * sin2


def _head_norm_rows(o, eps):
    mu = jnp.mean(o, axis=-1, keepdims=True)
    d = o - mu
    var = jnp.mean(d * d, axis=-1, keepdims=True)
    return d * lax.rsqrt(var + eps)


def _ret_seq_kernel(q_ref, k_ref, v_ref, g_ref, cos_ref, sin_ref, intra_ref, qd_ref, kd_ref,
                    blk_ref, gnw_ref, gnb_ref, yb_ref, ret_ref, s_ref):
    i = pl.program_id(1)
    nh = s_ref.shape[0]
    dk = s_ref.shape[1]

    @pl.when(i == 0)
    def _():
        s_ref[...] = jnp.zeros_like(s_ref)

    cos2, sin2 = cos_ref[...], sin_ref[...]
    for h in range(nh):
        hs = slice(h * LANES, (h + 1) * LANES)
        qh = _rope_rows(q_ref[:, hs], cos2, sin2)
        kh = _rope_rows(k_ref[:, hs], cos2, sin2) * (dk ** -0.5)
        vb = _bf(v_ref[:, hs])
        qb = _bf(qh)
        scores = _dot_nt(qb, _bf(kh)) * intra_ref[h]
        s0 = s_ref[h]
        o = _dot(_bf(scores), vb) + _dot(qb, _bf(s0)) * qd_ref[h]
        s_ref[h] = s0 * blk_ref[h] + _dot_tn(_bf(kh * kd_ref[h]), vb)
        on = _head_norm_rows(o, RET_GN_EPS)
        yb_ref[:, hs] = _bf((on * gnw_ref[:, hs] + gnb_ref[:, hs]) * jax.nn.silu(g_ref[:, hs]))

    @pl.when(i == pl.num_programs(1) - 1)
    def _():
        ret_ref[0] = s_ref[...]


def _ret_tables(nh, c):
    log_g = jnp.log1p(-jnp.exp2(-5.0 - jnp.arange(nh, dtype=F32)))
    i = jnp.arange(c, dtype=F32)
    rel = i[:, None] - i[None, :]
    intra = jnp.where(rel >= 0, jnp.exp(log_g[:, None, None] * jnp.maximum(rel, 0.0)), 0.0)
    q_decay = jnp.exp(log_g[:, None] * (i + 1.0))
    k_decay = jnp.exp(log_g[:, None] * (c - 1.0 - i))
    blk_decay = jnp.exp(log_g * c)
    return intra, q_decay, k_decay, blk_decay


def _rope_tables(pos, dk):
    half = dk // 2
    inv = ROPE_BASE ** (-jnp.arange(half, dtype=F32) / half)
    ang = pos[:, None] * inv[None, :]
    cos, sin = jnp.cos(ang), jnp.sin(ang)
    return jnp.concatenate([cos, cos], axis=-1), jnp.concatenate([-sin, sin], axis=-1)


def _ret_seq(proj, nb, seq, lay, gnw, gnb):
    qk, rv, nh = lay["qk"], lay["rv"], lay["ret_heads"]
    dk, dv = qk // nh, rv // nh
    assert dk == LANES and dv == LANES
    c = RET_CHUNK if seq % RET_CHUNK == 0 else seq
    assert c % SUBLANES == 0
    nt = seq // c
    intra, qd, kd, blk = _ret_tables(nh, c)
    qd = jnp.broadcast_to(qd[:, :, None], (nh, c, dv))
    kd = jnp.broadcast_to(kd[:, :, None], (nh, c, dk))
    blk = jnp.broadcast_to(blk[:, None, None], (nh, 1, dv))
    cos2, sin2 = _rope_tables(jnp.arange(seq, dtype=F32), dk)
    seg = lambda off, w: pl.BlockSpec((c, w), lambda b, i, off=off, w=w: (b * nt + i, off // w))
    full3 = lambda a: pl.BlockSpec(a.shape, lambda b, i: (0, 0, 0))
    return pl.pallas_call(
        _ret_seq_kernel,
        grid=(nb, nt),
        in_specs=[seg(lay["off_q"], qk), seg(lay["off_kr"], qk), seg(lay["off_vr"], rv),
                  seg(lay["off_rg"], rv),
                  pl.BlockSpec((c, dk), lambda b, i: (i, 0)), pl.BlockSpec((c, dk), lambda b, i: (i, 0)),
                  full3(intra), full3(qd), full3(kd), full3(blk),
                  pl.BlockSpec((1, rv), lambda b, i: (0, 0)), pl.BlockSpec((1, rv), lambda b, i: (0, 0))],
        out_specs=[pl.BlockSpec((c, rv), lambda b, i: (b * nt + i, 0)),
                   pl.BlockSpec((1, nh, dk, dv), lambda b, i: (b, 0, 0, 0))],
        out_shape=[jax.ShapeDtypeStruct((nb * seq, rv), BF16),
                   jax.ShapeDtypeStruct((nb, nh, dk, dv), F32)],
        scratch_shapes=[pltpu.VMEM((nh, dk, dv), F32)],
        compiler_params=_params("arbitrary", "arbitrary"),
    )(proj, proj, proj, proj, cos2, sin2, intra, qd, kd, blk, gnw, gnb)


_RC_INTRA, _RC_QD, _RC_KD, _RC_BLK = range(4)


def _ret_step_kernel(q_ref, k_ref, v_ref, g_ref, cos_ref, sin_ref, rc_ref, gnw_ref, gnb_ref, s_ref,
                     yb_ref, snew_ref, q_s, k_s, o_s):
    bb = q_ref.shape[0]
    nh = s_ref.shape[1]
    dk = s_ref.shape[2]
    cos2, sin2 = cos_ref[...], sin_ref[...]
    for h in range(nh):
        hs = slice(h * LANES, (h + 1) * LANES)
        q_s[:, hs] = _rope_rows(q_ref[:, hs], cos2, sin2)
        k_s[:, hs] = _rope_rows(k_ref[:, hs], cos2, sin2) * (dk ** -0.5)
    eye = (lax.broadcasted_iota(jnp.int32, (LANES, LANES), 0)
           == lax.broadcasted_iota(jnp.int32, (LANES, LANES), 1))
    for b in range(bb):
        for h in range(nh):
            hs = slice(h * LANES, (h + 1) * LANES)
            rc = lambda j: rc_ref[h, j:j + 1, :]
            q_row = q_s[b:b + 1, hs]
            k_row = k_s[b:b + 1, hs]
            v_row = v_ref[b:b + 1, hs]
            s0 = s_ref[b, h]
            q_col = jnp.sum(jnp.where(eye, q_row, 0.0), axis=-1, keepdims=True)
            k_col = jnp.sum(jnp.where(eye, k_row, 0.0), axis=-1, keepdims=True)
            score = jnp.sum(q_row * k_row, axis=-1, keepdims=True) * rc(_RC_INTRA)
            o_row = score * v_row + jnp.sum(s0 * q_col, axis=0, keepdims=True) * rc(_RC_QD)
            snew_ref[b, h] = s0 * rc(_RC_BLK) + (k_col * rc(_RC_KD)) * v_row
            o_s[b:b + 1, hs] = o_row
    for h in range(nh):
        hs = slice(h * LANES, (h + 1) * LANES)
        on = _head_norm_rows(o_s[:, hs], RET_GN_EPS)
        yb_ref[:, hs] = _bf((on * gnw_ref[:, hs] + gnb_ref[:, hs]) * jax.nn.silu(g_ref[:, hs]))


def _ret_step(proj, nb, lay, gnw, gnb, s_ret, pos0):
    qk, rv, nh = lay["qk"], lay["rv"], lay["ret_heads"]
    dk, dv = qk // nh, rv // nh
    assert dk == LANES and dv == LANES
    bb = SUBLANES
    intra, qd, kd, blk = _ret_tables(nh, 1)
    rc = jnp.stack([intra[:, 0, 0], qd[:, 0], kd[:, 0], blk], axis=1)
    rc = jnp.pad(rc, ((0, 0), (0, SUBLANES - 4)))
    rc = jnp.broadcast_to(rc[:, :, None], (nh, SUBLANES, LANES))
    cos2, sin2 = _rope_tables(jnp.asarray([pos0], dtype=F32), dk)
    seg = lambda off, w: pl.BlockSpec((bb, w), lambda j, off=off, w=w: (j, off // w))
    sspec = pl.BlockSpec((bb, nh, dk, dv), lambda j: (j, 0, 0, 0))
    row = lambda w: pl.BlockSpec((1, w), lambda j: (0, 0))
    return pl.pallas_call(
        _ret_step_kernel,
        grid=(nb // bb,),
        in_specs=[seg(lay["off_q"], qk), seg(lay["off_kr"], qk), seg(lay["off_vr"], rv),
                  seg(lay["off_rg"], rv), row(dk), row(dk),
                  pl.BlockSpec(rc.shape, lambda j: (0, 0, 0)), row(rv), row(rv), sspec],
        out_specs=[pl.BlockSpec((bb, rv), lambda j: (j, 0)), sspec],
        out_shape=[jax.ShapeDtypeStruct((nb, rv), BF16), jax.ShapeDtypeStruct(s_ret.shape, F32)],
        scratch_shapes=[pltpu.VMEM((bb, qk), F32), pltpu.VMEM((bb, qk), F32), pltpu.VMEM((bb, rv), F32)],
        compiler_params=_params("arbitrary"),
    )(proj, proj, proj, proj, cos2, sin2, rc, gnw, gnb, s_ret)


def _merge_kernel(alpha, ng, *refs):
    ya_ref, yb_ref = refs[0], refs[1]
    ga_refs = refs[2:2 + ng]
    gb_refs = refs[2 + ng:2 + 2 * ng]
    (x_ref, g1_ref, sc2_ref, sh2_ref, wa_ref, wb_ref, wo_ref, lnw_ref, lnb_ref,
     x1_ref, u2_ref) = refs[2 + 2 * ng:]
    cat = lambda rs: jnp.concatenate([r[...] for r in rs], axis=1) if ng > 1 else rs[0][...]
    merged = (jax.nn.sigmoid(cat(ga_refs)) * _dot(ya_ref[...], wa_ref[...])
              + jax.nn.sigmoid(cat(gb_refs)) * _dot(yb_ref[...], wb_ref[...]))
    t = alpha * x_ref[...] + g1_ref[0] * _dot(_bf(merged), wo_ref[...])
    x1 = _layer_norm_rows(t, lnw_ref[...], lnb_ref[...])
    x1_ref[...] = x1
    u2_ref[...] = _bf(x1 * (1.0 + sc2_ref[0]) + sh2_ref[0])


def _merge(ya, yb, proj, x, g1, sc2, sh2, wba, wbb, wout, lnw, lnb, lay, alpha, tm, tpg):
    m, d = x.shape
    rw, rv = ya.shape[1], yb.shape[1]
    gw = math.gcd(lay["off_ga"], d)
    ng = d // gw
    r = g1.shape[1]
    rowt = lambda w: pl.BlockSpec((tm, w), lambda i: (i, 0))
    gate = lambda off, q: pl.BlockSpec((tm, gw), lambda i, off=off, q=q: (i, off // gw + q))
    mod = pl.BlockSpec((1, r, d), lambda i: (i // tpg, 0, 0))
    const = lambda a: pl.BlockSpec(a.shape, lambda i: (0, 0), pipeline_mode=pl.Buffered(1))
    return pl.pallas_call(
        functools.partial(_merge_kernel, alpha, ng),
        grid=(m // tm,),
        in_specs=[rowt(rw), rowt(rv)]
                 + [gate(lay["off_ga"], q) for q in range(ng)]
                 + [gate(lay["off_gb"], q) for q in range(ng)]
                 + [rowt(d), mod, mod, mod, const(wba), const(wbb), const(wout), const(lnw), const(lnb)],
        out_specs=[rowt(d), rowt(d)],
        out_shape=[jax.ShapeDtypeStruct((m, d), F32), jax.ShapeDtypeStruct((m, d), BF16)],
        compiler_params=_params("arbitrary"),
    )(ya, yb, *([proj] * (2 * ng)), x, g1, sc2, sh2, wba, wbb, wout, lnw, lnb)


def _ffn_seq_kernel(tps, nj, alpha, u_ref, wa_ref, wb_ref, cwa_ref, cwb_ref, ia_ref, ib_ref, wd_ref,
                    x1_ref, g2_ref, lnw_ref, lnb_ref, o_ref, ta_ref, tb_ref,
                    ha_s, hb_s, ca_s, cb_s, act_s, acc_s):
    i, j = pl.program_id(0), pl.program_id(1)
    tm = u_ref.shape[0]

    @pl.when((i == 0) & (j == 0))
    def _():
        acc_s[...] = jnp.zeros_like(acc_s)
        act_s[...] = jnp.zeros_like(act_s)

    jc = jnp.minimum(j, nj - 1)
    first = i % tps == 0
    u = u_ref[...]
    halves, tails = [], []
    for w_ref, cw_ref, init_ref, h_s, c_s in ((wa_ref, cwa_ref, ia_ref, ha_s, ca_s),
                                               (wb_ref, cwb_ref, ib_ref, hb_s, cb_s)):
        h = _dot(u, w_ref[...])
        h_s[0:SUBLANES, :] = jnp.where(first, init_ref[0], c_s[jc])
        h_s[SUBLANES:SUBLANES + tm, :] = h
        tail = h[tm - SUBLANES:tm, :]
        c_s[jc] = tail
        tails.append(tail)
        cw = cw_ref[...]
        halves.append(cw[3:4, :] + cw[0:1, :] * h_s[SUBLANES - 2:SUBLANES - 2 + tm, :]
                      + cw[1:2, :] * h_s[SUBLANES - 1:SUBLANES - 1 + tm, :] + cw[2:3, :] * h)
    acc_s[...] = jnp.where(j == 0, 0.0, acc_s[...] + _dot(act_s[(j + 1) % 2], wd_ref[...]))
    act_s[j % 2] = _bf(jax.nn.silu(halves[0]) * halves[1])

    @pl.when((i % tps == tps - 1) & (j < nj))
    def _():
        ta_ref[i // tps, j] = tails[0]
        tb_ref[i // tps, j] = tails[1]

    @pl.when(j == nj)
    def _():
        t = alpha * x1_ref[...] + g2_ref[0] * acc_s[...]
        o_ref[...] = _layer_norm_rows(t, lnw_ref[...], lnb_ref[...])


def _ffn_up_step_kernel(u_ref, wa_ref, wb_ref, cwa_ref, cwb_ref, p1a_ref, p2a_ref, p1b_ref, p2b_ref,
                        act_ref, ha_ref, hb_ref):
    u = u_ref[...]
    halves = []
    for w_ref, cw_ref, p1_ref, p2_ref, h_ref in ((wa_ref, cwa_ref, p1a_ref, p2a_ref, ha_ref),
                                                 (wb_ref, cwb_ref, p1b_ref, p2b_ref, hb_ref)):
        h = _dot(u, w_ref[...])
        h_ref[...] = h
        cw = cw_ref[...]
        halves.append(cw[3:4, :] + cw[0:1, :] * p2_ref[...] + cw[1:2, :] * p1_ref[...] + cw[2:3, :] * h)
    act_ref[...] = _bf(jax.nn.silu(halves[0]) * halves[1])


def _ffn_up(u2, w_up_bf, cwt, nb, seq, s_conv, tm):
    m, d = u2.shape
    f2 = w_up_bf.shape[1]
    f = f2 // 2
    tn = _pick_tile(f, (512, 256, 128))
    nj = f // tn
    w_a = pl.BlockSpec((d, tn), lambda i, j: (0, j))
    w_b = pl.BlockSpec((d, tn), lambda i, j: (0, nj + j))
    cw_a = pl.BlockSpec((4, tn), lambda i, j: (0, j))
    cw_b = pl.BlockSpec((4, tn), lambda i, j: (0, nj + j))
    assert seq == 1
    p_a = pl.BlockSpec((tm, tn), lambda i, j: (i, j))
    p_b = pl.BlockSpec((tm, tn), lambda i, j: (i, nj + j))
    prev1, prev2 = s_conv[:, 1, :], s_conv[:, 0, :]
    act, h_a, h_b = pl.pallas_call(
        _ffn_up_step_kernel,
        grid=(m // tm, nj),
        in_specs=[pl.BlockSpec((tm, d), lambda i, j: (i, 0)), w_a, w_b, cw_a, cw_b, p_a, p_a, p_b, p_b],
        out_specs=[pl.BlockSpec((tm, tn), lambda i, j: (i, j))] * 3,
        out_shape=[jax.ShapeDtypeStruct((m, f), BF16), jax.ShapeDtypeStruct((m, f), F32),
                   jax.ShapeDtypeStruct((m, f), F32)],
        compiler_params=_params("arbitrary", "arbitrary"),
    )(u2, w_up_bf, w_up_bf, cwt, cwt, prev1, prev2, prev1, prev2)
    conv_new = jnp.stack([prev1, jnp.concatenate([h_a, h_b], axis=-1)], axis=1)
    return act, conv_new


def _ffn_seq(u2, w_up_bf, cwt, w_down_bf, x1, g2, lnw, lnb, alpha, nb, seq, s_conv, tm):
    m, d = u2.shape
    f2 = w_up_bf.shape[1]
    f = f2 // 2
    tn = _pick_tile(f, (512, 256, 128))
    nj = f // tn
    assert seq % tm == 0 and tm >= SUBLANES
    tps = seq // tm
    up = lambda i, j: jnp.minimum(j, nj - 1)
    w_a = pl.BlockSpec((d, tn), lambda i, j: (0, up(i, j)))
    w_b = pl.BlockSpec((d, tn), lambda i, j: (0, nj + up(i, j)))
    cw_a = pl.BlockSpec((4, tn), lambda i, j: (0, up(i, j)))
    cw_b = pl.BlockSpec((4, tn), lambda i, j: (0, nj + up(i, j)))
    init = jnp.pad(s_conv, ((0, 0), (SUBLANES - 2, 0), (0, 0)))
    i_a = pl.BlockSpec((1, SUBLANES, tn), lambda i, j: (i // tps, 0, up(i, j)))
    i_b = pl.BlockSpec((1, SUBLANES, tn), lambda i, j: (i // tps, 0, nj + up(i, j)))
    w_d = pl.BlockSpec((tn, d), lambda i, j: (jnp.maximum(j - 1, 0), 0))
    rowt = pl.BlockSpec((tm, d), lambda i, j: (i, 0))
    vec = pl.BlockSpec((1, d), lambda i, j: (0, 0))
    t_o = pl.BlockSpec((nb, nj, SUBLANES, tn), lambda i, j: (0, 0, 0, 0))
    tails = jax.ShapeDtypeStruct((nb, nj, SUBLANES, tn), F32)
    x2, t_a, t_b = pl.pallas_call(
        functools.partial(_ffn_seq_kernel, tps, nj, alpha),
        grid=(m // tm, nj + 1),
        in_specs=[rowt, w_a, w_b, cw_a, cw_b, i_a, i_b, w_d, rowt,
                  pl.BlockSpec((1, 1, d), lambda i, j: (i // tps, 0, 0)), vec, vec],
        out_specs=[rowt, t_o, t_o],
        out_shape=[jax.ShapeDtypeStruct((m, d), F32), tails, tails],
        scratch_shapes=[pltpu.VMEM((tm + SUBLANES, tn), F32), pltpu.VMEM((tm + SUBLANES, tn), F32),
                        pltpu.VMEM((nj, SUBLANES, tn), F32), pltpu.VMEM((nj, SUBLANES, tn), F32),
                        pltpu.VMEM((2, tm, tn), BF16), pltpu.VMEM((tm, d), F32)],
        compiler_params=_params("arbitrary", "arbitrary"),
    )(u2, w_up_bf, w_up_bf, cwt, cwt, init, init, w_down_bf, x1, g2, lnw, lnb)
    rows = lambda t: t[:, :, SUBLANES - 2:, :].transpose(0, 2, 1, 3).reshape(nb, 2, f)
    return x2, jnp.concatenate([rows(t_a), rows(t_b)], axis=-1)


def _ffn_down_kernel(alpha, act_ref, w_ref, x1_ref, g2_ref, lnw_ref, lnb_ref, o_ref, acc_ref):
    k = pl.program_id(1)

    @pl.when(k == 0)
    def _():
        acc_ref[...] = jnp.zeros_like(acc_ref)

    acc_ref[...] += _dot(act_ref[...], w_ref[...])

    @pl.when(k == pl.num_programs(1) - 1)
    def _():
        t = alpha * x1_ref[...] + g2_ref[0] * acc_ref[...]
        o_ref[...] = _layer_norm_rows(t, lnw_ref[...], lnb_ref[...])


def _ffn_down(act, w_down_bf, x1, g2, lnw, lnb, alpha, tm, tpg):
    m, f = act.shape
    d = w_down_bf.shape[1]
    tk = _pick_tile(f, (512, 256, 128))
    r = g2.shape[1]
    return pl.pallas_call(
        functools.partial(_ffn_down_kernel, alpha),
        grid=(m // tm, f // tk),
        in_specs=[pl.BlockSpec((tm, tk), lambda i, k: (i, k)),
                  pl.BlockSpec((tk, d), lambda i, k: (k, 0)),
                  pl.BlockSpec((tm, d), lambda i, k: (i, 0)),
                  pl.BlockSpec((1, r, d), lambda i, k: (i // tpg, 0, 0)),
                  pl.BlockSpec((1, d), lambda i, k: (0, 0)),
                  pl.BlockSpec((1, d), lambda i, k: (0, 0))],
        out_specs=pl.BlockSpec((tm, d), lambda i, k: (i, 0)),
        out_shape=jax.ShapeDtypeStruct((m, d), F32),
        scratch_shapes=[pltpu.VMEM((tm, d), F32)],
        compiler_params=_params("arbitrary", "arbitrary"),
    )(act, w_down_bf, x1, g2, lnw, lnb)


def _layout(d, rw, qk, rv, lora_w, ret_heads):
    lslot = next(s for s in (128, 256, 512, 1024, 2048) if s >= lora_w)
    off_q = 3 * rw
    off_kr = off_q + qk
    off_vr = off_kr + qk
    off_rg = off_vr + rv
    off_ga = off_rg + rv
    off_gb = off_ga + d
    off_l = off_gb + d
    assert rw % LANES == 0 and off_l % lslot == 0
    assert off_q % qk == 0 and off_kr % qk == 0 and off_vr % rv == 0 and off_rg % rv == 0
    return dict(d=d, rw=rw, qk=qk, rv=rv, lslot=lslot, lora_w=lora_w, ret_heads=ret_heads,
                off_q=off_q, off_kr=off_kr, off_vr=off_vr, off_rg=off_rg, off_ga=off_ga,
                off_gb=off_gb, off_l=off_l, nt=off_l + lslot)


def _prep_weights(lay, w_in, shift_mu, w0, w_decay_up, a0, w_aaa_up, w_gate_up, k_k, k_a, r_k,
                  lnx_w, lnx_b):
    rw, lslot, lora_w = lay["rw"], lay["lslot"], lay["lora_w"]
    p_rwkv = 3 * rw + lora_w
    d = w_in.shape[0]
    w_bf = _bf(w_in)
    w_perm = jnp.concatenate(
        [w_bf[:, :3 * rw], w_bf[:, p_rwkv:], w_bf[:, 3 * rw:p_rwkv],
         jnp.zeros((d, lslot - lora_w), BF16)], axis=1)
    dl, al, gl = w_decay_up.shape[0], w_aaa_up.shape[0], w_gate_up.shape[0]
    pad_rows = lambda w, lo: jnp.pad(w, ((lo, lslot - lo - w.shape[0]), (0, 0))).astype(BF16)
    wd = pad_rows(w_decay_up, 0)
    wa = pad_rows(w_aaa_up, dl)
    wg = pad_rows(w_gate_up, dl + al)
    rows = [shift_mu[:rw], shift_mu[rw:2 * rw], shift_mu[2 * rw:3 * rw], w0, a0, k_k, k_a,
            r_k.reshape(-1), lnx_w, lnx_b]
    pvec = jnp.pad(jnp.stack(rows, axis=0), ((0, _PV_ROWS - len(rows)), (0, 0)))
    mul = jnp.pad(shift_mu[3 * rw:], (0, lslot - lora_w))[None, :]
    return w_perm, wd, wa, wg, pvec, mul


def _run_layer(x2d, nb, seq, ada, states, pos0, lay, wts, alpha):
    d, rw, lslot, lora_w = lay["d"], lay["rw"], lay["lslot"], lay["lora_w"]
    (w_perm, wd, wa, wg, pvec, mul, gnw, gnb, wba, wbb, wout, ln1w, ln1b,
     w_up, cwt, w_down, ln2w, ln2b) = wts
    m = nb * seq
    sh1, sc1, g1, sh2, sc2, g2 = jnp.split(ada, 6, axis=-1)
    if seq == 1:
        tm, tpg = m, 1
        shape_mod = lambda t: t[None]
    else:
        tm = _pick_tile(seq, (512, 256, 128, 64, 32, 16, 8))
        tpg = seq // tm
        shape_mod = lambda t: t[:, None, :]
    sh1, sc1, g1, sh2, sc2, g2 = map(shape_mod, (sh1, sc1, g1, sh2, sc2, g2))
    tm_in = _pick_tile(seq, (1024, 512, 256, 128, 64, 32, 16, 8)) if seq > 1 else m
    proj = _modmm(x2d, sc1, sh1, w_perm, tm_in, (seq // tm_in) if seq > 1 else 1)

    s_wkv, s_shift, s_ret, s_conv = states
    shift_main = s_shift[:, :3 * rw]
    shift_lora = jnp.pad(s_shift[:, 3 * rw:], ((0, 0), (0, lslot - lora_w)))
    if seq == 1:
        ya, wkv_new = _rwkv_step(proj, nb, lay, (shift_main, shift_lora), pvec, mul, wd, wa, wg, s_wkv)
        yb, ret_new = _ret_step(proj, nb, lay, gnw, gnb, s_ret, pos0)
    else:
        first = (shift_main[:, None, :], shift_main[:, None, :], shift_main[:, None, :],
                 shift_lora[:, None, :])
        ya, wkv_new = _rwkv_seq(proj, nb, seq, lay, first, pvec, mul, wd, wa, wg)
        yb, ret_new = _ret_seq(proj, nb, seq, lay, gnw, gnb)
    last = proj.reshape(nb, seq, lay["nt"])[:, -1, :]
    shift_new = jnp.concatenate([last[:, :3 * rw], last[:, lay["off_l"]:lay["off_l"] + lora_w]], axis=-1)

    tm_merge = min(tm, 256)
    x1, u2 = _merge(ya, yb, proj, x2d, g1, sc2, sh2, wba, wbb, wout, ln1w, ln1b, lay, alpha,
                    tm_merge, (seq // tm_merge) if seq > 1 else 1)
    if seq == 1:
        act, conv_new = _ffn_up(u2, w_up, cwt, nb, seq, s_conv, tm)
        x2 = _ffn_down(act, w_down, x1, g2, ln2w, ln2b, alpha, tm, tpg)
    else:
        x2, conv_new = _ffn_seq(u2, w_up, cwt, w_down, x1, g2, ln2w, ln2b, alpha, nb, seq, s_conv, tm)
    return x2, wkv_new, shift_new, ret_new, conv_new


def kernel(x_prompt, x_sample, c_prompt, c_sample, state_wkv, state_shift, state_ret, state_conv, w_ada, b_ada, w_in, shift_mu, w0, w_decay_up, a0, w_aaa_up, w_gate_up, k_k, k_a, r_k, lnx_w, lnx_b, ret_gn_w, ret_gn_b, w_branch_a, w_branch_b, w_out, ln1_w, ln1_b, w_up, conv_w, conv_b, w_down, ln2_w, ln2_b):
    depth = w_ada.shape[0]
    nbp, seq_p, d = x_prompt.shape
    nbs, seq_s, _ = x_sample.shape
    assert seq_s == 1
    rw = k_k.shape[-1]
    ret_heads, dk, dv = state_ret.shape[2:]
    lora_w = w_decay_up.shape[1] + w_aaa_up.shape[1] + w_gate_up.shape[1]
    lay = _layout(d, rw, ret_heads * dk, ret_heads * dv, lora_w, ret_heads)
    heads, hn = r_k.shape[1:]
    assert hn == LANES // 2 and heads * hn == rw
    alpha = (2.0 * depth) ** 0.25
    f2 = w_up.shape[-1]

    xp = x_prompt.reshape(nbp * seq_p, d)
    xs = x_sample.reshape(nbs * seq_s, d)
    c_all = jnp.concatenate([c_prompt, c_sample], axis=0)
    pad = (-c_all.shape[0]) % SUBLANES
    c_all = _bf(jnp.pad(c_all, ((0, pad), (0, 0))))

    outs_p, outs_s = [], []
    for l in range(depth):
        w_perm, wd, wa, wg, pvec, mul = _prep_weights(
            lay, w_in[l], shift_mu[l], w0[l], w_decay_up[l], a0[l], w_aaa_up[l], w_gate_up[l],
            k_k[l], k_a[l], r_k[l], lnx_w[l], lnx_b[l])
        cwt = jnp.concatenate([conv_w[l], conv_b[l][None, :]], axis=0)
        wts = (w_perm, wd, wa, wg, pvec, mul, ret_gn_w[l][None, :], ret_gn_b[l][None, :],
               _bf(w_branch_a[l]), _bf(w_branch_b[l]), _bf(w_out[l]), ln1_w[l][None, :], ln1_b[l][None, :],
               _bf(w_up[l]), cwt, _bf(w_down[l]), ln2_w[l][None, :], ln2_b[l][None, :])
        ada = _mm_bias(c_all, w_ada[l], b_ada[l][None, :])
        zero_states = (None, jnp.zeros((nbp, state_shift.shape[-1]), F32), None,
                       jnp.zeros((nbp, state_conv.shape[2], f2), F32))
        xp, *st_p = _run_layer(xp, nbp, seq_p, ada[:nbp], zero_states, 0.0, lay, wts, alpha)
        xs, *st_s = _run_layer(xs, nbs, seq_s, ada[nbp:nbp + nbs],
                               (state_wkv[l], state_shift[l], state_ret[l], state_conv[l]),
                               float(PAST_LEN), lay, wts, alpha)
        outs_p.append(st_p)
        outs_s.append(st_s)

    def stack(lst, j, ref):
        layers = [s[j].astype(ref.dtype) for s in lst]
        return layers[0][None] if depth == 1 else jnp.stack(layers, axis=0)

    refs = (state_wkv, state_shift, state_ret, state_conv)
    return (xp.reshape(x_prompt.shape), xs.reshape(x_sample.shape),
            *[stack(outs_p, j, refs[j]) for j in range(4)],
            *[stack(outs_s, j, refs[j]) for j in range(4)])
```

```python
import functools
import math

import jax
import jax.numpy as jnp
from jax import lax
from jax.experimental import pallas as pl
from jax.experimental.pallas import tpu as pltpu

F32 = jnp.float32
BF16 = jnp.bfloat16

PAST_LEN = 16384
ROPE_BASE = 10000.0
RWKV_GN_EPS = 64e-5
RET_GN_EPS = 1e-5
LN_EPS = 1e-5
RET_CHUNK = 128

LANES = 128
SUBLANES = 8
MXU_WIDTH = 256
VMEM_LIMIT_BYTES = 56 * 1024 * 1024

RWKV_CHUNK = 64
RWKV_PAIRS_PER_STEP = 4


def _params(*sem):
    return pltpu.CompilerParams(dimension_semantics=sem, vmem_limit_bytes=VMEM_LIMIT_BYTES)


def _dot(a, b):
    return jnp.dot(a, b, preferred_element_type=F32)


def _dot_nt(a, b):
    return lax.dot_general(a, b, (((1,), (1,)), ((), ())), preferred_element_type=F32)


def _dot_tn(a, b):
    return lax.dot_general(a, b, (((0,), (0,)), ((), ())), preferred_element_type=F32)


def _bf(x):
    return x.astype(BF16)


def _split2(x):
    hi = x.astype(BF16)
    lo = (x - hi.astype(F32)).astype(BF16)
    return hi, lo


def _dot3_nn(a, b):
    ah, al = _split2(a)
    bh, bl = _split2(b)
    return _dot(jnp.concatenate([ah, ah, al], axis=1), jnp.concatenate([bh, bl, bh], axis=0))


def _dot3_nt(a, b):
    ah, al = _split2(a)
    bh, bl = _split2(b)
    return _dot_nt(jnp.concatenate([ah, ah, al], axis=1), jnp.concatenate([bh, bl, bh], axis=1))


def _dot_exact_lhs(a_bf, b):
    b1 = b.astype(BF16)
    r1 = b - b1.astype(F32)
    b2 = r1.astype(BF16)
    b3 = (r1 - b2.astype(F32)).astype(BF16)
    return _dot(jnp.concatenate([a_bf, a_bf, a_bf], axis=1), jnp.concatenate([b1, b2, b3], axis=0))


def _layer_norm_rows(t, w, b):
    mu = jnp.mean(t, axis=-1, keepdims=True)
    d = t - mu
    var = jnp.mean(d * d, axis=-1, keepdims=True)
    return d * lax.rsqrt(var + LN_EPS) * w + b


def _pick_tile(n, candidates):
    for c in candidates:
        if n % c == 0:
            return c
    return n


def _mm_bias_kernel(x_ref, w_ref, b_ref, o_ref):
    o_ref[...] = _dot(x_ref[...], _bf(w_ref[...])) + b_ref[...]


def _mm_bias(x_bf, w, b_row):
    m, k = x_bf.shape
    n = w.shape[1]
    tn = _pick_tile(n, (1024, 512, 256, 128))
    return pl.pallas_call(
        _mm_bias_kernel,
        grid=(n // tn,),
        in_specs=[pl.BlockSpec((m, k), lambda j: (0, 0)),
                  pl.BlockSpec((k, tn), lambda j: (0, j)),
                  pl.BlockSpec((1, tn), lambda j: (0, j))],
        out_specs=pl.BlockSpec((m, tn), lambda j: (0, j)),
        out_shape=jax.ShapeDtypeStruct((m, n), F32),
        compiler_params=_params("arbitrary"),
    )(x_bf, w, b_row)


def _modmm_kernel(x_ref, sc_ref, sh_ref, wt_ref, o_ref, u_ref):
    @pl.when(pl.program_id(1) == 0)
    def _():
        u_ref[...] = _bf(x_ref[...] * (1.0 + sc_ref[0]) + sh_ref[0])

    o_ref[...] = _dot_nt(u_ref[...], wt_ref[...]).astype(o_ref.dtype)


def _modmm(x, sc, sh, wt_bf, tm, tpg):
    m, d = x.shape
    n = wt_bf.shape[0]
    tn = _pick_tile(n, (512, 256, 128))
    r = sc.shape[1]
    mod_spec = pl.BlockSpec((1, r, d), lambda i, j: (i // tpg, 0, 0))
    return pl.pallas_call(
        _modmm_kernel,
        grid=(m // tm, n // tn),
        in_specs=[pl.BlockSpec((tm, d), lambda i, j: (i, 0)), mod_spec, mod_spec,
                  pl.BlockSpec((tn, d), lambda i, j: (j, 0))],
        out_specs=pl.BlockSpec((tm, tn), lambda i, j: (i, j)),
        out_shape=jax.ShapeDtypeStruct((m, n), BF16),
        scratch_shapes=[pltpu.VMEM((tm, d), BF16)],
        compiler_params=_params("arbitrary", "arbitrary"),
    )(x, sc, sh, wt_bf)


_PV_MU_R, _PV_MU_K, _PV_MU_V, _PV_W0, _PV_A0, _PV_KK, _PV_KA, _PV_RK, _PV_LNW, _PV_LNB = range(10)
_PV_ROWS = 16


def _head_half_mask(shape):
    return lax.broadcasted_iota(jnp.int32, shape, 1) < (LANES // 2)


def _head_sums(x):
    h0 = _head_half_mask((x.shape[0], LANES))
    parts = []
    for p in range(x.shape[1] // LANES):
        xs = x[:, p * LANES:(p + 1) * LANES]
        s0 = jnp.sum(jnp.where(h0, xs, 0.0), axis=-1, keepdims=True)
        s1 = jnp.sum(jnp.where(h0, 0.0, xs), axis=-1, keepdims=True)
        parts.append(jnp.where(h0, s0, s1))
    return parts[0] if len(parts) == 1 else jnp.concatenate(parts, axis=1)


def _rwkv_tokens(zr, zk, zv, zl, pr, pk, pv_, plr, pvec, mul, wd, wa, wg):
    row = lambda i: pvec[i:i + 1, :]
    r = zr + row(_PV_MU_R) * (pr - zr)
    k = zk + row(_PV_MU_K) * (pk - zk)
    v = zv + row(_PV_MU_V) * (pv_ - zv)
    ls = zl + mul * (plr - zl)
    wl = _dot(_bf(jnp.tanh(ls)), wd)
    al = _dot(_bf(ls), wa)
    g = _dot(_bf(jax.nn.sigmoid(ls)), wg)
    w = -jax.nn.softplus(-(row(_PV_W0) + wl)) - 0.5
    logd = -jnp.exp(w)
    a = jax.nn.sigmoid(row(_PV_A0) + al)
    kkr = k * row(_PV_KK)
    nrm = jnp.sqrt(_head_sums(kkr * kkr))
    kk = kkr / jnp.maximum(nrm, 1e-12)
    kp = k * (1.0 + (a - 1.0) * row(_PV_KA))
    bonus = _head_sums(r * kp * row(_PV_RK)) * v
    return r, kp, v, kk, a, logd, g, bonus


def _rwkv_finish(o, bonus, g, pvec):
    inv_n = 1.0 / (LANES // 2)
    mu = _head_sums(o) * inv_n
    d = o - mu
    var = _head_sums(d * d) * inv_n
    on = d * lax.rsqrt(var + RWKV_GN_EPS)
    return (on * pvec[_PV_LNW:_PV_LNW + 1, :] + pvec[_PV_LNB:_PV_LNB + 1, :] + bonus) * g


def _shift_rows(z, first_row):
    rolled = pltpu.roll(z, 1, 0)
    rowid = lax.broadcasted_iota(jnp.int32, z.shape, 0)
    return jnp.where(rowid == 0, first_row, rolled)


def _rwkv_seq_kernel(zr_ref, zk_ref, zv_ref, zl_ref, fr_ref, fk_ref, fv_ref, fl_ref,
                     pvec_ref, mul_ref, wd_ref, wa_ref, wg_ref, tri_ref,
                     ya_ref, wkv_ref, s_ref, cr_ref, ck_ref, cv_ref, cl_ref):
    i = pl.program_id(2)
    tc, width = zr_ref.shape
    npp = width // LANES
    c = RWKV_CHUNK
    nch = tc // c
    half = LANES // 2

    @pl.when(i == 0)
    def _():
        s_ref[...] = jnp.zeros_like(s_ref)
        cr_ref[0:1, :] = fr_ref[0]
        ck_ref[0:1, :] = fk_ref[0]
        cv_ref[0:1, :] = fv_ref[0]
        cl_ref[0:1, :] = fl_ref[0]

    zr, zk, zv, zl = (ref[...].astype(F32) for ref in (zr_ref, zk_ref, zv_ref, zl_ref))
    pvec = pvec_ref[...]
    r, kp, v, kk, a, logd, g, bonus = _rwkv_tokens(
        zr, zk, zv, zl,
        _shift_rows(zr, cr_ref[0:1, :]), _shift_rows(zk, ck_ref[0:1, :]),
        _shift_rows(zv, cv_ref[0:1, :]), _shift_rows(zl, cl_ref[0:1, :]),
        pvec, mul_ref[...], wd_ref[...], wa_ref[...], wg_ref[...])
    cr_ref[0:1, :] = zr[tc - 1:tc, :]
    ck_ref[0:1, :] = zk[tc - 1:tc, :]
    cv_ref[0:1, :] = zv[tc - 1:tc, :]
    cl_ref[0:1, :] = zl[tc - 1:tc, :]

    alpha = -kk
    beta = kk * a
    cum_incl = _dot_exact_lhs(tri_ref[...], logd)
    cum_excl = cum_incl - logd
    tot_rows = [cum_incl[(ci + 1) * c - 1:(ci + 1) * c, :] for ci in range(nch)]
    tot = jnp.concatenate([jnp.broadcast_to(t, (c, width)) for t in tot_rows], axis=0)
    e_neg = jnp.exp(-cum_incl)
    e_hat = jnp.exp(tot - cum_incl)
    r_t = r * jnp.exp(cum_incl)
    a_t = alpha * jnp.exp(cum_excl)
    b_t = beta * e_neg
    k_t = kp * e_neg
    b_h = beta * e_hat
    k_h = kp * e_hat

    h0 = _head_half_mask((c, LANES))

    def stack(x, p, ci):
        xb = x[ci * c:(ci + 1) * c, p * LANES:(p + 1) * LANES]
        return jnp.concatenate([jnp.where(h0, xb, 0.0), jnp.where(h0, 0.0, xb)], axis=0)

    rr = lax.broadcasted_iota(jnp.int32, (2 * c, 2 * c), 0)
    cc = lax.broadcasted_iota(jnp.int32, (2 * c, 2 * c), 1)
    same = (rr >= c) == (cc >= c)
    strict = same & (cc < rr)
    incl = same & (cc <= rr)
    eye = rr == cc
    zeros_blk = jnp.zeros((2 * c, LANES), BF16)

    probs = [(p, ci) for p in range(npp) for ci in range(nch)]
    a_s = {q: stack(a_t, *q) for q in probs}
    r_s = {q: stack(r_t, *q) for q in probs}
    v_bf = {q: _bf(stack(v, *q)) for q in probs}

    pmat, a_ak, a_r = {}, {}, {}
    for q in probs:
        amat = _dot_nt(_bf(jnp.concatenate([a_s[q], r_s[q]], axis=0)),
                       _bf(jnp.concatenate([stack(b_t, *q), stack(k_t, *q)], axis=0)))
        pmat[q] = jnp.where(strict, amat[:2 * c, :2 * c], 0.0)
        a_ak[q] = _bf(jnp.where(strict, amat[:2 * c, 2 * c:], 0.0))
        a_r[q] = _bf(jnp.concatenate([jnp.where(incl, amat[2 * c:, :2 * c], 0.0),
                                      jnp.where(incl, amat[2 * c:, 2 * c:], 0.0)], axis=1))

    x = {q: jnp.concatenate([a_s[q], _dot(a_ak[q], v_bf[q])], axis=1) for q in probs}

    nsteps = int(math.log2(c))
    for it in range(nsteps):
        for q in probs:
            p_bf = _bf(pmat[q])
            x[q] = x[q] + _dot(p_bf, _bf(x[q]))
            if it + 1 < nsteps:
                pmat[q] = _dot(p_bf, p_bf)

    r_pair, o_pair, g_t, h_t = {}, {}, {}, {}
    for q in probs:
        p, ci = q
        x_bf = _bf(x[q])
        rhs = jnp.concatenate([x_bf, jnp.concatenate([zeros_blk, v_bf[q]], axis=1)], axis=0)
        y = _dot(a_r[q], rhs)
        r_hat = r_s[q] + y[:, :LANES]
        r_pair[q] = r_hat[:c] + r_hat[c:]
        o_pair[q] = y[:c, LANES:] + y[c:, LANES:]
        z = _dot_tn(x_bf, _bf(stack(b_h, *q)))
        w_c = jnp.exp(tot_rows[ci][:, p * LANES:(p + 1) * LANES])
        g_t[q] = jnp.where(eye, w_c, 0.0) + z[:LANES]
        h_t[q] = z[LANES:] + _dot_tn(v_bf[q], _bf(stack(k_h, *q)))

    outs = {}
    for ci in range(nch):
        for p in range(npp):
            q = (p, ci)
            s0 = s_ref[p]
            outs[q] = _dot3_nt(r_pair[q], s0) + o_pair[q]
            s_ref[p] = _dot3_nn(s0, g_t[q]) + h_t[q]

    cols = [jnp.concatenate([outs[(p, ci)] for ci in range(nch)], axis=0) if nch > 1 else outs[(p, 0)]
            for p in range(npp)]
    o = jnp.concatenate(cols, axis=1) if npp > 1 else cols[0]
    ya_ref[...] = _bf(_rwkv_finish(o, bonus, g, pvec))

    @pl.when(i == pl.num_programs(2) - 1)
    def _():
        for p in range(npp):
            s = s_ref[p]
            wkv_ref[0, 2 * p] = s[:half, :half]
            wkv_ref[0, 2 * p + 1] = s[half:, half:]


def _rwkv_seq(proj, nb, seq, lay, first, pvec, mul, wd, wa, wg):
    rw, lslot = lay["rw"], lay["lslot"]
    npair = rw // LANES
    npp = RWKV_PAIRS_PER_STEP if npair % RWKV_PAIRS_PER_STEP == 0 else 1
    ngrp = npair // npp
    width = npp * LANES
    c = RWKV_CHUNK
    tc = _pick_tile(seq, (2 * c, c))
    nt = seq // tc
    t_idx = jnp.arange(tc)
    tri = ((t_idx[:, None] // c == t_idx[None, :] // c) & (t_idx[None, :] <= t_idx[:, None])).astype(BF16)
    fr, fk, fv, fl = first
    col = lambda s: pl.BlockSpec((tc, width), lambda b, gp, i, s=s: (b * nt + i, s * ngrp + gp))
    fcol = lambda s: pl.BlockSpec((1, 1, width), lambda b, gp, i, s=s: (b, 0, s * ngrp + gp))
    wspec = pl.BlockSpec((lslot, width), lambda b, gp, i: (0, gp))
    return pl.pallas_call(
        _rwkv_seq_kernel,
        grid=(nb, ngrp, nt),
        in_specs=[col(0), col(1), col(2),
                  pl.BlockSpec((tc, lslot), lambda b, gp, i: (b * nt + i, lay["off_l"] // lslot)),
                  fcol(0), fcol(1), fcol(2),
                  pl.BlockSpec((1, 1, lslot), lambda b, gp, i: (b, 0, 0)),
                  pl.BlockSpec((_PV_ROWS, width), lambda b, gp, i: (0, gp)),
                  pl.BlockSpec((1, lslot), lambda b, gp, i: (0, 0)),
                  wspec, wspec, wspec,
                  pl.BlockSpec((tc, tc), lambda b, gp, i: (0, 0))],
        out_specs=[pl.BlockSpec((tc, width), lambda b, gp, i: (b * nt + i, gp)),
                   pl.BlockSpec((1, 2 * npp, LANES // 2, LANES // 2), lambda b, gp, i: (b, gp, 0, 0))],
        out_shape=[jax.ShapeDtypeStruct((nb * seq, rw), BF16),
                   jax.ShapeDtypeStruct((nb, 2 * npair, LANES // 2, LANES // 2), F32)],
        scratch_shapes=[pltpu.VMEM((npp, LANES, LANES), F32),
                        pltpu.VMEM((SUBLANES, width), F32), pltpu.VMEM((SUBLANES, width), F32),
                        pltpu.VMEM((SUBLANES, width), F32), pltpu.VMEM((SUBLANES, lslot), F32)],
        compiler_params=_params("arbitrary", "arbitrary", "arbitrary"),
    )(proj, proj, proj, proj, fr, fk, fv, fl, pvec, mul, wd, wa, wg, tri)


def _rwkv_step_kernel(zr_ref, zk_ref, zv_ref, zl_ref, pr_ref, pk_ref, pv_ref, plr_ref,
                      pvec_ref, mul_ref, wd_ref, wa_ref, wg_ref, s_ref,
                      ya_ref, snew_ref, o_s):
    half = LANES // 2
    pvec = pvec_ref[...]
    r, kp, v, kk, a, logd, g, bonus = _rwkv_tokens(
        zr_ref[...].astype(F32), zk_ref[...].astype(F32), zv_ref[...].astype(F32),
        zl_ref[...].astype(F32), pr_ref[...], pk_ref[...], pv_ref[...], plr_ref[...],
        pvec, mul_ref[...], wd_ref[...], wa_ref[...], wg_ref[...])
    w = jnp.exp(logd)
    nkk_t, wr_t, w_t, beta_t, kp_t, r_t, v_t = (jnp.transpose(x) for x in (-kk, w * r, w, kk * a, kp, r, v))
    for e in range(2):
        ks = slice(e * half, (e + 1) * half)
        nkk_e, wr_e, w_e, beta_e, kp_e = nkk_t[ks], wr_t[ks], w_t[ks], beta_t[ks], kp_t[ks]
        c_beta = jnp.sum(beta_e * r_t[ks], axis=0, keepdims=True)
        c_k = jnp.sum(kp_e * r_t[ks], axis=0, keepdims=True)
        for vi in range(half):
            row = e * half + vi
            s = s_ref[e, vi]
            sa = jnp.sum(s * nkk_e, axis=0, keepdims=True)
            sw = jnp.sum(s * wr_e, axis=0, keepdims=True)
            v_row = v_t[row:row + 1, :]
            snew_ref[e, vi] = s * w_e + sa * beta_e + v_row * kp_e
            o_s[row:row + 1, :] = sw + sa * c_beta + v_row * c_k
    ya_ref[...] = _bf(_rwkv_finish(jnp.transpose(o_s[...]), bonus, g, pvec))


def _rwkv_step(proj, nb, lay, prev, pvec, mul, wd, wa, wg, s_wkv):
    rw, lslot = lay["rw"], lay["lslot"]
    npair = rw // LANES
    pm, plr = prev
    half = LANES // 2
    s_t = jnp.transpose(s_wkv, (1, 2, 3, 0))
    col = lambda off: pl.BlockSpec((nb, LANES), lambda p, off=off: (0, off + p))
    wspec = pl.BlockSpec((lslot, LANES), lambda p: (0, p))
    sspec = pl.BlockSpec((2, half, half, nb), lambda p: (p, 0, 0, 0))
    ya, snew_t = pl.pallas_call(
        _rwkv_step_kernel,
        grid=(npair,),
        in_specs=[col(0), col(npair), col(2 * npair),
                  pl.BlockSpec((nb, lslot), lambda p: (0, lay["off_l"] // lslot)),
                  col(0), col(npair), col(2 * npair),
                  pl.BlockSpec((nb, lslot), lambda p: (0, 0)),
                  pl.BlockSpec((_PV_ROWS, LANES), lambda p: (0, p)),
                  pl.BlockSpec((1, lslot), lambda p: (0, 0)),
                  wspec, wspec, wspec, sspec],
        out_specs=[pl.BlockSpec((nb, LANES), lambda p: (0, p)), sspec],
        out_shape=[jax.ShapeDtypeStruct((nb, rw), BF16), jax.ShapeDtypeStruct(s_t.shape, F32)],
        scratch_shapes=[pltpu.VMEM((LANES, nb), F32)],
        compiler_params=_params("arbitrary"),
    )(proj, proj, proj, proj, pm, pm, pm, plr, pvec, mul, wd, wa, wg, s_t)
    return ya, jnp.transpose(snew_t, (3, 0, 1, 2))


def _rope_rows(t, cos2, sin2):
    return t * cos2 + pltpu.roll(t, LANES // 2, 1) * sin2


def _head_norm_rows(o, eps):
    mu = jnp.mean(o, axis=-1, keepdims=True)
    d = o - mu
    var = jnp.mean(d * d, axis=-1, keepdims=True)
    return d * lax.rsqrt(var + eps)


def _ret_seq_kernel(q_ref, k_ref, v_ref, g_ref, cos_ref, sin_ref, intra_ref, qd_ref, kd_ref,
                    blk_ref, gnw_ref, gnb_ref, yb_ref, ret_ref, s_ref):
    i = pl.program_id(1)
    nh = s_ref.shape[0]
    dk = s_ref.shape[1]

    @pl.when(i == 0)
    def _():
        s_ref[...] = jnp.zeros_like(s_ref)

    cos2, sin2 = cos_ref[...], sin_ref[...]
    for h in range(nh):
        hs = slice(h * LANES, (h + 1) * LANES)
        qh = _rope_rows(q_ref[:, hs].astype(F32), cos2, sin2)
        kh = _rope_rows(k_ref[:, hs].astype(F32), cos2, sin2) * (dk ** -0.5)
        vb = _bf(v_ref[:, hs])
        qb = _bf(qh)
        scores = _dot_nt(qb, _bf(kh)) * intra_ref[h]
        s0 = s_ref[h]
        o = _dot(_bf(scores), vb) + _dot(qb, _bf(s0)) * qd_ref[h]
        s_ref[h] = s0 * blk_ref[h] + _dot_tn(_bf(kh * kd_ref[h]), vb)
        on = _head_norm_rows(o, RET_GN_EPS)
        yb_ref[:, hs] = _bf((on * gnw_ref[:, hs] + gnb_ref[:, hs]) * jax.nn.silu(g_ref[:, hs].astype(F32)))

    @pl.when(i == pl.num_programs(1) - 1)
    def _():
        ret_ref[0] = s_ref[...]


def _ret_tables(nh, c):
    log_g = jnp.log1p(-jnp.exp2(-5.0 - jnp.arange(nh, dtype=F32)))
    i = jnp.arange(c, dtype=F32)
    rel = i[:, None] - i[None, :]
    intra = jnp.where(rel >= 0, jnp.exp(log_g[:, None, None] * jnp.maximum(rel, 0.0)), 0.0)
    q_decay = jnp.exp(log_g[:, None] * (i + 1.0))
    k_decay = jnp.exp(log_g[:, None] * (c - 1.0 - i))
    blk_decay = jnp.exp(log_g * c)
    return intra, q_decay, k_decay, blk_decay


def _rope_tables(pos, dk):
    half = dk // 2
    inv = ROPE_BASE ** (-jnp.arange(half, dtype=F32) / half)
    ang = pos[:, None] * inv[None, :]
    cos, sin = jnp.cos(ang), jnp.sin(ang)
    return jnp.concatenate([cos, cos], axis=-1), jnp.concatenate([-sin, sin], axis=-1)


def _ret_seq(proj, nb, seq, lay, gnw, gnb):
    qk, rv, nh = lay["qk"], lay["rv"], lay["ret_heads"]
    dk, dv = qk // nh, rv // nh
    assert dk == LANES and dv == LANES
    c = RET_CHUNK if seq % RET_CHUNK == 0 else seq
    assert c % SUBLANES == 0
    nt = seq // c
    intra, qd, kd, blk = _ret_tables(nh, c)
    qd = jnp.broadcast_to(qd[:, :, None], (nh, c, dv))
    kd = jnp.broadcast_to(kd[:, :, None], (nh, c, dk))
    blk = jnp.broadcast_to(blk[:, None, None], (nh, 1, dv))
    cos2, sin2 = _rope_tables(jnp.arange(seq, dtype=F32), dk)
    seg = lambda off, w: pl.BlockSpec((c, w), lambda b, i, off=off, w=w: (b * nt + i, off // w))
    full3 = lambda a: pl.BlockSpec(a.shape, lambda b, i: (0, 0, 0))
    return pl.pallas_call(
        _ret_seq_kernel,
        grid=(nb, nt),
        in_specs=[seg(lay["off_q"], qk), seg(lay["off_kr"], qk), seg(lay["off_vr"], rv),
                  seg(lay["off_rg"], rv),
                  pl.BlockSpec((c, dk), lambda b, i: (i, 0)), pl.BlockSpec((c, dk), lambda b, i: (i, 0)),
                  full3(intra), full3(qd), full3(kd), full3(blk),
                  pl.BlockSpec((1, rv), lambda b, i: (0, 0)), pl.BlockSpec((1, rv), lambda b, i: (0, 0))],
        out_specs=[pl.BlockSpec((c, rv), lambda b, i: (b * nt + i, 0)),
                   pl.BlockSpec((1, nh, dk, dv), lambda b, i: (b, 0, 0, 0))],
        out_shape=[jax.ShapeDtypeStruct((nb * seq, rv), BF16),
                   jax.ShapeDtypeStruct((nb, nh, dk, dv), F32)],
        scratch_shapes=[pltpu.VMEM((nh, dk, dv), F32)],
        compiler_params=_params("arbitrary", "arbitrary"),
    )(proj, proj, proj, proj, cos2, sin2, intra, qd, kd, blk, gnw, gnb)


_RC_INTRA, _RC_QD, _RC_KD, _RC_BLK = range(4)


def _ret_step_kernel(q_ref, k_ref, v_ref, g_ref, cos_ref, sin_ref, rc_ref, gnw_ref, gnb_ref, s_ref,
                     yb_ref, snew_ref, q_s, k_s, v_s, o_s):
    bb = q_ref.shape[0]
    nh = s_ref.shape[1]
    dk = s_ref.shape[2]
    cos2, sin2 = cos_ref[...], sin_ref[...]
    for h in range(nh):
        hs = slice(h * LANES, (h + 1) * LANES)
        q_s[:, hs] = _rope_rows(q_ref[:, hs].astype(F32), cos2, sin2)
        k_s[:, hs] = _rope_rows(k_ref[:, hs].astype(F32), cos2, sin2) * (dk ** -0.5)
        v_s[:, hs] = v_ref[:, hs].astype(F32)
    eye = (lax.broadcasted_iota(jnp.int32, (LANES, LANES), 0)
           == lax.broadcasted_iota(jnp.int32, (LANES, LANES), 1))
    for b in range(bb):
        for h in range(nh):
            hs = slice(h * LANES, (h + 1) * LANES)
            rc = lambda j: rc_ref[h, j:j + 1, :]
            q_row = q_s[b:b + 1, hs]
            k_row = k_s[b:b + 1, hs]
            v_row = v_s[b:b + 1, hs]
            s0 = s_ref[b, h]
            q_col = jnp.sum(jnp.where(eye, q_row, 0.0), axis=-1, keepdims=True)
            k_col = jnp.sum(jnp.where(eye, k_row, 0.0), axis=-1, keepdims=True)
            score = jnp.sum(q_row * k_row, axis=-1, keepdims=True) * rc(_RC_INTRA)
            o_row = score * v_row + jnp.sum(s0 * q_col, axis=0, keepdims=True) * rc(_RC_QD)
            snew_ref[b, h] = s0 * rc(_RC_BLK) + (k_col * rc(_RC_KD)) * v_row
            o_s[b:b + 1, hs] = o_row
    for h in range(nh):
        hs = slice(h * LANES, (h + 1) * LANES)
        on = _head_norm_rows(o_s[:, hs], RET_GN_EPS)
        yb_ref[:, hs] = _bf((on * gnw_ref[:, hs] + gnb_ref[:, hs]) * jax.nn.silu(g_ref[:, hs].astype(F32)))


def _ret_step(proj, nb, lay, gnw, gnb, s_ret, pos0):
    qk, rv, nh = lay["qk"], lay["rv"], lay["ret_heads"]
    dk, dv = qk // nh, rv // nh
    assert dk == LANES and dv == LANES
    bb = 2 * SUBLANES
    assert nb % bb == 0
    intra, qd, kd, blk = _ret_tables(nh, 1)
    rc = jnp.stack([intra[:, 0, 0], qd[:, 0], kd[:, 0], blk], axis=1)
    rc = jnp.pad(rc, ((0, 0), (0, SUBLANES - 4)))
    rc = jnp.broadcast_to(rc[:, :, None], (nh, SUBLANES, LANES))
    cos2, sin2 = _rope_tables(jnp.asarray([pos0], dtype=F32), dk)
    seg = lambda off, w: pl.BlockSpec((bb, w), lambda j, off=off, w=w: (j, off // w))
    sspec = pl.BlockSpec((bb, nh, dk, dv), lambda j: (j, 0, 0, 0))
    row = lambda w: pl.BlockSpec((1, w), lambda j: (0, 0))
    return pl.pallas_call(
        _ret_step_kernel,
        grid=(nb // bb,),
        in_specs=[seg(lay["off_q"], qk), seg(lay["off_kr"], qk), seg(lay["off_vr"], rv),
                  seg(lay["off_rg"], rv), row(dk), row(dk),
                  pl.BlockSpec(rc.shape, lambda j: (0, 0, 0)), row(rv), row(rv), sspec],
        out_specs=[pl.BlockSpec((bb, rv), lambda j: (j, 0)), sspec],
        out_shape=[jax.ShapeDtypeStruct((nb, rv), BF16), jax.ShapeDtypeStruct(s_ret.shape, F32)],
        scratch_shapes=[pltpu.VMEM((bb, qk), F32), pltpu.VMEM((bb, qk), F32), pltpu.VMEM((bb, rv), F32),
                        pltpu.VMEM((bb, rv), F32)],
        compiler_params=_params("arbitrary"),
    )(proj, proj, proj, proj, cos2, sin2, rc, gnw, gnb, s_ret)


def _merge_kernel(alpha, ng, *refs):
    ya_ref, yb_ref = refs[0], refs[1]
    ga_refs = refs[2:2 + ng]
    gb_refs = refs[2 + ng:2 + 2 * ng]
    (x_ref, g1_ref, sc2_ref, sh2_ref, wa_ref, wb_ref, wo_ref, lnw_ref, lnb_ref,
     x1_ref, u2_ref) = refs[2 + 2 * ng:]
    cat = lambda rs: jnp.concatenate([r[...].astype(F32) for r in rs], axis=1)
    merged = (jax.nn.sigmoid(cat(ga_refs)) * _dot(ya_ref[...], wa_ref[...])
              + jax.nn.sigmoid(cat(gb_refs)) * _dot(yb_ref[...], wb_ref[...]))
    t = alpha * x_ref[...] + g1_ref[0] * _dot(_bf(merged), wo_ref[...])
    x1 = _layer_norm_rows(t, lnw_ref[...], lnb_ref[...])
    x1_ref[...] = x1
    u2_ref[...] = _bf(x1 * (1.0 + sc2_ref[0]) + sh2_ref[0])


def _merge(ya, yb, proj, x, g1, sc2, sh2, wba, wbb, wout, lnw, lnb, lay, alpha, tm, tpg):
    m, d = x.shape
    rw, rv = ya.shape[1], yb.shape[1]
    gw = math.gcd(lay["off_ga"], d)
    ng = d // gw
    r = g1.shape[1]
    rowt = lambda w: pl.BlockSpec((tm, w), lambda i: (i, 0))
    gate = lambda off, q: pl.BlockSpec((tm, gw), lambda i, off=off, q=q: (i, off // gw + q))
    mod = pl.BlockSpec((1, r, d), lambda i: (i // tpg, 0, 0))
    const = lambda a: pl.BlockSpec(a.shape, lambda i: (0, 0), pipeline_mode=pl.Buffered(1))
    return pl.pallas_call(
        functools.partial(_merge_kernel, alpha, ng),
        grid=(m // tm,),
        in_specs=[rowt(rw), rowt(rv)]
                 + [gate(lay["off_ga"], q) for q in range(ng)]
                 + [gate(lay["off_gb"], q) for q in range(ng)]
                 + [rowt(d), mod, mod, mod, const(wba), const(wbb), const(wout), const(lnw), const(lnb)],
        out_specs=[rowt(d), rowt(d)],
        out_shape=[jax.ShapeDtypeStruct((m, d), F32), jax.ShapeDtypeStruct((m, d), BF16)],
        compiler_params=_params("arbitrary"),
    )(ya, yb, *([proj] * (2 * ng)), x, g1, sc2, sh2, wba, wbb, wout, lnw, lnb)


def _ffn_seq_kernel(tps, nj, alpha, u_ref, wa_ref, wb_ref, cwa_ref, cwb_ref, ia_ref, ib_ref, wd_ref,
                    x1_ref, g2_ref, lnw_ref, lnb_ref, o_ref, ta_ref, tb_ref,
                    ha_s, hb_s, ca_s, cb_s, act_s, acc_s):
    i, j = pl.program_id(0), pl.program_id(1)
    tm = u_ref.shape[0]

    @pl.when((i == 0) & (j == 0))
    def _():
        acc_s[...] = jnp.zeros_like(acc_s)
        act_s[...] = jnp.zeros_like(act_s)

    jc = jnp.minimum(j, nj - 1)
    first = i % tps == 0
    u = u_ref[...]
    halves, tails = [], []
    for w_ref, cw_ref, init_ref, h_s, c_s in ((wa_ref, cwa_ref, ia_ref, ha_s, ca_s),
                                               (wb_ref, cwb_ref, ib_ref, hb_s, cb_s)):
        h = _dot(u, w_ref[...])
        h_s[0:SUBLANES, :] = jnp.where(first, init_ref[0], c_s[jc])
        h_s[SUBLANES:SUBLANES + tm, :] = h
        tail = h[tm - SUBLANES:tm, :]
        c_s[jc] = tail
        tails.append(tail)
        cw = cw_ref[...]
        halves.append(cw[3:4, :] + cw[0:1, :] * h_s[SUBLANES - 2:SUBLANES - 2 + tm, :]
                      + cw[1:2, :] * h_s[SUBLANES - 1:SUBLANES - 1 + tm, :] + cw[2:3, :] * h)
    acc_s[...] = jnp.where(j == 0, 0.0, acc_s[...] + _dot(act_s[(j + 1) % 2], wd_ref[...]))
    act_s[j % 2] = _bf(jax.nn.silu(halves[0]) * halves[1])

    @pl.when((i % tps == tps - 1) & (j < nj))
    def _():
        ta_ref[i // tps, j] = tails[0]
        tb_ref[i // tps, j] = tails[1]

    @pl.when(j == nj)
    def _():
        t = alpha * x1_ref[...] + g2_ref[0] * acc_s[...]
        o_ref[...] = _layer_norm_rows(t, lnw_ref[...], lnb_ref[...])


def _ffn_up_step_kernel(u_ref, wa_ref, wb_ref, cwa_ref, cwb_ref, p1a_ref, p2a_ref, p1b_ref, p2b_ref,
                        act_ref, ha_ref, hb_ref):
    u = u_ref[...]
    halves = []
    for w_ref, cw_ref, p1_ref, p2_ref, h_ref in ((wa_ref, cwa_ref, p1a_ref, p2a_ref, ha_ref),
                                                 (wb_ref, cwb_ref, p1b_ref, p2b_ref, hb_ref)):
        h = _dot(u, w_ref[...])
        h_ref[...] = h
        cw = cw_ref[...]
        halves.append(cw[3:4, :] + cw[0:1, :] * p2_ref[...] + cw[1:2, :] * p1_ref[...] + cw[2:3, :] * h)
    act_ref[...] = _bf(jax.nn.silu(halves[0]) * halves[1])


def _ffn_up(u2, w_up_bf, cwt, nb, seq, s_conv, tm):
    m, d = u2.shape
    f2 = w_up_bf.shape[1]
    f = f2 // 2
    tn = _pick_tile(f, (512, 256, 128))
    nj = f // tn
    w_a = pl.BlockSpec((d, tn), lambda i, j: (0, j))
    w_b = pl.BlockSpec((d, tn), lambda i, j: (0, nj + j))
    cw_a = pl.BlockSpec((4, tn), lambda i, j: (0, j))
    cw_b = pl.BlockSpec((4, tn), lambda i, j: (0, nj + j))
    assert seq == 1
    p_a = pl.BlockSpec((tm, tn), lambda i, j: (i, j))
    p_b = pl.BlockSpec((tm, tn), lambda i, j: (i, nj + j))
    prev1, prev2 = s_conv[:, 1, :], s_conv[:, 0, :]
    act, h_a, h_b = pl.pallas_call(
        _ffn_up_step_kernel,
        grid=(m // tm, nj),
        in_specs=[pl.BlockSpec((tm, d), lambda i, j: (i, 0)), w_a, w_b, cw_a, cw_b, p_a, p_a, p_b, p_b],
        out_specs=[pl.BlockSpec((tm, tn), lambda i, j: (i, j))] * 3,
        out_shape=[jax.ShapeDtypeStruct((m, f), BF16), jax.ShapeDtypeStruct((m, f), F32),
                   jax.ShapeDtypeStruct((m, f), F32)],
        compiler_params=_params("arbitrary", "arbitrary"),
    )(u2, w_up_bf, w_up_bf, cwt, cwt, prev1, prev2, prev1, prev2)
    conv_new = jnp.stack([prev1, jnp.concatenate([h_a, h_b], axis=-1)], axis=1)
    return act, conv_new


def _ffn_seq(u2, w_up_bf, cwt, w_down_bf, x1, g2, lnw, lnb, alpha, nb, seq, s_conv, tm):
    m, d = u2.shape
    f2 = w_up_bf.shape[1]
    f = f2 // 2
    tn = _pick_tile(f, (512, 256, 128))
    nj = f // tn
    assert seq % tm == 0 and tm >= SUBLANES
    tps = seq // tm
    up = lambda i, j: jnp.minimum(j, nj - 1)
    w_a = pl.BlockSpec((d, tn), lambda i, j: (0, up(i, j)))
    w_b = pl.BlockSpec((d, tn), lambda i, j: (0, nj + up(i, j)))
    cw_a = pl.BlockSpec((4, tn), lambda i, j: (0, up(i, j)))
    cw_b = pl.BlockSpec((4, tn), lambda i, j: (0, nj + up(i, j)))
    init = jnp.pad(s_conv, ((0, 0), (SUBLANES - 2, 0), (0, 0)))
    i_a = pl.BlockSpec((1, SUBLANES, tn), lambda i, j: (i // tps, 0, up(i, j)))
    i_b = pl.BlockSpec((1, SUBLANES, tn), lambda i, j: (i // tps, 0, nj + up(i, j)))
    w_d = pl.BlockSpec((tn, d), lambda i, j: (jnp.maximum(j - 1, 0), 0))
    rowt = pl.BlockSpec((tm, d), lambda i, j: (i, 0))
    vec = pl.BlockSpec((1, d), lambda i, j: (0, 0))
    t_o = pl.BlockSpec((nb, nj, SUBLANES, tn), lambda i, j: (0, 0, 0, 0))
    tails = jax.ShapeDtypeStruct((nb, nj, SUBLANES, tn), F32)
    x2, t_a, t_b = pl.pallas_call(
        functools.partial(_ffn_seq_kernel, tps, nj, alpha),
        grid=(m // tm, nj + 1),
        in_specs=[rowt, w_a, w_b, cw_a, cw_b, i_a, i_b, w_d, rowt,
                  pl.BlockSpec((1, 1, d), lambda i, j: (i // tps, 0, 0)), vec, vec],
        out_specs=[rowt, t_o, t_o],
        out_shape=[jax.ShapeDtypeStruct((m, d), F32), tails, tails],
        scratch_shapes=[pltpu.VMEM((tm + SUBLANES, tn), F32), pltpu.VMEM((tm + SUBLANES, tn), F32),
                        pltpu.VMEM((nj, SUBLANES, tn), F32), pltpu.VMEM((nj, SUBLANES, tn), F32),
                        pltpu.VMEM((2, tm, tn), BF16), pltpu.VMEM((tm, d), F32)],
        compiler_params=_params("arbitrary", "arbitrary"),
    )(u2, w_up_bf, w_up_bf, cwt, cwt, init, init, w_down_bf, x1, g2, lnw, lnb)
    rows = lambda t: t[:, :, SUBLANES - 2:, :].transpose(0, 2, 1, 3).reshape(nb, 2, f)
    return x2, jnp.concatenate([rows(t_a), rows(t_b)], axis=-1)


def _ffn_down_kernel(alpha, act_ref, w_ref, x1_ref, g2_ref, lnw_ref, lnb_ref, o_ref, acc_ref):
    k = pl.program_id(1)

    @pl.when(k == 0)
    def _():
        acc_ref[...] = jnp.zeros_like(acc_ref)

    acc_ref[...] += _dot(act_ref[...], w_ref[...])

    @pl.when(k == pl.num_programs(1) - 1)
    def _():
        t = alpha * x1_ref[...] + g2_ref[0] * acc_ref[...]
        o_ref[...] = _layer_norm_rows(t, lnw_ref[...], lnb_ref[...])


def _ffn_down(act, w_down_bf, x1, g2, lnw, lnb, alpha, tm, tpg):
    m, f = act.shape
    d = w_down_bf.shape[1]
    tk = _pick_tile(f, (512, 256, 128))
    r = g2.shape[1]
    return pl.pallas_call(
        functools.partial(_ffn_down_kernel, alpha),
        grid=(m // tm, f // tk),
        in_specs=[pl.BlockSpec((tm, tk), lambda i, k: (i, k)),
                  pl.BlockSpec((tk, d), lambda i, k: (k, 0)),
                  pl.BlockSpec((tm, d), lambda i, k: (i, 0)),
                  pl.BlockSpec((1, r, d), lambda i, k: (i // tpg, 0, 0)),
                  pl.BlockSpec((1, d), lambda i, k: (0, 0)),
                  pl.BlockSpec((1, d), lambda i, k: (0, 0))],
        out_specs=pl.BlockSpec((tm, d), lambda i, k: (i, 0)),
        out_shape=jax.ShapeDtypeStruct((m, d), F32),
        scratch_shapes=[pltpu.VMEM((tm, d), F32)],
        compiler_params=_params("arbitrary", "arbitrary"),
    )(act, w_down_bf, x1, g2, lnw, lnb)


def _layout(d, rw, qk, rv, lora_w, ret_heads):
    lslot = next(s for s in (128, 256, 512, 1024, 2048) if s >= lora_w)
    off_q = 3 * rw
    off_kr = off_q + qk
    off_vr = off_kr + qk
    off_rg = off_vr + rv
    off_ga = off_rg + rv
    off_gb = off_ga + d
    off_l = off_gb + d
    assert rw % LANES == 0 and off_l % lslot == 0
    assert off_q % qk == 0 and off_kr % qk == 0 and off_vr % rv == 0 and off_rg % rv == 0
    return dict(d=d, rw=rw, qk=qk, rv=rv, lslot=lslot, lora_w=lora_w, ret_heads=ret_heads,
                off_q=off_q, off_kr=off_kr, off_vr=off_vr, off_rg=off_rg, off_ga=off_ga,
                off_gb=off_gb, off_l=off_l, nt=off_l + lslot)


def _prep_weights(lay, w_in, shift_mu, w0, w_decay_up, a0, w_aaa_up, w_gate_up, k_k, k_a, r_k,
                  lnx_w, lnx_b):
    rw, lslot, lora_w = lay["rw"], lay["lslot"], lay["lora_w"]
    p_rwkv = 3 * rw + lora_w
    d = w_in.shape[0]
    wt = jnp.transpose(w_in)
    w_perm = _bf(jnp.concatenate(
        [wt[:3 * rw], wt[p_rwkv:], wt[3 * rw:p_rwkv], jnp.zeros((lslot - lora_w, d), w_in.dtype)], axis=0))
    dl, al, gl = w_decay_up.shape[0], w_aaa_up.shape[0], w_gate_up.shape[0]
    pad_rows = lambda w, lo: jnp.pad(w, ((lo, lslot - lo - w.shape[0]), (0, 0))).astype(BF16)
    wd = pad_rows(w_decay_up, 0)
    wa = pad_rows(w_aaa_up, dl)
    wg = pad_rows(w_gate_up, dl + al)
    rows = [shift_mu[:rw], shift_mu[rw:2 * rw], shift_mu[2 * rw:3 * rw], w0, a0, k_k, k_a,
            r_k.reshape(-1), lnx_w, lnx_b]
    pvec = jnp.pad(jnp.stack(rows, axis=0), ((0, _PV_ROWS - len(rows)), (0, 0)))
    mul = jnp.pad(shift_mu[3 * rw:], (0, lslot - lora_w))[None, :]
    return w_perm, wd, wa, wg, pvec, mul


def _run_layer(x2d, nb, seq, ada, states, pos0, lay, wts, alpha):
    d, rw, lslot, lora_w = lay["d"], lay["rw"], lay["lslot"], lay["lora_w"]
    (w_perm, wd, wa, wg, pvec, mul, gnw, gnb, wba, wbb, wout, ln1w, ln1b,
     w_up, cwt, w_down, ln2w, ln2b) = wts
    m = nb * seq
    sh1, sc1, g1, sh2, sc2, g2 = jnp.split(ada, 6, axis=-1)
    if seq == 1:
        tm, tpg = m, 1
        shape_mod = lambda t: t[None]
    else:
        tm = _pick_tile(seq, (512, 256, 128, 64, 32, 16, 8))
        tpg = seq // tm
        shape_mod = lambda t: t[:, None, :]
    sh1, sc1, g1, sh2, sc2, g2 = map(shape_mod, (sh1, sc1, g1, sh2, sc2, g2))
    tm_in = _pick_tile(seq, (1024, 512, 256, 128, 64, 32, 16, 8)) if seq > 1 else m
    proj = _modmm(x2d, sc1, sh1, w_perm, tm_in, (seq // tm_in) if seq > 1 else 1)

    s_wkv, s_shift, s_ret, s_conv = states
    shift_main = s_shift[:, :3 * rw]
    shift_lora = jnp.pad(s_shift[:, 3 * rw:], ((0, 0), (0, lslot - lora_w)))
    if seq == 1:
        ya, wkv_new = _rwkv_step(proj, nb, lay, (shift_main, shift_lora), pvec, mul, wd, wa, wg, s_wkv)
        yb, ret_new = _ret_step(proj, nb, lay, gnw, gnb, s_ret, pos0)
    else:
        first = (shift_main[:, None, :], shift_main[:, None, :], shift_main[:, None, :],
                 shift_lora[:, None, :])
        ya, wkv_new = _rwkv_seq(proj, nb, seq, lay, first, pvec, mul, wd, wa, wg)
        yb, ret_new = _ret_seq(proj, nb, seq, lay, gnw, gnb)
    last = proj.reshape(nb, seq, lay["nt"])[:, -1, :]
    shift_new = jnp.concatenate([last[:, :3 * rw], last[:, lay["off_l"]:lay["off_l"] + lora_w]],
                                axis=-1).astype(F32)

    tm_merge = min(tm, 256)
    x1, u2 = _merge(ya, yb, proj, x2d, g1, sc2, sh2, wba, wbb, wout, ln1w, ln1b, lay, alpha,
                    tm_merge, (seq // tm_merge) if seq > 1 else 1)
    if seq == 1:
        act, conv_new = _ffn_up(u2, w_up, cwt, nb, seq, s_conv, tm)
        x2 = _ffn_down(act, w_down, x1, g2, ln2w, ln2b, alpha, tm, tpg)
    else:
        x2, conv_new = _ffn_seq(u2, w_up, cwt, w_down, x1, g2, ln2w, ln2b, alpha, nb, seq, s_conv, tm)
    return x2, wkv_new, shift_new, ret_new, conv_new


def kernel(x_prompt, x_sample, c_prompt, c_sample, state_wkv, state_shift, state_ret, state_conv, w_ada, b_ada, w_in, shift_mu, w0, w_decay_up, a0, w_aaa_up, w_gate_up, k_k, k_a, r_k, lnx_w, lnx_b, ret_gn_w, ret_gn_b, w_branch_a, w_branch_b, w_out, ln1_w, ln1_b, w_up, conv_w, conv_b, w_down, ln2_w, ln2_b):
    depth = w_ada.shape[0]
    nbp, seq_p, d = x_prompt.shape
    nbs, seq_s, _ = x_sample.shape
    assert seq_s == 1
    rw = k_k.shape[-1]
    ret_heads, dk, dv = state_ret.shape[2:]
    lora_w = w_decay_up.shape[1] + w_aaa_up.shape[1] + w_gate_up.shape[1]
    lay = _layout(d, rw, ret_heads * dk, ret_heads * dv, lora_w, ret_heads)
    heads, hn = r_k.shape[1:]
    assert hn == LANES // 2 and heads * hn == rw
    alpha = (2.0 * depth) ** 0.25
    f2 = w_up.shape[-1]

    xp = x_prompt.reshape(nbp * seq_p, d)
    xs = x_sample.reshape(nbs * seq_s, d)
    c_all = jnp.concatenate([c_prompt, c_sample], axis=0)
    pad = (-c_all.shape[0]) % SUBLANES
    c_all = _bf(jnp.pad(c_all, ((0, pad), (0, 0))))

    outs_p, outs_s = [], []
    for l in range(depth):
        w_perm, wd, wa, wg, pvec, mul = _prep_weights(
            lay, w_in[l], shift_mu[l], w0[l], w_decay_up[l], a0[l], w_aaa_up[l], w_gate_up[l],
            k_k[l], k_a[l], r_k[l], lnx_w[l], lnx_b[l])
        cwt = jnp.concatenate([conv_w[l], conv_b[l][None, :]], axis=0)
        wts = (w_perm, wd, wa, wg, pvec, mul, ret_gn_w[l][None, :], ret_gn_b[l][None, :],
               _bf(w_branch_a[l]), _bf(w_branch_b[l]), _bf(w_out[l]), ln1_w[l][None, :], ln1_b[l][None, :],
               _bf(w_up[l]), cwt, _bf(w_down[l]), ln2_w[l][None, :], ln2_b[l][None, :])
        ada = _mm_bias(c_all, w_ada[l], b_ada[l][None, :])
        zero_states = (None, jnp.zeros((nbp, state_shift.shape[-1]), F32), None,
                       jnp.zeros((nbp, state_conv.shape[2], f2), F32))
        xp, *st_p = _run_layer(xp, nbp, seq_p, ada[:nbp], zero_states, 0.0, lay, wts, alpha)
        xs, *st_s = _run_layer(xs, nbs, seq_s, ada[nbp:nbp + nbs],
                               (state_wkv[l], state_shift[l], state_ret[l], state_conv[l]),
                               float(PAST_LEN), lay, wts, alpha)
        outs_p.append(st_p)
        outs_s.append(st_s)

    def stack(lst, j, ref):
        layers = [s[j].astype(ref.dtype) for s in lst]
        return layers[0][None] if depth == 1 else jnp.stack(layers, axis=0)

    refs = (state_wkv, state_shift, state_ret, state_conv)
    return (xp.reshape(x_prompt.shape), xs.reshape(x_sample.shape),
            *[stack(outs_p, j, refs[j]) for j in range(4)],
            *[stack(outs_s, j, refs[j]) for j in range(4)])
```

```python
import functools
import math

import jax
import jax.numpy as jnp
from jax import lax
from jax.experimental import pallas as pl
from jax.experimental.pallas import tpu as pltpu

F32 = jnp.float32
BF16 = jnp.bfloat16

PAST_LEN = 16384
ROPE_BASE = 10000.0
RWKV_GN_EPS = 64e-5
RET_GN_EPS = 1e-5
LN_EPS = 1e-5
RET_CHUNK = 128

LANES = 128
SUBLANES = 8
MXU_WIDTH = 256
VMEM_LIMIT_BYTES = 56 * 1024 * 1024

RWKV_CHUNK = 64
RWKV_PAIRS_PER_STEP = 8


def _params(*sem):
    return pltpu.CompilerParams(dimension_semantics=sem, vmem_limit_bytes=VMEM_LIMIT_BYTES)


def _dot(a, b):
    return jnp.dot(a, b, preferred_element_type=F32)


def _dot_nt(a, b):
    return lax.dot_general(a, b, (((1,), (1,)), ((), ())), preferred_element_type=F32)


def _dot_tn(a, b):
    return lax.dot_general(a, b, (((0,), (0,)), ((), ())), preferred_element_type=F32)


def _bf(x):
    return x.astype(BF16)


def _split2(x):
    hi = x.astype(BF16)
    lo = (x - hi.astype(F32)).astype(BF16)
    return hi, lo


def _dot3_nn(a, b):
    ah, al = _split2(a)
    bh, bl = _split2(b)
    return _dot(jnp.concatenate([ah, ah, al], axis=1), jnp.concatenate([bh, bl, bh], axis=0))


def _dot3_nt(a, b):
    ah, al = _split2(a)
    bh, bl = _split2(b)
    return _dot_nt(jnp.concatenate([ah, ah, al], axis=1), jnp.concatenate([bh, bl, bh], axis=1))


def _dot_exact_lhs(a_bf, b):
    b1 = b.astype(BF16)
    r1 = b - b1.astype(F32)
    b2 = r1.astype(BF16)
    b3 = (r1 - b2.astype(F32)).astype(BF16)
    return _dot(jnp.concatenate([a_bf, a_bf, a_bf], axis=1), jnp.concatenate([b1, b2, b3], axis=0))


def _layer_norm_rows(t, w, b):
    mu = jnp.mean(t, axis=-1, keepdims=True)
    d = t - mu
    var = jnp.mean(d * d, axis=-1, keepdims=True)
    return d * lax.rsqrt(var + LN_EPS) * w + b


def _pick_tile(n, candidates):
    for c in candidates:
        if n % c == 0:
            return c
    return n


def _mm_bias_kernel(x_ref, w_ref, b_ref, o_ref):
    o_ref[...] = _dot(x_ref[...], _bf(w_ref[...])) + b_ref[...]


def _mm_bias(x_bf, w, b_row):
    m, k = x_bf.shape
    n = w.shape[1]
    tn = _pick_tile(n, (1024, 512, 256, 128))
    return pl.pallas_call(
        _mm_bias_kernel,
        grid=(n // tn,),
        in_specs=[pl.BlockSpec((m, k), lambda j: (0, 0)),
                  pl.BlockSpec((k, tn), lambda j: (0, j)),
                  pl.BlockSpec((1, tn), lambda j: (0, j))],
        out_specs=pl.BlockSpec((m, tn), lambda j: (0, j)),
        out_shape=jax.ShapeDtypeStruct((m, n), F32),
        compiler_params=_params("arbitrary"),
    )(x_bf, w, b_row)


def _modmm_kernel(x_ref, sc_ref, sh_ref, wt_ref, o_ref, u_ref):
    @pl.when(pl.program_id(1) == 0)
    def _():
        u_ref[...] = _bf(x_ref[...] * (1.0 + sc_ref[0]) + sh_ref[0])

    o_ref[...] = _dot_nt(u_ref[...], wt_ref[...]).astype(o_ref.dtype)


def _modmm(x, sc, sh, wt_bf, lay, tm, tpg):
    m, d = x.shape
    n = lay["nt"]
    tn = _pick_tile(n, (512, 256, 128))
    rw3, lora_w, lslot = 3 * lay["rw"], lay["lora_w"], lay["lslot"]
    n_main, n_rest = rw3 // tn, (lay["off_l"] - rw3) // tn
    assert rw3 % tn == 0 and (lay["off_l"] - rw3) % tn == 0 and lslot == tn
    assert rw3 + lslot <= wt_bf.shape[0]

    row_align = 2 * SUBLANES
    assert tn % row_align == 0 and rw3 % row_align == 0 and lora_w % row_align == 0

    def w_row(j):
        row = jnp.where(j < n_main, j * tn,
                        jnp.where(j < n_main + n_rest, rw3 + lora_w + (j - n_main) * tn, rw3))
        return pl.multiple_of(row, row_align)

    r = sc.shape[1]
    mod_spec = pl.BlockSpec((1, r, d), lambda i, j: (i // tpg, 0, 0))
    return pl.pallas_call(
        _modmm_kernel,
        grid=(m // tm, n // tn),
        in_specs=[pl.BlockSpec((tm, d), lambda i, j: (i, 0)), mod_spec, mod_spec,
                  pl.BlockSpec((pl.Element(tn), pl.Element(d)), lambda i, j: (w_row(j), 0))],
        out_specs=pl.BlockSpec((tm, tn), lambda i, j: (i, j)),
        out_shape=jax.ShapeDtypeStruct((m, n), BF16),
        scratch_shapes=[pltpu.VMEM((tm, d), BF16)],
        compiler_params=_params("arbitrary", "arbitrary"),
    )(x, sc, sh, wt_bf)


_PV_MU_R, _PV_MU_K, _PV_MU_V, _PV_W0, _PV_A0, _PV_KK, _PV_KA, _PV_RK, _PV_LNW, _PV_LNB = range(10)
_PV_ROWS = 16


def _head_half_mask(shape):
    return lax.broadcasted_iota(jnp.int32, shape, 1) < (LANES // 2)


def _head_sums(x):
    h0 = _head_half_mask((x.shape[0], LANES))
    parts = []
    for p in range(x.shape[1] // LANES):
        xs = x[:, p * LANES:(p + 1) * LANES]
        s0 = jnp.sum(jnp.where(h0, xs, 0.0), axis=-1, keepdims=True)
        s1 = jnp.sum(jnp.where(h0, 0.0, xs), axis=-1, keepdims=True)
        parts.append(jnp.where(h0, s0, s1))
    return parts[0] if len(parts) == 1 else jnp.concatenate(parts, axis=1)


def _rwkv_tokens(zr, zk, zv, zl, pr, pk, pv_, plr, pvec, mul, wd, wa, wg):
    row = lambda i: pvec[i:i + 1, :]
    r = zr + row(_PV_MU_R) * (pr - zr)
    k = zk + row(_PV_MU_K) * (pk - zk)
    v = zv + row(_PV_MU_V) * (pv_ - zv)
    ls = zl + mul * (plr - zl)
    wl = _dot(_bf(jnp.tanh(ls)), wd)
    al = _dot(_bf(ls), wa)
    g = _dot(_bf(jax.nn.sigmoid(ls)), wg)
    w = -jax.nn.softplus(-(row(_PV_W0) + wl)) - 0.5
    logd = -jnp.exp(w)
    a = jax.nn.sigmoid(row(_PV_A0) + al)
    kkr = k * row(_PV_KK)
    nrm = jnp.sqrt(_head_sums(kkr * kkr))
    kk = kkr / jnp.maximum(nrm, 1e-12)
    kp = k * (1.0 + (a - 1.0) * row(_PV_KA))
    bonus = _head_sums(r * kp * row(_PV_RK)) * v
    return r, kp, v, kk, a, logd, g, bonus


def _rwkv_finish(o, bonus, g, pvec):
    inv_n = 1.0 / (LANES // 2)
    mu = _head_sums(o) * inv_n
    d = o - mu
    var = _head_sums(d * d) * inv_n
    on = d * lax.rsqrt(var + RWKV_GN_EPS)
    return (on * pvec[_PV_LNW:_PV_LNW + 1, :] + pvec[_PV_LNB:_PV_LNB + 1, :] + bonus) * g


def _shift_rows(z, first_row):
    rolled = pltpu.roll(z, 1, 0)
    rowid = lax.broadcasted_iota(jnp.int32, z.shape, 0)
    return jnp.where(rowid == 0, first_row, rolled)


def _rwkv_seq_kernel(zr_ref, zk_ref, zv_ref, zl_ref, fr_ref, fk_ref, fv_ref, fl_ref,
                     pvec_ref, mul_ref, wd_ref, wa_ref, wg_ref, tri_ref,
                     ya_ref, wkv_ref, s_ref, cr_ref, ck_ref, cv_ref, cl_ref):
    i = pl.program_id(2)
    tc, width = zr_ref.shape
    npp = width // LANES
    c = RWKV_CHUNK
    nch = tc // c
    half = LANES // 2

    @pl.when(i == 0)
    def _():
        s_ref[...] = jnp.zeros_like(s_ref)
        cr_ref[0:1, :] = fr_ref[0]
        ck_ref[0:1, :] = fk_ref[0]
        cv_ref[0:1, :] = fv_ref[0]
        cl_ref[0:1, :] = fl_ref[0]

    zr, zk, zv, zl = (ref[...].astype(F32) for ref in (zr_ref, zk_ref, zv_ref, zl_ref))
    pvec = pvec_ref[...]
    r, kp, v, kk, a, logd, g, bonus = _rwkv_tokens(
        zr, zk, zv, zl,
        _shift_rows(zr, cr_ref[0:1, :]), _shift_rows(zk, ck_ref[0:1, :]),
        _shift_rows(zv, cv_ref[0:1, :]), _shift_rows(zl, cl_ref[0:1, :]),
        pvec, mul_ref[...], wd_ref[...], wa_ref[...], wg_ref[...])
    cr_ref[0:1, :] = zr[tc - 1:tc, :]
    ck_ref[0:1, :] = zk[tc - 1:tc, :]
    cv_ref[0:1, :] = zv[tc - 1:tc, :]
    cl_ref[0:1, :] = zl[tc - 1:tc, :]

    alpha = -kk
    beta = kk * a
    cum_incl = _dot_exact_lhs(tri_ref[...], logd)
    cum_excl = cum_incl - logd
    tot_rows = [cum_incl[(ci + 1) * c - 1:(ci + 1) * c, :] for ci in range(nch)]
    tot = jnp.concatenate([jnp.broadcast_to(t, (c, width)) for t in tot_rows], axis=0)
    e_neg = jnp.exp(-cum_incl)
    e_hat = jnp.exp(tot - cum_incl)
    r_t = r * jnp.exp(cum_incl)
    a_t = alpha * jnp.exp(cum_excl)
    b_t = beta * e_neg
    k_t = kp * e_neg
    b_h = beta * e_hat
    k_h = kp * e_hat

    h0 = _head_half_mask((c, LANES))

    def stack(x, p, ci):
        xb = x[ci * c:(ci + 1) * c, p * LANES:(p + 1) * LANES]
        return jnp.concatenate([jnp.where(h0, xb, 0.0), jnp.where(h0, 0.0, xb)], axis=0)

    rr = lax.broadcasted_iota(jnp.int32, (2 * c, 2 * c), 0)
    cc = lax.broadcasted_iota(jnp.int32, (2 * c, 2 * c), 1)
    same = (rr >= c) == (cc >= c)
    strict = same & (cc < rr)
    incl = same & (cc <= rr)
    eye = rr == cc
    zeros_blk = jnp.zeros((2 * c, LANES), BF16)

    probs = [(p, ci) for p in range(npp) for ci in range(nch)]
    a_s = {q: stack(a_t, *q) for q in probs}
    r_s = {q: stack(r_t, *q) for q in probs}
    v_bf = {q: _bf(stack(v, *q)) for q in probs}

    pmat, a_ak, a_r = {}, {}, {}
    for q in probs:
        amat = _dot_nt(_bf(jnp.concatenate([a_s[q], r_s[q]], axis=0)),
                       _bf(jnp.concatenate([stack(b_t, *q), stack(k_t, *q)], axis=0)))
        pmat[q] = jnp.where(strict, amat[:2 * c, :2 * c], 0.0)
        a_ak[q] = _bf(jnp.where(strict, amat[:2 * c, 2 * c:], 0.0))
        a_r[q] = _bf(jnp.concatenate([jnp.where(incl, amat[2 * c:, :2 * c], 0.0),
                                      jnp.where(incl, amat[2 * c:, 2 * c:], 0.0)], axis=1))

    x = {q: jnp.concatenate([a_s[q], _dot(a_ak[q], v_bf[q])], axis=1) for q in probs}

    nsteps = int(math.log2(c))
    for it in range(nsteps):
        for q in probs:
            p_bf = _bf(pmat[q])
            x[q] = x[q] + _dot(p_bf, _bf(x[q]))
            if it + 1 < nsteps:
                pmat[q] = _dot(p_bf, p_bf)

    r_pair, o_pair, g_t, h_t = {}, {}, {}, {}
    for q in probs:
        p, ci = q
        x_bf = _bf(x[q])
        rhs = jnp.concatenate([x_bf, jnp.concatenate([zeros_blk, v_bf[q]], axis=1)], axis=0)
        y = _dot(a_r[q], rhs)
        r_hat = r_s[q] + y[:, :LANES]
        r_pair[q] = r_hat[:c] + r_hat[c:]
        o_pair[q] = y[:c, LANES:] + y[c:, LANES:]
        z = _dot_tn(x_bf, _bf(stack(b_h, *q)))
        w_c = jnp.exp(tot_rows[ci][:, p * LANES:(p + 1) * LANES])
        g_t[q] = jnp.where(eye, w_c, 0.0) + z[:LANES]
        h_t[q] = z[LANES:] + _dot_tn(v_bf[q], _bf(stack(k_h, *q)))

    outs = {}
    for ci in range(nch):
        for p in range(npp):
            q = (p, ci)
            s0 = s_ref[p]
            outs[q] = _dot3_nt(r_pair[q], s0) + o_pair[q]
            s_ref[p] = _dot3_nn(s0, g_t[q]) + h_t[q]

    cols = [jnp.concatenate([outs[(p, ci)] for ci in range(nch)], axis=0) if nch > 1 else outs[(p, 0)]
            for p in range(npp)]
    o = jnp.concatenate(cols, axis=1) if npp > 1 else cols[0]
    ya_ref[...] = _bf(_rwkv_finish(o, bonus, g, pvec))

    @pl.when(i == pl.num_programs(2) - 1)
    def _():
        for p in range(npp):
            s = s_ref[p]
            wkv_ref[0, 2 * p] = s[:half, :half]
            wkv_ref[0, 2 * p + 1] = s[half:, half:]


def _rwkv_seq(proj, nb, seq, lay, first, pvec, mul, wd, wa, wg):
    rw, lslot = lay["rw"], lay["lslot"]
    npair = rw // LANES
    npp = RWKV_PAIRS_PER_STEP if npair % RWKV_PAIRS_PER_STEP == 0 else 1
    ngrp = npair // npp
    width = npp * LANES
    c = RWKV_CHUNK
    tc = _pick_tile(seq, (2 * c, c))
    nt = seq // tc
    t_idx = jnp.arange(tc)
    tri = ((t_idx[:, None] // c == t_idx[None, :] // c) & (t_idx[None, :] <= t_idx[:, None])).astype(BF16)
    fr, fk, fv, fl = first
    col = lambda s: pl.BlockSpec((tc, width), lambda b, gp, i, s=s: (b * nt + i, s * ngrp + gp))
    fcol = lambda s: pl.BlockSpec((1, 1, width), lambda b, gp, i, s=s: (b, 0, s * ngrp + gp))
    wspec = pl.BlockSpec((lslot, width), lambda b, gp, i: (0, gp))
    return pl.pallas_call(
        _rwkv_seq_kernel,
        grid=(nb, ngrp, nt),
        in_specs=[col(0), col(1), col(2),
                  pl.BlockSpec((tc, lslot), lambda b, gp, i: (b * nt + i, lay["off_l"] // lslot)),
                  fcol(0), fcol(1), fcol(2),
                  pl.BlockSpec((1, 1, lslot), lambda b, gp, i: (b, 0, 0)),
                  pl.BlockSpec((_PV_ROWS, width), lambda b, gp, i: (0, gp)),
                  pl.BlockSpec((1, lslot), lambda b, gp, i: (0, 0)),
                  wspec, wspec, wspec,
                  pl.BlockSpec((tc, tc), lambda b, gp, i: (0, 0))],
        out_specs=[pl.BlockSpec((tc, width), lambda b, gp, i: (b * nt + i, gp)),
                   pl.BlockSpec((1, 2 * npp, LANES // 2, LANES // 2), lambda b, gp, i: (b, gp, 0, 0))],
        out_shape=[jax.ShapeDtypeStruct((nb * seq, rw), BF16),
                   jax.ShapeDtypeStruct((nb, 2 * npair, LANES // 2, LANES // 2), F32)],
        scratch_shapes=[pltpu.VMEM((npp, LANES, LANES), F32),
                        pltpu.VMEM((SUBLANES, width), F32), pltpu.VMEM((SUBLANES, width), F32),
                        pltpu.VMEM((SUBLANES, width), F32), pltpu.VMEM((SUBLANES, lslot), F32)],
        compiler_params=_params("arbitrary", "arbitrary", "arbitrary"),
    )(proj, proj, proj, proj, fr, fk, fv, fl, pvec, mul, wd, wa, wg, tri)


def _rwkv_step_kernel(zr_ref, zk_ref, zv_ref, zl_ref, pr_ref, pk_ref, pv_ref, plr_ref,
                      pvec_ref, mul_ref, wd_ref, wa_ref, wg_ref, s_ref,
                      ya_ref, snew_ref, o_s):
    half = LANES // 2
    pvec = pvec_ref[...]
    r, kp, v, kk, a, logd, g, bonus = _rwkv_tokens(
        zr_ref[...].astype(F32), zk_ref[...].astype(F32), zv_ref[...].astype(F32),
        zl_ref[...].astype(F32), pr_ref[...], pk_ref[...], pv_ref[...], plr_ref[...],
        pvec, mul_ref[...], wd_ref[...], wa_ref[...], wg_ref[...])
    w = jnp.exp(logd)
    nkk_t, wr_t, w_t, beta_t, kp_t, r_t, v_t = (jnp.transpose(x) for x in (-kk, w * r, w, kk * a, kp, r, v))
    for e in range(2):
        ks = slice(e * half, (e + 1) * half)
        nkk_e, wr_e, w_e, beta_e, kp_e = nkk_t[ks], wr_t[ks], w_t[ks], beta_t[ks], kp_t[ks]
        c_beta = jnp.sum(beta_e * r_t[ks], axis=0, keepdims=True)
        c_k = jnp.sum(kp_e * r_t[ks], axis=0, keepdims=True)
        for vi in range(half):
            row = e * half + vi
            s = s_ref[e, vi]
            sa = jnp.sum(s * nkk_e, axis=0, keepdims=True)
            sw = jnp.sum(s * wr_e, axis=0, keepdims=True)
            v_row = v_t[row:row + 1, :]
            snew_ref[e, vi] = s * w_e + sa * beta_e + v_row * kp_e
            o_s[row:row + 1, :] = sw + sa * c_beta + v_row * c_k
    ya_ref[...] = _bf(_rwkv_finish(jnp.transpose(o_s[...]), bonus, g, pvec))


def _rwkv_step(proj, nb, lay, prev, pvec, mul, wd, wa, wg, s_wkv):
    rw, lslot = lay["rw"], lay["lslot"]
    npair = rw // LANES
    pm, plr = prev
    half = LANES // 2
    s_t = jnp.transpose(s_wkv, (1, 2, 3, 0))
    col = lambda off: pl.BlockSpec((nb, LANES), lambda p, off=off: (0, off + p))
    wspec = pl.BlockSpec((lslot, LANES), lambda p: (0, p))
    sspec = pl.BlockSpec((2, half, half, nb), lambda p: (p, 0, 0, 0))
    ya, snew_t = pl.pallas_call(
        _rwkv_step_kernel,
        grid=(npair,),
        in_specs=[col(0), col(npair), col(2 * npair),
                  pl.BlockSpec((nb, lslot), lambda p: (0, lay["off_l"] // lslot)),
                  col(0), col(npair), col(2 * npair),
                  pl.BlockSpec((nb, lslot), lambda p: (0, 0)),
                  pl.BlockSpec((_PV_ROWS, LANES), lambda p: (0, p)),
                  pl.BlockSpec((1, lslot), lambda p: (0, 0)),
                  wspec, wspec, wspec, sspec],
        out_specs=[pl.BlockSpec((nb, LANES), lambda p: (0, p)), sspec],
        out_shape=[jax.ShapeDtypeStruct((nb, rw), BF16), jax.ShapeDtypeStruct(s_t.shape, F32)],
        scratch_shapes=[pltpu.VMEM((LANES, nb), F32)],
        compiler_params=_params("arbitrary"),
    )(proj, proj, proj, proj, pm, pm, pm, plr, pvec, mul, wd, wa, wg, s_t)
    return ya, jnp.transpose(snew_t, (3, 0, 1, 2))


def _rope_rows(t, cos2, sin2):
    return t * cos2 + pltpu.roll(t, LANES // 2, 1) * sin2


def _head_norm_rows(o, eps):
    mu = jnp.mean(o, axis=-1, keepdims=True)
    d = o - mu
    var = jnp.mean(d * d, axis=-1, keepdims=True)
    return d * lax.rsqrt(var + eps)


def _ret_seq_kernel(q_ref, k_ref, v_ref, g_ref, cos_ref, sin_ref, intra_ref, qd_ref, kd_ref,
                    blk_ref, gnw_ref, gnb_ref, yb_ref, ret_ref, s_ref):
    i = pl.program_id(1)
    nh = s_ref.shape[0]
    dk = s_ref.shape[1]

    @pl.when(i == 0)
    def _():
        s_ref[...] = jnp.zeros_like(s_ref)

    cos2, sin2 = cos_ref[...], sin_ref[...]
    for h in range(nh):
        hs = slice(h * LANES, (h + 1) * LANES)
        qh = _rope_rows(q_ref[:, hs].astype(F32), cos2, sin2)
        kh = _rope_rows(k_ref[:, hs].astype(F32), cos2, sin2) * (dk ** -0.5)
        vb = _bf(v_ref[:, hs])
        qb = _bf(qh)
        scores = _dot_nt(qb, _bf(kh)) * intra_ref[h]
        s0 = s_ref[h]
        o = _dot(_bf(scores), vb) + _dot(qb, _bf(s0)) * qd_ref[h]
        s_ref[h] = s0 * blk_ref[h] + _dot_tn(_bf(kh * kd_ref[h]), vb)
        on = _head_norm_rows(o, RET_GN_EPS)
        yb_ref[:, hs] = _bf((on * gnw_ref[:, hs] + gnb_ref[:, hs]) * jax.nn.silu(g_ref[:, hs].astype(F32)))

    @pl.when(i == pl.num_programs(1) - 1)
    def _():
        ret_ref[0] = s_ref[...]


def _ret_tables(nh, c):
    log_g = jnp.log1p(-jnp.exp2(-5.0 - jnp.arange(nh, dtype=F32)))
    i = jnp.arange(c, dtype=F32)
    rel = i[:, None] - i[None, :]
    intra = jnp.where(rel >= 0, jnp.exp(log_g[:, None, None] * jnp.maximum(rel, 0.0)), 0.0)
    q_decay = jnp.exp(log_g[:, None] * (i + 1.0))
    k_decay = jnp.exp(log_g[:, None] * (c - 1.0 - i))
    blk_decay = jnp.exp(log_g * c)
    return intra, q_decay, k_decay, blk_decay


def _rope_tables(pos, dk):
    half = dk // 2
    inv = ROPE_BASE ** (-jnp.arange(half, dtype=F32) / half)
    ang = pos[:, None] * inv[None, :]
    cos, sin = jnp.cos(ang), jnp.sin(ang)
    return jnp.concatenate([cos, cos], axis=-1), jnp.concatenate([-sin, sin], axis=-1)


def _ret_seq(proj, nb, seq, lay, gnw, gnb):
    qk, rv, nh = lay["qk"], lay["rv"], lay["ret_heads"]
    dk, dv = qk // nh, rv // nh
    assert dk == LANES and dv == LANES
    c = RET_CHUNK if seq % RET_CHUNK == 0 else seq
    assert c % SUBLANES == 0
    nt = seq // c
    intra, qd, kd, blk = _ret_tables(nh, c)
    qd = jnp.broadcast_to(qd[:, :, None], (nh, c, dv))
    kd = jnp.broadcast_to(kd[:, :, None], (nh, c, dk))
    blk = jnp.broadcast_to(blk[:, None, None], (nh, 1, dv))
    cos2, sin2 = _rope_tables(jnp.arange(seq, dtype=F32), dk)
    seg = lambda off, w: pl.BlockSpec((c, w), lambda b, i, off=off, w=w: (b * nt + i, off // w))
    full3 = lambda a: pl.BlockSpec(a.shape, lambda b, i: (0, 0, 0))
    return pl.pallas_call(
        _ret_seq_kernel,
        grid=(nb, nt),
        in_specs=[seg(lay["off_q"], qk), seg(lay["off_kr"], qk), seg(lay["off_vr"], rv),
                  seg(lay["off_rg"], rv),
                  pl.BlockSpec((c, dk), lambda b, i: (i, 0)), pl.BlockSpec((c, dk), lambda b, i: (i, 0)),
                  full3(intra), full3(qd), full3(kd), full3(blk),
                  pl.BlockSpec((1, rv), lambda b, i: (0, 0)), pl.BlockSpec((1, rv), lambda b, i: (0, 0))],
        out_specs=[pl.BlockSpec((c, rv), lambda b, i: (b * nt + i, 0)),
                   pl.BlockSpec((1, nh, dk, dv), lambda b, i: (b, 0, 0, 0))],
        out_shape=[jax.ShapeDtypeStruct((nb * seq, rv), BF16),
                   jax.ShapeDtypeStruct((nb, nh, dk, dv), F32)],
        scratch_shapes=[pltpu.VMEM((nh, dk, dv), F32)],
        compiler_params=_params("arbitrary", "arbitrary"),
    )(proj, proj, proj, proj, cos2, sin2, intra, qd, kd, blk, gnw, gnb)


_RC_INTRA, _RC_QD, _RC_KD, _RC_BLK = range(4)


def _ret_step_kernel(q_ref, k_ref, v_ref, g_ref, cos_ref, sin_ref, rc_ref, gnw_ref, gnb_ref, s_ref,
                     yb_ref, snew_ref, q_s, k_s, v_s, o_s):
    bb = q_ref.shape[0]
    nh = s_ref.shape[1]
    dk = s_ref.shape[2]
    cos2, sin2 = cos_ref[...], sin_ref[...]
    for h in range(nh):
        hs = slice(h * LANES, (h + 1) * LANES)
        q_s[:, hs] = _rope_rows(q_ref[:, hs].astype(F32), cos2, sin2)
        k_s[:, hs] = _rope_rows(k_ref[:, hs].astype(F32), cos2, sin2) * (dk ** -0.5)
        v_s[:, hs] = v_ref[:, hs].astype(F32)
    eye = (lax.broadcasted_iota(jnp.int32, (LANES, LANES), 0)
           == lax.broadcasted_iota(jnp.int32, (LANES, LANES), 1))
    for b in range(bb):
        for h in range(nh):
            hs = slice(h * LANES, (h + 1) * LANES)
            rc = lambda j: rc_ref[h, j:j + 1, :]
            q_row = q_s[b:b + 1, hs]
            k_row = k_s[b:b + 1, hs]
            v_row = v_s[b:b + 1, hs]
            s0 = s_ref[b, h]
            q_col = jnp.sum(jnp.where(eye, q_row, 0.0), axis=-1, keepdims=True)
            k_col = jnp.sum(jnp.where(eye, k_row, 0.0), axis=-1, keepdims=True)
            score = jnp.sum(q_row * k_row, axis=-1, keepdims=True) * rc(_RC_INTRA)
            o_row = score * v_row + jnp.sum(s0 * q_col, axis=0, keepdims=True) * rc(_RC_QD)
            snew_ref[b, h] = s0 * rc(_RC_BLK) + (k_col * rc(_RC_KD)) * v_row
            o_s[b:b + 1, hs] = o_row
    for h in range(nh):
        hs = slice(h * LANES, (h + 1) * LANES)
        on = _head_norm_rows(o_s[:, hs], RET_GN_EPS)
        yb_ref[:, hs] = _bf((on * gnw_ref[:, hs] + gnb_ref[:, hs]) * jax.nn.silu(g_ref[:, hs].astype(F32)))


def _ret_step(proj, nb, lay, gnw, gnb, s_ret, pos0):
    qk, rv, nh = lay["qk"], lay["rv"], lay["ret_heads"]
    dk, dv = qk // nh, rv // nh
    assert dk == LANES and dv == LANES
    bb = 2 * SUBLANES
    assert nb % bb == 0
    intra, qd, kd, blk = _ret_tables(nh, 1)
    rc = jnp.stack([intra[:, 0, 0], qd[:, 0], kd[:, 0], blk], axis=1)
    rc = jnp.pad(rc, ((0, 0), (0, SUBLANES - 4)))
    rc = jnp.broadcast_to(rc[:, :, None], (nh, SUBLANES, LANES))
    cos2, sin2 = _rope_tables(jnp.asarray([pos0], dtype=F32), dk)
    seg = lambda off, w: pl.BlockSpec((bb, w), lambda j, off=off, w=w: (j, off // w))
    sspec = pl.BlockSpec((bb, nh, dk, dv), lambda j: (j, 0, 0, 0))
    row = lambda w: pl.BlockSpec((1, w), lambda j: (0, 0))
    return pl.pallas_call(
        _ret_step_kernel,
        grid=(nb // bb,),
        in_specs=[seg(lay["off_q"], qk), seg(lay["off_kr"], qk), seg(lay["off_vr"], rv),
                  seg(lay["off_rg"], rv), row(dk), row(dk),
                  pl.BlockSpec(rc.shape, lambda j: (0, 0, 0)), row(rv), row(rv), sspec],
        out_specs=[pl.BlockSpec((bb, rv), lambda j: (j, 0)), sspec],
        out_shape=[jax.ShapeDtypeStruct((nb, rv), BF16), jax.ShapeDtypeStruct(s_ret.shape, F32)],
        scratch_shapes=[pltpu.VMEM((bb, qk), F32), pltpu.VMEM((bb, qk), F32), pltpu.VMEM((bb, rv), F32),
                        pltpu.VMEM((bb, rv), F32)],
        compiler_params=_params("arbitrary"),
    )(proj, proj, proj, proj, cos2, sin2, rc, gnw, gnb, s_ret)


def _merge_kernel(alpha, ng, *refs):
    ya_ref, yb_ref = refs[0], refs[1]
    ga_refs = refs[2:2 + ng]
    gb_refs = refs[2 + ng:2 + 2 * ng]
    (x_ref, g1_ref, sc2_ref, sh2_ref, wa_ref, wb_ref, wo_ref, lnw_ref, lnb_ref,
     x1_ref, u2_ref) = refs[2 + 2 * ng:]
    cat = lambda rs: jnp.concatenate([r[...].astype(F32) for r in rs], axis=1)
    merged = (jax.nn.sigmoid(cat(ga_refs)) * _dot(ya_ref[...], wa_ref[...])
              + jax.nn.sigmoid(cat(gb_refs)) * _dot(yb_ref[...], wb_ref[...]))
    t = alpha * x_ref[...] + g1_ref[0] * _dot(_bf(merged), wo_ref[...])
    x1 = _layer_norm_rows(t, lnw_ref[...], lnb_ref[...])
    x1_ref[...] = x1
    u2_ref[...] = _bf(x1 * (1.0 + sc2_ref[0]) + sh2_ref[0])


def _merge(ya, yb, proj, x, g1, sc2, sh2, wba, wbb, wout, lnw, lnb, lay, alpha, tm, tpg):
    m, d = x.shape
    rw, rv = ya.shape[1], yb.shape[1]
    gw = math.gcd(lay["off_ga"], d)
    ng = d // gw
    r = g1.shape[1]
    rowt = lambda w: pl.BlockSpec((tm, w), lambda i: (i, 0))
    gate = lambda off, q: pl.BlockSpec((tm, gw), lambda i, off=off, q=q: (i, off // gw + q))
    mod = pl.BlockSpec((1, r, d), lambda i: (i // tpg, 0, 0))
    const = lambda a: pl.BlockSpec(a.shape, lambda i: (0, 0), pipeline_mode=pl.Buffered(1))
    return pl.pallas_call(
        functools.partial(_merge_kernel, alpha, ng),
        grid=(m // tm,),
        in_specs=[rowt(rw), rowt(rv)]
                 + [gate(lay["off_ga"], q) for q in range(ng)]
                 + [gate(lay["off_gb"], q) for q in range(ng)]
                 + [rowt(d), mod, mod, mod, const(wba), const(wbb), const(wout), const(lnw), const(lnb)],
        out_specs=[rowt(d), rowt(d)],
        out_shape=[jax.ShapeDtypeStruct((m, d), F32), jax.ShapeDtypeStruct((m, d), BF16)],
        compiler_params=_params("arbitrary"),
    )(ya, yb, *([proj] * (2 * ng)), x, g1, sc2, sh2, wba, wbb, wout, lnw, lnb)


def _ffn_seq_kernel(tps, nj, nsteps, alpha, u_ref, wa_ref, wb_ref, cwa_ref, cwb_ref, ia_ref, ib_ref,
                    wd_ref, x1_ref, g2_ref, lnw_ref, lnb_ref, o_ref, ta_ref, tb_ref,
                    ha_s, hb_s, ca_s, cb_s, act_s, acc_s):
    t = pl.program_id(0)
    tm = u_ref.shape[0]

    @pl.when(t == 0)
    def _():
        acc_s[...] = jnp.zeros_like(acc_s)
        act_s[...] = jnp.zeros_like(act_s)

    tu = jnp.minimum(t, nsteps - 1)
    i, jc = tu // nj, tu % nj
    first = i % tps == 0
    u = u_ref[...]
    halves, tails = [], []
    for w_ref, cw_ref, init_ref, h_s, c_s in ((wa_ref, cwa_ref, ia_ref, ha_s, ca_s),
                                               (wb_ref, cwb_ref, ib_ref, hb_s, cb_s)):
        h = _dot(u, w_ref[...])
        h_s[0:SUBLANES, :] = jnp.where(first, init_ref[0], c_s[jc])
        h_s[SUBLANES:SUBLANES + tm, :] = h
        tail = h[tm - SUBLANES:tm, :]
        c_s[jc] = tail
        tails.append(tail)
        cw = cw_ref[...]
        halves.append(cw[3:4, :] + cw[0:1, :] * h_s[SUBLANES - 2:SUBLANES - 2 + tm, :]
                      + cw[1:2, :] * h_s[SUBLANES - 1:SUBLANES - 1 + tm, :] + cw[2:3, :] * h)
    acc_s[...] += _dot(act_s[(t + 1) % 2], wd_ref[...])
    act_s[t % 2] = _bf(jax.nn.silu(halves[0]) * halves[1])

    @pl.when((i % tps == tps - 1) & (t < nsteps))
    def _():
        ta_ref[i // tps, jc] = tails[0]
        tb_ref[i // tps, jc] = tails[1]

    @pl.when((t > 0) & (t % nj == 0))
    def _():
        y = alpha * x1_ref[...] + g2_ref[0] * acc_s[...]
        o_ref[...] = _layer_norm_rows(y, lnw_ref[...], lnb_ref[...])
        acc_s[...] = jnp.zeros_like(acc_s)


def _ffn_up_step_kernel(u_ref, wa_ref, wb_ref, cwa_ref, cwb_ref, p1a_ref, p2a_ref, p1b_ref, p2b_ref,
                        act_ref, ha_ref, hb_ref):
    u = u_ref[...]
    halves = []
    for w_ref, cw_ref, p1_ref, p2_ref, h_ref in ((wa_ref, cwa_ref, p1a_ref, p2a_ref, ha_ref),
                                                 (wb_ref, cwb_ref, p1b_ref, p2b_ref, hb_ref)):
        h = _dot(u, w_ref[...])
        h_ref[...] = h
        cw = cw_ref[...]
        halves.append(cw[3:4, :] + cw[0:1, :] * p2_ref[...] + cw[1:2, :] * p1_ref[...] + cw[2:3, :] * h)
    act_ref[...] = _bf(jax.nn.silu(halves[0]) * halves[1])


def _ffn_up(u2, w_up_bf, cwt, nb, seq, s_conv, tm):
    m, d = u2.shape
    f2 = w_up_bf.shape[1]
    f = f2 // 2
    tn = _pick_tile(f, (512, 256, 128))
    nj = f // tn
    w_a = pl.BlockSpec((d, tn), lambda i, j: (0, j))
    w_b = pl.BlockSpec((d, tn), lambda i, j: (0, nj + j))
    cw_a = pl.BlockSpec((4, tn), lambda i, j: (0, j))
    cw_b = pl.BlockSpec((4, tn), lambda i, j: (0, nj + j))
    assert seq == 1
    p_a = pl.BlockSpec((tm, tn), lambda i, j: (i, j))
    p_b = pl.BlockSpec((tm, tn), lambda i, j: (i, nj + j))
    prev1, prev2 = s_conv[:, 1, :], s_conv[:, 0, :]
    act, h_a, h_b = pl.pallas_call(
        _ffn_up_step_kernel,
        grid=(m // tm, nj),
        in_specs=[pl.BlockSpec((tm, d), lambda i, j: (i, 0)), w_a, w_b, cw_a, cw_b, p_a, p_a, p_b, p_b],
        out_specs=[pl.BlockSpec((tm, tn), lambda i, j: (i, j))] * 3,
        out_shape=[jax.ShapeDtypeStruct((m, f), BF16), jax.ShapeDtypeStruct((m, f), F32),
                   jax.ShapeDtypeStruct((m, f), F32)],
        compiler_params=_params("arbitrary", "arbitrary"),
    )(u2, w_up_bf, w_up_bf, cwt, cwt, prev1, prev2, prev1, prev2)
    conv_new = jnp.stack([prev1, jnp.concatenate([h_a, h_b], axis=-1)], axis=1)
    return act, conv_new


def _ffn_seq(u2, w_up_bf, cwt, w_down_bf, x1, g2, lnw, lnb, alpha, nb, seq, s_conv, tm):
    m, d = u2.shape
    f2 = w_up_bf.shape[1]
    f = f2 // 2
    tn = _pick_tile(f, (512, 256, 128))
    nj = f // tn
    assert seq % tm == 0 and tm >= SUBLANES
    tps = seq // tm
    nsteps = (m // tm) * nj
    up_i = lambda t: jnp.minimum(t, nsteps - 1) // nj
    up_j = lambda t: jnp.minimum(t, nsteps - 1) % nj
    dn_i = lambda t: jnp.maximum(t - 1, 0) // nj
    dn_j = lambda t: jnp.maximum(t - 1, 0) % nj
    w_a = pl.BlockSpec((d, tn), lambda t: (0, up_j(t)))
    w_b = pl.BlockSpec((d, tn), lambda t: (0, nj + up_j(t)))
    cw_a = pl.BlockSpec((4, tn), lambda t: (0, up_j(t)))
    cw_b = pl.BlockSpec((4, tn), lambda t: (0, nj + up_j(t)))
    init = jnp.pad(s_conv, ((0, 0), (SUBLANES - 2, 0), (0, 0)))
    i_a = pl.BlockSpec((1, SUBLANES, tn), lambda t: (up_i(t) // tps, 0, up_j(t)))
    i_b = pl.BlockSpec((1, SUBLANES, tn), lambda t: (up_i(t) // tps, 0, nj + up_j(t)))
    w_d = pl.BlockSpec((tn, d), lambda t: (dn_j(t), 0))
    row_up = pl.BlockSpec((tm, d), lambda t: (up_i(t), 0))
    row_dn = pl.BlockSpec((tm, d), lambda t: (dn_i(t), 0))
    vec = pl.BlockSpec((1, d), lambda t: (0, 0))
    t_o = pl.BlockSpec((nb, nj, SUBLANES, tn), lambda t: (0, 0, 0, 0))
    tails = jax.ShapeDtypeStruct((nb, nj, SUBLANES, tn), F32)
    x2, t_a, t_b = pl.pallas_call(
        functools.partial(_ffn_seq_kernel, tps, nj, nsteps, alpha),
        grid=(nsteps + 1,),
        in_specs=[row_up, w_a, w_b, cw_a, cw_b, i_a, i_b, w_d, row_dn,
                  pl.BlockSpec((1, 1, d), lambda t: (dn_i(t) // tps, 0, 0)), vec, vec],
        out_specs=[row_dn, t_o, t_o],
        out_shape=[jax.ShapeDtypeStruct((m, d), F32), tails, tails],
        scratch_shapes=[pltpu.VMEM((tm + SUBLANES, tn), F32), pltpu.VMEM((tm + SUBLANES, tn), F32),
                        pltpu.VMEM((nj, SUBLANES, tn), F32), pltpu.VMEM((nj, SUBLANES, tn), F32),
                        pltpu.VMEM((2, tm, tn), BF16), pltpu.VMEM((tm, d), F32)],
        compiler_params=_params("arbitrary"),
    )(u2, w_up_bf, w_up_bf, cwt, cwt, init, init, w_down_bf, x1, g2, lnw, lnb)
    rows = lambda t: t[:, :, SUBLANES - 2:, :].transpose(0, 2, 1, 3).reshape(nb, 2, f)
    return x2, jnp.concatenate([rows(t_a), rows(t_b)], axis=-1)


def _ffn_down_kernel(alpha, act_ref, w_ref, x1_ref, g2_ref, lnw_ref, lnb_ref, o_ref, acc_ref):
    k = pl.program_id(1)

    @pl.when(k == 0)
    def _():
        acc_ref[...] = jnp.zeros_like(acc_ref)

    acc_ref[...] += _dot(act_ref[...], w_ref[...])

    @pl.when(k == pl.num_programs(1) - 1)
    def _():
        t = alpha * x1_ref[...] + g2_ref[0] * acc_ref[...]
        o_ref[...] = _layer_norm_rows(t, lnw_ref[...], lnb_ref[...])


def _ffn_down(act, w_down_bf, x1, g2, lnw, lnb, alpha, tm, tpg):
    m, f = act.shape
    d = w_down_bf.shape[1]
    tk = _pick_tile(f, (512, 256, 128))
    r = g2.shape[1]
    return pl.pallas_call(
        functools.partial(_ffn_down_kernel, alpha),
        grid=(m // tm, f // tk),
        in_specs=[pl.BlockSpec((tm, tk), lambda i, k: (i, k)),
                  pl.BlockSpec((tk, d), lambda i, k: (k, 0)),
                  pl.BlockSpec((tm, d), lambda i, k: (i, 0)),
                  pl.BlockSpec((1, r, d), lambda i, k: (i // tpg, 0, 0)),
                  pl.BlockSpec((1, d), lambda i, k: (0, 0)),
                  pl.BlockSpec((1, d), lambda i, k: (0, 0))],
        out_specs=pl.BlockSpec((tm, d), lambda i, k: (i, 0)),
        out_shape=jax.ShapeDtypeStruct((m, d), F32),
        scratch_shapes=[pltpu.VMEM((tm, d), F32)],
        compiler_params=_params("arbitrary", "arbitrary"),
    )(act, w_down_bf, x1, g2, lnw, lnb)


def _layout(d, rw, qk, rv, lora_w, ret_heads):
    lslot = next(s for s in (128, 256, 512, 1024, 2048) if s >= lora_w)
    off_q = 3 * rw
    off_kr = off_q + qk
    off_vr = off_kr + qk
    off_rg = off_vr + rv
    off_ga = off_rg + rv
    off_gb = off_ga + d
    off_l = off_gb + d
    assert rw % LANES == 0 and off_l % lslot == 0
    assert off_q % qk == 0 and off_kr % qk == 0 and off_vr % rv == 0 and off_rg % rv == 0
    return dict(d=d, rw=rw, qk=qk, rv=rv, lslot=lslot, lora_w=lora_w, ret_heads=ret_heads,
                off_q=off_q, off_kr=off_kr, off_vr=off_vr, off_rg=off_rg, off_ga=off_ga,
                off_gb=off_gb, off_l=off_l, nt=off_l + lslot)


def _prep_weights(lay, w_in, shift_mu, w0, w_decay_up, a0, w_aaa_up, w_gate_up, k_k, k_a, r_k,
                  lnx_w, lnx_b):
    rw, lslot, lora_w = lay["rw"], lay["lslot"], lay["lora_w"]
    w_perm = _bf(jnp.transpose(w_in))
    dl, al, gl = w_decay_up.shape[0], w_aaa_up.shape[0], w_gate_up.shape[0]
    pad_rows = lambda w, lo: jnp.pad(w, ((lo, lslot - lo - w.shape[0]), (0, 0))).astype(BF16)
    wd = pad_rows(w_decay_up, 0)
    wa = pad_rows(w_aaa_up, dl)
    wg = pad_rows(w_gate_up, dl + al)
    rows = [shift_mu[:rw], shift_mu[rw:2 * rw], shift_mu[2 * rw:3 * rw], w0, a0, k_k, k_a,
            r_k.reshape(-1), lnx_w, lnx_b]
    pvec = jnp.pad(jnp.stack(rows, axis=0), ((0, _PV_ROWS - len(rows)), (0, 0)))
    mul = jnp.pad(shift_mu[3 * rw:], (0, lslot - lora_w))[None, :]
    return w_perm, wd, wa, wg, pvec, mul


def _run_layer(x2d, nb, seq, ada, states, pos0, lay, wts, alpha):
    d, rw, lslot, lora_w = lay["d"], lay["rw"], lay["lslot"], lay["lora_w"]
    (w_perm, wd, wa, wg, pvec, mul, gnw, gnb, wba, wbb, wout, ln1w, ln1b,
     w_up, cwt, w_down, ln2w, ln2b) = wts
    m = nb * seq
    sh1, sc1, g1, sh2, sc2, g2 = jnp.split(ada, 6, axis=-1)
    if seq == 1:
        tm, tpg = m, 1
        shape_mod = lambda t: t[None]
    else:
        tm = _pick_tile(seq, (512, 256, 128, 64, 32, 16, 8))
        tpg = seq // tm
        shape_mod = lambda t: t[:, None, :]
    sh1, sc1, g1, sh2, sc2, g2 = map(shape_mod, (sh1, sc1, g1, sh2, sc2, g2))
    tm_in = _pick_tile(seq, (1024, 512, 256, 128, 64, 32, 16, 8)) if seq > 1 else m
    proj = _modmm(x2d, sc1, sh1, w_perm, lay, tm_in, (seq // tm_in) if seq > 1 else 1)

    s_wkv, s_shift, s_ret, s_conv = states
    shift_main = s_shift[:, :3 * rw]
    shift_lora = jnp.pad(s_shift[:, 3 * rw:], ((0, 0), (0, lslot - lora_w)))
    if seq == 1:
        ya, wkv_new = _rwkv_step(proj, nb, lay, (shift_main, shift_lora), pvec, mul, wd, wa, wg, s_wkv)
        yb, ret_new = _ret_step(proj, nb, lay, gnw, gnb, s_ret, pos0)
    else:
        first = (shift_main[:, None, :], shift_main[:, None, :], shift_main[:, None, :],
                 shift_lora[:, None, :])
        ya, wkv_new = _rwkv_seq(proj, nb, seq, lay, first, pvec, mul, wd, wa, wg)
        yb, ret_new = _ret_seq(proj, nb, seq, lay, gnw, gnb)
    last = proj.reshape(nb, seq, lay["nt"])[:, -1, :]
    shift_new = jnp.concatenate([last[:, :3 * rw], last[:, lay["off_l"]:lay["off_l"] + lora_w]],
                                axis=-1).astype(F32)

    tm_merge = min(tm, 256)
    x1, u2 = _merge(ya, yb, proj, x2d, g1, sc2, sh2, wba, wbb, wout, ln1w, ln1b, lay, alpha,
                    tm_merge, (seq // tm_merge) if seq > 1 else 1)
    if seq == 1:
        act, conv_new = _ffn_up(u2, w_up, cwt, nb, seq, s_conv, tm)
        x2 = _ffn_down(act, w_down, x1, g2, ln2w, ln2b, alpha, tm, tpg)
    else:
        x2, conv_new = _ffn_seq(u2, w_up, cwt, w_down, x1, g2, ln2w, ln2b, alpha, nb, seq, s_conv, tm)
    return x2, wkv_new, shift_new, ret_new, conv_new


def kernel(x_prompt, x_sample, c_prompt, c_sample, state_wkv, state_shift, state_ret, state_conv, w_ada, b_ada, w_in, shift_mu, w0, w_decay_up, a0, w_aaa_up, w_gate_up, k_k, k_a, r_k, lnx_w, lnx_b, ret_gn_w, ret_gn_b, w_branch_a, w_branch_b, w_out, ln1_w, ln1_b, w_up, conv_w, conv_b, w_down, ln2_w, ln2_b):
    depth = w_ada.shape[0]
    nbp, seq_p, d = x_prompt.shape
    nbs, seq_s, _ = x_sample.shape
    assert seq_s == 1
    rw = k_k.shape[-1]
    ret_heads, dk, dv = state_ret.shape[2:]
    lora_w = w_decay_up.shape[1] + w_aaa_up.shape[1] + w_gate_up.shape[1]
    lay = _layout(d, rw, ret_heads * dk, ret_heads * dv, lora_w, ret_heads)
    heads, hn = r_k.shape[1:]
    assert hn == LANES // 2 and heads * hn == rw
    alpha = (2.0 * depth) ** 0.25
    f2 = w_up.shape[-1]

    xp = x_prompt.reshape(nbp * seq_p, d)
    xs = x_sample.reshape(nbs * seq_s, d)
    c_all = jnp.concatenate([c_prompt, c_sample], axis=0)
    pad = (-c_all.shape[0]) % SUBLANES
    c_all = _bf(jnp.pad(c_all, ((0, pad), (0, 0))))

    outs_p, outs_s = [], []
    for l in range(depth):
        w_perm, wd, wa, wg, pvec, mul = _prep_weights(
            lay, w_in[l], shift_mu[l], w0[l], w_decay_up[l], a0[l], w_aaa_up[l], w_gate_up[l],
            k_k[l], k_a[l], r_k[l], lnx_w[l], lnx_b[l])
        cwt = jnp.concatenate([conv_w[l], conv_b[l][None, :]], axis=0)
        wts = (w_perm, wd, wa, wg, pvec, mul, ret_gn_w[l][None, :], ret_gn_b[l][None, :],
               _bf(w_branch_a[l]), _bf(w_branch_b[l]), _bf(w_out[l]), ln1_w[l][None, :], ln1_b[l][None, :],
               _bf(w_up[l]), cwt, _bf(w_down[l]), ln2_w[l][None, :], ln2_b[l][None, :])
        ada = _mm_bias(c_all, w_ada[l], b_ada[l][None, :])
        zero_states = (None, jnp.zeros((nbp, state_shift.shape[-1]), F32), None,
                       jnp.zeros((nbp, state_conv.shape[2], f2), F32))
        xp, *st_p = _run_layer(xp, nbp, seq_p, ada[:nbp], zero_states, 0.0, lay, wts, alpha)
        xs, *st_s = _run_layer(xs, nbs, seq_s, ada[nbp:nbp + nbs],
                               (state_wkv[l], state_shift[l], state_ret[l], state_conv[l]),
                               float(PAST_LEN), lay, wts, alpha)
        outs_p.append(st_p)
        outs_s.append(st_s)

    def stack(lst, j, ref):
        layers = [s[j].astype(ref.dtype) for s in lst]
        return layers[0][None] if depth == 1 else jnp.stack(layers, axis=0)

    refs = (state_wkv, state_shift, state_ret, state_conv)
    return (xp.reshape(x_prompt.shape), xs.reshape(x_sample.shape),
            *[stack(outs_p, j, refs[j]) for j in range(4)],
            *[stack(outs_s, j, refs[j]) for j in range(4)])
```

```python
import functools
import math

import jax
import jax.numpy as jnp
from jax import lax
from jax.experimental import pallas as pl
from jax.experimental.pallas import tpu as pltpu

F32 = jnp.float32
BF16 = jnp.bfloat16

PAST_LEN = 16384
ROPE_BASE = 10000.0
RWKV_GN_EPS = 64e-5
RET_GN_EPS = 1e-5
LN_EPS = 1e-5
RET_CHUNK = 128

LANES = 128
SUBLANES = 8
MXU_WIDTH = 256
VMEM_LIMIT_BYTES = 50 * 1024 * 1024

RWKV_CHUNK = 64
RWKV_PAIRS_PER_STEP = 8


def _params(*sem):
    return pltpu.CompilerParams(dimension_semantics=sem, vmem_limit_bytes=VMEM_LIMIT_BYTES)


def _dot(a, b):
    return jnp.dot(a, b, preferred_element_type=F32)


def _dot_nt(a, b):
    return lax.dot_general(a, b, (((1,), (1,)), ((), ())), preferred_element_type=F32)


def _dot_tn(a, b):
    return lax.dot_general(a, b, (((0,), (0,)), ((), ())), preferred_element_type=F32)


def _bf(x):
    return x.astype(BF16)


def _split2(x):
    hi = x.astype(BF16)
    lo = (x - hi.astype(F32)).astype(BF16)
    return hi, lo


def _dot3_nn(a, b):
    ah, al = _split2(a)
    bh, bl = _split2(b)
    return _dot(jnp.concatenate([ah, ah, al], axis=1), jnp.concatenate([bh, bl, bh], axis=0))


def _dot3_nt(a, b):
    ah, al = _split2(a)
    bh, bl = _split2(b)
    return _dot_nt(jnp.concatenate([ah, ah, al], axis=1), jnp.concatenate([bh, bl, bh], axis=1))


def _dot_exact_lhs(a_bf, b):
    b1 = b.astype(BF16)
    r1 = b - b1.astype(F32)
    b2 = r1.astype(BF16)
    b3 = (r1 - b2.astype(F32)).astype(BF16)
    return _dot(jnp.concatenate([a_bf, a_bf, a_bf], axis=1), jnp.concatenate([b1, b2, b3], axis=0))


def _layer_norm_rows(t, w, b):
    mu = jnp.mean(t, axis=-1, keepdims=True)
    d = t - mu
    var = jnp.mean(d * d, axis=-1, keepdims=True)
    return d * lax.rsqrt(var + LN_EPS) * w + b


def _pick_tile(n, candidates):
    for c in candidates:
        if n % c == 0:
            return c
    return n


def _mm_bias_kernel(x_ref, w_ref, b_ref, o_ref):
    o_ref[...] = _dot(x_ref[...], _bf(w_ref[...])) + b_ref[...]


def _mm_bias(x_bf, w, b_row):
    m, k = x_bf.shape
    n = w.shape[1]
    tn = _pick_tile(n, (1024, 512, 256, 128))
    return pl.pallas_call(
        _mm_bias_kernel,
        grid=(n // tn,),
        in_specs=[pl.BlockSpec((m, k), lambda j: (0, 0)),
                  pl.BlockSpec((k, tn), lambda j: (0, j)),
                  pl.BlockSpec((1, tn), lambda j: (0, j))],
        out_specs=pl.BlockSpec((m, tn), lambda j: (0, j)),
        out_shape=jax.ShapeDtypeStruct((m, n), F32),
        compiler_params=_params("arbitrary"),
    )(x_bf, w, b_row)


def _modmm_kernel(x_ref, sc_ref, sh_ref, wt_ref, o_ref, u_ref):
    @pl.when(pl.program_id(1) == 0)
    def _():
        u_ref[...] = _bf(x_ref[...] * (1.0 + sc_ref[0]) + sh_ref[0])

    o_ref[...] = _dot_nt(u_ref[...], wt_ref[...]).astype(o_ref.dtype)


def _modmm(x, sc, sh, wt_bf, lay, tm, tpg):
    m, d = x.shape
    n = lay["nt"]
    tn = _pick_tile(n, (512, 256, 128))
    rw3, lora_w, lslot = 3 * lay["rw"], lay["lora_w"], lay["lslot"]
    n_main, n_rest = rw3 // tn, (lay["off_l"] - rw3) // tn
    assert rw3 % tn == 0 and (lay["off_l"] - rw3) % tn == 0 and lslot == tn
    assert rw3 + lslot <= wt_bf.shape[0]

    row_align = 2 * SUBLANES
    assert tn % row_align == 0 and rw3 % row_align == 0 and lora_w % row_align == 0

    def w_row(j):
        row = jnp.where(j < n_main, j * tn,
                        jnp.where(j < n_main + n_rest, rw3 + lora_w + (j - n_main) * tn, rw3))
        return pl.multiple_of(row, row_align)

    r = sc.shape[1]
    mod_spec = pl.BlockSpec((1, r, d), lambda i, j: (i // tpg, 0, 0))
    return pl.pallas_call(
        _modmm_kernel,
        grid=(m // tm, n // tn),
        in_specs=[pl.BlockSpec((tm, d), lambda i, j: (i, 0)), mod_spec, mod_spec,
                  pl.BlockSpec((pl.Element(tn), pl.Element(d)), lambda i, j: (w_row(j), 0))],
        out_specs=pl.BlockSpec((tm, tn), lambda i, j: (i, j)),
        out_shape=jax.ShapeDtypeStruct((m, n), BF16),
        scratch_shapes=[pltpu.VMEM((tm, d), BF16)],
        compiler_params=_params("arbitrary", "arbitrary"),
    )(x, sc, sh, wt_bf)


_PV_MU_R, _PV_MU_K, _PV_MU_V, _PV_W0, _PV_A0, _PV_KK, _PV_KA, _PV_RK, _PV_LNW, _PV_LNB = range(10)
_PV_ROWS = 16


def _head_half_mask(shape):
    return lax.broadcasted_iota(jnp.int32, shape, 1) < (LANES // 2)


def _head_sums(x):
    h0 = _head_half_mask((x.shape[0], LANES))
    parts = []
    for p in range(x.shape[1] // LANES):
        xs = x[:, p * LANES:(p + 1) * LANES]
        s0 = jnp.sum(jnp.where(h0, xs, 0.0), axis=-1, keepdims=True)
        s1 = jnp.sum(jnp.where(h0, 0.0, xs), axis=-1, keepdims=True)
        parts.append(jnp.where(h0, s0, s1))
    return parts[0] if len(parts) == 1 else jnp.concatenate(parts, axis=1)


def _rwkv_tokens(zr, zk, zv, zl, pr, pk, pv_, plr, pvec, mul, wd, wa, wg):
    row = lambda i: pvec[i:i + 1, :]
    r = zr + row(_PV_MU_R) * (pr - zr)
    k = zk + row(_PV_MU_K) * (pk - zk)
    v = zv + row(_PV_MU_V) * (pv_ - zv)
    ls = zl + mul * (plr - zl)
    wl = _dot(_bf(jnp.tanh(ls)), wd)
    al = _dot(_bf(ls), wa)
    g = _dot(_bf(jax.nn.sigmoid(ls)), wg)
    w = -jax.nn.softplus(-(row(_PV_W0) + wl)) - 0.5
    logd = -jnp.exp(w)
    a = jax.nn.sigmoid(row(_PV_A0) + al)
    kkr = k * row(_PV_KK)
    nrm = jnp.sqrt(_head_sums(kkr * kkr))
    kk = kkr / jnp.maximum(nrm, 1e-12)
    kp = k * (1.0 + (a - 1.0) * row(_PV_KA))
    bonus = _head_sums(r * kp * row(_PV_RK)) * v
    return r, kp, v, kk, a, logd, g, bonus


def _rwkv_finish(o, bonus, g, pvec):
    inv_n = 1.0 / (LANES // 2)
    mu = _head_sums(o) * inv_n
    d = o - mu
    var = _head_sums(d * d) * inv_n
    on = d * lax.rsqrt(var + RWKV_GN_EPS)
    return (on * pvec[_PV_LNW:_PV_LNW + 1, :] + pvec[_PV_LNB:_PV_LNB + 1, :] + bonus) * g


def _shift_rows(z, first_row):
    rolled = pltpu.roll(z, 1, 0)
    rowid = lax.broadcasted_iota(jnp.int32, z.shape, 0)
    return jnp.where(rowid == 0, first_row, rolled)


def _rwkv_seq_kernel(zr_ref, zk_ref, zv_ref, zl_ref, fr_ref, fk_ref, fv_ref, fl_ref,
                     pvec_ref, mul_ref, wd_ref, wa_ref, wg_ref, tri_ref,
                     ya_ref, wkv_ref, s_ref, cr_ref, ck_ref, cv_ref, cl_ref):
    i = pl.program_id(2)
    tc, width = zr_ref.shape
    npp = width // LANES
    c = RWKV_CHUNK
    nch = tc // c
    half = LANES // 2

    @pl.when(i == 0)
    def _():
        s_ref[...] = jnp.zeros_like(s_ref)
        cr_ref[0:1, :] = fr_ref[0]
        ck_ref[0:1, :] = fk_ref[0]
        cv_ref[0:1, :] = fv_ref[0]
        cl_ref[0:1, :] = fl_ref[0]

    zr, zk, zv, zl = (ref[...].astype(F32) for ref in (zr_ref, zk_ref, zv_ref, zl_ref))
    pvec = pvec_ref[...]
    r, kp, v, kk, a, logd, g, bonus = _rwkv_tokens(
        zr, zk, zv, zl,
        _shift_rows(zr, cr_ref[0:1, :]), _shift_rows(zk, ck_ref[0:1, :]),
        _shift_rows(zv, cv_ref[0:1, :]), _shift_rows(zl, cl_ref[0:1, :]),
        pvec, mul_ref[...], wd_ref[...], wa_ref[...], wg_ref[...])
    cr_ref[0:1, :] = zr[tc - 1:tc, :]
    ck_ref[0:1, :] = zk[tc - 1:tc, :]
    cv_ref[0:1, :] = zv[tc - 1:tc, :]
    cl_ref[0:1, :] = zl[tc - 1:tc, :]

    alpha = -kk
    beta = kk * a
    cum_incl = _dot_exact_lhs(tri_ref[...], logd)
    cum_excl = cum_incl - logd
    tot_rows = [cum_incl[(ci + 1) * c - 1:(ci + 1) * c, :] for ci in range(nch)]
    tot = jnp.concatenate([jnp.broadcast_to(t, (c, width)) for t in tot_rows], axis=0)
    e_neg = jnp.exp(-cum_incl)
    e_hat = jnp.exp(tot - cum_incl)
    r_t = r * jnp.exp(cum_incl)
    a_t = alpha * jnp.exp(cum_excl)
    b_t = beta * e_neg
    k_t = kp * e_neg
    b_h = beta * e_hat
    k_h = kp * e_hat

    h0 = _head_half_mask((c, LANES))

    def stack(x, p, ci):
        xb = x[ci * c:(ci + 1) * c, p * LANES:(p + 1) * LANES]
        return jnp.concatenate([jnp.where(h0, xb, 0.0), jnp.where(h0, 0.0, xb)], axis=0)

    rr = lax.broadcasted_iota(jnp.int32, (2 * c, 2 * c), 0)
    cc = lax.broadcasted_iota(jnp.int32, (2 * c, 2 * c), 1)
    same = (rr >= c) == (cc >= c)
    strict = same & (cc < rr)
    incl = same & (cc <= rr)
    eye = rr == cc
    zeros_blk = jnp.zeros((2 * c, LANES), BF16)

    probs = [(p, ci) for p in range(npp) for ci in range(nch)]
    a_s = {q: stack(a_t, *q) for q in probs}
    r_s = {q: stack(r_t, *q) for q in probs}
    v_bf = {q: _bf(stack(v, *q)) for q in probs}

    pmat, a_ak, a_r = {}, {}, {}
    for q in probs:
        amat = _dot_nt(_bf(jnp.concatenate([a_s[q], r_s[q]], axis=0)),
                       _bf(jnp.concatenate([stack(b_t, *q), stack(k_t, *q)], axis=0)))
        pmat[q] = jnp.where(strict, amat[:2 * c, :2 * c], 0.0)
        a_ak[q] = _bf(jnp.where(strict, amat[:2 * c, 2 * c:], 0.0))
        a_r[q] = _bf(jnp.concatenate([jnp.where(incl, amat[2 * c:, :2 * c], 0.0),
                                      jnp.where(incl, amat[2 * c:, 2 * c:], 0.0)], axis=1))

    x = {q: jnp.concatenate([a_s[q], _dot(a_ak[q], v_bf[q])], axis=1) for q in probs}

    nsteps = int(math.log2(c))
    for it in range(nsteps):
        for q in probs:
            p_bf = _bf(pmat[q])
            x[q] = x[q] + _dot(p_bf, _bf(x[q]))
            if it + 1 < nsteps:
                pmat[q] = _dot(p_bf, p_bf)

    r_pair, o_pair, g_t, h_t = {}, {}, {}, {}
    for q in probs:
        p, ci = q
        x_bf = _bf(x[q])
        rhs = jnp.concatenate([x_bf, jnp.concatenate([zeros_blk, v_bf[q]], axis=1)], axis=0)
        y = _dot(a_r[q], rhs)
        r_hat = r_s[q] + y[:, :LANES]
        r_pair[q] = r_hat[:c] + r_hat[c:]
        o_pair[q] = y[:c, LANES:] + y[c:, LANES:]
        z = _dot_tn(x_bf, _bf(stack(b_h, *q)))
        w_c = jnp.exp(tot_rows[ci][:, p * LANES:(p + 1) * LANES])
        g_t[q] = jnp.where(eye, w_c, 0.0) + z[:LANES]
        h_t[q] = z[LANES:] + _dot_tn(v_bf[q], _bf(stack(k_h, *q)))

    outs = {}
    for ci in range(nch):
        for p in range(npp):
            q = (p, ci)
            s0 = s_ref[p]
            outs[q] = _dot3_nt(r_pair[q], s0) + o_pair[q]
            s_ref[p] = _dot3_nn(s0, g_t[q]) + h_t[q]

    cols = [jnp.concatenate([outs[(p, ci)] for ci in range(nch)], axis=0) if nch > 1 else outs[(p, 0)]
            for p in range(npp)]
    o = jnp.concatenate(cols, axis=1) if npp > 1 else cols[0]
    ya_ref[...] = _bf(_rwkv_finish(o, bonus, g, pvec))

    @pl.when(i == pl.num_programs(2) - 1)
    def _():
        for p in range(npp):
            s = s_ref[p]
            wkv_ref[0, 2 * p] = s[:half, :half]
            wkv_ref[0, 2 * p + 1] = s[half:, half:]


def _rwkv_seq(proj, nb, seq, lay, first, pvec, mul, wd, wa, wg):
    rw, lslot = lay["rw"], lay["lslot"]
    npair = rw // LANES
    npp = RWKV_PAIRS_PER_STEP if npair % RWKV_PAIRS_PER_STEP == 0 else 1
    ngrp = npair // npp
    width = npp * LANES
    c = RWKV_CHUNK
    tc = _pick_tile(seq, (2 * c, c))
    nt = seq // tc
    t_idx = jnp.arange(tc)
    tri = ((t_idx[:, None] // c == t_idx[None, :] // c) & (t_idx[None, :] <= t_idx[:, None])).astype(BF16)
    fr, fk, fv, fl = first
    col = lambda s: pl.BlockSpec((tc, width), lambda b, gp, i, s=s: (b * nt + i, s * ngrp + gp))
    fcol = lambda s: pl.BlockSpec((1, 1, width), lambda b, gp, i, s=s: (b, 0, s * ngrp + gp))
    wspec = pl.BlockSpec((lslot, width), lambda b, gp, i: (0, gp))
    return pl.pallas_call(
        _rwkv_seq_kernel,
        grid=(nb, ngrp, nt),
        in_specs=[col(0), col(1), col(2),
                  pl.BlockSpec((tc, lslot), lambda b, gp, i: (b * nt + i, lay["off_l"] // lslot)),
                  fcol(0), fcol(1), fcol(2),
                  pl.BlockSpec((1, 1, lslot), lambda b, gp, i: (b, 0, 0)),
                  pl.BlockSpec((_PV_ROWS, width), lambda b, gp, i: (0, gp)),
                  pl.BlockSpec((1, lslot), lambda b, gp, i: (0, 0)),
                  wspec, wspec, wspec,
                  pl.BlockSpec((tc, tc), lambda b, gp, i: (0, 0))],
        out_specs=[pl.BlockSpec((tc, width), lambda b, gp, i: (b * nt + i, gp)),
                   pl.BlockSpec((1, 2 * npp, LANES // 2, LANES // 2), lambda b, gp, i: (b, gp, 0, 0))],
        out_shape=[jax.ShapeDtypeStruct((nb * seq, rw), BF16),
                   jax.ShapeDtypeStruct((nb, 2 * npair, LANES // 2, LANES // 2), F32)],
        scratch_shapes=[pltpu.VMEM((npp, LANES, LANES), F32),
                        pltpu.VMEM((SUBLANES, width), F32), pltpu.VMEM((SUBLANES, width), F32),
                        pltpu.VMEM((SUBLANES, width), F32), pltpu.VMEM((SUBLANES, lslot), F32)],
        compiler_params=_params("arbitrary", "arbitrary", "arbitrary"),
    )(proj, proj, proj, proj, fr, fk, fv, fl, pvec, mul, wd, wa, wg, tri)


def _rwkv_step_kernel(zr_ref, zk_ref, zv_ref, zl_ref, pr_ref, pk_ref, pv_ref, plr_ref,
                      pvec_ref, mul_ref, wd_ref, wa_ref, wg_ref, s_ref,
                      ya_ref, snew_ref, o_s):
    half = LANES // 2
    pvec = pvec_ref[...]
    r, kp, v, kk, a, logd, g, bonus = _rwkv_tokens(
        zr_ref[...].astype(F32), zk_ref[...].astype(F32), zv_ref[...].astype(F32),
        zl_ref[...].astype(F32), pr_ref[...], pk_ref[...], pv_ref[...], plr_ref[...],
        pvec, mul_ref[...], wd_ref[...], wa_ref[...], wg_ref[...])
    w = jnp.exp(logd)
    nkk_t, wr_t, w_t, beta_t, kp_t, r_t, v_t = (jnp.transpose(x) for x in (-kk, w * r, w, kk * a, kp, r, v))
    for e in range(2):
        ks = slice(e * half, (e + 1) * half)
        nkk_e, wr_e, w_e, beta_e, kp_e = nkk_t[ks], wr_t[ks], w_t[ks], beta_t[ks], kp_t[ks]
        c_beta = jnp.sum(beta_e * r_t[ks], axis=0, keepdims=True)
        c_k = jnp.sum(kp_e * r_t[ks], axis=0, keepdims=True)
        for vi in range(half):
            row = e * half + vi
            s = s_ref[e, vi]
            sa = jnp.sum(s * nkk_e, axis=0, keepdims=True)
            sw = jnp.sum(s * wr_e, axis=0, keepdims=True)
            v_row = v_t[row:row + 1, :]
            snew_ref[e, vi] = s * w_e + sa * beta_e + v_row * kp_e
            o_s[row:row + 1, :] = sw + sa * c_beta + v_row * c_k
    ya_ref[...] = _bf(_rwkv_finish(jnp.transpose(o_s[...]), bonus, g, pvec))


def _rwkv_step(proj, nb, lay, prev, pvec, mul, wd, wa, wg, s_wkv):
    rw, lslot = lay["rw"], lay["lslot"]
    npair = rw // LANES
    pm, plr = prev
    half = LANES // 2
    s_t = jnp.transpose(s_wkv, (1, 2, 3, 0))
    col = lambda off: pl.BlockSpec((nb, LANES), lambda p, off=off: (0, off + p))
    wspec = pl.BlockSpec((lslot, LANES), lambda p: (0, p))
    sspec = pl.BlockSpec((2, half, half, nb), lambda p: (p, 0, 0, 0))
    ya, snew_t = pl.pallas_call(
        _rwkv_step_kernel,
        grid=(npair,),
        in_specs=[col(0), col(npair), col(2 * npair),
                  pl.BlockSpec((nb, lslot), lambda p: (0, lay["off_l"] // lslot)),
                  col(0), col(npair), col(2 * npair),
                  pl.BlockSpec((nb, lslot), lambda p: (0, 0)),
                  pl.BlockSpec((_PV_ROWS, LANES), lambda p: (0, p)),
                  pl.BlockSpec((1, lslot), lambda p: (0, 0)),
                  wspec, wspec, wspec, sspec],
        out_specs=[pl.BlockSpec((nb, LANES), lambda p: (0, p)), sspec],
        out_shape=[jax.ShapeDtypeStruct((nb, rw), BF16), jax.ShapeDtypeStruct(s_t.shape, F32)],
        scratch_shapes=[pltpu.VMEM((LANES, nb), F32)],
        compiler_params=_params("arbitrary"),
    )(proj, proj, proj, proj, pm, pm, pm, plr, pvec, mul, wd, wa, wg, s_t)
    return ya, jnp.transpose(snew_t, (3, 0, 1, 2))


def _rope_rows(t, cos2, sin2):
    return t * cos2 + pltpu.roll(t, LANES // 2, 1) * sin2


def _head_norm_rows(o, eps):
    mu = jnp.mean(o, axis=-1, keepdims=True)
    d = o - mu
    var = jnp.mean(d * d, axis=-1, keepdims=True)
    return d * lax.rsqrt(var + eps)


def _ret_seq_kernel(q_ref, k_ref, v_ref, g_ref, cos_ref, sin_ref, intra_ref, qd_ref, kd_ref,
                    blk_ref, gnw_ref, gnb_ref, yb_ref, ret_ref, s_ref):
    i = pl.program_id(1)
    nh = s_ref.shape[0]
    dk = s_ref.shape[1]

    @pl.when(i == 0)
    def _():
        s_ref[...] = jnp.zeros_like(s_ref)

    cos2, sin2 = cos_ref[...], sin_ref[...]
    heads = range(nh)
    hs = [slice(h * LANES, (h + 1) * LANES) for h in heads]
    kh = [_rope_rows(k_ref[:, hs[h]].astype(F32), cos2, sin2) * (dk ** -0.5) for h in heads]
    qb = [_bf(_rope_rows(q_ref[:, hs[h]].astype(F32), cos2, sin2)) for h in heads]
    vb = [_bf(v_ref[:, hs[h]]) for h in heads]
    scores = [_bf(_dot_nt(qb[h], _bf(kh[h])) * intra_ref[h]) for h in heads]
    s0 = [s_ref[h] for h in heads]
    o = [_dot(scores[h], vb[h]) + _dot(qb[h], _bf(s0[h])) * qd_ref[h] for h in heads]
    for h in heads:
        s_ref[h] = s0[h] * blk_ref[h] + _dot_tn(_bf(kh[h] * kd_ref[h]), vb[h])
    for h in heads:
        on = _head_norm_rows(o[h], RET_GN_EPS)
        yb_ref[:, hs[h]] = _bf((on * gnw_ref[:, hs[h]] + gnb_ref[:, hs[h]])
                               * jax.nn.silu(g_ref[:, hs[h]].astype(F32)))

    @pl.when(i == pl.num_programs(1) - 1)
    def _():
        ret_ref[0] = s_ref[...]


def _ret_tables(nh, c):
    log_g = jnp.log1p(-jnp.exp2(-5.0 - jnp.arange(nh, dtype=F32)))
    i = jnp.arange(c, dtype=F32)
    rel = i[:, None] - i[None, :]
    intra = jnp.where(rel >= 0, jnp.exp(log_g[:, None, None] * jnp.maximum(rel, 0.0)), 0.0)
    q_decay = jnp.exp(log_g[:, None] * (i + 1.0))
    k_decay = jnp.exp(log_g[:, None] * (c - 1.0 - i))
    blk_decay = jnp.exp(log_g * c)
    return intra, q_decay, k_decay, blk_decay


def _rope_tables(pos, dk):
    half = dk // 2
    inv = ROPE_BASE ** (-jnp.arange(half, dtype=F32) / half)
    ang = pos[:, None] * inv[None, :]
    cos, sin = jnp.cos(ang), jnp.sin(ang)
    return jnp.concatenate([cos, cos], axis=-1), jnp.concatenate([-sin, sin], axis=-1)


def _ret_seq(proj, nb, seq, lay, gnw, gnb):
    qk, rv, nh = lay["qk"], lay["rv"], lay["ret_heads"]
    dk, dv = qk // nh, rv // nh
    assert dk == LANES and dv == LANES
    c = RET_CHUNK if seq % RET_CHUNK == 0 else seq
    assert c % SUBLANES == 0
    nt = seq // c
    intra, qd, kd, blk = _ret_tables(nh, c)
    qd = jnp.broadcast_to(qd[:, :, None], (nh, c, dv))
    kd = jnp.broadcast_to(kd[:, :, None], (nh, c, dk))
    blk = jnp.broadcast_to(blk[:, None, None], (nh, 1, dv))
    cos2, sin2 = _rope_tables(jnp.arange(seq, dtype=F32), dk)
    seg = lambda off, w: pl.BlockSpec((c, w), lambda b, i, off=off, w=w: (b * nt + i, off // w))
    full3 = lambda a: pl.BlockSpec(a.shape, lambda b, i: (0, 0, 0))
    return pl.pallas_call(
        _ret_seq_kernel,
        grid=(nb, nt),
        in_specs=[seg(lay["off_q"], qk), seg(lay["off_kr"], qk), seg(lay["off_vr"], rv),
                  seg(lay["off_rg"], rv),
                  pl.BlockSpec((c, dk), lambda b, i: (i, 0)), pl.BlockSpec((c, dk), lambda b, i: (i, 0)),
                  full3(intra), full3(qd), full3(kd), full3(blk),
                  pl.BlockSpec((1, rv), lambda b, i: (0, 0)), pl.BlockSpec((1, rv), lambda b, i: (0, 0))],
        out_specs=[pl.BlockSpec((c, rv), lambda b, i: (b * nt + i, 0)),
                   pl.BlockSpec((1, nh, dk, dv), lambda b, i: (b, 0, 0, 0))],
        out_shape=[jax.ShapeDtypeStruct((nb * seq, rv), BF16),
                   jax.ShapeDtypeStruct((nb, nh, dk, dv), F32)],
        scratch_shapes=[pltpu.VMEM((nh, dk, dv), F32)],
        compiler_params=_params("arbitrary", "arbitrary"),
    )(proj, proj, proj, proj, cos2, sin2, intra, qd, kd, blk, gnw, gnb)


_RC_INTRA, _RC_QD, _RC_KD, _RC_BLK = range(4)


def _ret_step_kernel(q_ref, k_ref, v_ref, g_ref, cos_ref, sin_ref, rc_ref, gnw_ref, gnb_ref, s_ref,
                     yb_ref, snew_ref, q_s, k_s, v_s, o_s):
    bb = q_ref.shape[0]
    nh = s_ref.shape[1]
    dk = s_ref.shape[2]
    cos2, sin2 = cos_ref[...], sin_ref[...]
    for h in range(nh):
        hs = slice(h * LANES, (h + 1) * LANES)
        q_s[:, hs] = _rope_rows(q_ref[:, hs].astype(F32), cos2, sin2)
        k_s[:, hs] = _rope_rows(k_ref[:, hs].astype(F32), cos2, sin2) * (dk ** -0.5)
        v_s[:, hs] = v_ref[:, hs].astype(F32)
    eye = (lax.broadcasted_iota(jnp.int32, (LANES, LANES), 0)
           == lax.broadcasted_iota(jnp.int32, (LANES, LANES), 1))
    for b in range(bb):
        for h in range(nh):
            hs = slice(h * LANES, (h + 1) * LANES)
            rc = lambda j: rc_ref[h, j:j + 1, :]
            q_row = q_s[b:b + 1, hs]
            k_row = k_s[b:b + 1, hs]
            v_row = v_s[b:b + 1, hs]
            s0 = s_ref[b, h]
            q_col = jnp.sum(jnp.where(eye, q_row, 0.0), axis=-1, keepdims=True)
            k_col = jnp.sum(jnp.where(eye, k_row, 0.0), axis=-1, keepdims=True)
            score = jnp.sum(q_row * k_row, axis=-1, keepdims=True) * rc(_RC_INTRA)
            o_row = score * v_row + jnp.sum(s0 * q_col, axis=0, keepdims=True) * rc(_RC_QD)
            snew_ref[b, h] = s0 * rc(_RC_BLK) + (k_col * rc(_RC_KD)) * v_row
            o_s[b:b + 1, hs] = o_row
    for h in range(nh):
        hs = slice(h * LANES, (h + 1) * LANES)
        on = _head_norm_rows(o_s[:, hs], RET_GN_EPS)
        yb_ref[:, hs] = _bf((on * gnw_ref[:, hs] + gnb_ref[:, hs]) * jax.nn.silu(g_ref[:, hs].astype(F32)))


def _ret_step(proj, nb, lay, gnw, gnb, s_ret, pos0):
    qk, rv, nh = lay["qk"], lay["rv"], lay["ret_heads"]
    dk, dv = qk // nh, rv // nh
    assert dk == LANES and dv == LANES
    bb = 2 * SUBLANES
    assert nb % bb == 0
    intra, qd, kd, blk = _ret_tables(nh, 1)
    rc = jnp.stack([intra[:, 0, 0], qd[:, 0], kd[:, 0], blk], axis=1)
    rc = jnp.pad(rc, ((0, 0), (0, SUBLANES - 4)))
    rc = jnp.broadcast_to(rc[:, :, None], (nh, SUBLANES, LANES))
    cos2, sin2 = _rope_tables(jnp.asarray([pos0], dtype=F32), dk)
    seg = lambda off, w: pl.BlockSpec((bb, w), lambda j, off=off, w=w: (j, off // w))
    sspec = pl.BlockSpec((bb, nh, dk, dv), lambda j: (j, 0, 0, 0))
    row = lambda w: pl.BlockSpec((1, w), lambda j: (0, 0))
    return pl.pallas_call(
        _ret_step_kernel,
        grid=(nb // bb,),
        in_specs=[seg(lay["off_q"], qk), seg(lay["off_kr"], qk), seg(lay["off_vr"], rv),
                  seg(lay["off_rg"], rv), row(dk), row(dk),
                  pl.BlockSpec(rc.shape, lambda j: (0, 0, 0)), row(rv), row(rv), sspec],
        out_specs=[pl.BlockSpec((bb, rv), lambda j: (j, 0)), sspec],
        out_shape=[jax.ShapeDtypeStruct((nb, rv), BF16), jax.ShapeDtypeStruct(s_ret.shape, F32)],
        scratch_shapes=[pltpu.VMEM((bb, qk), F32), pltpu.VMEM((bb, qk), F32), pltpu.VMEM((bb, rv), F32),
                        pltpu.VMEM((bb, rv), F32)],
        compiler_params=_params("arbitrary"),
    )(proj, proj, proj, proj, cos2, sin2, rc, gnw, gnb, s_ret)


def _merge_kernel(alpha, ng, *refs):
    ya_ref, yb_ref = refs[0], refs[1]
    ga_refs = refs[2:2 + ng]
    gb_refs = refs[2 + ng:2 + 2 * ng]
    (x_ref, g1_ref, sc2_ref, sh2_ref, wa_ref, wb_ref, wo_ref, lnw_ref, lnb_ref,
     x1_ref, u2_ref) = refs[2 + 2 * ng:]
    cat = lambda rs: jnp.concatenate([r[...].astype(F32) for r in rs], axis=1)
    merged = (jax.nn.sigmoid(cat(ga_refs)) * _dot(ya_ref[...], wa_ref[...])
              + jax.nn.sigmoid(cat(gb_refs)) * _dot(yb_ref[...], wb_ref[...]))
    t = alpha * x_ref[...] + g1_ref[0] * _dot(_bf(merged), wo_ref[...])
    x1 = _layer_norm_rows(t, lnw_ref[...], lnb_ref[...])
    x1_ref[...] = x1
    u2_ref[...] = _bf(x1 * (1.0 + sc2_ref[0]) + sh2_ref[0])


def _merge(ya, yb, proj, x, g1, sc2, sh2, wba, wbb, wout, lnw, lnb, lay, alpha, tm, tpg):
    m, d = x.shape
    rw, rv = ya.shape[1], yb.shape[1]
    gw = math.gcd(lay["off_ga"], d)
    ng = d // gw
    r = g1.shape[1]
    rowt = lambda w: pl.BlockSpec((tm, w), lambda i: (i, 0))
    gate = lambda off, q: pl.BlockSpec((tm, gw), lambda i, off=off, q=q: (i, off // gw + q))
    mod = pl.BlockSpec((1, r, d), lambda i: (i // tpg, 0, 0))
    const = lambda a: pl.BlockSpec(a.shape, lambda i: (0, 0), pipeline_mode=pl.Buffered(1))
    return pl.pallas_call(
        functools.partial(_merge_kernel, alpha, ng),
        grid=(m // tm,),
        in_specs=[rowt(rw), rowt(rv)]
                 + [gate(lay["off_ga"], q) for q in range(ng)]
                 + [gate(lay["off_gb"], q) for q in range(ng)]
                 + [rowt(d), mod, mod, mod, const(wba), const(wbb), const(wout), const(lnw), const(lnb)],
        out_specs=[rowt(d), rowt(d)],
        out_shape=[jax.ShapeDtypeStruct((m, d), F32), jax.ShapeDtypeStruct((m, d), BF16)],
        compiler_params=_params("arbitrary"),
    )(ya, yb, *([proj] * (2 * ng)), x, g1, sc2, sh2, wba, wbb, wout, lnw, lnb)


def _ffn_seq_kernel(tps, nj, nsteps, alpha, u_ref, wa_ref, wb_ref, cwa_ref, cwb_ref, ia_ref, ib_ref,
                    wd_ref, x1_ref, g2_ref, lnw_ref, lnb_ref, o_ref, ta_ref, tb_ref,
                    ca_s, cb_s, act_s):
    t = pl.program_id(0)
    tm = u_ref.shape[0]

    @pl.when(t == 0)
    def _():
        act_s[...] = jnp.zeros_like(act_s)

    @pl.when((t == 0) | ((t - 1) % nj == 0))
    def _():
        o_ref[...] = jnp.zeros_like(o_ref)

    tu = jnp.minimum(t, nsteps - 1)
    i, jc = tu // nj, tu % nj
    first = i % tps == 0
    u = u_ref[...]
    tn = act_s.shape[2]
    act_prev = act_s[(t + 1) % 2]
    tails = [[], []]
    for c0 in range(0, tn, MXU_WIDTH):
        cs = slice(c0, min(c0 + MXU_WIDTH, tn))
        halves = []
        for k, (w_ref, cw_ref, init_ref, c_s) in enumerate(((wa_ref, cwa_ref, ia_ref, ca_s),
                                                            (wb_ref, cwb_ref, ib_ref, cb_s))):
            h = _dot(u, w_ref[:, cs])
            prev = jnp.where(first, init_ref[0, :, cs], c_s[jc, :, cs])
            rid = lax.broadcasted_iota(jnp.int32, h.shape, 0)
            h1 = jnp.where(rid == 0, prev[SUBLANES - 1:SUBLANES, :], pltpu.roll(h, 1, 0))
            h2 = jnp.where(rid == 0, prev[SUBLANES - 2:SUBLANES - 1, :],
                           jnp.where(rid == 1, prev[SUBLANES - 1:SUBLANES, :], pltpu.roll(h, 2, 0)))
            tail = h[tm - SUBLANES:tm, :]
            c_s[jc, :, cs] = tail
            tails[k].append(tail)
            cw = cw_ref[:, cs]
            halves.append(cw[3:4, :] + cw[0:1, :] * h2 + cw[1:2, :] * h1 + cw[2:3, :] * h)
        act_s[t % 2, :, cs] = _bf(jax.nn.silu(halves[0]) * halves[1])
    o_ref[...] += _dot(act_prev, wd_ref[...])

    @pl.when((i % tps == tps - 1) & (t < nsteps))
    def _():
        ta_ref[i // tps, jc] = jnp.concatenate(tails[0], axis=1)
        tb_ref[i // tps, jc] = jnp.concatenate(tails[1], axis=1)

    @pl.when((t > 0) & (t % nj == 0))
    def _():
        y = alpha * x1_ref[...] + g2_ref[0] * o_ref[...]
        o_ref[...] = _layer_norm_rows(y, lnw_ref[...], lnb_ref[...])


def _ffn_up_step_kernel(u_ref, wa_ref, wb_ref, cwa_ref, cwb_ref, p1a_ref, p2a_ref, p1b_ref, p2b_ref,
                        act_ref, ha_ref, hb_ref):
    u = u_ref[...]
    halves = []
    for w_ref, cw_ref, p1_ref, p2_ref, h_ref in ((wa_ref, cwa_ref, p1a_ref, p2a_ref, ha_ref),
                                                 (wb_ref, cwb_ref, p1b_ref, p2b_ref, hb_ref)):
        h = _dot(u, w_ref[...])
        h_ref[...] = h
        cw = cw_ref[...]
        halves.append(cw[3:4, :] + cw[0:1, :] * p2_ref[...] + cw[1:2, :] * p1_ref[...] + cw[2:3, :] * h)
    act_ref[...] = _bf(jax.nn.silu(halves[0]) * halves[1])


def _ffn_up(u2, w_up_bf, cwt, nb, seq, s_conv, tm):
    m, d = u2.shape
    f2 = w_up_bf.shape[1]
    f = f2 // 2
    tn = _pick_tile(f, (512, 256, 128))
    nj = f // tn
    w_a = pl.BlockSpec((d, tn), lambda i, j: (0, j))
    w_b = pl.BlockSpec((d, tn), lambda i, j: (0, nj + j))
    cw_a = pl.BlockSpec((4, tn), lambda i, j: (0, j))
    cw_b = pl.BlockSpec((4, tn), lambda i, j: (0, nj + j))
    assert seq == 1
    p_a = pl.BlockSpec((tm, tn), lambda i, j: (i, j))
    p_b = pl.BlockSpec((tm, tn), lambda i, j: (i, nj + j))
    prev1, prev2 = s_conv[:, 1, :], s_conv[:, 0, :]
    act, h_a, h_b = pl.pallas_call(
        _ffn_up_step_kernel,
        grid=(m // tm, nj),
        in_specs=[pl.BlockSpec((tm, d), lambda i, j: (i, 0)), w_a, w_b, cw_a, cw_b, p_a, p_a, p_b, p_b],
        out_specs=[pl.BlockSpec((tm, tn), lambda i, j: (i, j))] * 3,
        out_shape=[jax.ShapeDtypeStruct((m, f), BF16), jax.ShapeDtypeStruct((m, f), F32),
                   jax.ShapeDtypeStruct((m, f), F32)],
        compiler_params=_params("arbitrary", "arbitrary"),
    )(u2, w_up_bf, w_up_bf, cwt, cwt, prev1, prev2, prev1, prev2)
    conv_new = jnp.stack([prev1, jnp.concatenate([h_a, h_b], axis=-1)], axis=1)
    return act, conv_new


def _ffn_seq(u2, w_up_bf, cwt, w_down_bf, x1, g2, lnw, lnb, alpha, nb, seq, s_conv, tm):
    m, d = u2.shape
    f2 = w_up_bf.shape[1]
    f = f2 // 2
    tn = _pick_tile(f, (512, 256, 128))
    nj = f // tn
    assert seq % tm == 0 and tm >= SUBLANES
    tps = seq // tm
    nsteps = (m // tm) * nj
    up_i = lambda t: jnp.minimum(t, nsteps - 1) // nj
    up_j = lambda t: jnp.minimum(t, nsteps - 1) % nj
    dn_i = lambda t: jnp.maximum(t - 1, 0) // nj
    dn_j = lambda t: jnp.maximum(t - 1, 0) % nj
    w_a = pl.BlockSpec((d, tn), lambda t: (0, up_j(t)))
    w_b = pl.BlockSpec((d, tn), lambda t: (0, nj + up_j(t)))
    cw_a = pl.BlockSpec((4, tn), lambda t: (0, up_j(t)))
    cw_b = pl.BlockSpec((4, tn), lambda t: (0, nj + up_j(t)))
    init = jnp.pad(s_conv, ((0, 0), (SUBLANES - 2, 0), (0, 0)))
    i_a = pl.BlockSpec((1, SUBLANES, tn), lambda t: (up_i(t) // tps, 0, up_j(t)))
    i_b = pl.BlockSpec((1, SUBLANES, tn), lambda t: (up_i(t) // tps, 0, nj + up_j(t)))
    w_d = pl.BlockSpec((tn, d), lambda t: (dn_j(t), 0))
    row_up = pl.BlockSpec((tm, d), lambda t: (up_i(t), 0))
    row_dn = pl.BlockSpec((tm, d), lambda t: (dn_i(t), 0))
    row_x1 = pl.BlockSpec((tm, d), lambda t: (dn_i(t), 0))
    vec = pl.BlockSpec((1, d), lambda t: (0, 0))
    t_o = pl.BlockSpec((nb, nj, SUBLANES, tn), lambda t: (0, 0, 0, 0))
    tails = jax.ShapeDtypeStruct((nb, nj, SUBLANES, tn), F32)
    x2, t_a, t_b = pl.pallas_call(
        functools.partial(_ffn_seq_kernel, tps, nj, nsteps, alpha),
        grid=(nsteps + 1,),
        in_specs=[row_up, w_a, w_b, cw_a, cw_b, i_a, i_b, w_d, row_x1,
                  pl.BlockSpec((1, 1, d), lambda t: (dn_i(t) // tps, 0, 0)), vec, vec],
        out_specs=[row_dn, t_o, t_o],
        out_shape=[jax.ShapeDtypeStruct((m, d), F32), tails, tails],
        scratch_shapes=[pltpu.VMEM((nj, SUBLANES, tn), F32), pltpu.VMEM((nj, SUBLANES, tn), F32),
                        pltpu.VMEM((2, tm, tn), BF16)],
        compiler_params=_params("arbitrary"),
    )(u2, w_up_bf, w_up_bf, cwt, cwt, init, init, w_down_bf, x1, g2, lnw, lnb)
    rows = lambda t: t[:, :, SUBLANES - 2:, :].transpose(0, 2, 1, 3).reshape(nb, 2, f)
    return x2, jnp.concatenate([rows(t_a), rows(t_b)], axis=-1)


def _ffn_down_kernel(alpha, act_ref, w_ref, x1_ref, g2_ref, lnw_ref, lnb_ref, o_ref, acc_ref):
    k = pl.program_id(1)

    @pl.when(k == 0)
    def _():
        acc_ref[...] = jnp.zeros_like(acc_ref)

    acc_ref[...] += _dot(act_ref[...], w_ref[...])

    @pl.when(k == pl.num_programs(1) - 1)
    def _():
        t = alpha * x1_ref[...] + g2_ref[0] * acc_ref[...]
        o_ref[...] = _layer_norm_rows(t, lnw_ref[...], lnb_ref[...])


def _ffn_down(act, w_down_bf, x1, g2, lnw, lnb, alpha, tm, tpg):
    m, f = act.shape
    d = w_down_bf.shape[1]
    tk = _pick_tile(f, (512, 256, 128))
    r = g2.shape[1]
    return pl.pallas_call(
        functools.partial(_ffn_down_kernel, alpha),
        grid=(m // tm, f // tk),
        in_specs=[pl.BlockSpec((tm, tk), lambda i, k: (i, k)),
                  pl.BlockSpec((tk, d), lambda i, k: (k, 0)),
                  pl.BlockSpec((tm, d), lambda i, k: (i, 0)),
                  pl.BlockSpec((1, r, d), lambda i, k: (i // tpg, 0, 0)),
                  pl.BlockSpec((1, d), lambda i, k: (0, 0)),
                  pl.BlockSpec((1, d), lambda i, k: (0, 0))],
        out_specs=pl.BlockSpec((tm, d), lambda i, k: (i, 0)),
        out_shape=jax.ShapeDtypeStruct((m, d), F32),
        scratch_shapes=[pltpu.VMEM((tm, d), F32)],
        compiler_params=_params("arbitrary", "arbitrary"),
    )(act, w_down_bf, x1, g2, lnw, lnb)


def _layout(d, rw, qk, rv, lora_w, ret_heads):
    lslot = next(s for s in (128, 256, 512, 1024, 2048) if s >= lora_w)
    off_q = 3 * rw
    off_kr = off_q + qk
    off_vr = off_kr + qk
    off_rg = off_vr + rv
    off_ga = off_rg + rv
    off_gb = off_ga + d
    off_l = off_gb + d
    assert rw % LANES == 0 and off_l % lslot == 0
    assert off_q % qk == 0 and off_kr % qk == 0 and off_vr % rv == 0 and off_rg % rv == 0
    return dict(d=d, rw=rw, qk=qk, rv=rv, lslot=lslot, lora_w=lora_w, ret_heads=ret_heads,
                off_q=off_q, off_kr=off_kr, off_vr=off_vr, off_rg=off_rg, off_ga=off_ga,
                off_gb=off_gb, off_l=off_l, nt=off_l + lslot)


def _prep_weights(lay, w_in, shift_mu, w0, w_decay_up, a0, w_aaa_up, w_gate_up, k_k, k_a, r_k,
                  lnx_w, lnx_b):
    rw, lslot, lora_w = lay["rw"], lay["lslot"], lay["lora_w"]
    w_perm = _bf(jnp.transpose(w_in))
    dl, al, gl = w_decay_up.shape[0], w_aaa_up.shape[0], w_gate_up.shape[0]
    pad_rows = lambda w, lo: jnp.pad(w, ((lo, lslot - lo - w.shape[0]), (0, 0))).astype(BF16)
    wd = pad_rows(w_decay_up, 0)
    wa = pad_rows(w_aaa_up, dl)
    wg = pad_rows(w_gate_up, dl + al)
    rows = [shift_mu[:rw], shift_mu[rw:2 * rw], shift_mu[2 * rw:3 * rw], w0, a0, k_k, k_a,
            r_k.reshape(-1), lnx_w, lnx_b]
    pvec = jnp.pad(jnp.stack(rows, axis=0), ((0, _PV_ROWS - len(rows)), (0, 0)))
    mul = jnp.pad(shift_mu[3 * rw:], (0, lslot - lora_w))[None, :]
    return w_perm, wd, wa, wg, pvec, mul


def _run_layer(x2d, nb, seq, ada, states, pos0, lay, wts, alpha):
    d, rw, lslot, lora_w = lay["d"], lay["rw"], lay["lslot"], lay["lora_w"]
    (w_perm, wd, wa, wg, pvec, mul, gnw, gnb, wba, wbb, wout, ln1w, ln1b,
     w_up, cwt, w_down, ln2w, ln2b) = wts
    m = nb * seq
    sh1, sc1, g1, sh2, sc2, g2 = jnp.split(ada, 6, axis=-1)
    if seq == 1:
        tm, tpg = m, 1
        shape_mod = lambda t: t[None]
    else:
        tm = _pick_tile(seq, (512, 256, 128, 64, 32, 16, 8))
        tpg = seq // tm
        shape_mod = lambda t: t[:, None, :]
    sh1, sc1, g1, sh2, sc2, g2 = map(shape_mod, (sh1, sc1, g1, sh2, sc2, g2))
    tm_in = _pick_tile(seq, (1024, 512, 256, 128, 64, 32, 16, 8)) if seq > 1 else m
    proj = _modmm(x2d, sc1, sh1, w_perm, lay, tm_in, (seq // tm_in) if seq > 1 else 1)

    s_wkv, s_shift, s_ret, s_conv = states
    shift_main = s_shift[:, :3 * rw]
    shift_lora = jnp.pad(s_shift[:, 3 * rw:], ((0, 0), (0, lslot - lora_w)))
    if seq == 1:
        ya, wkv_new = _rwkv_step(proj, nb, lay, (shift_main, shift_lora), pvec, mul, wd, wa, wg, s_wkv)
        yb, ret_new = _ret_step(proj, nb, lay, gnw, gnb, s_ret, pos0)
    else:
        first = (shift_main[:, None, :], shift_main[:, None, :], shift_main[:, None, :],
                 shift_lora[:, None, :])
        ya, wkv_new = _rwkv_seq(proj, nb, seq, lay, first, pvec, mul, wd, wa, wg)
        yb, ret_new = _ret_seq(proj, nb, seq, lay, gnw, gnb)
    last = proj.reshape(nb, seq, lay["nt"])[:, -1, :]
    shift_new = jnp.concatenate([last[:, :3 * rw], last[:, lay["off_l"]:lay["off_l"] + lora_w]],
                                axis=-1).astype(F32)

    tm_merge = min(tm, 256)
    x1, u2 = _merge(ya, yb, proj, x2d, g1, sc2, sh2, wba, wbb, wout, ln1w, ln1b, lay, alpha,
                    tm_merge, (seq // tm_merge) if seq > 1 else 1)
    if seq == 1:
        act, conv_new = _ffn_up(u2, w_up, cwt, nb, seq, s_conv, tm)
        x2 = _ffn_down(act, w_down, x1, g2, ln2w, ln2b, alpha, tm, tpg)
    else:
        x2, conv_new = _ffn_seq(u2, w_up, cwt, w_down, x1, g2, ln2w, ln2b, alpha, nb, seq, s_conv, tm)
    return x2, wkv_new, shift_new, ret_new, conv_new


def kernel(x_prompt, x_sample, c_prompt, c_sample, state_wkv, state_shift, state_ret, state_conv, w_ada, b_ada, w_in, shift_mu, w0, w_decay_up, a0, w_aaa_up, w_gate_up, k_k, k_a, r_k, lnx_w, lnx_b, ret_gn_w, ret_gn_b, w_branch_a, w_branch_b, w_out, ln1_w, ln1_b, w_up, conv_w, conv_b, w_down, ln2_w, ln2_b):
    depth = w_ada.shape[0]
    nbp, seq_p, d = x_prompt.shape
    nbs, seq_s, _ = x_sample.shape
    assert seq_s == 1
    rw = k_k.shape[-1]
    ret_heads, dk, dv = state_ret.shape[2:]
    lora_w = w_decay_up.shape[1] + w_aaa_up.shape[1] + w_gate_up.shape[1]
    lay = _layout(d, rw, ret_heads * dk, ret_heads * dv, lora_w, ret_heads)
    heads, hn = r_k.shape[1:]
    assert hn == LANES // 2 and heads * hn == rw
    alpha = (2.0 * depth) ** 0.25
    f2 = w_up.shape[-1]

    xp = x_prompt.reshape(nbp * seq_p, d)
    xs = x_sample.reshape(nbs * seq_s, d)
    c_all = jnp.concatenate([c_prompt, c_sample], axis=0)
    pad = (-c_all.shape[0]) % SUBLANES
    c_all = _bf(jnp.pad(c_all, ((0, pad), (0, 0))))

    outs_p, outs_s = [], []
    for l in range(depth):
        w_perm, wd, wa, wg, pvec, mul = _prep_weights(
            lay, w_in[l], shift_mu[l], w0[l], w_decay_up[l], a0[l], w_aaa_up[l], w_gate_up[l],
            k_k[l], k_a[l], r_k[l], lnx_w[l], lnx_b[l])
        cwt = jnp.concatenate([conv_w[l], conv_b[l][None, :]], axis=0)
        wts = (w_perm, wd, wa, wg, pvec, mul, ret_gn_w[l][None, :], ret_gn_b[l][None, :],
               _bf(w_branch_a[l]), _bf(w_branch_b[l]), _bf(w_out[l]), ln1_w[l][None, :], ln1_b[l][None, :],
               _bf(w_up[l]), cwt, _bf(w_down[l]), ln2_w[l][None, :], ln2_b[l][None, :])
        ada = _mm_bias(c_all, w_ada[l], b_ada[l][None, :])
        zero_states = (None, jnp.zeros((nbp, state_shift.shape[-1]), F32), None,
                       jnp.zeros((nbp, state_conv.shape[2], f2), F32))
        xp, *st_p = _run_layer(xp, nbp, seq_p, ada[:nbp], zero_states, 0.0, lay, wts, alpha)
        xs, *st_s = _run_layer(xs, nbs, seq_s, ada[nbp:nbp + nbs],
                               (state_wkv[l], state_shift[l], state_ret[l], state_conv[l]),
                               float(PAST_LEN), lay, wts, alpha)
        outs_p.append(st_p)
        outs_s.append(st_s)

    def stack(lst, j, ref):
        layers = [s[j].astype(ref.dtype) for s in lst]
        return layers[0][None] if depth == 1 else jnp.stack(layers, axis=0)

    refs = (state_wkv, state_shift, state_ret, state_conv)
    return (xp.reshape(x_prompt.shape), xs.reshape(x_sample.shape),
            *[stack(outs_p, j, refs[j]) for j in range(4)],
            *[stack(outs_s, j, refs[j]) for j in range(4)])
```

```python
import functools
import math

import jax
import jax.numpy as jnp
from jax import lax
from jax.experimental import pallas as pl
from jax.experimental.pallas import tpu as pltpu

F32 = jnp.float32
BF16 = jnp.bfloat16

PAST_LEN = 16384
ROPE_BASE = 10000.0
RWKV_GN_EPS = 64e-5
RET_GN_EPS = 1e-5
LN_EPS = 1e-5
RET_CHUNK = 128

LANES = 128
SUBLANES = 8
MXU_WIDTH = 256
VMEM_LIMIT_BYTES = 50 * 1024 * 1024

RWKV_CHUNK = 64
RWKV_PAIRS_PER_STEP = 8


def _params(*sem):
    return pltpu.CompilerParams(dimension_semantics=sem, vmem_limit_bytes=VMEM_LIMIT_BYTES)


def _dot(a, b):
    return jnp.dot(a, b, preferred_element_type=F32)


def _dot_nt(a, b):
    return lax.dot_general(a, b, (((1,), (1,)), ((), ())), preferred_element_type=F32)


def _dot_tn(a, b):
    return lax.dot_general(a, b, (((0,), (0,)), ((), ())), preferred_element_type=F32)


def _bf(x):
    return x.astype(BF16)


def _split2(x):
    hi = x.astype(BF16)
    lo = (x - hi.astype(F32)).astype(BF16)
    return hi, lo


def _dot3_nn(a, b):
    ah, al = _split2(a)
    bh, bl = _split2(b)
    return _dot(jnp.concatenate([ah, ah, al], axis=1), jnp.concatenate([bh, bl, bh], axis=0))


def _dot3_nt(a, b):
    ah, al = _split2(a)
    bh, bl = _split2(b)
    return _dot_nt(jnp.concatenate([ah, ah, al], axis=1), jnp.concatenate([bh, bl, bh], axis=1))


def _dot_exact_lhs(a_bf, b):
    b1 = b.astype(BF16)
    r1 = b - b1.astype(F32)
    b2 = r1.astype(BF16)
    b3 = (r1 - b2.astype(F32)).astype(BF16)
    return _dot(jnp.concatenate([a_bf, a_bf, a_bf], axis=1), jnp.concatenate([b1, b2, b3], axis=0))


def _layer_norm_rows(t, w, b):
    mu = jnp.mean(t, axis=-1, keepdims=True)
    d = t - mu
    var = jnp.mean(d * d, axis=-1, keepdims=True)
    return d * lax.rsqrt(var + LN_EPS) * w + b


def _pick_tile(n, candidates):
    for c in candidates:
        if n % c == 0:
            return c
    return n


def _mm_bias_kernel(x_ref, w_ref, b_ref, o_ref):
    o_ref[...] = _dot(x_ref[...], _bf(w_ref[...])) + b_ref[...]


def _mm_bias(x_bf, w, b_row):
    m, k = x_bf.shape
    n = w.shape[1]
    tn = _pick_tile(n, (1024, 512, 256, 128))
    return pl.pallas_call(
        _mm_bias_kernel,
        grid=(n // tn,),
        in_specs=[pl.BlockSpec((m, k), lambda j: (0, 0)),
                  pl.BlockSpec((k, tn), lambda j: (0, j)),
                  pl.BlockSpec((1, tn), lambda j: (0, j))],
        out_specs=pl.BlockSpec((m, tn), lambda j: (0, j)),
        out_shape=jax.ShapeDtypeStruct((m, n), F32),
        compiler_params=_params("arbitrary"),
    )(x_bf, w, b_row)


def _modmm_kernel(x_ref, sc_ref, sh_ref, wt_ref, o_ref, u_ref):
    @pl.when(pl.program_id(1) == 0)
    def _():
        u_ref[...] = _bf(x_ref[...] * (1.0 + sc_ref[0]) + sh_ref[0])

    o_ref[...] = _dot_nt(u_ref[...], wt_ref[...]).astype(o_ref.dtype)


def _modmm_cast_kernel(x_ref, sc_ref, sh_ref, wt_ref, o_ref, wbf_ref, u_ref):
    @pl.when(pl.program_id(1) == 0)
    def _():
        u_ref[...] = _bf(x_ref[...] * (1.0 + sc_ref[0]) + sh_ref[0])

    w = _bf(wt_ref[...])
    wbf_ref[...] = w
    o_ref[...] = _dot_nt(u_ref[...], w).astype(o_ref.dtype)


def _modmm(x, sc, sh, wt, lay, tm, tpg):
    m, d = x.shape
    n = lay["nt"]
    tn = _pick_tile(n, (512, 256, 128))
    r = sc.shape[1]
    mod_spec = pl.BlockSpec((1, r, d), lambda i, j: (i // tpg, 0, 0))
    x_spec = pl.BlockSpec((tm, d), lambda i, j: (i, 0))
    o_spec = pl.BlockSpec((tm, tn), lambda i, j: (i, j))
    w_blocked = pl.BlockSpec((tn, d), lambda i, j: (j, 0))
    if wt.dtype == BF16:
        return pl.pallas_call(
            _modmm_kernel,
            grid=(m // tm, n // tn),
            in_specs=[x_spec, mod_spec, mod_spec, w_blocked],
            out_specs=o_spec,
            out_shape=jax.ShapeDtypeStruct((m, n), BF16),
            scratch_shapes=[pltpu.VMEM((tm, d), BF16)],
            compiler_params=_params("arbitrary", "arbitrary"),
        )(x, sc, sh, wt)

    assert m == tm
    rw3, lora_w, lslot = 3 * lay["rw"], lay["lora_w"], lay["lslot"]
    n_main, n_rest = rw3 // tn, (lay["off_l"] - rw3) // tn
    assert rw3 % tn == 0 and (lay["off_l"] - rw3) % tn == 0 and lslot == tn
    assert rw3 + lslot <= wt.shape[0]
    assert tn % SUBLANES == 0 and rw3 % SUBLANES == 0 and lora_w % SUBLANES == 0

    def w_row(j):
        row = jnp.where(j < n_main, j * tn,
                        jnp.where(j < n_main + n_rest, rw3 + lora_w + (j - n_main) * tn, rw3))
        return pl.multiple_of(row, SUBLANES)

    return pl.pallas_call(
        _modmm_cast_kernel,
        grid=(1, n // tn),
        in_specs=[x_spec, mod_spec, mod_spec,
                  pl.BlockSpec((pl.Element(tn), pl.Element(d)), lambda i, j: (w_row(j), 0))],
        out_specs=[o_spec, w_blocked],
        out_shape=[jax.ShapeDtypeStruct((m, n), BF16), jax.ShapeDtypeStruct((n, d), BF16)],
        scratch_shapes=[pltpu.VMEM((tm, d), BF16)],
        compiler_params=_params("arbitrary", "arbitrary"),
    )(x, sc, sh, wt)


_PV_MU_R, _PV_MU_K, _PV_MU_V, _PV_W0, _PV_A0, _PV_KK, _PV_KA, _PV_RK, _PV_LNW, _PV_LNB = range(10)
_PV_ROWS = 16


def _head_half_mask(shape):
    return lax.broadcasted_iota(jnp.int32, shape, 1) < (LANES // 2)


def _head_sums(x):
    h0 = _head_half_mask((x.shape[0], LANES))
    parts = []
    for p in range(x.shape[1] // LANES):
        xs = x[:, p * LANES:(p + 1) * LANES]
        s0 = jnp.sum(jnp.where(h0, xs, 0.0), axis=-1, keepdims=True)
        s1 = jnp.sum(jnp.where(h0, 0.0, xs), axis=-1, keepdims=True)
        parts.append(jnp.where(h0, s0, s1))
    return parts[0] if len(parts) == 1 else jnp.concatenate(parts, axis=1)


def _rwkv_tokens(zr, zk, zv, zl, pr, pk, pv_, plr, pvec, mul, wd, wa, wg):
    row = lambda i: pvec[i:i + 1, :]
    r = zr + row(_PV_MU_R) * (pr - zr)
    k = zk + row(_PV_MU_K) * (pk - zk)
    v = zv + row(_PV_MU_V) * (pv_ - zv)
    ls = zl + mul * (plr - zl)
    wl = _dot(_bf(jnp.tanh(ls)), wd)
    al = _dot(_bf(ls), wa)
    g = _dot(_bf(jax.nn.sigmoid(ls)), wg)
    w = -jax.nn.softplus(-(row(_PV_W0) + wl)) - 0.5
    logd = -jnp.exp(w)
    a = jax.nn.sigmoid(row(_PV_A0) + al)
    kkr = k * row(_PV_KK)
    nrm = jnp.sqrt(_head_sums(kkr * kkr))
    kk = kkr / jnp.maximum(nrm, 1e-12)
    kp = k * (1.0 + (a - 1.0) * row(_PV_KA))
    bonus = _head_sums(r * kp * row(_PV_RK)) * v
    return r, kp, v, kk, a, logd, g, bonus


def _rwkv_finish(o, bonus, g, pvec):
    inv_n = 1.0 / (LANES // 2)
    mu = _head_sums(o) * inv_n
    d = o - mu
    var = _head_sums(d * d) * inv_n
    on = d * lax.rsqrt(var + RWKV_GN_EPS)
    return (on * pvec[_PV_LNW:_PV_LNW + 1, :] + pvec[_PV_LNB:_PV_LNB + 1, :] + bonus) * g


def _shift_rows(z, first_row):
    rolled = pltpu.roll(z, 1, 0)
    rowid = lax.broadcasted_iota(jnp.int32, z.shape, 0)
    return jnp.where(rowid == 0, first_row, rolled)


def _rwkv_seq_kernel(zr_ref, zk_ref, zv_ref, zl_ref, fr_ref, fk_ref, fv_ref, fl_ref,
                     pvec_ref, mul_ref, wd_ref, wa_ref, wg_ref, tri_ref,
                     ya_ref, wkv_ref, s_ref, cr_ref, ck_ref, cv_ref, cl_ref):
    i = pl.program_id(2)
    tc, width = zr_ref.shape
    npp = width // LANES
    c = RWKV_CHUNK
    nch = tc // c
    half = LANES // 2

    @pl.when(i == 0)
    def _():
        s_ref[...] = jnp.zeros_like(s_ref)
        cr_ref[0:1, :] = fr_ref[0]
        ck_ref[0:1, :] = fk_ref[0]
        cv_ref[0:1, :] = fv_ref[0]
        cl_ref[0:1, :] = fl_ref[0]

    zr, zk, zv, zl = (ref[...].astype(F32) for ref in (zr_ref, zk_ref, zv_ref, zl_ref))
    pvec = pvec_ref[...]
    r, kp, v, kk, a, logd, g, bonus = _rwkv_tokens(
        zr, zk, zv, zl,
        _shift_rows(zr, cr_ref[0:1, :]), _shift_rows(zk, ck_ref[0:1, :]),
        _shift_rows(zv, cv_ref[0:1, :]), _shift_rows(zl, cl_ref[0:1, :]),
        pvec, mul_ref[...], wd_ref[...], wa_ref[...], wg_ref[...])
    cr_ref[0:1, :] = zr[tc - 1:tc, :]
    ck_ref[0:1, :] = zk[tc - 1:tc, :]
    cv_ref[0:1, :] = zv[tc - 1:tc, :]
    cl_ref[0:1, :] = zl[tc - 1:tc, :]

    alpha = -kk
    beta = kk * a
    cum_incl = _dot_exact_lhs(tri_ref[...], logd)
    cum_excl = cum_incl - logd
    tot_rows = [cum_incl[(ci + 1) * c - 1:(ci + 1) * c, :] for ci in range(nch)]
    tot = jnp.concatenate([jnp.broadcast_to(t, (c, width)) for t in tot_rows], axis=0)
    e_neg = jnp.exp(-cum_incl)
    e_hat = jnp.exp(tot - cum_incl)
    r_t = r * jnp.exp(cum_incl)
    a_t = alpha * jnp.exp(cum_excl)
    b_t = beta * e_neg
    k_t = kp * e_neg
    b_h = beta * e_hat
    k_h = kp * e_hat

    h0 = _head_half_mask((c, LANES))

    def stack(x, p, ci):
        xb = x[ci * c:(ci + 1) * c, p * LANES:(p + 1) * LANES]
        return jnp.concatenate([jnp.where(h0, xb, 0.0), jnp.where(h0, 0.0, xb)], axis=0)

    rr = lax.broadcasted_iota(jnp.int32, (2 * c, 2 * c), 0)
    cc = lax.broadcasted_iota(jnp.int32, (2 * c, 2 * c), 1)
    same = (rr >= c) == (cc >= c)
    strict = same & (cc < rr)
    incl = same & (cc <= rr)
    eye = rr == cc
    zeros_blk = jnp.zeros((2 * c, LANES), BF16)

    probs = [(p, ci) for p in range(npp) for ci in range(nch)]
    a_s = {q: stack(a_t, *q) for q in probs}
    r_s = {q: stack(r_t, *q) for q in probs}
    v_bf = {q: _bf(stack(v, *q)) for q in probs}

    pmat, a_ak, a_r = {}, {}, {}
    for q in probs:
        amat = _dot_nt(_bf(jnp.concatenate([a_s[q], r_s[q]], axis=0)),
                       _bf(jnp.concatenate([stack(b_t, *q), stack(k_t, *q)], axis=0)))
        pmat[q] = jnp.where(strict, amat[:2 * c, :2 * c], 0.0)
        a_ak[q] = _bf(jnp.where(strict, amat[:2 * c, 2 * c:], 0.0))
        a_r[q] = _bf(jnp.concatenate([jnp.where(incl, amat[2 * c:, :2 * c], 0.0),
                                      jnp.where(incl, amat[2 * c:, 2 * c:], 0.0)], axis=1))

    x = {q: jnp.concatenate([a_s[q], _dot(a_ak[q], v_bf[q])], axis=1) for q in probs}

    nsteps = int(math.log2(c))
    for it in range(nsteps):
        for q in probs:
            p_bf = _bf(pmat[q])
            x[q] = x[q] + _dot(p_bf, _bf(x[q]))
            if it + 1 < nsteps:
                pmat[q] = _dot(p_bf, p_bf)

    r_pair, o_pair, g_t, h_t = {}, {}, {}, {}
    for q in probs:
        p, ci = q
        x_bf = _bf(x[q])
        rhs = jnp.concatenate([x_bf, jnp.concatenate([zeros_blk, v_bf[q]], axis=1)], axis=0)
        y = _dot(a_r[q], rhs)
        r_hat = r_s[q] + y[:, :LANES]
        r_pair[q] = r_hat[:c] + r_hat[c:]
        o_pair[q] = y[:c, LANES:] + y[c:, LANES:]
        z = _dot_tn(x_bf, _bf(stack(b_h, *q)))
        w_c = jnp.exp(tot_rows[ci][:, p * LANES:(p + 1) * LANES])
        g_t[q] = jnp.where(eye, w_c, 0.0) + z[:LANES]
        h_t[q] = z[LANES:] + _dot_tn(v_bf[q], _bf(stack(k_h, *q)))

    outs = {}
    for ci in range(nch):
        for p in range(npp):
            q = (p, ci)
            s0 = s_ref[p]
            outs[q] = _dot3_nt(r_pair[q], s0) + o_pair[q]
            s_ref[p] = _dot3_nn(s0, g_t[q]) + h_t[q]

    cols = [jnp.concatenate([outs[(p, ci)] for ci in range(nch)], axis=0) if nch > 1 else outs[(p, 0)]
            for p in range(npp)]
    o = jnp.concatenate(cols, axis=1) if npp > 1 else cols[0]
    ya_ref[...] = _bf(_rwkv_finish(o, bonus, g, pvec))

    @pl.when(i == pl.num_programs(2) - 1)
    def _():
        for p in range(npp):
            s = s_ref[p]
            wkv_ref[0, 2 * p] = s[:half, :half]
            wkv_ref[0, 2 * p + 1] = s[half:, half:]


def _rwkv_seq(proj, nb, seq, lay, first, pvec, mul, wd, wa, wg):
    rw, lslot = lay["rw"], lay["lslot"]
    npair = rw // LANES
    npp = RWKV_PAIRS_PER_STEP if npair % RWKV_PAIRS_PER_STEP == 0 else 1
    ngrp = npair // npp
    width = npp * LANES
    c = RWKV_CHUNK
    tc = _pick_tile(seq, (2 * c, c))
    nt = seq // tc
    t_idx = jnp.arange(tc)
    tri = ((t_idx[:, None] // c == t_idx[None, :] // c) & (t_idx[None, :] <= t_idx[:, None])).astype(BF16)
    fr, fk, fv, fl = first
    col = lambda s: pl.BlockSpec((tc, width), lambda b, gp, i, s=s: (b * nt + i, s * ngrp + gp))
    fcol = lambda s: pl.BlockSpec((1, 1, width), lambda b, gp, i, s=s: (b, 0, s * ngrp + gp))
    wspec = pl.BlockSpec((lslot, width), lambda b, gp, i: (0, gp))
    return pl.pallas_call(
        _rwkv_seq_kernel,
        grid=(nb, ngrp, nt),
        in_specs=[col(0), col(1), col(2),
                  pl.BlockSpec((tc, lslot), lambda b, gp, i: (b * nt + i, lay["off_l"] // lslot)),
                  fcol(0), fcol(1), fcol(2),
                  pl.BlockSpec((1, 1, lslot), lambda b, gp, i: (b, 0, 0)),
                  pl.BlockSpec((_PV_ROWS, width), lambda b, gp, i: (0, gp)),
                  pl.BlockSpec((1, lslot), lambda b, gp, i: (0, 0)),
                  wspec, wspec, wspec,
                  pl.BlockSpec((tc, tc), lambda b, gp, i: (0, 0))],
        out_specs=[pl.BlockSpec((tc, width), lambda b, gp, i: (b * nt + i, gp)),
                   pl.BlockSpec((1, 2 * npp, LANES // 2, LANES // 2), lambda b, gp, i: (b, gp, 0, 0))],
        out_shape=[jax.ShapeDtypeStruct((nb * seq, rw), BF16),
                   jax.ShapeDtypeStruct((nb, 2 * npair, LANES // 2, LANES // 2), F32)],
        scratch_shapes=[pltpu.VMEM((npp, LANES, LANES), F32),
                        pltpu.VMEM((SUBLANES, width), F32), pltpu.VMEM((SUBLANES, width), F32),
                        pltpu.VMEM((SUBLANES, width), F32), pltpu.VMEM((SUBLANES, lslot), F32)],
        compiler_params=_params("arbitrary", "arbitrary", "arbitrary"),
    )(proj, proj, proj, proj, fr, fk, fv, fl, pvec, mul, wd, wa, wg, tri)


def _rwkv_step_kernel(zr_ref, zk_ref, zv_ref, zl_ref, pr_ref, pk_ref, pv_ref, plr_ref,
                      pvec_ref, mul_ref, wd_ref, wa_ref, wg_ref, s_ref,
                      ya_ref, snew_ref, o_s):
    half = LANES // 2
    pvec = pvec_ref[...]
    r, kp, v, kk, a, logd, g, bonus = _rwkv_tokens(
        zr_ref[...].astype(F32), zk_ref[...].astype(F32), zv_ref[...].astype(F32),
        zl_ref[...].astype(F32), pr_ref[...], pk_ref[...], pv_ref[...], plr_ref[...],
        pvec, mul_ref[...], wd_ref[...], wa_ref[...], wg_ref[...])
    w = jnp.exp(logd)
    nkk_t, wr_t, w_t, beta_t, kp_t, r_t, v_t = (jnp.transpose(x) for x in (-kk, w * r, w, kk * a, kp, r, v))
    for e in range(2):
        ks = slice(e * half, (e + 1) * half)
        nkk_e, wr_e, w_e, beta_e, kp_e = nkk_t[ks], wr_t[ks], w_t[ks], beta_t[ks], kp_t[ks]
        c_beta = jnp.sum(beta_e * r_t[ks], axis=0, keepdims=True)
        c_k = jnp.sum(kp_e * r_t[ks], axis=0, keepdims=True)
        for vi in range(half):
            row = e * half + vi
            s = s_ref[e, vi]
            sa = jnp.sum(s * nkk_e, axis=0, keepdims=True)
            sw = jnp.sum(s * wr_e, axis=0, keepdims=True)
            v_row = v_t[row:row + 1, :]
            snew_ref[e, vi] = s * w_e + sa * beta_e + v_row * kp_e
            o_s[row:row + 1, :] = sw + sa * c_beta + v_row * c_k
    ya_ref[...] = _bf(_rwkv_finish(jnp.transpose(o_s[...]), bonus, g, pvec))


def _rwkv_step(proj, nb, lay, prev, pvec, mul, wd, wa, wg, s_wkv):
    rw, lslot = lay["rw"], lay["lslot"]
    npair = rw // LANES
    pm, plr = prev
    half = LANES // 2
    s_t = jnp.transpose(s_wkv, (1, 2, 3, 0))
    col = lambda off: pl.BlockSpec((nb, LANES), lambda p, off=off: (0, off + p))
    wspec = pl.BlockSpec((lslot, LANES), lambda p: (0, p))
    sspec = pl.BlockSpec((2, half, half, nb), lambda p: (p, 0, 0, 0))
    ya, snew_t = pl.pallas_call(
        _rwkv_step_kernel,
        grid=(npair,),
        in_specs=[col(0), col(npair), col(2 * npair),
                  pl.BlockSpec((nb, lslot), lambda p: (0, lay["off_l"] // lslot)),
                  col(0), col(npair), col(2 * npair),
                  pl.BlockSpec((nb, lslot), lambda p: (0, 0)),
                  pl.BlockSpec((_PV_ROWS, LANES), lambda p: (0, p)),
                  pl.BlockSpec((1, lslot), lambda p: (0, 0)),
                  wspec, wspec, wspec, sspec],
        out_specs=[pl.BlockSpec((nb, LANES), lambda p: (0, p)), sspec],
        out_shape=[jax.ShapeDtypeStruct((nb, rw), BF16), jax.ShapeDtypeStruct(s_t.shape, F32)],
        scratch_shapes=[pltpu.VMEM((LANES, nb), F32)],
        compiler_params=_params("arbitrary"),
    )(proj, proj, proj, proj, pm, pm, pm, plr, pvec, mul, wd, wa, wg, s_t)
    return ya, jnp.transpose(snew_t, (3, 0, 1, 2))


def _rope_rows(t, cos2, sin2):
    return t * cos2 + pltpu.roll(t, LANES // 2, 1) * sin2


def _head_norm_rows(o, eps):
    mu = jnp.mean(o, axis=-1, keepdims=True)
    d = o - mu
    var = jnp.mean(d * d, axis=-1, keepdims=True)
    return d * lax.rsqrt(var + eps)


def _ret_seq_kernel(q_ref, k_ref, v_ref, g_ref, cos_ref, sin_ref, intra_ref, qd_ref, kd_ref,
                    blk_ref, gnw_ref, gnb_ref, yb_ref, ret_ref, s_ref):
    i = pl.program_id(1)
    nh = s_ref.shape[0]
    dk = s_ref.shape[1]

    @pl.when(i == 0)
    def _():
        s_ref[...] = jnp.zeros_like(s_ref)

    cos2, sin2 = cos_ref[...], sin_ref[...]
    heads = range(nh)
    hs = [slice(h * LANES, (h + 1) * LANES) for h in heads]
    kh = [_rope_rows(k_ref[:, hs[h]].astype(F32), cos2, sin2) * (dk ** -0.5) for h in heads]
    qb = [_bf(_rope_rows(q_ref[:, hs[h]].astype(F32), cos2, sin2)) for h in heads]
    vb = [_bf(v_ref[:, hs[h]]) for h in heads]
    scores = [_bf(_dot_nt(qb[h], _bf(kh[h])) * intra_ref[h]) for h in heads]
    s0 = [s_ref[h] for h in heads]
    o = [_dot(scores[h], vb[h]) + _dot(qb[h], _bf(s0[h])) * qd_ref[h] for h in heads]
    for h in heads:
        s_ref[h] = s0[h] * blk_ref[h] + _dot_tn(_bf(kh[h] * kd_ref[h]), vb[h])
    for h in heads:
        on = _head_norm_rows(o[h], RET_GN_EPS)
        yb_ref[:, hs[h]] = _bf((on * gnw_ref[:, hs[h]] + gnb_ref[:, hs[h]])
                               * jax.nn.silu(g_ref[:, hs[h]].astype(F32)))

    @pl.when(i == pl.num_programs(1) - 1)
    def _():
        ret_ref[0] = s_ref[...]


def _ret_tables(nh, c):
    log_g = jnp.log1p(-jnp.exp2(-5.0 - jnp.arange(nh, dtype=F32)))
    i = jnp.arange(c, dtype=F32)
    rel = i[:, None] - i[None, :]
    intra = jnp.where(rel >= 0, jnp.exp(log_g[:, None, None] * jnp.maximum(rel, 0.0)), 0.0)
    q_decay = jnp.exp(log_g[:, None] * (i + 1.0))
    k_decay = jnp.exp(log_g[:, None] * (c - 1.0 - i))
    blk_decay = jnp.exp(log_g * c)
    return intra, q_decay, k_decay, blk_decay


def _rope_tables(pos, dk):
    half = dk // 2
    inv = ROPE_BASE ** (-jnp.arange(half, dtype=F32) / half)
    ang = pos[:, None] * inv[None, :]
    cos, sin = jnp.cos(ang), jnp.sin(ang)
    return jnp.concatenate([cos, cos], axis=-1), jnp.concatenate([-sin, sin], axis=-1)


def _ret_seq(proj, nb, seq, lay, gnw, gnb):
    qk, rv, nh = lay["qk"], lay["rv"], lay["ret_heads"]
    dk, dv = qk // nh, rv // nh
    assert dk == LANES and dv == LANES
    c = RET_CHUNK if seq % RET_CHUNK == 0 else seq
    assert c % SUBLANES == 0
    nt = seq // c
    intra, qd, kd, blk = _ret_tables(nh, c)
    qd = jnp.broadcast_to(qd[:, :, None], (nh, c, dv))
    kd = jnp.broadcast_to(kd[:, :, None], (nh, c, dk))
    blk = jnp.broadcast_to(blk[:, None, None], (nh, 1, dv))
    cos2, sin2 = _rope_tables(jnp.arange(seq, dtype=F32), dk)
    seg = lambda off, w: pl.BlockSpec((c, w), lambda b, i, off=off, w=w: (b * nt + i, off // w))
    full3 = lambda a: pl.BlockSpec(a.shape, lambda b, i: (0, 0, 0))
    return pl.pallas_call(
        _ret_seq_kernel,
        grid=(nb, nt),
        in_specs=[seg(lay["off_q"], qk), seg(lay["off_kr"], qk), seg(lay["off_vr"], rv),
                  seg(lay["off_rg"], rv),
                  pl.BlockSpec((c, dk), lambda b, i: (i, 0)), pl.BlockSpec((c, dk), lambda b, i: (i, 0)),
                  full3(intra), full3(qd), full3(kd), full3(blk),
                  pl.BlockSpec((1, rv), lambda b, i: (0, 0)), pl.BlockSpec((1, rv), lambda b, i: (0, 0))],
        out_specs=[pl.BlockSpec((c, rv), lambda b, i: (b * nt + i, 0)),
                   pl.BlockSpec((1, nh, dk, dv), lambda b, i: (b, 0, 0, 0))],
        out_shape=[jax.ShapeDtypeStruct((nb * seq, rv), BF16),
                   jax.ShapeDtypeStruct((nb, nh, dk, dv), F32)],
        scratch_shapes=[pltpu.VMEM((nh, dk, dv), F32)],
        compiler_params=_params("arbitrary", "arbitrary"),
    )(proj, proj, proj, proj, cos2, sin2, intra, qd, kd, blk, gnw, gnb)


_RC_INTRA, _RC_QD, _RC_KD, _RC_BLK = range(4)


def _ret_step_kernel(q_ref, k_ref, v_ref, g_ref, cos_ref, sin_ref, rc_ref, gnw_ref, gnb_ref, s_ref,
                     yb_ref, snew_ref, q_s, k_s, v_s, o_s):
    bb = q_ref.shape[0]
    nh = s_ref.shape[1]
    dk = s_ref.shape[2]
    cos2, sin2 = cos_ref[...], sin_ref[...]
    for h in range(nh):
        hs = slice(h * LANES, (h + 1) * LANES)
        q_s[:, hs] = _rope_rows(q_ref[:, hs].astype(F32), cos2, sin2)
        k_s[:, hs] = _rope_rows(k_ref[:, hs].astype(F32), cos2, sin2) * (dk ** -0.5)
        v_s[:, hs] = v_ref[:, hs].astype(F32)
    eye = (lax.broadcasted_iota(jnp.int32, (LANES, LANES), 0)
           == lax.broadcasted_iota(jnp.int32, (LANES, LANES), 1))
    for b in range(bb):
        for h in range(nh):
            hs = slice(h * LANES, (h + 1) * LANES)
            rc = lambda j: rc_ref[h, j:j + 1, :]
            q_row = q_s[b:b + 1, hs]
            k_row = k_s[b:b + 1, hs]
            v_row = v_s[b:b + 1, hs]
            s0 = s_ref[b, h]
            q_col = jnp.sum(jnp.where(eye, q_row, 0.0), axis=-1, keepdims=True)
            k_col = jnp.sum(jnp.where(eye, k_row, 0.0), axis=-1, keepdims=True)
            score = jnp.sum(q_row * k_row, axis=-1, keepdims=True) * rc(_RC_INTRA)
            o_row = score * v_row + jnp.sum(s0 * q_col, axis=0, keepdims=True) * rc(_RC_QD)
            snew_ref[b, h] = s0 * rc(_RC_BLK) + (k_col * rc(_RC_KD)) * v_row
            o_s[b:b + 1, hs] = o_row
    for h in range(nh):
        hs = slice(h * LANES, (h + 1) * LANES)
        on = _head_norm_rows(o_s[:, hs], RET_GN_EPS)
        yb_ref[:, hs] = _bf((on * gnw_ref[:, hs] + gnb_ref[:, hs]) * jax.nn.silu(g_ref[:, hs].astype(F32)))


def _ret_step(proj, nb, lay, gnw, gnb, s_ret, pos0):
    qk, rv, nh = lay["qk"], lay["rv"], lay["ret_heads"]
    dk, dv = qk // nh, rv // nh
    assert dk == LANES and dv == LANES
    bb = 2 * SUBLANES
    assert nb % bb == 0
    intra, qd, kd, blk = _ret_tables(nh, 1)
    rc = jnp.stack([intra[:, 0, 0], qd[:, 0], kd[:, 0], blk], axis=1)
    rc = jnp.pad(rc, ((0, 0), (0, SUBLANES - 4)))
    rc = jnp.broadcast_to(rc[:, :, None], (nh, SUBLANES, LANES))
    cos2, sin2 = _rope_tables(jnp.asarray([pos0], dtype=F32), dk)
    seg = lambda off, w: pl.BlockSpec((bb, w), lambda j, off=off, w=w: (j, off // w))
    sspec = pl.BlockSpec((bb, nh, dk, dv), lambda j: (j, 0, 0, 0))
    row = lambda w: pl.BlockSpec((1, w), lambda j: (0, 0))
    return pl.pallas_call(
        _ret_step_kernel,
        grid=(nb // bb,),
        in_specs=[seg(lay["off_q"], qk), seg(lay["off_kr"], qk), seg(lay["off_vr"], rv),
                  seg(lay["off_rg"], rv), row(dk), row(dk),
                  pl.BlockSpec(rc.shape, lambda j: (0, 0, 0)), row(rv), row(rv), sspec],
        out_specs=[pl.BlockSpec((bb, rv), lambda j: (j, 0)), sspec],
        out_shape=[jax.ShapeDtypeStruct((nb, rv), BF16), jax.ShapeDtypeStruct(s_ret.shape, F32)],
        scratch_shapes=[pltpu.VMEM((bb, qk), F32), pltpu.VMEM((bb, qk), F32), pltpu.VMEM((bb, rv), F32),
                        pltpu.VMEM((bb, rv), F32)],
        compiler_params=_params("arbitrary"),
    )(proj, proj, proj, proj, cos2, sin2, rc, gnw, gnb, s_ret)


def _merge_kernel(alpha, ng, *refs):
    ya_ref, yb_ref = refs[0], refs[1]
    ga_refs = refs[2:2 + ng]
    gb_refs = refs[2 + ng:2 + 2 * ng]
    (x_ref, g1_ref, sc2_ref, sh2_ref, wa_ref, wb_ref, wo_ref, lnw_ref, lnb_ref,
     x1_ref, u2_ref) = refs[2 + 2 * ng:]
    cat = lambda rs: jnp.concatenate([r[...].astype(F32) for r in rs], axis=1)
    merged = (jax.nn.sigmoid(cat(ga_refs)) * _dot(ya_ref[...], wa_ref[...])
              + jax.nn.sigmoid(cat(gb_refs)) * _dot(yb_ref[...], wb_ref[...]))
    t = alpha * x_ref[...] + g1_ref[0] * _dot(_bf(merged), wo_ref[...])
    x1 = _layer_norm_rows(t, lnw_ref[...], lnb_ref[...])
    x1_ref[...] = x1
    u2_ref[...] = _bf(x1 * (1.0 + sc2_ref[0]) + sh2_ref[0])


def _merge(ya, yb, proj, x, g1, sc2, sh2, wba, wbb, wout, lnw, lnb, lay, alpha, tm, tpg):
    m, d = x.shape
    rw, rv = ya.shape[1], yb.shape[1]
    gw = math.gcd(lay["off_ga"], d)
    ng = d // gw
    r = g1.shape[1]
    rowt = lambda w: pl.BlockSpec((tm, w), lambda i: (i, 0))
    gate = lambda off, q: pl.BlockSpec((tm, gw), lambda i, off=off, q=q: (i, off // gw + q))
    mod = pl.BlockSpec((1, r, d), lambda i: (i // tpg, 0, 0))
    const = lambda a: pl.BlockSpec(a.shape, lambda i: (0, 0), pipeline_mode=pl.Buffered(1))
    return pl.pallas_call(
        functools.partial(_merge_kernel, alpha, ng),
        grid=(m // tm,),
        in_specs=[rowt(rw), rowt(rv)]
                 + [gate(lay["off_ga"], q) for q in range(ng)]
                 + [gate(lay["off_gb"], q) for q in range(ng)]
                 + [rowt(d), mod, mod, mod, const(wba), const(wbb), const(wout), const(lnw), const(lnb)],
        out_specs=[rowt(d), rowt(d)],
        out_shape=[jax.ShapeDtypeStruct((m, d), F32), jax.ShapeDtypeStruct((m, d), BF16)],
        compiler_params=_params("arbitrary"),
    )(ya, yb, *([proj] * (2 * ng)), x, g1, sc2, sh2, wba, wbb, wout, lnw, lnb)


def _ffn_seq_kernel(tps, nj, nsteps, alpha, u_ref, wa_ref, wb_ref, cwa_ref, cwb_ref, ia_ref, ib_ref,
                    wd_ref, x1_ref, g2_ref, lnw_ref, lnb_ref, o_ref, ta_ref, tb_ref,
                    ca_s, cb_s, act_s):
    t = pl.program_id(0)
    tm = u_ref.shape[0]

    @pl.when(t == 0)
    def _():
        act_s[...] = jnp.zeros_like(act_s)

    @pl.when((t == 0) | ((t - 1) % nj == 0))
    def _():
        o_ref[...] = jnp.zeros_like(o_ref)

    tu = jnp.minimum(t, nsteps - 1)
    i, jc = tu // nj, tu % nj
    first = i % tps == 0
    u = u_ref[...]
    tn = act_s.shape[2]
    act_prev = act_s[(t + 1) % 2]
    tails = [[], []]
    for c0 in range(0, tn, MXU_WIDTH):
        cs = slice(c0, min(c0 + MXU_WIDTH, tn))
        halves = []
        for k, (w_ref, cw_ref, init_ref, c_s) in enumerate(((wa_ref, cwa_ref, ia_ref, ca_s),
                                                            (wb_ref, cwb_ref, ib_ref, cb_s))):
            h = _dot(u, w_ref[:, cs])
            prev = jnp.where(first, init_ref[0, :, cs], c_s[jc, :, cs])
            rid = lax.broadcasted_iota(jnp.int32, h.shape, 0)
            h1 = jnp.where(rid == 0, prev[SUBLANES - 1:SUBLANES, :], pltpu.roll(h, 1, 0))
            h2 = jnp.where(rid == 0, prev[SUBLANES - 2:SUBLANES - 1, :],
                           jnp.where(rid == 1, prev[SUBLANES - 1:SUBLANES, :], pltpu.roll(h, 2, 0)))
            tail = h[tm - SUBLANES:tm, :]
            c_s[jc, :, cs] = tail
            tails[k].append(tail)
            cw = cw_ref[:, cs]
            halves.append(cw[3:4, :] + cw[0:1, :] * h2 + cw[1:2, :] * h1 + cw[2:3, :] * h)
        act_s[t % 2, :, cs] = _bf(jax.nn.silu(halves[0]) * halves[1])
    o_ref[...] += _dot(act_prev, wd_ref[...])

    @pl.when((i % tps == tps - 1) & (t < nsteps))
    def _():
        ta_ref[i // tps, jc] = jnp.concatenate(tails[0], axis=1)
        tb_ref[i // tps, jc] = jnp.concatenate(tails[1], axis=1)

    @pl.when((t > 0) & (t % nj == 0))
    def _():
        y = alpha * x1_ref[...] + g2_ref[0] * o_ref[...]
        o_ref[...] = _layer_norm_rows(y, lnw_ref[...], lnb_ref[...])


def _ffn_up_step_kernel(u_ref, wa_ref, wb_ref, cwa_ref, cwb_ref, p1a_ref, p2a_ref, p1b_ref, p2b_ref,
                        act_ref, ha_ref, hb_ref, wa_bf_ref, wb_bf_ref):
    u = u_ref[...]
    halves = []
    for w_ref, cw_ref, p1_ref, p2_ref, h_ref, w_bf_ref in (
            (wa_ref, cwa_ref, p1a_ref, p2a_ref, ha_ref, wa_bf_ref),
            (wb_ref, cwb_ref, p1b_ref, p2b_ref, hb_ref, wb_bf_ref)):
        w = _bf(w_ref[...])
        w_bf_ref[...] = w
        h = _dot(u, w)
        h_ref[...] = h
        cw = cw_ref[...]
        halves.append(cw[3:4, :] + cw[0:1, :] * p2_ref[...] + cw[1:2, :] * p1_ref[...] + cw[2:3, :] * h)
    act_ref[...] = _bf(jax.nn.silu(halves[0]) * halves[1])


def _ffn_up(u2, w_up, cwt, nb, seq, s_conv, tm):
    m, d = u2.shape
    assert m == tm
    f2 = w_up.shape[1]
    f = f2 // 2
    tn = _pick_tile(f, (512, 256, 128))
    nj = f // tn
    w_a = pl.BlockSpec((d, tn), lambda i, j: (0, j))
    w_b = pl.BlockSpec((d, tn), lambda i, j: (0, nj + j))
    cw_a = pl.BlockSpec((4, tn), lambda i, j: (0, j))
    cw_b = pl.BlockSpec((4, tn), lambda i, j: (0, nj + j))
    assert seq == 1
    p_a = pl.BlockSpec((tm, tn), lambda i, j: (i, j))
    p_b = pl.BlockSpec((tm, tn), lambda i, j: (i, nj + j))
    prev1, prev2 = s_conv[:, 1, :], s_conv[:, 0, :]
    w_o = pl.BlockSpec((d, tn), lambda i, j: (0, j))
    act, h_a, h_b, w_a_bf, w_b_bf = pl.pallas_call(
        _ffn_up_step_kernel,
        grid=(m // tm, nj),
        in_specs=[pl.BlockSpec((tm, d), lambda i, j: (i, 0)), w_a, w_b, cw_a, cw_b, p_a, p_a, p_b, p_b],
        out_specs=[pl.BlockSpec((tm, tn), lambda i, j: (i, j))] * 3 + [w_o, w_o],
        out_shape=[jax.ShapeDtypeStruct((m, f), BF16), jax.ShapeDtypeStruct((m, f), F32),
                   jax.ShapeDtypeStruct((m, f), F32), jax.ShapeDtypeStruct((d, f), BF16),
                   jax.ShapeDtypeStruct((d, f), BF16)],
        compiler_params=_params("arbitrary", "arbitrary"),
    )(u2, w_up, w_up, cwt, cwt, prev1, prev2, prev1, prev2)
    conv_new = jnp.stack([prev1, jnp.concatenate([h_a, h_b], axis=-1)], axis=1)
    return act, conv_new, (w_a_bf, w_b_bf)


def _ffn_seq(u2, w_up_halves, cwt, w_down_bf, x1, g2, lnw, lnb, alpha, nb, seq, s_conv, tm):
    m, d = u2.shape
    w_up_a, w_up_b = w_up_halves
    f = w_up_a.shape[1]
    tn = _pick_tile(f, (512, 256, 128))
    nj = f // tn
    assert seq % tm == 0 and tm >= SUBLANES
    tps = seq // tm
    nsteps = (m // tm) * nj
    up_i = lambda t: jnp.minimum(t, nsteps - 1) // nj
    up_j = lambda t: jnp.minimum(t, nsteps - 1) % nj
    dn_i = lambda t: jnp.maximum(t - 1, 0) // nj
    dn_j = lambda t: jnp.maximum(t - 1, 0) % nj
    w_a = pl.BlockSpec((d, tn), lambda t: (0, up_j(t)))
    w_b = w_a
    cw_a = pl.BlockSpec((4, tn), lambda t: (0, up_j(t)))
    cw_b = pl.BlockSpec((4, tn), lambda t: (0, nj + up_j(t)))
    init = jnp.pad(s_conv, ((0, 0), (SUBLANES - 2, 0), (0, 0)))
    i_a = pl.BlockSpec((1, SUBLANES, tn), lambda t: (up_i(t) // tps, 0, up_j(t)))
    i_b = pl.BlockSpec((1, SUBLANES, tn), lambda t: (up_i(t) // tps, 0, nj + up_j(t)))
    w_d = pl.BlockSpec((tn, d), lambda t: (dn_j(t), 0))
    row_up = pl.BlockSpec((tm, d), lambda t: (up_i(t), 0))
    row_dn = pl.BlockSpec((tm, d), lambda t: (dn_i(t), 0))
    row_x1 = pl.BlockSpec((tm, d), lambda t: (dn_i(t), 0))
    vec = pl.BlockSpec((1, d), lambda t: (0, 0))
    t_o = pl.BlockSpec((nb, nj, SUBLANES, tn), lambda t: (0, 0, 0, 0))
    tails = jax.ShapeDtypeStruct((nb, nj, SUBLANES, tn), F32)
    x2, t_a, t_b = pl.pallas_call(
        functools.partial(_ffn_seq_kernel, tps, nj, nsteps, alpha),
        grid=(nsteps + 1,),
        in_specs=[row_up, w_a, w_b, cw_a, cw_b, i_a, i_b, w_d, row_x1,
                  pl.BlockSpec((1, 1, d), lambda t: (dn_i(t) // tps, 0, 0)), vec, vec],
        out_specs=[row_dn, t_o, t_o],
        out_shape=[jax.ShapeDtypeStruct((m, d), F32), tails, tails],
        scratch_shapes=[pltpu.VMEM((nj, SUBLANES, tn), F32), pltpu.VMEM((nj, SUBLANES, tn), F32),
                        pltpu.VMEM((2, tm, tn), BF16)],
        compiler_params=_params("arbitrary"),
    )(u2, w_up_a, w_up_b, cwt, cwt, init, init, w_down_bf, x1, g2, lnw, lnb)
    rows = lambda t: t[:, :, SUBLANES - 2:, :].transpose(0, 2, 1, 3).reshape(nb, 2, f)
    return x2, jnp.concatenate([rows(t_a), rows(t_b)], axis=-1)


def _ffn_down_kernel(alpha, act_ref, w_ref, x1_ref, g2_ref, lnw_ref, lnb_ref, o_ref, w_bf_ref, acc_ref):
    k = pl.program_id(1)

    @pl.when(k == 0)
    def _():
        acc_ref[...] = jnp.zeros_like(acc_ref)

    w = _bf(w_ref[...])
    w_bf_ref[...] = w
    acc_ref[...] += _dot(act_ref[...], w)

    @pl.when(k == pl.num_programs(1) - 1)
    def _():
        t = alpha * x1_ref[...] + g2_ref[0] * acc_ref[...]
        o_ref[...] = _layer_norm_rows(t, lnw_ref[...], lnb_ref[...])


def _ffn_down(act, w_down, x1, g2, lnw, lnb, alpha, tm, tpg):
    m, f = act.shape
    assert m == tm
    d = w_down.shape[1]
    tk = _pick_tile(f, (512, 256, 128))
    r = g2.shape[1]
    return pl.pallas_call(
        functools.partial(_ffn_down_kernel, alpha),
        grid=(m // tm, f // tk),
        in_specs=[pl.BlockSpec((tm, tk), lambda i, k: (i, k)),
                  pl.BlockSpec((tk, d), lambda i, k: (k, 0)),
                  pl.BlockSpec((tm, d), lambda i, k: (i, 0)),
                  pl.BlockSpec((1, r, d), lambda i, k: (i // tpg, 0, 0)),
                  pl.BlockSpec((1, d), lambda i, k: (0, 0)),
                  pl.BlockSpec((1, d), lambda i, k: (0, 0))],
        out_specs=[pl.BlockSpec((tm, d), lambda i, k: (i, 0)), pl.BlockSpec((tk, d), lambda i, k: (k, 0))],
        out_shape=[jax.ShapeDtypeStruct((m, d), F32), jax.ShapeDtypeStruct((f, d), BF16)],
        scratch_shapes=[pltpu.VMEM((tm, d), F32)],
        compiler_params=_params("arbitrary", "arbitrary"),
    )(act, w_down, x1, g2, lnw, lnb)


def _layout(d, rw, qk, rv, lora_w, ret_heads):
    lslot = next(s for s in (128, 256, 512, 1024, 2048) if s >= lora_w)
    off_q = 3 * rw
    off_kr = off_q + qk
    off_vr = off_kr + qk
    off_rg = off_vr + rv
    off_ga = off_rg + rv
    off_gb = off_ga + d
    off_l = off_gb + d
    assert rw % LANES == 0 and off_l % lslot == 0
    assert off_q % qk == 0 and off_kr % qk == 0 and off_vr % rv == 0 and off_rg % rv == 0
    return dict(d=d, rw=rw, qk=qk, rv=rv, lslot=lslot, lora_w=lora_w, ret_heads=ret_heads,
                off_q=off_q, off_kr=off_kr, off_vr=off_vr, off_rg=off_rg, off_ga=off_ga,
                off_gb=off_gb, off_l=off_l, nt=off_l + lslot)


def _prep_weights(lay, shift_mu, w0, w_decay_up, a0, w_aaa_up, w_gate_up, k_k, k_a, r_k, lnx_w, lnx_b):
    rw, lslot, lora_w = lay["rw"], lay["lslot"], lay["lora_w"]
    dl, al, gl = w_decay_up.shape[0], w_aaa_up.shape[0], w_gate_up.shape[0]
    pad_rows = lambda w, lo: jnp.pad(w, ((lo, lslot - lo - w.shape[0]), (0, 0))).astype(BF16)
    wd = pad_rows(w_decay_up, 0)
    wa = pad_rows(w_aaa_up, dl)
    wg = pad_rows(w_gate_up, dl + al)
    rows = [shift_mu[:rw], shift_mu[rw:2 * rw], shift_mu[2 * rw:3 * rw], w0, a0, k_k, k_a,
            r_k.reshape(-1), lnx_w, lnx_b]
    pvec = jnp.pad(jnp.stack(rows, axis=0), ((0, _PV_ROWS - len(rows)), (0, 0)))
    mul = jnp.pad(shift_mu[3 * rw:], (0, lslot - lora_w))[None, :]
    return wd, wa, wg, pvec, mul


def _run_layer(x2d, nb, seq, ada, states, pos0, lay, wts, big, alpha):
    d, rw, lslot, lora_w = lay["d"], lay["rw"], lay["lslot"], lay["lora_w"]
    (wd, wa, wg, pvec, mul, gnw, gnb, wba, wbb, wout, ln1w, ln1b, cwt, ln2w, ln2b) = wts
    w_in_t, w_up, w_down = big
    m = nb * seq
    sh1, sc1, g1, sh2, sc2, g2 = jnp.split(ada, 6, axis=-1)
    if seq == 1:
        tm, tpg = m, 1
        shape_mod = lambda t: t[None]
    else:
        tm = _pick_tile(seq, (512, 256, 128, 64, 32, 16, 8))
        tpg = seq // tm
        shape_mod = lambda t: t[:, None, :]
    sh1, sc1, g1, sh2, sc2, g2 = map(shape_mod, (sh1, sc1, g1, sh2, sc2, g2))

    s_wkv, s_shift, s_ret, s_conv = states
    shift_main = s_shift[:, :3 * rw]
    shift_lora = jnp.pad(s_shift[:, 3 * rw:], ((0, 0), (0, lslot - lora_w)))
    if seq == 1:
        proj, w_in_bf = _modmm(x2d, sc1, sh1, w_in_t, lay, m, 1)
        ya, wkv_new = _rwkv_step(proj, nb, lay, (shift_main, shift_lora), pvec, mul, wd, wa, wg, s_wkv)
        yb, ret_new = _ret_step(proj, nb, lay, gnw, gnb, s_ret, pos0)
    else:
        tm_in = _pick_tile(seq, (1024, 512, 256, 128, 64, 32, 16, 8))
        proj = _modmm(x2d, sc1, sh1, w_in_t, lay, tm_in, seq // tm_in)
        first = (shift_main[:, None, :], shift_main[:, None, :], shift_main[:, None, :],
                 shift_lora[:, None, :])
        ya, wkv_new = _rwkv_seq(proj, nb, seq, lay, first, pvec, mul, wd, wa, wg)
        yb, ret_new = _ret_seq(proj, nb, seq, lay, gnw, gnb)
    last = proj.reshape(nb, seq, lay["nt"])[:, -1, :]
    shift_new = jnp.concatenate([last[:, :3 * rw], last[:, lay["off_l"]:lay["off_l"] + lora_w]],
                                axis=-1).astype(F32)

    tm_merge = min(tm, 256)
    x1, u2 = _merge(ya, yb, proj, x2d, g1, sc2, sh2, wba, wbb, wout, ln1w, ln1b, lay, alpha,
                    tm_merge, (seq // tm_merge) if seq > 1 else 1)
    if seq == 1:
        act, conv_new, w_up_halves = _ffn_up(u2, w_up, cwt, nb, seq, s_conv, tm)
        x2, w_down_bf = _ffn_down(act, w_down, x1, g2, ln2w, ln2b, alpha, tm, tpg)
        rounded = (w_in_bf, w_up_halves, w_down_bf)
    else:
        x2, conv_new = _ffn_seq(u2, w_up, cwt, w_down, x1, g2, ln2w, ln2b, alpha, nb, seq, s_conv, tm)
        rounded = None
    return (x2, wkv_new, shift_new, ret_new, conv_new), rounded


def kernel(x_prompt, x_sample, c_prompt, c_sample, state_wkv, state_shift, state_ret, state_conv, w_ada, b_ada, w_in, shift_mu, w0, w_decay_up, a0, w_aaa_up, w_gate_up, k_k, k_a, r_k, lnx_w, lnx_b, ret_gn_w, ret_gn_b, w_branch_a, w_branch_b, w_out, ln1_w, ln1_b, w_up, conv_w, conv_b, w_down, ln2_w, ln2_b):
    depth = w_ada.shape[0]
    nbp, seq_p, d = x_prompt.shape
    nbs, seq_s, _ = x_sample.shape
    assert seq_s == 1
    rw = k_k.shape[-1]
    ret_heads, dk, dv = state_ret.shape[2:]
    lora_w = w_decay_up.shape[1] + w_aaa_up.shape[1] + w_gate_up.shape[1]
    lay = _layout(d, rw, ret_heads * dk, ret_heads * dv, lora_w, ret_heads)
    heads, hn = r_k.shape[1:]
    assert hn == LANES // 2 and heads * hn == rw
    alpha = (2.0 * depth) ** 0.25
    f2 = w_up.shape[-1]

    xp = x_prompt.reshape(nbp * seq_p, d)
    xs = x_sample.reshape(nbs * seq_s, d)
    c_all = jnp.concatenate([c_prompt, c_sample], axis=0)
    pad = (-c_all.shape[0]) % SUBLANES
    c_all = _bf(jnp.pad(c_all, ((0, pad), (0, 0))))

    outs_p, outs_s = [], []
    for l in range(depth):
        wd, wa, wg, pvec, mul = _prep_weights(
            lay, shift_mu[l], w0[l], w_decay_up[l], a0[l], w_aaa_up[l], w_gate_up[l],
            k_k[l], k_a[l], r_k[l], lnx_w[l], lnx_b[l])
        cwt = jnp.concatenate([conv_w[l], conv_b[l][None, :]], axis=0)
        wts = (wd, wa, wg, pvec, mul, ret_gn_w[l][None, :], ret_gn_b[l][None, :],
               _bf(w_branch_a[l]), _bf(w_branch_b[l]), _bf(w_out[l]), ln1_w[l][None, :], ln1_b[l][None, :],
               cwt, ln2_w[l][None, :], ln2_b[l][None, :])
        ada = _mm_bias(c_all, w_ada[l], b_ada[l][None, :])
        (xs, *st_s), rounded = _run_layer(
            xs, nbs, seq_s, ada[nbp:nbp + nbs],
            (state_wkv[l], state_shift[l], state_ret[l], state_conv[l]), float(PAST_LEN), lay, wts,
            (jnp.transpose(w_in[l]), w_up[l], w_down[l]), alpha)
        zero_states = (None, jnp.zeros((nbp, state_shift.shape[-1]), F32), None,
                       jnp.zeros((nbp, state_conv.shape[2], f2), F32))
        (xp, *st_p), _ = _run_layer(xp, nbp, seq_p, ada[:nbp], zero_states, 0.0, lay, wts, rounded, alpha)
        outs_p.append(st_p)
        outs_s.append(st_s)

    def stack(lst, j, ref):
        layers = [s[j].astype(ref.dtype) for s in lst]
        return layers[0][None] if depth == 1 else jnp.stack(layers, axis=0)

    refs = (state_wkv, state_shift, state_ret, state_conv)
    return (xp.reshape(x_prompt.shape), xs.reshape(x_sample.shape),
            *[stack(outs_p, j, refs[j]) for j in range(4)],
            *[stack(outs_s, j, refs[j]) for j in range(4)])
```

```python
import functools
import math

import jax
import jax.numpy as jnp
from jax import lax
from jax.experimental import pallas as pl
from jax.experimental.pallas import tpu as pltpu

F32 = jnp.float32
BF16 = jnp.bfloat16

PAST_LEN = 16384
ROPE_BASE = 10000.0
RWKV_GN_EPS = 64e-5
RET_GN_EPS = 1e-5
LN_EPS = 1e-5
RET_CHUNK = 128

LANES = 128
SUBLANES = 8
MXU_WIDTH = 256
VMEM_LIMIT_BYTES = 50 * 1024 * 1024

RWKV_CHUNK = 64
RWKV_PAIRS_PER_STEP = 8
RWKV_GROUPS_PER_STEP = 1


def _params(*sem):
    return pltpu.CompilerParams(dimension_semantics=sem, vmem_limit_bytes=VMEM_LIMIT_BYTES)


def _dot(a, b):
    return jnp.dot(a, b, preferred_element_type=F32)


def _dot_nt(a, b):
    return lax.dot_general(a, b, (((1,), (1,)), ((), ())), preferred_element_type=F32)


def _dot_tn(a, b):
    return lax.dot_general(a, b, (((0,), (0,)), ((), ())), preferred_element_type=F32)


def _bf(x):
    return x.astype(BF16)


def _split2(x):
    hi = x.astype(BF16)
    lo = (x - hi.astype(F32)).astype(BF16)
    return hi, lo


def _dot3_nn(a, b):
    ah, al = _split2(a)
    bh, bl = _split2(b)
    return _dot(jnp.concatenate([ah, ah, al], axis=1), jnp.concatenate([bh, bl, bh], axis=0))


def _dot3_nt(a, b):
    ah, al = _split2(a)
    bh, bl = _split2(b)
    return _dot_nt(jnp.concatenate([ah, ah, al], axis=1), jnp.concatenate([bh, bl, bh], axis=1))


def _dot_exact_lhs(a_bf, b):
    b1 = b.astype(BF16)
    r1 = b - b1.astype(F32)
    b2 = r1.astype(BF16)
    b3 = (r1 - b2.astype(F32)).astype(BF16)
    return _dot(jnp.concatenate([a_bf, a_bf, a_bf], axis=1), jnp.concatenate([b1, b2, b3], axis=0))


def _layer_norm_rows(t, w, b):
    mu = jnp.mean(t, axis=-1, keepdims=True)
    d = t - mu
    var = jnp.mean(d * d, axis=-1, keepdims=True)
    return d * lax.rsqrt(var + LN_EPS) * w + b


def _pick_tile(n, candidates):
    for c in candidates:
        if n % c == 0:
            return c
    return n


def _mm_bias_kernel(x_ref, w_ref, b_ref, o_ref):
    o_ref[...] = _dot(x_ref[...], _bf(w_ref[...])) + b_ref[...]


def _mm_bias(x_bf, w, b_row):
    m, k = x_bf.shape
    n = w.shape[1]
    tn = _pick_tile(n, (1024, 512, 256, 128))
    return pl.pallas_call(
        _mm_bias_kernel,
        grid=(n // tn,),
        in_specs=[pl.BlockSpec((m, k), lambda j: (0, 0)),
                  pl.BlockSpec((k, tn), lambda j: (0, j)),
                  pl.BlockSpec((1, tn), lambda j: (0, j))],
        out_specs=pl.BlockSpec((m, tn), lambda j: (0, j)),
        out_shape=jax.ShapeDtypeStruct((m, n), F32),
        compiler_params=_params("arbitrary"),
    )(x_bf, w, b_row)


def _modmm_kernel(x_ref, sc_ref, sh_ref, wt_ref, o_ref, u_ref):
    @pl.when(pl.program_id(1) == 0)
    def _():
        u_ref[...] = _bf(x_ref[...] * (1.0 + sc_ref[0]) + sh_ref[0])

    o_ref[...] = _dot_nt(u_ref[...], wt_ref[...]).astype(o_ref.dtype)


def _modmm_cast_kernel(x_ref, sc_ref, sh_ref, wt_ref, o_ref, wbf_ref, u_ref):
    @pl.when(pl.program_id(1) == 0)
    def _():
        u_ref[...] = _bf(x_ref[...] * (1.0 + sc_ref[0]) + sh_ref[0])

    w = _bf(wt_ref[...])
    wbf_ref[...] = w
    o_ref[...] = _dot_nt(u_ref[...], w).astype(o_ref.dtype)


def _modmm(x, sc, sh, wt, lay, tm, tpg):
    m, d = x.shape
    n = lay["nt"]
    tn = _pick_tile(n, (512, 256, 128))
    r = sc.shape[1]
    mod_spec = pl.BlockSpec((1, r, d), lambda i, j: (i // tpg, 0, 0))
    x_spec = pl.BlockSpec((tm, d), lambda i, j: (i, 0))
    o_spec = pl.BlockSpec((tm, tn), lambda i, j: (i, j))
    w_blocked = pl.BlockSpec((tn, d), lambda i, j: (j, 0))
    if wt.dtype == BF16:
        return pl.pallas_call(
            _modmm_kernel,
            grid=(m // tm, n // tn),
            in_specs=[x_spec, mod_spec, mod_spec, w_blocked],
            out_specs=o_spec,
            out_shape=jax.ShapeDtypeStruct((m, n), BF16),
            scratch_shapes=[pltpu.VMEM((tm, d), BF16)],
            compiler_params=_params("arbitrary", "arbitrary"),
        )(x, sc, sh, wt)

    assert m == tm
    rw3, lora_w, lslot = 3 * lay["rw"], lay["lora_w"], lay["lslot"]
    n_main, n_rest = rw3 // tn, (lay["off_l"] - rw3) // tn
    assert rw3 % tn == 0 and (lay["off_l"] - rw3) % tn == 0 and lslot == tn
    assert rw3 + lslot <= wt.shape[0]
    assert tn % SUBLANES == 0 and rw3 % SUBLANES == 0 and lora_w % SUBLANES == 0

    def w_row(j):
        row = jnp.where(j < n_main, j * tn,
                        jnp.where(j < n_main + n_rest, rw3 + lora_w + (j - n_main) * tn, rw3))
        return pl.multiple_of(row, SUBLANES)

    return pl.pallas_call(
        _modmm_cast_kernel,
        grid=(1, n // tn),
        in_specs=[x_spec, mod_spec, mod_spec,
                  pl.BlockSpec((pl.Element(tn), pl.Element(d)), lambda i, j: (w_row(j), 0))],
        out_specs=[o_spec, w_blocked],
        out_shape=[jax.ShapeDtypeStruct((m, n), BF16), jax.ShapeDtypeStruct((n, d), BF16)],
        scratch_shapes=[pltpu.VMEM((tm, d), BF16)],
        compiler_params=_params("arbitrary", "arbitrary"),
    )(x, sc, sh, wt)


_PV_MU_R, _PV_MU_K, _PV_MU_V, _PV_W0, _PV_A0, _PV_KK, _PV_KA, _PV_RK, _PV_LNW, _PV_LNB = range(10)
_PV_ROWS = 16


def _head_half_mask(shape):
    return lax.broadcasted_iota(jnp.int32, shape, 1) < (LANES // 2)


def _head_sums(x):
    h0 = _head_half_mask((x.shape[0], LANES))
    parts = []
    for p in range(x.shape[1] // LANES):
        xs = x[:, p * LANES:(p + 1) * LANES]
        s0 = jnp.sum(jnp.where(h0, xs, 0.0), axis=-1, keepdims=True)
        s1 = jnp.sum(jnp.where(h0, 0.0, xs), axis=-1, keepdims=True)
        parts.append(jnp.where(h0, s0, s1))
    return parts[0] if len(parts) == 1 else jnp.concatenate(parts, axis=1)


def _rwkv_lora_inputs(zl, plr, mul):
    ls = zl + mul * (plr - zl)
    return _bf(jnp.tanh(ls)), _bf(ls), _bf(jax.nn.sigmoid(ls))


def _rwkv_tokens(zr, zk, zv, pr, pk, pv_, lora_in, pvec, wd, wa, wg):
    row = lambda i: pvec[i:i + 1, :]
    r = zr + row(_PV_MU_R) * (pr - zr)
    k = zk + row(_PV_MU_K) * (pk - zk)
    v = zv + row(_PV_MU_V) * (pv_ - zv)
    wl = _dot(lora_in[0], wd)
    al = _dot(lora_in[1], wa)
    g = _dot(lora_in[2], wg)
    w = -jax.nn.softplus(-(row(_PV_W0) + wl)) - 0.5
    logd = -jnp.exp(w)
    a = jax.nn.sigmoid(row(_PV_A0) + al)
    kkr = k * row(_PV_KK)
    nrm = jnp.sqrt(_head_sums(kkr * kkr))
    kk = kkr / jnp.maximum(nrm, 1e-12)
    kp = k * (1.0 + (a - 1.0) * row(_PV_KA))
    bonus = _head_sums(r * kp * row(_PV_RK)) * v
    return r, kp, v, kk, a, logd, g, bonus


def _rwkv_finish(o, bonus, g, pvec):
    inv_n = 1.0 / (LANES // 2)
    mu = _head_sums(o) * inv_n
    d = o - mu
    var = _head_sums(d * d) * inv_n
    on = d * lax.rsqrt(var + RWKV_GN_EPS)
    return (on * pvec[_PV_LNW:_PV_LNW + 1, :] + pvec[_PV_LNB:_PV_LNB + 1, :] + bonus) * g


def _shift_rows(z, first_row):
    rolled = pltpu.roll(z, 1, 0)
    rowid = lax.broadcasted_iota(jnp.int32, z.shape, 0)
    return jnp.where(rowid == 0, first_row, rolled)


def _rwkv_seq_kernel(zr_ref, zk_ref, zv_ref, zl_ref, fr_ref, fk_ref, fv_ref, fl_ref,
                     pvec_ref, mul_ref, wd_ref, wa_ref, wg_ref, tri_ref,
                     ya_ref, wkv_ref, s_ref, cr_ref, ck_ref, cv_ref, cl_ref):
    i = pl.program_id(2)
    tc, full_width = zr_ref.shape
    c = RWKV_CHUNK
    nch = tc // c
    half = LANES // 2
    ngroups = RWKV_GROUPS_PER_STEP if (full_width // LANES) % RWKV_GROUPS_PER_STEP == 0 else 1
    npp = full_width // LANES // ngroups
    width = npp * LANES

    @pl.when(i == 0)
    def _():
        s_ref[...] = jnp.zeros_like(s_ref)
        cr_ref[0:1, :] = fr_ref[0]
        ck_ref[0:1, :] = fk_ref[0]
        cv_ref[0:1, :] = fv_ref[0]
        cl_ref[0:1, :] = fl_ref[0]

    zl = zl_ref[...].astype(F32)
    lora_in = _rwkv_lora_inputs(zl, _shift_rows(zl, cl_ref[0:1, :]), mul_ref[...])
    cl_ref[0:1, :] = zl[tc - 1:tc, :]
    tri = tri_ref[...]
    for gi in range(ngroups):
        _rwkv_seq_group(gi, npp, tc, c, nch, tri, lora_in, zr_ref, zk_ref, zv_ref, pvec_ref,
                        wd_ref, wa_ref, wg_ref, ya_ref, s_ref, cr_ref, ck_ref, cv_ref)

    @pl.when(i == pl.num_programs(2) - 1)
    def _():
        for p in range(s_ref.shape[0]):
            s = s_ref[p]
            wkv_ref[0, 2 * p] = s[:half, :half]
            wkv_ref[0, 2 * p + 1] = s[half:, half:]


def _rwkv_seq_group(gi, npp, tc, c, nch, tri, lora_in, zr_ref, zk_ref, zv_ref, pvec_ref,
                    wd_ref, wa_ref, wg_ref, ya_ref, s_ref, cr_ref, ck_ref, cv_ref):
    width = npp * LANES
    gl = slice(gi * width, (gi + 1) * width)
    zr, zk, zv = (ref[:, gl].astype(F32) for ref in (zr_ref, zk_ref, zv_ref))
    pvec = pvec_ref[:, gl]
    r, kp, v, kk, a, logd, g, bonus = _rwkv_tokens(
        zr, zk, zv,
        _shift_rows(zr, cr_ref[0:1, gl]), _shift_rows(zk, ck_ref[0:1, gl]), _shift_rows(zv, cv_ref[0:1, gl]),
        lora_in, pvec, wd_ref[:, gl], wa_ref[:, gl], wg_ref[:, gl])
    cr_ref[0:1, gl] = zr[tc - 1:tc, :]
    ck_ref[0:1, gl] = zk[tc - 1:tc, :]
    cv_ref[0:1, gl] = zv[tc - 1:tc, :]

    alpha = -kk
    beta = kk * a
    cum_incl = _dot_exact_lhs(tri, logd)
    cum_excl = cum_incl - logd
    tot_rows = [cum_incl[(ci + 1) * c - 1:(ci + 1) * c, :] for ci in range(nch)]
    tot = jnp.concatenate([jnp.broadcast_to(t, (c, width)) for t in tot_rows], axis=0)
    e_neg = jnp.exp(-cum_incl)
    e_hat = jnp.exp(tot - cum_incl)
    r_t = r * jnp.exp(cum_incl)
    a_t = alpha * jnp.exp(cum_excl)
    b_t = beta * e_neg
    k_t = kp * e_neg
    b_h = beta * e_hat
    k_h = kp * e_hat

    h0 = _head_half_mask((c, LANES))

    def stack(x, p, ci):
        xb = x[ci * c:(ci + 1) * c, p * LANES:(p + 1) * LANES]
        return jnp.concatenate([jnp.where(h0, xb, 0.0), jnp.where(h0, 0.0, xb)], axis=0)

    rr = lax.broadcasted_iota(jnp.int32, (2 * c, 2 * c), 0)
    cc = lax.broadcasted_iota(jnp.int32, (2 * c, 2 * c), 1)
    same = (rr >= c) == (cc >= c)
    strict = same & (cc < rr)
    incl = same & (cc <= rr)
    eye = rr == cc
    zeros_blk = jnp.zeros((2 * c, LANES), BF16)

    probs = [(p, ci) for p in range(npp) for ci in range(nch)]
    a_s = {q: stack(a_t, *q) for q in probs}
    r_s = {q: stack(r_t, *q) for q in probs}
    v_bf = {q: _bf(stack(v, *q)) for q in probs}

    def twice(x, p, ci):
        xb = _bf(x[ci * c:(ci + 1) * c, p * LANES:(p + 1) * LANES])
        return jnp.concatenate([xb, xb], axis=0)

    pmat, a_ak, a_r = {}, {}, {}
    for q in probs:
        amat = _dot_nt(_bf(jnp.concatenate([a_s[q], r_s[q]], axis=0)),
                       jnp.concatenate([twice(b_t, *q), twice(k_t, *q)], axis=0))
        pmat[q] = jnp.where(strict, amat[:2 * c, :2 * c], 0.0)
        a_ak[q] = _bf(jnp.where(strict, amat[:2 * c, 2 * c:], 0.0))
        a_r[q] = _bf(jnp.concatenate([jnp.where(incl, amat[2 * c:, :2 * c], 0.0),
                                      jnp.where(incl, amat[2 * c:, 2 * c:], 0.0)], axis=1))

    x = {q: jnp.concatenate([a_s[q], _dot(a_ak[q], v_bf[q])], axis=1) for q in probs}

    nsteps = int(math.log2(c))
    for it in range(nsteps):
        for q in probs:
            p_bf = _bf(pmat[q])
            x[q] = x[q] + _dot(p_bf, _bf(x[q]))
            if it + 1 < nsteps:
                pmat[q] = _dot(p_bf, p_bf)

    r_pair, o_pair, g_t, h_t = {}, {}, {}, {}
    for q in probs:
        p, ci = q
        x_bf = _bf(x[q])
        rhs = jnp.concatenate([x_bf, jnp.concatenate([zeros_blk, v_bf[q]], axis=1)], axis=0)
        y = _dot(a_r[q], rhs)
        r_hat = r_s[q] + y[:, :LANES]
        r_pair[q] = r_hat[:c] + r_hat[c:]
        o_pair[q] = y[:c, LANES:] + y[c:, LANES:]
        z = _dot_tn(x_bf, _bf(stack(b_h, *q)))
        w_c = jnp.exp(tot_rows[ci][:, p * LANES:(p + 1) * LANES])
        g_t[q] = jnp.where(eye, w_c, 0.0) + z[:LANES]
        h_t[q] = z[LANES:] + _dot_tn(v_bf[q], _bf(stack(k_h, *q)))

    outs = {}
    for ci in range(nch):
        for p in range(npp):
            q = (p, ci)
            s0 = s_ref[gi * npp + p]
            outs[q] = _dot3_nt(r_pair[q], s0) + o_pair[q]
            s_ref[gi * npp + p] = _dot3_nn(s0, g_t[q]) + h_t[q]

    cols = [jnp.concatenate([outs[(p, ci)] for ci in range(nch)], axis=0) if nch > 1 else outs[(p, 0)]
            for p in range(npp)]
    o = jnp.concatenate(cols, axis=1) if npp > 1 else cols[0]
    ya_ref[:, gl] = _bf(_rwkv_finish(o, bonus, g, pvec))


def _rwkv_seq(proj, nb, seq, lay, first, pvec, mul, wd, wa, wg):
    rw, lslot = lay["rw"], lay["lslot"]
    npair = rw // LANES
    npp = RWKV_PAIRS_PER_STEP if npair % RWKV_PAIRS_PER_STEP == 0 else 1
    ngrp = npair // npp
    width = npp * LANES
    c = RWKV_CHUNK
    tc = _pick_tile(seq, (2 * c, c))
    nt = seq // tc
    t_idx = jnp.arange(tc)
    tri = ((t_idx[:, None] // c == t_idx[None, :] // c) & (t_idx[None, :] <= t_idx[:, None])).astype(BF16)
    fr, fk, fv, fl = first
    col = lambda s: pl.BlockSpec((tc, width), lambda b, gp, i, s=s: (b * nt + i, s * ngrp + gp))
    fcol = lambda s: pl.BlockSpec((1, 1, width), lambda b, gp, i, s=s: (b, 0, s * ngrp + gp))
    wspec = pl.BlockSpec((lslot, width), lambda b, gp, i: (0, gp))
    return pl.pallas_call(
        _rwkv_seq_kernel,
        grid=(nb, ngrp, nt),
        in_specs=[col(0), col(1), col(2),
                  pl.BlockSpec((tc, lslot), lambda b, gp, i: (b * nt + i, lay["off_l"] // lslot)),
                  fcol(0), fcol(1), fcol(2),
                  pl.BlockSpec((1, 1, lslot), lambda b, gp, i: (b, 0, 0)),
                  pl.BlockSpec((_PV_ROWS, width), lambda b, gp, i: (0, gp)),
                  pl.BlockSpec((1, lslot), lambda b, gp, i: (0, 0)),
                  wspec, wspec, wspec,
                  pl.BlockSpec((tc, tc), lambda b, gp, i: (0, 0))],
        out_specs=[pl.BlockSpec((tc, width), lambda b, gp, i: (b * nt + i, gp)),
                   pl.BlockSpec((1, 2 * npp, LANES // 2, LANES // 2), lambda b, gp, i: (b, gp, 0, 0))],
        out_shape=[jax.ShapeDtypeStruct((nb * seq, rw), BF16),
                   jax.ShapeDtypeStruct((nb, 2 * npair, LANES // 2, LANES // 2), F32)],
        scratch_shapes=[pltpu.VMEM((npp, LANES, LANES), F32),
                        pltpu.VMEM((SUBLANES, width), F32), pltpu.VMEM((SUBLANES, width), F32),
                        pltpu.VMEM((SUBLANES, width), F32), pltpu.VMEM((SUBLANES, lslot), F32)],
        compiler_params=_params("arbitrary", "arbitrary", "arbitrary"),
    )(proj, proj, proj, proj, fr, fk, fv, fl, pvec, mul, wd, wa, wg, tri)


def _rwkv_step_kernel(zr_ref, zk_ref, zv_ref, zl_ref, pr_ref, pk_ref, pv_ref, plr_ref,
                      pvec_ref, mul_ref, wd_ref, wa_ref, wg_ref, s_ref,
                      ya_ref, snew_ref, o_s):
    half = LANES // 2
    pvec = pvec_ref[...]
    r, kp, v, kk, a, logd, g, bonus = _rwkv_tokens(
        zr_ref[...].astype(F32), zk_ref[...].astype(F32), zv_ref[...].astype(F32),
        pr_ref[...], pk_ref[...], pv_ref[...],
        _rwkv_lora_inputs(zl_ref[...].astype(F32), plr_ref[...], mul_ref[...]),
        pvec, wd_ref[...], wa_ref[...], wg_ref[...])
    w = jnp.exp(logd)
    nkk_t, wr_t, w_t, beta_t, kp_t, r_t, v_t = (jnp.transpose(x) for x in (-kk, w * r, w, kk * a, kp, r, v))
    for e in range(2):
        ks = slice(e * half, (e + 1) * half)
        nkk_e, wr_e, w_e, beta_e, kp_e = nkk_t[ks], wr_t[ks], w_t[ks], beta_t[ks], kp_t[ks]
        c_beta = jnp.sum(beta_e * r_t[ks], axis=0, keepdims=True)
        c_k = jnp.sum(kp_e * r_t[ks], axis=0, keepdims=True)
        for vi in range(half):
            row = e * half + vi
            s = s_ref[e, vi]
            sa = jnp.sum(s * nkk_e, axis=0, keepdims=True)
            sw = jnp.sum(s * wr_e, axis=0, keepdims=True)
            v_row = v_t[row:row + 1, :]
            snew_ref[e, vi] = s * w_e + sa * beta_e + v_row * kp_e
            o_s[row:row + 1, :] = sw + sa * c_beta + v_row * c_k
    ya_ref[...] = _bf(_rwkv_finish(jnp.transpose(o_s[...]), bonus, g, pvec))


def _rwkv_step(proj, nb, lay, prev, pvec, mul, wd, wa, wg, s_wkv):
    rw, lslot = lay["rw"], lay["lslot"]
    npair = rw // LANES
    pm, plr = prev
    half = LANES // 2
    s_t = jnp.transpose(s_wkv, (1, 2, 3, 0))
    col = lambda off: pl.BlockSpec((nb, LANES), lambda p, off=off: (0, off + p))
    wspec = pl.BlockSpec((lslot, LANES), lambda p: (0, p))
    sspec = pl.BlockSpec((2, half, half, nb), lambda p: (p, 0, 0, 0))
    ya, snew_t = pl.pallas_call(
        _rwkv_step_kernel,
        grid=(npair,),
        in_specs=[col(0), col(npair), col(2 * npair),
                  pl.BlockSpec((nb, lslot), lambda p: (0, lay["off_l"] // lslot)),
                  col(0), col(npair), col(2 * npair),
                  pl.BlockSpec((nb, lslot), lambda p: (0, 0)),
                  pl.BlockSpec((_PV_ROWS, LANES), lambda p: (0, p)),
                  pl.BlockSpec((1, lslot), lambda p: (0, 0)),
                  wspec, wspec, wspec, sspec],
        out_specs=[pl.BlockSpec((nb, LANES), lambda p: (0, p)), sspec],
        out_shape=[jax.ShapeDtypeStruct((nb, rw), BF16), jax.ShapeDtypeStruct(s_t.shape, F32)],
        scratch_shapes=[pltpu.VMEM((LANES, nb), F32)],
        compiler_params=_params("arbitrary"),
    )(proj, proj, proj, proj, pm, pm, pm, plr, pvec, mul, wd, wa, wg, s_t)
    return ya, jnp.transpose(snew_t, (3, 0, 1, 2))


def _rope_rows(t, cos2, sin2):
    return t * cos2 + pltpu.roll(t, LANES // 2, 1) * sin2


def _head_norm_rows(o, eps):
    mu = jnp.mean(o, axis=-1, keepdims=True)
    d = o - mu
    var = jnp.mean(d * d, axis=-1, keepdims=True)
    return d * lax.rsqrt(var + eps)


def _ret_seq_kernel(q_ref, k_ref, v_ref, g_ref, cos_ref, sin_ref, intra_ref, qd_ref, kd_ref,
                    blk_ref, gnw_ref, gnb_ref, yb_ref, ret_ref, s_ref):
    i = pl.program_id(1)
    nh = s_ref.shape[0]
    dk = s_ref.shape[1]

    @pl.when(i == 0)
    def _():
        s_ref[...] = jnp.zeros_like(s_ref)

    cos2, sin2 = cos_ref[...], sin_ref[...]
    heads = range(nh)
    hs = [slice(h * LANES, (h + 1) * LANES) for h in heads]
    kh = [_rope_rows(k_ref[:, hs[h]].astype(F32), cos2, sin2) * (dk ** -0.5) for h in heads]
    qb = [_bf(_rope_rows(q_ref[:, hs[h]].astype(F32), cos2, sin2)) for h in heads]
    vb = [_bf(v_ref[:, hs[h]]) for h in heads]
    scores = [_bf(_dot_nt(qb[h], _bf(kh[h])) * intra_ref[h]) for h in heads]
    s0 = [s_ref[h] for h in heads]
    o = [_dot(scores[h], vb[h]) + _dot(qb[h], _bf(s0[h])) * qd_ref[h] for h in heads]
    for h in heads:
        s_ref[h] = s0[h] * blk_ref[h] + _dot_tn(_bf(kh[h] * kd_ref[h]), vb[h])
    for h in heads:
        on = _head_norm_rows(o[h], RET_GN_EPS)
        yb_ref[:, hs[h]] = _bf((on * gnw_ref[:, hs[h]] + gnb_ref[:, hs[h]])
                               * jax.nn.silu(g_ref[:, hs[h]].astype(F32)))

    @pl.when(i == pl.num_programs(1) - 1)
    def _():
        ret_ref[0] = s_ref[...]


def _ret_tables(nh, c):
    log_g = jnp.log1p(-jnp.exp2(-5.0 - jnp.arange(nh, dtype=F32)))
    i = jnp.arange(c, dtype=F32)
    rel = i[:, None] - i[None, :]
    intra = jnp.where(rel >= 0, jnp.exp(log_g[:, None, None] * jnp.maximum(rel, 0.0)), 0.0)
    q_decay = jnp.exp(log_g[:, None] * (i + 1.0))
    k_decay = jnp.exp(log_g[:, None] * (c - 1.0 - i))
    blk_decay = jnp.exp(log_g * c)
    return intra, q_decay, k_decay, blk_decay


def _rope_tables(pos, dk):
    half = dk // 2
    inv = ROPE_BASE ** (-jnp.arange(half, dtype=F32) / half)
    ang = pos[:, None] * inv[None, :]
    cos, sin = jnp.cos(ang), jnp.sin(ang)
    return jnp.concatenate([cos, cos], axis=-1), jnp.concatenate([-sin, sin], axis=-1)


def _ret_seq(proj, nb, seq, lay, gnw, gnb):
    qk, rv, nh = lay["qk"], lay["rv"], lay["ret_heads"]
    dk, dv = qk // nh, rv // nh
    assert dk == LANES and dv == LANES
    c = RET_CHUNK if seq % RET_CHUNK == 0 else seq
    assert c % SUBLANES == 0
    nt = seq // c
    intra, qd, kd, blk = _ret_tables(nh, c)
    qd = jnp.broadcast_to(qd[:, :, None], (nh, c, dv))
    kd = jnp.broadcast_to(kd[:, :, None], (nh, c, dk))
    blk = jnp.broadcast_to(blk[:, None, None], (nh, 1, dv))
    cos2, sin2 = _rope_tables(jnp.arange(seq, dtype=F32), dk)
    seg = lambda off, w: pl.BlockSpec((c, w), lambda b, i, off=off, w=w: (b * nt + i, off // w))
    full3 = lambda a: pl.BlockSpec(a.shape, lambda b, i: (0, 0, 0))
    return pl.pallas_call(
        _ret_seq_kernel,
        grid=(nb, nt),
        in_specs=[seg(lay["off_q"], qk), seg(lay["off_kr"], qk), seg(lay["off_vr"], rv),
                  seg(lay["off_rg"], rv),
                  pl.BlockSpec((c, dk), lambda b, i: (i, 0)), pl.BlockSpec((c, dk), lambda b, i: (i, 0)),
                  full3(intra), full3(qd), full3(kd), full3(blk),
                  pl.BlockSpec((1, rv), lambda b, i: (0, 0)), pl.BlockSpec((1, rv), lambda b, i: (0, 0))],
        out_specs=[pl.BlockSpec((c, rv), lambda b, i: (b * nt + i, 0)),
                   pl.BlockSpec((1, nh, dk, dv), lambda b, i: (b, 0, 0, 0))],
        out_shape=[jax.ShapeDtypeStruct((nb * seq, rv), BF16),
                   jax.ShapeDtypeStruct((nb, nh, dk, dv), F32)],
        scratch_shapes=[pltpu.VMEM((nh, dk, dv), F32)],
        compiler_params=_params("arbitrary", "arbitrary"),
    )(proj, proj, proj, proj, cos2, sin2, intra, qd, kd, blk, gnw, gnb)


_RC_INTRA, _RC_QD, _RC_KD, _RC_BLK = range(4)


def _ret_step_kernel(q_ref, k_ref, v_ref, g_ref, cos_ref, sin_ref, rc_ref, gnw_ref, gnb_ref, s_ref,
                     yb_ref, snew_ref, q_s, k_s, v_s, o_s):
    bb = q_ref.shape[0]
    nh = s_ref.shape[1]
    dk = s_ref.shape[2]
    cos2, sin2 = cos_ref[...], sin_ref[...]
    for h in range(nh):
        hs = slice(h * LANES, (h + 1) * LANES)
        q_s[:, hs] = _rope_rows(q_ref[:, hs].astype(F32), cos2, sin2)
        k_s[:, hs] = _rope_rows(k_ref[:, hs].astype(F32), cos2, sin2) * (dk ** -0.5)
        v_s[:, hs] = v_ref[:, hs].astype(F32)
    eye = (lax.broadcasted_iota(jnp.int32, (LANES, LANES), 0)
           == lax.broadcasted_iota(jnp.int32, (LANES, LANES), 1))
    for b in range(bb):
        for h in range(nh):
            hs = slice(h * LANES, (h + 1) * LANES)
            rc = lambda j: rc_ref[h, j:j + 1, :]
            q_row = q_s[b:b + 1, hs]
            k_row = k_s[b:b + 1, hs]
            v_row = v_s[b:b + 1, hs]
            s0 = s_ref[b, h]
            q_col = jnp.sum(jnp.where(eye, q_row, 0.0), axis=-1, keepdims=True)
            k_col = jnp.sum(jnp.where(eye, k_row, 0.0), axis=-1, keepdims=True)
            score = jnp.sum(q_row * k_row, axis=-1, keepdims=True) * rc(_RC_INTRA)
            o_row = score * v_row + jnp.sum(s0 * q_col, axis=0, keepdims=True) * rc(_RC_QD)
            snew_ref[b, h] = s0 * rc(_RC_BLK) + (k_col * rc(_RC_KD)) * v_row
            o_s[b:b + 1, hs] = o_row
    for h in range(nh):
        hs = slice(h * LANES, (h + 1) * LANES)
        on = _head_norm_rows(o_s[:, hs], RET_GN_EPS)
        yb_ref[:, hs] = _bf((on * gnw_ref[:, hs] + gnb_ref[:, hs]) * jax.nn.silu(g_ref[:, hs].astype(F32)))


def _ret_step(proj, nb, lay, gnw, gnb, s_ret, pos0):
    qk, rv, nh = lay["qk"], lay["rv"], lay["ret_heads"]
    dk, dv = qk // nh, rv // nh
    assert dk == LANES and dv == LANES
    bb = 2 * SUBLANES
    assert nb % bb == 0
    intra, qd, kd, blk = _ret_tables(nh, 1)
    rc = jnp.stack([intra[:, 0, 0], qd[:, 0], kd[:, 0], blk], axis=1)
    rc = jnp.pad(rc, ((0, 0), (0, SUBLANES - 4)))
    rc = jnp.broadcast_to(rc[:, :, None], (nh, SUBLANES, LANES))
    cos2, sin2 = _rope_tables(jnp.asarray([pos0], dtype=F32), dk)
    seg = lambda off, w: pl.BlockSpec((bb, w), lambda j, off=off, w=w: (j, off // w))
    sspec = pl.BlockSpec((bb, nh, dk, dv), lambda j: (j, 0, 0, 0))
    row = lambda w: pl.BlockSpec((1, w), lambda j: (0, 0))
    return pl.pallas_call(
        _ret_step_kernel,
        grid=(nb // bb,),
        in_specs=[seg(lay["off_q"], qk), seg(lay["off_kr"], qk), seg(lay["off_vr"], rv),
                  seg(lay["off_rg"], rv), row(dk), row(dk),
                  pl.BlockSpec(rc.shape, lambda j: (0, 0, 0)), row(rv), row(rv), sspec],
        out_specs=[pl.BlockSpec((bb, rv), lambda j: (j, 0)), sspec],
        out_shape=[jax.ShapeDtypeStruct((nb, rv), BF16), jax.ShapeDtypeStruct(s_ret.shape, F32)],
        scratch_shapes=[pltpu.VMEM((bb, qk), F32), pltpu.VMEM((bb, qk), F32), pltpu.VMEM((bb, rv), F32),
                        pltpu.VMEM((bb, rv), F32)],
        compiler_params=_params("arbitrary"),
    )(proj, proj, proj, proj, cos2, sin2, rc, gnw, gnb, s_ret)


def _merge_kernel(alpha, ng, *refs):
    ya_ref, yb_ref = refs[0], refs[1]
    ga_refs = refs[2:2 + ng]
    gb_refs = refs[2 + ng:2 + 2 * ng]
    (x_ref, g1_ref, sc2_ref, sh2_ref, wa_ref, wb_ref, wo_ref, lnw_ref, lnb_ref,
     x1_ref, u2_ref) = refs[2 + 2 * ng:]
    cat = lambda rs: jnp.concatenate([r[...].astype(F32) for r in rs], axis=1)
    merged = (jax.nn.sigmoid(cat(ga_refs)) * _dot(ya_ref[...], wa_ref[...])
              + jax.nn.sigmoid(cat(gb_refs)) * _dot(yb_ref[...], wb_ref[...]))
    t = alpha * x_ref[...] + g1_ref[0] * _dot(_bf(merged), wo_ref[...])
    x1 = _layer_norm_rows(t, lnw_ref[...], lnb_ref[...])
    x1_ref[...] = x1
    u2_ref[...] = _bf(x1 * (1.0 + sc2_ref[0]) + sh2_ref[0])


def _merge(ya, yb, proj, x, g1, sc2, sh2, wba, wbb, wout, lnw, lnb, lay, alpha, tm, tpg):
    m, d = x.shape
    rw, rv = ya.shape[1], yb.shape[1]
    gw = math.gcd(lay["off_ga"], d)
    ng = d // gw
    r = g1.shape[1]
    rowt = lambda w: pl.BlockSpec((tm, w), lambda i: (i, 0))
    gate = lambda off, q: pl.BlockSpec((tm, gw), lambda i, off=off, q=q: (i, off // gw + q))
    mod = pl.BlockSpec((1, r, d), lambda i: (i // tpg, 0, 0))
    const = lambda a: pl.BlockSpec(a.shape, lambda i: (0, 0), pipeline_mode=pl.Buffered(1))
    return pl.pallas_call(
        functools.partial(_merge_kernel, alpha, ng),
        grid=(m // tm,),
        in_specs=[rowt(rw), rowt(rv)]
                 + [gate(lay["off_ga"], q) for q in range(ng)]
                 + [gate(lay["off_gb"], q) for q in range(ng)]
                 + [rowt(d), mod, mod, mod, const(wba), const(wbb), const(wout), const(lnw), const(lnb)],
        out_specs=[rowt(d), rowt(d)],
        out_shape=[jax.ShapeDtypeStruct((m, d), F32), jax.ShapeDtypeStruct((m, d), BF16)],
        compiler_params=_params("arbitrary"),
    )(ya, yb, *([proj] * (2 * ng)), x, g1, sc2, sh2, wba, wbb, wout, lnw, lnb)


def _ffn_seq_kernel(tps, nj, nsteps, alpha, u_ref, wa_ref, wb_ref, cwa_ref, cwb_ref, ia_ref, ib_ref,
                    wd_ref, x1_ref, g2_ref, lnw_ref, lnb_ref, o_ref, ta_ref, tb_ref,
                    ca_s, cb_s, ha_s, hb_s, act_s):
    t = pl.program_id(0)
    tm = u_ref.shape[0]

    @pl.when(t == 0)
    def _():
        act_s[...] = jnp.zeros_like(act_s)
        ha_s[...] = jnp.zeros_like(ha_s)
        hb_s[...] = jnp.zeros_like(hb_s)

    @pl.when((t < 2) | ((t - 2) % nj == 0))
    def _():
        o_ref[...] = jnp.zeros_like(o_ref)

    u = u_ref[...]
    ha_s[t % 2] = _dot(u, wa_ref[...])
    hb_s[t % 2] = _dot(u, wb_ref[...])

    o_ref[...] += _dot(act_s[t % 2], wd_ref[...])

    te = jnp.clip(t - 1, 0, nsteps - 1)
    i, jc = te // nj, te % nj
    first = i % tps == 0
    halves, tails = [], []
    for h_s, cw_ref, init_ref, c_s in ((ha_s, cwa_ref, ia_ref, ca_s), (hb_s, cwb_ref, ib_ref, cb_s)):
        h = h_s[(t + 1) % 2]
        prev = jnp.where(first, init_ref[0], c_s[jc])
        rid = lax.broadcasted_iota(jnp.int32, h.shape, 0)
        h1 = jnp.where(rid == 0, prev[SUBLANES - 1:SUBLANES, :], pltpu.roll(h, 1, 0))
        h2 = jnp.where(rid == 0, prev[SUBLANES - 2:SUBLANES - 1, :],
                       jnp.where(rid == 1, prev[SUBLANES - 1:SUBLANES, :], pltpu.roll(h, 2, 0)))
        tail = h[tm - SUBLANES:tm, :]
        tails.append(tail)
        cw = cw_ref[...]
        halves.append(cw[3:4, :] + cw[0:1, :] * h2 + cw[1:2, :] * h1 + cw[2:3, :] * h)
    act_s[(t + 1) % 2] = _bf(jax.nn.silu(halves[0]) * halves[1])

    @pl.when(t >= 1)
    def _():
        ca_s[jc] = tails[0]
        cb_s[jc] = tails[1]

    @pl.when((t >= 1) & (t <= nsteps) & (i % tps == tps - 1))
    def _():
        ta_ref[i // tps, jc] = tails[0]
        tb_ref[i // tps, jc] = tails[1]

    @pl.when((t >= 2) & ((t - 1) % nj == 0))
    def _():
        y = alpha * x1_ref[...] + g2_ref[0] * o_ref[...]
        o_ref[...] = _layer_norm_rows(y, lnw_ref[...], lnb_ref[...])


def _ffn_up_step_kernel(u_ref, wa_ref, wb_ref, cwa_ref, cwb_ref, p1a_ref, p2a_ref, p1b_ref, p2b_ref,
                        act_ref, ha_ref, hb_ref, wa_bf_ref, wb_bf_ref):
    u = u_ref[...]
    halves = []
    for w_ref, cw_ref, p1_ref, p2_ref, h_ref, w_bf_ref in (
            (wa_ref, cwa_ref, p1a_ref, p2a_ref, ha_ref, wa_bf_ref),
            (wb_ref, cwb_ref, p1b_ref, p2b_ref, hb_ref, wb_bf_ref)):
        w = _bf(w_ref[...])
        w_bf_ref[...] = w
        h = _dot(u, w)
        h_ref[...] = h
        cw = cw_ref[...]
        halves.append(cw[3:4, :] + cw[0:1, :] * p2_ref[...] + cw[1:2, :] * p1_ref[...] + cw[2:3, :] * h)
    act_ref[...] = _bf(jax.nn.silu(halves[0]) * halves[1])


def _ffn_up(u2, w_up, cwt, nb, seq, s_conv, tm):
    m, d = u2.shape
    assert m == tm
    f2 = w_up.shape[1]
    f = f2 // 2
    tn = _pick_tile(f, (512, 256, 128))
    nj = f // tn
    w_a = pl.BlockSpec((d, tn), lambda i, j: (0, j))
    w_b = pl.BlockSpec((d, tn), lambda i, j: (0, nj + j))
    cw_a = pl.BlockSpec((4, tn), lambda i, j: (0, j))
    cw_b = pl.BlockSpec((4, tn), lambda i, j: (0, nj + j))
    assert seq == 1
    p_a = pl.BlockSpec((tm, tn), lambda i, j: (i, j))
    p_b = pl.BlockSpec((tm, tn), lambda i, j: (i, nj + j))
    prev1, prev2 = s_conv[:, 1, :], s_conv[:, 0, :]
    w_o = pl.BlockSpec((d, tn), lambda i, j: (0, j))
    act, h_a, h_b, w_a_bf, w_b_bf = pl.pallas_call(
        _ffn_up_step_kernel,
        grid=(m // tm, nj),
        in_specs=[pl.BlockSpec((tm, d), lambda i, j: (i, 0)), w_a, w_b, cw_a, cw_b, p_a, p_a, p_b, p_b],
        out_specs=[pl.BlockSpec((tm, tn), lambda i, j: (i, j))] * 3 + [w_o, w_o],
        out_shape=[jax.ShapeDtypeStruct((m, f), BF16), jax.ShapeDtypeStruct((m, f), F32),
                   jax.ShapeDtypeStruct((m, f), F32), jax.ShapeDtypeStruct((d, f), BF16),
                   jax.ShapeDtypeStruct((d, f), BF16)],
        compiler_params=_params("arbitrary", "arbitrary"),
    )(u2, w_up, w_up, cwt, cwt, prev1, prev2, prev1, prev2)
    conv_new = jnp.stack([prev1, jnp.concatenate([h_a, h_b], axis=-1)], axis=1)
    return act, conv_new, (w_a_bf, w_b_bf)


def _ffn_seq(u2, w_up_halves, cwt, w_down_bf, x1, g2, lnw, lnb, alpha, nb, seq, s_conv, tm):
    m, d = u2.shape
    w_up_a, w_up_b = w_up_halves
    f = w_up_a.shape[1]
    tn = _pick_tile(f, (512, 256, 128))
    nj = f // tn
    assert seq % tm == 0 and tm >= SUBLANES
    tps = seq // tm
    nsteps = (m // tm) * nj
    pair = lambda t, s: jnp.clip(t - s, 0, nsteps - 1)
    row_i = lambda t, s: pair(t, s) // nj
    col_j = lambda t, s: pair(t, s) % nj
    w_up = pl.BlockSpec((d, tn), lambda t: (0, col_j(t, 0)))
    cw_a = pl.BlockSpec((4, tn), lambda t: (0, col_j(t, 1)))
    cw_b = pl.BlockSpec((4, tn), lambda t: (0, nj + col_j(t, 1)))
    init = jnp.pad(s_conv, ((0, 0), (SUBLANES - 2, 0), (0, 0)))
    i_a = pl.BlockSpec((1, SUBLANES, tn), lambda t: (row_i(t, 1) // tps, 0, col_j(t, 1)))
    i_b = pl.BlockSpec((1, SUBLANES, tn), lambda t: (row_i(t, 1) // tps, 0, nj + col_j(t, 1)))
    w_d = pl.BlockSpec((tn, d), lambda t: (col_j(t, 2), 0))
    row_up = pl.BlockSpec((tm, d), lambda t: (row_i(t, 0), 0))
    row_dn = pl.BlockSpec((tm, d), lambda t: (row_i(t, 2), 0))
    vec = pl.BlockSpec((1, d), lambda t: (0, 0))
    t_o = pl.BlockSpec((nb, nj, SUBLANES, tn), lambda t: (0, 0, 0, 0))
    tails = jax.ShapeDtypeStruct((nb, nj, SUBLANES, tn), F32)
    x2, t_a, t_b = pl.pallas_call(
        functools.partial(_ffn_seq_kernel, tps, nj, nsteps, alpha),
        grid=(nsteps + 2,),
        in_specs=[row_up, w_up, w_up, cw_a, cw_b, i_a, i_b, w_d, row_dn,
                  pl.BlockSpec((1, 1, d), lambda t: (row_i(t, 2) // tps, 0, 0)), vec, vec],
        out_specs=[row_dn, t_o, t_o],
        out_shape=[jax.ShapeDtypeStruct((m, d), F32), tails, tails],
        scratch_shapes=[pltpu.VMEM((nj, SUBLANES, tn), F32), pltpu.VMEM((nj, SUBLANES, tn), F32),
                        pltpu.VMEM((2, tm, tn), F32), pltpu.VMEM((2, tm, tn), F32),
                        pltpu.VMEM((2, tm, tn), BF16)],
        compiler_params=_params("arbitrary"),
    )(u2, w_up_a, w_up_b, cwt, cwt, init, init, w_down_bf, x1, g2, lnw, lnb)
    rows = lambda t: t[:, :, SUBLANES - 2:, :].transpose(0, 2, 1, 3).reshape(nb, 2, f)
    return x2, jnp.concatenate([rows(t_a), rows(t_b)], axis=-1)


def _ffn_down_kernel(alpha, act_ref, w_ref, x1_ref, g2_ref, lnw_ref, lnb_ref, o_ref, w_bf_ref, acc_ref):
    k = pl.program_id(1)

    @pl.when(k == 0)
    def _():
        acc_ref[...] = jnp.zeros_like(acc_ref)

    w = _bf(w_ref[...])
    w_bf_ref[...] = w
    acc_ref[...] += _dot(act_ref[...], w)

    @pl.when(k == pl.num_programs(1) - 1)
    def _():
        t = alpha * x1_ref[...] + g2_ref[0] * acc_ref[...]
        o_ref[...] = _layer_norm_rows(t, lnw_ref[...], lnb_ref[...])


def _ffn_down(act, w_down, x1, g2, lnw, lnb, alpha, tm, tpg):
    m, f = act.shape
    assert m == tm
    d = w_down.shape[1]
    tk = _pick_tile(f, (512, 256, 128))
    r = g2.shape[1]
    return pl.pallas_call(
        functools.partial(_ffn_down_kernel, alpha),
        grid=(m // tm, f // tk),
        in_specs=[pl.BlockSpec((tm, tk), lambda i, k: (i, k)),
                  pl.BlockSpec((tk, d), lambda i, k: (k, 0)),
                  pl.BlockSpec((tm, d), lambda i, k: (i, 0)),
                  pl.BlockSpec((1, r, d), lambda i, k: (i // tpg, 0, 0)),
                  pl.BlockSpec((1, d), lambda i, k: (0, 0)),
                  pl.BlockSpec((1, d), lambda i, k: (0, 0))],
        out_specs=[pl.BlockSpec((tm, d), lambda i, k: (i, 0)), pl.BlockSpec((tk, d), lambda i, k: (k, 0))],
        out_shape=[jax.ShapeDtypeStruct((m, d), F32), jax.ShapeDtypeStruct((f, d), BF16)],
        scratch_shapes=[pltpu.VMEM((tm, d), F32)],
        compiler_params=_params("arbitrary", "arbitrary"),
    )(act, w_down, x1, g2, lnw, lnb)


def _layout(d, rw, qk, rv, lora_w, ret_heads):
    lslot = next(s for s in (128, 256, 512, 1024, 2048) if s >= lora_w)
    off_q = 3 * rw
    off_kr = off_q + qk
    off_vr = off_kr + qk
    off_rg = off_vr + rv
    off_ga = off_rg + rv
    off_gb = off_ga + d
    off_l = off_gb + d
    assert rw % LANES == 0 and off_l % lslot == 0
    assert off_q % qk == 0 and off_kr % qk == 0 and off_vr % rv == 0 and off_rg % rv == 0
    return dict(d=d, rw=rw, qk=qk, rv=rv, lslot=lslot, lora_w=lora_w, ret_heads=ret_heads,
                off_q=off_q, off_kr=off_kr, off_vr=off_vr, off_rg=off_rg, off_ga=off_ga,
                off_gb=off_gb, off_l=off_l, nt=off_l + lslot)


def _prep_weights(lay, shift_mu, w0, w_decay_up, a0, w_aaa_up, w_gate_up, k_k, k_a, r_k, lnx_w, lnx_b):
    rw, lslot, lora_w = lay["rw"], lay["lslot"], lay["lora_w"]
    dl, al, gl = w_decay_up.shape[0], w_aaa_up.shape[0], w_gate_up.shape[0]
    pad_rows = lambda w, lo: jnp.pad(w, ((lo, lslot - lo - w.shape[0]), (0, 0))).astype(BF16)
    wd = pad_rows(w_decay_up, 0)
    wa = pad_rows(w_aaa_up, dl)
    wg = pad_rows(w_gate_up, dl + al)
    rows = [shift_mu[:rw], shift_mu[rw:2 * rw], shift_mu[2 * rw:3 * rw], w0, a0, k_k, k_a,
            r_k.reshape(-1), lnx_w, lnx_b]
    pvec = jnp.pad(jnp.stack(rows, axis=0), ((0, _PV_ROWS - len(rows)), (0, 0)))
    mul = jnp.pad(shift_mu[3 * rw:], (0, lslot - lora_w))[None, :]
    return wd, wa, wg, pvec, mul


def _run_layer(x2d, nb, seq, ada, states, pos0, lay, wts, big, alpha):
    d, rw, lslot, lora_w = lay["d"], lay["rw"], lay["lslot"], lay["lora_w"]
    (wd, wa, wg, pvec, mul, gnw, gnb, wba, wbb, wout, ln1w, ln1b, cwt, ln2w, ln2b) = wts
    w_in_t, w_up, w_down = big
    m = nb * seq
    sh1, sc1, g1, sh2, sc2, g2 = jnp.split(ada, 6, axis=-1)
    if seq == 1:
        tm, tpg = m, 1
        shape_mod = lambda t: t[None]
    else:
        tm = _pick_tile(seq, (512, 256, 128, 64, 32, 16, 8))
        tpg = seq // tm
        shape_mod = lambda t: t[:, None, :]
    sh1, sc1, g1, sh2, sc2, g2 = map(shape_mod, (sh1, sc1, g1, sh2, sc2, g2))

    s_wkv, s_shift, s_ret, s_conv = states
    shift_main = s_shift[:, :3 * rw]
    shift_lora = jnp.pad(s_shift[:, 3 * rw:], ((0, 0), (0, lslot - lora_w)))
    if seq == 1:
        proj, w_in_bf = _modmm(x2d, sc1, sh1, w_in_t, lay, m, 1)
        ya, wkv_new = _rwkv_step(proj, nb, lay, (shift_main, shift_lora), pvec, mul, wd, wa, wg, s_wkv)
        yb, ret_new = _ret_step(proj, nb, lay, gnw, gnb, s_ret, pos0)
    else:
        tm_in = _pick_tile(seq, (1024, 512, 256, 128, 64, 32, 16, 8))
        proj = _modmm(x2d, sc1, sh1, w_in_t, lay, tm_in, seq // tm_in)
        first = (shift_main[:, None, :], shift_main[:, None, :], shift_main[:, None, :],
                 shift_lora[:, None, :])
        ya, wkv_new = _rwkv_seq(proj, nb, seq, lay, first, pvec, mul, wd, wa, wg)
        yb, ret_new = _ret_seq(proj, nb, seq, lay, gnw, gnb)
    last = proj.reshape(nb, seq, lay["nt"])[:, -1, :]
    shift_new = jnp.concatenate([last[:, :3 * rw], last[:, lay["off_l"]:lay["off_l"] + lora_w]],
                                axis=-1).astype(F32)

    tm_merge = min(tm, 256)
    x1, u2 = _merge(ya, yb, proj, x2d, g1, sc2, sh2, wba, wbb, wout, ln1w, ln1b, lay, alpha,
                    tm_merge, (seq // tm_merge) if seq > 1 else 1)
    if seq == 1:
        act, conv_new, w_up_halves = _ffn_up(u2, w_up, cwt, nb, seq, s_conv, tm)
        x2, w_down_bf = _ffn_down(act, w_down, x1, g2, ln2w, ln2b, alpha, tm, tpg)
        rounded = (w_in_bf, w_up_halves, w_down_bf)
    else:
        x2, conv_new = _ffn_seq(u2, w_up, cwt, w_down, x1, g2, ln2w, ln2b, alpha, nb, seq, s_conv, tm)
        rounded = None
    return (x2, wkv_new, shift_new, ret_new, conv_new), rounded


def kernel(x_prompt, x_sample, c_prompt, c_sample, state_wkv, state_shift, state_ret, state_conv, w_ada, b_ada, w_in, shift_mu, w0, w_decay_up, a0, w_aaa_up, w_gate_up, k_k, k_a, r_k, lnx_w, lnx_b, ret_gn_w, ret_gn_b, w_branch_a, w_branch_b, w_out, ln1_w, ln1_b, w_up, conv_w, conv_b, w_down, ln2_w, ln2_b):
    depth = w_ada.shape[0]
    nbp, seq_p, d = x_prompt.shape
    nbs, seq_s, _ = x_sample.shape
    assert seq_s == 1
    rw = k_k.shape[-1]
    ret_heads, dk, dv = state_ret.shape[2:]
    lora_w = w_decay_up.shape[1] + w_aaa_up.shape[1] + w_gate_up.shape[1]
    lay = _layout(d, rw, ret_heads * dk, ret_heads * dv, lora_w, ret_heads)
    heads, hn = r_k.shape[1:]
    assert hn == LANES // 2 and heads * hn == rw
    alpha = (2.0 * depth) ** 0.25
    f2 = w_up.shape[-1]

    xp = x_prompt.reshape(nbp * seq_p, d)
    xs = x_sample.reshape(nbs * seq_s, d)
    c_all = jnp.concatenate([c_prompt, c_sample], axis=0)
    pad = (-c_all.shape[0]) % SUBLANES
    c_all = _bf(jnp.pad(c_all, ((0, pad), (0, 0))))

    outs_p, outs_s = [], []
    for l in range(depth):
        wd, wa, wg, pvec, mul = _prep_weights(
            lay, shift_mu[l], w0[l], w_decay_up[l], a0[l], w_aaa_up[l], w_gate_up[l],
            k_k[l], k_a[l], r_k[l], lnx_w[l], lnx_b[l])
        cwt = jnp.concatenate([conv_w[l], conv_b[l][None, :]], axis=0)
        wts = (wd, wa, wg, pvec, mul, ret_gn_w[l][None, :], ret_gn_b[l][None, :],
               _bf(w_branch_a[l]), _bf(w_branch_b[l]), _bf(w_out[l]), ln1_w[l][None, :], ln1_b[l][None, :],
               cwt, ln2_w[l][None, :], ln2_b[l][None, :])
        ada = _mm_bias(c_all, w_ada[l], b_ada[l][None, :])
        (xs, *st_s), rounded = _run_layer(
            xs, nbs, seq_s, ada[nbp:nbp + nbs],
            (state_wkv[l], state_shift[l], state_ret[l], state_conv[l]), float(PAST_LEN), lay, wts,
            (jnp.transpose(w_in[l]), w_up[l], w_down[l]), alpha)
        zero_states = (None, jnp.zeros((nbp, state_shift.shape[-1]), F32), None,
                       jnp.zeros((nbp, state_conv.shape[2], f2), F32))
        (xp, *st_p), _ = _run_layer(xp, nbp, seq_p, ada[:nbp], zero_states, 0.0, lay, wts, rounded, alpha)
        outs_p.append(st_p)
        outs_s.append(st_s)

    def stack(lst, j, ref):
        layers = [s[j].astype(ref.dtype) for s in lst]
        return layers[0][None] if depth == 1 else jnp.stack(layers, axis=0)

    refs = (state_wkv, state_shift, state_ret, state_conv)
    return (xp.reshape(x_prompt.shape), xs.reshape(x_sample.shape),
            *[stack(outs_p, j, refs[j]) for j in range(4)],
            *[stack(outs_s, j, refs[j]) for j in range(4)])
```

```python
import functools
import math

import jax
import jax.numpy as jnp
from jax import lax
from jax.experimental import pallas as pl
from jax.experimental.pallas import tpu as pltpu

F32 = jnp.float32
BF16 = jnp.bfloat16

PAST_LEN = 16384
ROPE_BASE = 10000.0
RWKV_GN_EPS = 64e-5
RET_GN_EPS = 1e-5
LN_EPS = 1e-5
RET_CHUNK = 128

LANES = 128
SUBLANES = 8
MXU_WIDTH = 256
VMEM_LIMIT_BYTES = 50 * 1024 * 1024

RWKV_CHUNK = 64
RWKV_PAIRS_PER_STEP = 8
RWKV_GROUPS_PER_STEP = 1


def _params(*sem):
    return pltpu.CompilerParams(dimension_semantics=sem, vmem_limit_bytes=VMEM_LIMIT_BYTES)


def _dot(a, b):
    return jnp.dot(a, b, preferred_element_type=F32)


def _dot_nt(a, b):
    return lax.dot_general(a, b, (((1,), (1,)), ((), ())), preferred_element_type=F32)


def _dot_tn(a, b):
    return lax.dot_general(a, b, (((0,), (0,)), ((), ())), preferred_element_type=F32)


def _bf(x):
    return x.astype(BF16)


def _split2(x):
    hi = x.astype(BF16)
    lo = (x - hi.astype(F32)).astype(BF16)
    return hi, lo


def _dot3_nn(a, b):
    ah, al = _split2(a)
    bh, bl = _split2(b)
    return _dot(jnp.concatenate([ah, ah, al], axis=1), jnp.concatenate([bh, bl, bh], axis=0))


def _dot3_nt(a, b):
    ah, al = _split2(a)
    bh, bl = _split2(b)
    return _dot_nt(jnp.concatenate([ah, ah, al], axis=1), jnp.concatenate([bh, bl, bh], axis=1))


def _dot_exact_lhs(a_bf, b):
    b1 = b.astype(BF16)
    r1 = b - b1.astype(F32)
    b2 = r1.astype(BF16)
    b3 = (r1 - b2.astype(F32)).astype(BF16)
    return _dot(jnp.concatenate([a_bf, a_bf, a_bf], axis=1), jnp.concatenate([b1, b2, b3], axis=0))


def _layer_norm_rows(t, w, b):
    mu = jnp.mean(t, axis=-1, keepdims=True)
    d = t - mu
    var = jnp.mean(d * d, axis=-1, keepdims=True)
    return d * lax.rsqrt(var + LN_EPS) * w + b


def _pick_tile(n, candidates):
    for c in candidates:
        if n % c == 0:
            return c
    return n


def _mm_bias_kernel(x_ref, w_ref, b_ref, o_ref):
    o_ref[...] = _dot(x_ref[...], _bf(w_ref[...])) + b_ref[...]


def _mm_bias(x_bf, w, b_row):
    m, k = x_bf.shape
    n = w.shape[1]
    tn = _pick_tile(n, (1024, 512, 256, 128))
    return pl.pallas_call(
        _mm_bias_kernel,
        grid=(n // tn,),
        in_specs=[pl.BlockSpec((m, k), lambda j: (0, 0)),
                  pl.BlockSpec((k, tn), lambda j: (0, j)),
                  pl.BlockSpec((1, tn), lambda j: (0, j))],
        out_specs=pl.BlockSpec((m, tn), lambda j: (0, j)),
        out_shape=jax.ShapeDtypeStruct((m, n), F32),
        compiler_params=_params("arbitrary"),
    )(x_bf, w, b_row)


def _modmm_kernel(x_ref, sc_ref, sh_ref, wt_ref, o_ref, u_ref):
    @pl.when(pl.program_id(1) == 0)
    def _():
        u_ref[...] = _bf(x_ref[...] * (1.0 + sc_ref[0]) + sh_ref[0])

    o_ref[...] = _dot_nt(u_ref[...], wt_ref[...]).astype(o_ref.dtype)


def _modmm_cast_kernel(x_ref, sc_ref, sh_ref, wt_ref, o_ref, wbf_ref, u_ref):
    @pl.when(pl.program_id(1) == 0)
    def _():
        u_ref[...] = _bf(x_ref[...] * (1.0 + sc_ref[0]) + sh_ref[0])

    w = _bf(wt_ref[...])
    wbf_ref[...] = w
    o_ref[...] = _dot_nt(u_ref[...], w).astype(o_ref.dtype)


def _modmm(x, sc, sh, wt, lay, tm, tpg):
    m, d = x.shape
    n = lay["nt"]
    tn = _pick_tile(n, (512, 256, 128))
    r = sc.shape[1]
    mod_spec = pl.BlockSpec((1, r, d), lambda i, j: (i // tpg, 0, 0))
    x_spec = pl.BlockSpec((tm, d), lambda i, j: (i, 0))
    o_spec = pl.BlockSpec((tm, tn), lambda i, j: (i, j))
    w_blocked = pl.BlockSpec((tn, d), lambda i, j: (j, 0))
    if wt.dtype == BF16:
        return pl.pallas_call(
            _modmm_kernel,
            grid=(m // tm, n // tn),
            in_specs=[x_spec, mod_spec, mod_spec, w_blocked],
            out_specs=o_spec,
            out_shape=jax.ShapeDtypeStruct((m, n), BF16),
            scratch_shapes=[pltpu.VMEM((tm, d), BF16)],
            compiler_params=_params("arbitrary", "arbitrary"),
        )(x, sc, sh, wt)

    assert m == tm
    rw3, lora_w, lslot = 3 * lay["rw"], lay["lora_w"], lay["lslot"]
    n_main, n_rest = rw3 // tn, (lay["off_l"] - rw3) // tn
    assert rw3 % tn == 0 and (lay["off_l"] - rw3) % tn == 0 and lslot == tn
    assert rw3 + lslot <= wt.shape[0]
    assert tn % SUBLANES == 0 and rw3 % SUBLANES == 0 and lora_w % SUBLANES == 0

    def w_row(j):
        row = jnp.where(j < n_main, j * tn,
                        jnp.where(j < n_main + n_rest, rw3 + lora_w + (j - n_main) * tn, rw3))
        return pl.multiple_of(row, SUBLANES)

    return pl.pallas_call(
        _modmm_cast_kernel,
        grid=(1, n // tn),
        in_specs=[x_spec, mod_spec, mod_spec,
                  pl.BlockSpec((pl.Element(tn), pl.Element(d)), lambda i, j: (w_row(j), 0))],
        out_specs=[o_spec, w_blocked],
        out_shape=[jax.ShapeDtypeStruct((m, n), BF16), jax.ShapeDtypeStruct((n, d), BF16)],
        scratch_shapes=[pltpu.VMEM((tm, d), BF16)],
        compiler_params=_params("arbitrary", "arbitrary"),
    )(x, sc, sh, wt)


_PV_MU_R, _PV_MU_K, _PV_MU_V, _PV_W0, _PV_A0, _PV_KK, _PV_KA, _PV_RK, _PV_LNW, _PV_LNB = range(10)
_PV_ROWS = 16


def _head_half_mask(shape):
    return lax.broadcasted_iota(jnp.int32, shape, 1) < (LANES // 2)


def _head_sums(x):
    h0 = _head_half_mask((x.shape[0], LANES))
    parts = []
    for p in range(x.shape[1] // LANES):
        xs = x[:, p * LANES:(p + 1) * LANES]
        s0 = jnp.sum(jnp.where(h0, xs, 0.0), axis=-1, keepdims=True)
        s1 = jnp.sum(jnp.where(h0, 0.0, xs), axis=-1, keepdims=True)
        parts.append(jnp.where(h0, s0, s1))
    return parts[0] if len(parts) == 1 else jnp.concatenate(parts, axis=1)


def _lora_split(lay):
    dl, al, gl = lay["lora_dims"]
    up = lambda n: -(-n // LANES) * LANES
    a1, g0, g1 = up(dl + al), (dl + al) // LANES * LANES, up(dl + al + gl)
    assert g1 <= lay["lslot"]
    return a1, g0, g1


def _rwkv_lora_inputs(zl, plr, mul, split):
    a1, g0, g1 = split
    ls = zl + mul * (plr - zl)
    return _bf(jnp.tanh(ls[:, :a1])), _bf(ls[:, :a1]), _bf(jax.nn.sigmoid(ls[:, g0:g1]))


def _rwkv_tokens(zr, zk, zv, pr, pk, pv_, lora_in, pvec, wd, wa, wg):
    row = lambda i: pvec[i:i + 1, :]
    r = zr + row(_PV_MU_R) * (pr - zr)
    k = zk + row(_PV_MU_K) * (pk - zk)
    v = zv + row(_PV_MU_V) * (pv_ - zv)
    wl = _dot(lora_in[0], wd)
    al = _dot(lora_in[1], wa)
    g = _dot(lora_in[2], wg)
    logd = -math.exp(-0.5) * jax.nn.sigmoid(row(_PV_W0) + wl)
    a = jax.nn.sigmoid(row(_PV_A0) + al)
    kkr = k * row(_PV_KK)
    kk = kkr * jnp.minimum(lax.rsqrt(_head_sums(kkr * kkr)), 1e12)
    kp = k * (1.0 + (a - 1.0) * row(_PV_KA))
    bonus = _head_sums(r * kp * row(_PV_RK)) * v
    return r, kp, v, kk, a, logd, g, bonus


def _rwkv_finish(o, bonus, g, pvec):
    inv_n = 1.0 / (LANES // 2)
    mu = _head_sums(o) * inv_n
    d = o - mu
    var = _head_sums(d * d) * inv_n
    on = d * lax.rsqrt(var + RWKV_GN_EPS)
    return (on * pvec[_PV_LNW:_PV_LNW + 1, :] + pvec[_PV_LNB:_PV_LNB + 1, :] + bonus) * g


def _shift_rows(z, first_row):
    rolled = pltpu.roll(z, 1, 0)
    rowid = lax.broadcasted_iota(jnp.int32, z.shape, 0)
    return jnp.where(rowid == 0, first_row, rolled)


def _rwkv_seq_kernel(split, zr_ref, zk_ref, zv_ref, zl_ref, fr_ref, fk_ref, fv_ref, fl_ref,
                     pvec_ref, mul_ref, wd_ref, wa_ref, wg_ref, tri_ref,
                     ya_ref, wkv_ref, s_ref, cr_ref, ck_ref, cv_ref, cl_ref):
    i = pl.program_id(2)
    tc, full_width = zr_ref.shape
    c = RWKV_CHUNK
    nch = tc // c
    half = LANES // 2
    ngroups = RWKV_GROUPS_PER_STEP if (full_width // LANES) % RWKV_GROUPS_PER_STEP == 0 else 1
    npp = full_width // LANES // ngroups
    width = npp * LANES

    @pl.when(i == 0)
    def _():
        s_ref[...] = jnp.zeros_like(s_ref)
        cr_ref[0:1, :] = fr_ref[0]
        ck_ref[0:1, :] = fk_ref[0]
        cv_ref[0:1, :] = fv_ref[0]
        cl_ref[0:1, :] = fl_ref[0]

    zl = zl_ref[...].astype(F32)
    lora_in = _rwkv_lora_inputs(zl, _shift_rows(zl, cl_ref[0:1, :]), mul_ref[...], split)
    cl_ref[0:1, :] = zl[tc - 1:tc, :]
    tri = tri_ref[...]
    for gi in range(ngroups):
        _rwkv_seq_group(gi, npp, tc, c, nch, tri, lora_in, zr_ref, zk_ref, zv_ref, pvec_ref,
                        wd_ref, wa_ref, wg_ref, ya_ref, s_ref, cr_ref, ck_ref, cv_ref)

    @pl.when(i == pl.num_programs(2) - 1)
    def _():
        for p in range(s_ref.shape[0]):
            s = jnp.transpose(s_ref[p])
            wkv_ref[0, 2 * p] = s[:half, :half]
            wkv_ref[0, 2 * p + 1] = s[half:, half:]


def _rwkv_seq_group(gi, npp, tc, c, nch, tri, lora_in, zr_ref, zk_ref, zv_ref, pvec_ref,
                    wd_ref, wa_ref, wg_ref, ya_ref, s_ref, cr_ref, ck_ref, cv_ref):
    width = npp * LANES
    gl = slice(gi * width, (gi + 1) * width)
    zr, zk, zv = (ref[:, gl].astype(F32) for ref in (zr_ref, zk_ref, zv_ref))
    pvec = pvec_ref[:, gl]
    r, kp, v, kk, a, logd, g, bonus = _rwkv_tokens(
        zr, zk, zv,
        _shift_rows(zr, cr_ref[0:1, gl]), _shift_rows(zk, ck_ref[0:1, gl]), _shift_rows(zv, cv_ref[0:1, gl]),
        lora_in, pvec, wd_ref[:, gl], wa_ref[:, gl], wg_ref[:, gl])
    cr_ref[0:1, gl] = zr[tc - 1:tc, :]
    ck_ref[0:1, gl] = zk[tc - 1:tc, :]
    cv_ref[0:1, gl] = zv[tc - 1:tc, :]

    alpha = -kk
    beta = kk * a
    cum_incl = _dot_exact_lhs(tri, logd)
    cum_excl = cum_incl - logd
    tot_rows = [cum_incl[(ci + 1) * c - 1:(ci + 1) * c, :] for ci in range(nch)]
    w_tot = jnp.concatenate([jnp.broadcast_to(jnp.exp(t), (c, width)) for t in tot_rows], axis=0)
    e_neg = jnp.exp(-cum_incl)
    e_hat = w_tot * e_neg
    r_t = r * jnp.exp(cum_incl)
    a_t = alpha * jnp.exp(cum_excl)
    b_t = beta * e_neg
    k_t = kp * e_neg
    b_h = beta * e_hat
    k_h = kp * e_hat

    h0 = _head_half_mask((c, LANES))

    def stack(x, p, ci):
        xb = x[ci * c:(ci + 1) * c, p * LANES:(p + 1) * LANES]
        return jnp.concatenate([jnp.where(h0, xb, 0.0), jnp.where(h0, 0.0, xb)], axis=0)

    rr = lax.broadcasted_iota(jnp.int32, (2 * c, 2 * c), 0)
    cc = lax.broadcasted_iota(jnp.int32, (2 * c, 2 * c), 1)
    same = (rr >= c) == (cc >= c)
    strict = same & (cc < rr)
    incl = same & (cc <= rr)
    eye = rr == cc
    zeros_blk = jnp.zeros((2 * c, LANES), BF16)

    probs = [(p, ci) for p in range(npp) for ci in range(nch)]
    a_s = {q: stack(a_t, *q) for q in probs}
    r_s = {q: stack(r_t, *q) for q in probs}
    v_bf = {q: _bf(stack(v, *q)) for q in probs}

    def twice(x, p, ci):
        xb = _bf(x[ci * c:(ci + 1) * c, p * LANES:(p + 1) * LANES])
        return jnp.concatenate([xb, xb], axis=0)

    pmat, a_ak, a_r = {}, {}, {}
    for q in probs:
        amat = _dot_nt(_bf(jnp.concatenate([a_s[q], r_s[q]], axis=0)),
                       jnp.concatenate([twice(b_t, *q), twice(k_t, *q)], axis=0))
        pmat[q] = jnp.where(strict, amat[:2 * c, :2 * c], 0.0)
        a_ak[q] = _bf(jnp.where(strict, amat[:2 * c, 2 * c:], 0.0))
        a_r[q] = _bf(jnp.concatenate([jnp.where(incl, amat[2 * c:, :2 * c], 0.0),
                                      jnp.where(incl, amat[2 * c:, 2 * c:], 0.0)], axis=1))

    x = {q: jnp.concatenate([a_s[q], _dot(a_ak[q], v_bf[q])], axis=1) for q in probs}

    nsteps = int(math.log2(c))
    for it in range(nsteps):
        for q in probs:
            p_bf = _bf(pmat[q])
            x[q] = x[q] + _dot(p_bf, _bf(x[q]))
            if it + 1 < nsteps:
                pmat[q] = _dot(p_bf, p_bf)

    r_pair, o_pair, g_t, h_t = {}, {}, {}, {}
    for q in probs:
        p, ci = q
        x_bf = _bf(x[q])
        rhs = jnp.concatenate([x_bf, jnp.concatenate([zeros_blk, v_bf[q]], axis=1)], axis=0)
        y = _dot(a_r[q], rhs)
        r_hat = r_s[q] + y[:, :LANES]
        r_pair[q] = r_hat[:c] + r_hat[c:]
        o_pair[q] = y[:c, LANES:] + y[c:, LANES:]
        z = _dot_tn(_bf(stack(b_h, *q)), x_bf)
        w_c = jnp.exp(tot_rows[ci][:, p * LANES:(p + 1) * LANES])
        g_t[q] = jnp.where(eye, w_c, 0.0) + z[:, :LANES]
        h_t[q] = z[:, LANES:] + _dot_tn(_bf(stack(k_h, *q)), v_bf[q])

    outs = {}
    for ci in range(nch):
        for p in range(npp):
            q = (p, ci)
            t0 = s_ref[gi * npp + p]
            outs[q] = _dot3_nn(r_pair[q], t0) + o_pair[q]
            s_ref[gi * npp + p] = _dot3_nn(g_t[q], t0) + h_t[q]

    cols = [jnp.concatenate([outs[(p, ci)] for ci in range(nch)], axis=0) if nch > 1 else outs[(p, 0)]
            for p in range(npp)]
    o = jnp.concatenate(cols, axis=1) if npp > 1 else cols[0]
    ya_ref[:, gl] = _bf(_rwkv_finish(o, bonus, g, pvec))


def _rwkv_seq(proj, nb, seq, lay, first, pvec, mul, wd, wa, wg):
    rw, lslot = lay["rw"], lay["lslot"]
    npair = rw // LANES
    npp = RWKV_PAIRS_PER_STEP if npair % RWKV_PAIRS_PER_STEP == 0 else 1
    ngrp = npair // npp
    width = npp * LANES
    c = RWKV_CHUNK
    tc = _pick_tile(seq, (2 * c, c))
    nt = seq // tc
    t_idx = jnp.arange(tc)
    tri = ((t_idx[:, None] // c == t_idx[None, :] // c) & (t_idx[None, :] <= t_idx[:, None])).astype(BF16)
    fr, fk, fv, fl = first
    col = lambda s: pl.BlockSpec((tc, width), lambda b, gp, i, s=s: (b * nt + i, s * ngrp + gp))
    fcol = lambda s: pl.BlockSpec((1, 1, width), lambda b, gp, i, s=s: (b, 0, s * ngrp + gp))
    split = _lora_split(lay)
    wspec = pl.BlockSpec((split[0], width), lambda b, gp, i: (0, gp))
    wgspec = pl.BlockSpec((split[2] - split[1], width), lambda b, gp, i: (0, gp))
    return pl.pallas_call(
        functools.partial(_rwkv_seq_kernel, split),
        grid=(nb, ngrp, nt),
        in_specs=[col(0), col(1), col(2),
                  pl.BlockSpec((tc, lslot), lambda b, gp, i: (b * nt + i, lay["off_l"] // lslot)),
                  fcol(0), fcol(1), fcol(2),
                  pl.BlockSpec((1, 1, lslot), lambda b, gp, i: (b, 0, 0)),
                  pl.BlockSpec((_PV_ROWS, width), lambda b, gp, i: (0, gp)),
                  pl.BlockSpec((1, lslot), lambda b, gp, i: (0, 0)),
                  wspec, wspec, wgspec,
                  pl.BlockSpec((tc, tc), lambda b, gp, i: (0, 0))],
        out_specs=[pl.BlockSpec((tc, width), lambda b, gp, i: (b * nt + i, gp)),
                   pl.BlockSpec((1, 2 * npp, LANES // 2, LANES // 2), lambda b, gp, i: (b, gp, 0, 0))],
        out_shape=[jax.ShapeDtypeStruct((nb * seq, rw), BF16),
                   jax.ShapeDtypeStruct((nb, 2 * npair, LANES // 2, LANES // 2), F32)],
        scratch_shapes=[pltpu.VMEM((npp, LANES, LANES), F32),
                        pltpu.VMEM((SUBLANES, width), F32), pltpu.VMEM((SUBLANES, width), F32),
                        pltpu.VMEM((SUBLANES, width), F32), pltpu.VMEM((SUBLANES, lslot), F32)],
        compiler_params=_params("arbitrary", "arbitrary", "arbitrary"),
    )(proj, proj, proj, proj, fr, fk, fv, fl, pvec, mul, wd, wa, wg, tri)


def _rwkv_step_kernel(split, zr_ref, zk_ref, zv_ref, zl_ref, pr_ref, pk_ref, pv_ref, plr_ref,
                      pvec_ref, mul_ref, wd_ref, wa_ref, wg_ref, s_ref,
                      ya_ref, snew_ref, o_s):
    half = LANES // 2
    pvec = pvec_ref[...]
    r, kp, v, kk, a, logd, g, bonus = _rwkv_tokens(
        zr_ref[...].astype(F32), zk_ref[...].astype(F32), zv_ref[...].astype(F32),
        pr_ref[...], pk_ref[...], pv_ref[...],
        _rwkv_lora_inputs(zl_ref[...].astype(F32), plr_ref[...], mul_ref[...], split),
        pvec, wd_ref[...], wa_ref[...], wg_ref[...])
    w = jnp.exp(logd)
    nkk_t, wr_t, w_t, beta_t, kp_t, r_t, v_t = (jnp.transpose(x) for x in (-kk, w * r, w, kk * a, kp, r, v))
    for e in range(2):
        ks = slice(e * half, (e + 1) * half)
        nkk_e, wr_e, w_e, beta_e, kp_e = nkk_t[ks], wr_t[ks], w_t[ks], beta_t[ks], kp_t[ks]
        c_beta = jnp.sum(beta_e * r_t[ks], axis=0, keepdims=True)
        c_k = jnp.sum(kp_e * r_t[ks], axis=0, keepdims=True)
        for vi in range(half):
            row = e * half + vi
            s = s_ref[e, vi]
            sa = jnp.sum(s * nkk_e, axis=0, keepdims=True)
            sw = jnp.sum(s * wr_e, axis=0, keepdims=True)
            v_row = v_t[row:row + 1, :]
            snew_ref[e, vi] = s * w_e + sa * beta_e + v_row * kp_e
            o_s[row:row + 1, :] = sw + sa * c_beta + v_row * c_k
    ya_ref[...] = _bf(_rwkv_finish(jnp.transpose(o_s[...]), bonus, g, pvec))


def _rwkv_step(proj, nb, lay, prev, pvec, mul, wd, wa, wg, s_wkv):
    rw, lslot = lay["rw"], lay["lslot"]
    npair = rw // LANES
    pm, plr = prev
    half = LANES // 2
    s_t = jnp.transpose(s_wkv, (1, 2, 3, 0))
    col = lambda off: pl.BlockSpec((nb, LANES), lambda p, off=off: (0, off + p))
    split = _lora_split(lay)
    wspec = pl.BlockSpec((split[0], LANES), lambda p: (0, p))
    wgspec = pl.BlockSpec((split[2] - split[1], LANES), lambda p: (0, p))
    sspec = pl.BlockSpec((2, half, half, nb), lambda p: (p, 0, 0, 0))
    ya, snew_t = pl.pallas_call(
        functools.partial(_rwkv_step_kernel, split),
        grid=(npair,),
        in_specs=[col(0), col(npair), col(2 * npair),
                  pl.BlockSpec((nb, lslot), lambda p: (0, lay["off_l"] // lslot)),
                  col(0), col(npair), col(2 * npair),
                  pl.BlockSpec((nb, lslot), lambda p: (0, 0)),
                  pl.BlockSpec((_PV_ROWS, LANES), lambda p: (0, p)),
                  pl.BlockSpec((1, lslot), lambda p: (0, 0)),
                  wspec, wspec, wgspec, sspec],
        out_specs=[pl.BlockSpec((nb, LANES), lambda p: (0, p)), sspec],
        out_shape=[jax.ShapeDtypeStruct((nb, rw), BF16), jax.ShapeDtypeStruct(s_t.shape, F32)],
        scratch_shapes=[pltpu.VMEM((LANES, nb), F32)],
        compiler_params=_params("arbitrary"),
    )(proj, proj, proj, proj, pm, pm, pm, plr, pvec, mul, wd, wa, wg, s_t)
    return ya, jnp.transpose(snew_t, (3, 0, 1, 2))


def _rope_rows(t, cos2, sin2):
    return t * cos2 + pltpu.roll(t, LANES // 2, 1) * sin2


def _head_norm_rows(o, eps):
    mu = jnp.mean(o, axis=-1, keepdims=True)
    d = o - mu
    var = jnp.mean(d * d, axis=-1, keepdims=True)
    return d * lax.rsqrt(var + eps)


def _ret_seq_kernel(q_ref, k_ref, v_ref, g_ref, cos_ref, sin_ref, intra_ref, qd_ref, kd_ref,
                    blk_ref, gnw_ref, gnb_ref, yb_ref, ret_ref, s_ref):
    i = pl.program_id(1)
    nh = s_ref.shape[0]
    dk = s_ref.shape[1]

    @pl.when(i == 0)
    def _():
        s_ref[...] = jnp.zeros_like(s_ref)

    cos2, sin2 = cos_ref[...], sin_ref[...]
    heads = range(nh)
    hs = [slice(h * LANES, (h + 1) * LANES) for h in heads]
    kh = [_rope_rows(k_ref[:, hs[h]].astype(F32), cos2, sin2) * (dk ** -0.5) for h in heads]
    qb = [_bf(_rope_rows(q_ref[:, hs[h]].astype(F32), cos2, sin2)) for h in heads]
    vb = [_bf(v_ref[:, hs[h]]) for h in heads]
    scores = [_bf(_dot_nt(qb[h], _bf(kh[h])) * intra_ref[h]) for h in heads]
    s0 = [s_ref[h] for h in heads]
    o = [_dot(scores[h], vb[h]) + _dot(qb[h], _bf(s0[h])) * qd_ref[h] for h in heads]
    for h in heads:
        s_ref[h] = s0[h] * blk_ref[h] + _dot_tn(_bf(kh[h] * kd_ref[h]), vb[h])
    for h in heads:
        on = _head_norm_rows(o[h], RET_GN_EPS)
        yb_ref[:, hs[h]] = _bf((on * gnw_ref[:, hs[h]] + gnb_ref[:, hs[h]])
                               * jax.nn.silu(g_ref[:, hs[h]].astype(F32)))

    @pl.when(i == pl.num_programs(1) - 1)
    def _():
        ret_ref[0] = s_ref[...]


def _ret_tables(nh, c):
    log_g = jnp.log1p(-jnp.exp2(-5.0 - jnp.arange(nh, dtype=F32)))
    i = jnp.arange(c, dtype=F32)
    rel = i[:, None] - i[None, :]
    intra = jnp.where(rel >= 0, jnp.exp(log_g[:, None, None] * jnp.maximum(rel, 0.0)), 0.0)
    q_decay = jnp.exp(log_g[:, None] * (i + 1.0))
    k_decay = jnp.exp(log_g[:, None] * (c - 1.0 - i))
    blk_decay = jnp.exp(log_g * c)
    return intra, q_decay, k_decay, blk_decay


def _rope_tables(pos, dk):
    half = dk // 2
    inv = ROPE_BASE ** (-jnp.arange(half, dtype=F32) / half)
    ang = pos[:, None] * inv[None, :]
    cos, sin = jnp.cos(ang), jnp.sin(ang)
    return jnp.concatenate([cos, cos], axis=-1), jnp.concatenate([-sin, sin], axis=-1)


def _ret_seq(proj, nb, seq, lay, gnw, gnb):
    qk, rv, nh = lay["qk"], lay["rv"], lay["ret_heads"]
    dk, dv = qk // nh, rv // nh
    assert dk == LANES and dv == LANES
    c = RET_CHUNK if seq % RET_CHUNK == 0 else seq
    assert c % SUBLANES == 0
    nt = seq // c
    intra, qd, kd, blk = _ret_tables(nh, c)
    qd = jnp.broadcast_to(qd[:, :, None], (nh, c, dv))
    kd = jnp.broadcast_to(kd[:, :, None], (nh, c, dk))
    blk = jnp.broadcast_to(blk[:, None, None], (nh, 1, dv))
    cos2, sin2 = _rope_tables(jnp.arange(seq, dtype=F32), dk)
    seg = lambda off, w: pl.BlockSpec((c, w), lambda b, i, off=off, w=w: (b * nt + i, off // w))
    full3 = lambda a: pl.BlockSpec(a.shape, lambda b, i: (0, 0, 0))
    return pl.pallas_call(
        _ret_seq_kernel,
        grid=(nb, nt),
        in_specs=[seg(lay["off_q"], qk), seg(lay["off_kr"], qk), seg(lay["off_vr"], rv),
                  seg(lay["off_rg"], rv),
                  pl.BlockSpec((c, dk), lambda b, i: (i, 0)), pl.BlockSpec((c, dk), lambda b, i: (i, 0)),
                  full3(intra), full3(qd), full3(kd), full3(blk),
                  pl.BlockSpec((1, rv), lambda b, i: (0, 0)), pl.BlockSpec((1, rv), lambda b, i: (0, 0))],
        out_specs=[pl.BlockSpec((c, rv), lambda b, i: (b * nt + i, 0)),
                   pl.BlockSpec((1, nh, dk, dv), lambda b, i: (b, 0, 0, 0))],
        out_shape=[jax.ShapeDtypeStruct((nb * seq, rv), BF16),
                   jax.ShapeDtypeStruct((nb, nh, dk, dv), F32)],
        scratch_shapes=[pltpu.VMEM((nh, dk, dv), F32)],
        compiler_params=_params("arbitrary", "arbitrary"),
    )(proj, proj, proj, proj, cos2, sin2, intra, qd, kd, blk, gnw, gnb)


_RC_INTRA, _RC_QD, _RC_KD, _RC_BLK = range(4)


def _ret_step_kernel(q_ref, k_ref, v_ref, g_ref, cos_ref, sin_ref, rc_ref, gnw_ref, gnb_ref, s_ref,
                     yb_ref, snew_ref, q_s, k_s, v_s, o_s):
    bb = q_ref.shape[0]
    nh = s_ref.shape[1]
    dk = s_ref.shape[2]
    cos2, sin2 = cos_ref[...], sin_ref[...]
    for h in range(nh):
        hs = slice(h * LANES, (h + 1) * LANES)
        q_s[:, hs] = _rope_rows(q_ref[:, hs].astype(F32), cos2, sin2)
        k_s[:, hs] = _rope_rows(k_ref[:, hs].astype(F32), cos2, sin2) * (dk ** -0.5)
        v_s[:, hs] = v_ref[:, hs].astype(F32)
    eye = (lax.broadcasted_iota(jnp.int32, (LANES, LANES), 0)
           == lax.broadcasted_iota(jnp.int32, (LANES, LANES), 1))
    for b in range(bb):
        for h in range(nh):
            hs = slice(h * LANES, (h + 1) * LANES)
            rc = lambda j: rc_ref[h, j:j + 1, :]
            q_row = q_s[b:b + 1, hs]
            k_row = k_s[b:b + 1, hs]
            v_row = v_s[b:b + 1, hs]
            s0 = s_ref[b, h]
            q_col = jnp.sum(jnp.where(eye, q_row, 0.0), axis=-1, keepdims=True)
            k_col = jnp.sum(jnp.where(eye, k_row, 0.0), axis=-1, keepdims=True)
            score = jnp.sum(q_row * k_row, axis=-1, keepdims=True) * rc(_RC_INTRA)
            o_row = score * v_row + jnp.sum(s0 * q_col, axis=0, keepdims=True) * rc(_RC_QD)
            snew_ref[b, h] = s0 * rc(_RC_BLK) + (k_col * rc(_RC_KD)) * v_row
            o_s[b:b + 1, hs] = o_row
    for h in range(nh):
        hs = slice(h * LANES, (h + 1) * LANES)
        on = _head_norm_rows(o_s[:, hs], RET_GN_EPS)
        yb_ref[:, hs] = _bf((on * gnw_ref[:, hs] + gnb_ref[:, hs]) * jax.nn.silu(g_ref[:, hs].astype(F32)))


def _ret_step(proj, nb, lay, gnw, gnb, s_ret, pos0):
    qk, rv, nh = lay["qk"], lay["rv"], lay["ret_heads"]
    dk, dv = qk // nh, rv // nh
    assert dk == LANES and dv == LANES
    bb = 2 * SUBLANES
    assert nb % bb == 0
    intra, qd, kd, blk = _ret_tables(nh, 1)
    rc = jnp.stack([intra[:, 0, 0], qd[:, 0], kd[:, 0], blk], axis=1)
    rc = jnp.pad(rc, ((0, 0), (0, SUBLANES - 4)))
    rc = jnp.broadcast_to(rc[:, :, None], (nh, SUBLANES, LANES))
    cos2, sin2 = _rope_tables(jnp.asarray([pos0], dtype=F32), dk)
    seg = lambda off, w: pl.BlockSpec((bb, w), lambda j, off=off, w=w: (j, off // w))
    sspec = pl.BlockSpec((bb, nh, dk, dv), lambda j: (j, 0, 0, 0))
    row = lambda w: pl.BlockSpec((1, w), lambda j: (0, 0))
    return pl.pallas_call(
        _ret_step_kernel,
        grid=(nb // bb,),
        in_specs=[seg(lay["off_q"], qk), seg(lay["off_kr"], qk), seg(lay["off_vr"], rv),
                  seg(lay["off_rg"], rv), row(dk), row(dk),
                  pl.BlockSpec(rc.shape, lambda j: (0, 0, 0)), row(rv), row(rv), sspec],
        out_specs=[pl.BlockSpec((bb, rv), lambda j: (j, 0)), sspec],
        out_shape=[jax.ShapeDtypeStruct((nb, rv), BF16), jax.ShapeDtypeStruct(s_ret.shape, F32)],
        scratch_shapes=[pltpu.VMEM((bb, qk), F32), pltpu.VMEM((bb, qk), F32), pltpu.VMEM((bb, rv), F32),
                        pltpu.VMEM((bb, rv), F32)],
        compiler_params=_params("arbitrary"),
    )(proj, proj, proj, proj, cos2, sin2, rc, gnw, gnb, s_ret)


def _merge_kernel(alpha, ng, *refs):
    ya_ref, yb_ref = refs[0], refs[1]
    ga_refs = refs[2:2 + ng]
    gb_refs = refs[2 + ng:2 + 2 * ng]
    (x_ref, g1_ref, sc2_ref, sh2_ref, wa_ref, wb_ref, wo_ref, lnw_ref, lnb_ref,
     x1_ref, u2_ref) = refs[2 + 2 * ng:]
    cat = lambda rs: jnp.concatenate([r[...].astype(F32) for r in rs], axis=1)
    merged = (jax.nn.sigmoid(cat(ga_refs)) * _dot(ya_ref[...], wa_ref[...])
              + jax.nn.sigmoid(cat(gb_refs)) * _dot(yb_ref[...], wb_ref[...]))
    t = alpha * x_ref[...] + g1_ref[0] * _dot(_bf(merged), wo_ref[...])
    x1 = _layer_norm_rows(t, lnw_ref[...], lnb_ref[...])
    x1_ref[...] = x1
    u2_ref[...] = _bf(x1 * (1.0 + sc2_ref[0]) + sh2_ref[0])


def _merge(ya, yb, proj, x, g1, sc2, sh2, wba, wbb, wout, lnw, lnb, lay, alpha, tm, tpg):
    m, d = x.shape
    rw, rv = ya.shape[1], yb.shape[1]
    gw = math.gcd(lay["off_ga"], d)
    ng = d // gw
    r = g1.shape[1]
    rowt = lambda w: pl.BlockSpec((tm, w), lambda i: (i, 0))
    gate = lambda off, q: pl.BlockSpec((tm, gw), lambda i, off=off, q=q: (i, off // gw + q))
    mod = pl.BlockSpec((1, r, d), lambda i: (i // tpg, 0, 0))
    const = lambda a: pl.BlockSpec(a.shape, lambda i: (0, 0), pipeline_mode=pl.Buffered(1))
    return pl.pallas_call(
        functools.partial(_merge_kernel, alpha, ng),
        grid=(m // tm,),
        in_specs=[rowt(rw), rowt(rv)]
                 + [gate(lay["off_ga"], q) for q in range(ng)]
                 + [gate(lay["off_gb"], q) for q in range(ng)]
                 + [rowt(d), mod, mod, mod, const(wba), const(wbb), const(wout), const(lnw), const(lnb)],
        out_specs=[rowt(d), rowt(d)],
        out_shape=[jax.ShapeDtypeStruct((m, d), F32), jax.ShapeDtypeStruct((m, d), BF16)],
        compiler_params=_params("arbitrary"),
    )(ya, yb, *([proj] * (2 * ng)), x, g1, sc2, sh2, wba, wbb, wout, lnw, lnb)


def _ffn_seq_kernel(tps, nj, nsteps, alpha, u_ref, wa_ref, wb_ref, cwa_ref, cwb_ref, ia_ref, ib_ref,
                    wd_ref, x1_ref, g2_ref, lnw_ref, lnb_ref, o_ref, ta_ref, tb_ref,
                    ca_s, cb_s, ha_s, hb_s, act_s):
    t = pl.program_id(0)
    tm = u_ref.shape[0]

    @pl.when(t == 0)
    def _():
        act_s[...] = jnp.zeros_like(act_s)
        ha_s[...] = jnp.zeros_like(ha_s)
        hb_s[...] = jnp.zeros_like(hb_s)

    @pl.when((t < 2) | ((t - 2) % nj == 0))
    def _():
        o_ref[...] = jnp.zeros_like(o_ref)

    u = u_ref[...]
    ha_s[t % 2] = _dot(u, wa_ref[...])
    hb_s[t % 2] = _dot(u, wb_ref[...])

    o_ref[...] += _dot(act_s[t % 2], wd_ref[...])

    te = jnp.clip(t - 1, 0, nsteps - 1)
    i, jc = te // nj, te % nj
    first = i % tps == 0
    halves, tails = [], []
    for h_s, cw_ref, init_ref, c_s in ((ha_s, cwa_ref, ia_ref, ca_s), (hb_s, cwb_ref, ib_ref, cb_s)):
        h = h_s[(t + 1) % 2]
        prev = jnp.where(first, init_ref[0], c_s[jc])
        rid = lax.broadcasted_iota(jnp.int32, h.shape, 0)
        h1 = jnp.where(rid == 0, prev[SUBLANES - 1:SUBLANES, :], pltpu.roll(h, 1, 0))
        h2 = jnp.where(rid == 0, prev[SUBLANES - 2:SUBLANES - 1, :],
                       jnp.where(rid == 1, prev[SUBLANES - 1:SUBLANES, :], pltpu.roll(h, 2, 0)))
        tail = h[tm - SUBLANES:tm, :]
        tails.append(tail)
        cw = cw_ref[...]
        halves.append(cw[3:4, :] + cw[0:1, :] * h2 + cw[1:2, :] * h1 + cw[2:3, :] * h)
    act_s[(t + 1) % 2] = _bf(jax.nn.silu(halves[0]) * halves[1])

    @pl.when(t >= 1)
    def _():
        ca_s[jc] = tails[0]
        cb_s[jc] = tails[1]

    @pl.when((t >= 1) & (t <= nsteps) & (i % tps == tps - 1))
    def _():
        ta_ref[i // tps, jc] = tails[0]
        tb_ref[i // tps, jc] = tails[1]

    @pl.when((t >= 2) & ((t - 1) % nj == 0))
    def _():
        y = alpha * x1_ref[...] + g2_ref[0] * o_ref[...]
        o_ref[...] = _layer_norm_rows(y, lnw_ref[...], lnb_ref[...])


def _ffn_up_step_kernel(u_ref, wa_ref, wb_ref, cwa_ref, cwb_ref, p1a_ref, p2a_ref, p1b_ref, p2b_ref,
                        act_ref, ha_ref, hb_ref, wa_bf_ref, wb_bf_ref):
    u = u_ref[...]
    halves = []
    for w_ref, cw_ref, p1_ref, p2_ref, h_ref, w_bf_ref in (
            (wa_ref, cwa_ref, p1a_ref, p2a_ref, ha_ref, wa_bf_ref),
            (wb_ref, cwb_ref, p1b_ref, p2b_ref, hb_ref, wb_bf_ref)):
        w = _bf(w_ref[...])
        w_bf_ref[...] = w
        h = _dot(u, w)
        h_ref[...] = h
        cw = cw_ref[...]
        halves.append(cw[3:4, :] + cw[0:1, :] * p2_ref[...] + cw[1:2, :] * p1_ref[...] + cw[2:3, :] * h)
    act_ref[...] = _bf(jax.nn.silu(halves[0]) * halves[1])


def _ffn_up(u2, w_up, cwt, nb, seq, s_conv, tm):
    m, d = u2.shape
    assert m == tm
    f2 = w_up.shape[1]
    f = f2 // 2
    tn = _pick_tile(f, (512, 256, 128))
    nj = f // tn
    w_a = pl.BlockSpec((d, tn), lambda i, j: (0, j))
    w_b = pl.BlockSpec((d, tn), lambda i, j: (0, nj + j))
    cw_a = pl.BlockSpec((4, tn), lambda i, j: (0, j))
    cw_b = pl.BlockSpec((4, tn), lambda i, j: (0, nj + j))
    assert seq == 1
    p_a = pl.BlockSpec((tm, tn), lambda i, j: (i, j))
    p_b = pl.BlockSpec((tm, tn), lambda i, j: (i, nj + j))
    prev1, prev2 = s_conv[:, 1, :], s_conv[:, 0, :]
    w_o = pl.BlockSpec((d, tn), lambda i, j: (0, j))
    act, h_a, h_b, w_a_bf, w_b_bf = pl.pallas_call(
        _ffn_up_step_kernel,
        grid=(m // tm, nj),
        in_specs=[pl.BlockSpec((tm, d), lambda i, j: (i, 0)), w_a, w_b, cw_a, cw_b, p_a, p_a, p_b, p_b],
        out_specs=[pl.BlockSpec((tm, tn), lambda i, j: (i, j))] * 3 + [w_o, w_o],
        out_shape=[jax.ShapeDtypeStruct((m, f), BF16), jax.ShapeDtypeStruct((m, f), F32),
                   jax.ShapeDtypeStruct((m, f), F32), jax.ShapeDtypeStruct((d, f), BF16),
                   jax.ShapeDtypeStruct((d, f), BF16)],
        compiler_params=_params("arbitrary", "arbitrary"),
    )(u2, w_up, w_up, cwt, cwt, prev1, prev2, prev1, prev2)
    conv_new = jnp.stack([prev1, jnp.concatenate([h_a, h_b], axis=-1)], axis=1)
    return act, conv_new, (w_a_bf, w_b_bf)


def _ffn_seq(u2, w_up_halves, cwt, w_down_bf, x1, g2, lnw, lnb, alpha, nb, seq, s_conv, tm):
    m, d = u2.shape
    w_up_a, w_up_b = w_up_halves
    f = w_up_a.shape[1]
    tn = _pick_tile(f, (512, 256, 128))
    nj = f // tn
    assert seq % tm == 0 and tm >= SUBLANES
    tps = seq // tm
    nsteps = (m // tm) * nj
    pair = lambda t, s: jnp.clip(t - s, 0, nsteps - 1)
    row_i = lambda t, s: pair(t, s) // nj
    col_j = lambda t, s: pair(t, s) % nj
    w_up = pl.BlockSpec((d, tn), lambda t: (0, col_j(t, 0)))
    cw_a = pl.BlockSpec((4, tn), lambda t: (0, col_j(t, 1)))
    cw_b = pl.BlockSpec((4, tn), lambda t: (0, nj + col_j(t, 1)))
    init = jnp.pad(s_conv, ((0, 0), (SUBLANES - 2, 0), (0, 0)))
    i_a = pl.BlockSpec((1, SUBLANES, tn), lambda t: (row_i(t, 1) // tps, 0, col_j(t, 1)))
    i_b = pl.BlockSpec((1, SUBLANES, tn), lambda t: (row_i(t, 1) // tps, 0, nj + col_j(t, 1)))
    w_d = pl.BlockSpec((tn, d), lambda t: (col_j(t, 2), 0))
    row_up = pl.BlockSpec((tm, d), lambda t: (row_i(t, 0), 0))
    row_dn = pl.BlockSpec((tm, d), lambda t: (row_i(t, 2), 0))
    vec = pl.BlockSpec((1, d), lambda t: (0, 0))
    t_o = pl.BlockSpec((nb, nj, SUBLANES, tn), lambda t: (0, 0, 0, 0))
    tails = jax.ShapeDtypeStruct((nb, nj, SUBLANES, tn), F32)
    x2, t_a, t_b = pl.pallas_call(
        functools.partial(_ffn_seq_kernel, tps, nj, nsteps, alpha),
        grid=(nsteps + 2,),
        in_specs=[row_up, w_up, w_up, cw_a, cw_b, i_a, i_b, w_d, row_dn,
                  pl.BlockSpec((1, 1, d), lambda t: (row_i(t, 2) // tps, 0, 0)), vec, vec],
        out_specs=[row_dn, t_o, t_o],
        out_shape=[jax.ShapeDtypeStruct((m, d), F32), tails, tails],
        scratch_shapes=[pltpu.VMEM((nj, SUBLANES, tn), F32), pltpu.VMEM((nj, SUBLANES, tn), F32),
                        pltpu.VMEM((2, tm, tn), F32), pltpu.VMEM((2, tm, tn), F32),
                        pltpu.VMEM((2, tm, tn), BF16)],
        compiler_params=_params("arbitrary"),
    )(u2, w_up_a, w_up_b, cwt, cwt, init, init, w_down_bf, x1, g2, lnw, lnb)
    rows = lambda t: t[:, :, SUBLANES - 2:, :].transpose(0, 2, 1, 3).reshape(nb, 2, f)
    return x2, jnp.concatenate([rows(t_a), rows(t_b)], axis=-1)


def _ffn_down_kernel(alpha, act_ref, w_ref, x1_ref, g2_ref, lnw_ref, lnb_ref, o_ref, w_bf_ref, acc_ref):
    k = pl.program_id(1)

    @pl.when(k == 0)
    def _():
        acc_ref[...] = jnp.zeros_like(acc_ref)

    w = _bf(w_ref[...])
    w_bf_ref[...] = w
    acc_ref[...] += _dot(act_ref[...], w)

    @pl.when(k == pl.num_programs(1) - 1)
    def _():
        t = alpha * x1_ref[...] + g2_ref[0] * acc_ref[...]
        o_ref[...] = _layer_norm_rows(t, lnw_ref[...], lnb_ref[...])


def _ffn_down(act, w_down, x1, g2, lnw, lnb, alpha, tm, tpg):
    m, f = act.shape
    assert m == tm
    d = w_down.shape[1]
    tk = _pick_tile(f, (512, 256, 128))
    r = g2.shape[1]
    return pl.pallas_call(
        functools.partial(_ffn_down_kernel, alpha),
        grid=(m // tm, f // tk),
        in_specs=[pl.BlockSpec((tm, tk), lambda i, k: (i, k)),
                  pl.BlockSpec((tk, d), lambda i, k: (k, 0)),
                  pl.BlockSpec((tm, d), lambda i, k: (i, 0)),
                  pl.BlockSpec((1, r, d), lambda i, k: (i // tpg, 0, 0)),
                  pl.BlockSpec((1, d), lambda i, k: (0, 0)),
                  pl.BlockSpec((1, d), lambda i, k: (0, 0))],
        out_specs=[pl.BlockSpec((tm, d), lambda i, k: (i, 0)), pl.BlockSpec((tk, d), lambda i, k: (k, 0))],
        out_shape=[jax.ShapeDtypeStruct((m, d), F32), jax.ShapeDtypeStruct((f, d), BF16)],
        scratch_shapes=[pltpu.VMEM((tm, d), F32)],
        compiler_params=_params("arbitrary", "arbitrary"),
    )(act, w_down, x1, g2, lnw, lnb)


def _layout(d, rw, qk, rv, lora_w, ret_heads):
    lslot = next(s for s in (128, 256, 512, 1024, 2048) if s >= lora_w)
    off_q = 3 * rw
    off_kr = off_q + qk
    off_vr = off_kr + qk
    off_rg = off_vr + rv
    off_ga = off_rg + rv
    off_gb = off_ga + d
    off_l = off_gb + d
    assert rw % LANES == 0 and off_l % lslot == 0
    assert off_q % qk == 0 and off_kr % qk == 0 and off_vr % rv == 0 and off_rg % rv == 0
    return dict(d=d, rw=rw, qk=qk, rv=rv, lslot=lslot, lora_w=lora_w, ret_heads=ret_heads,
                off_q=off_q, off_kr=off_kr, off_vr=off_vr, off_rg=off_rg, off_ga=off_ga,
                off_gb=off_gb, off_l=off_l, nt=off_l + lslot)


def _prep_weights(lay, shift_mu, w0, w_decay_up, a0, w_aaa_up, w_gate_up, k_k, k_a, r_k, lnx_w, lnx_b):
    rw, lslot, lora_w = lay["rw"], lay["lslot"], lay["lora_w"]
    dl, al, gl = lay["lora_dims"]
    a1, g0, g1 = _lora_split(lay)
    pad_rows = lambda w, lo, n: jnp.pad(w, ((lo, n - lo - w.shape[0]), (0, 0))).astype(BF16)
    wd = pad_rows(w_decay_up, 0, a1)
    wa = pad_rows(w_aaa_up, dl, a1)
    wg = pad_rows(w_gate_up, dl + al - g0, g1 - g0)
    rows = [shift_mu[:rw], shift_mu[rw:2 * rw], shift_mu[2 * rw:3 * rw], w0, a0, k_k, k_a,
            r_k.reshape(-1), lnx_w, lnx_b]
    pvec = jnp.pad(jnp.stack(rows, axis=0), ((0, _PV_ROWS - len(rows)), (0, 0)))
    mul = jnp.pad(shift_mu[3 * rw:], (0, lslot - lora_w))[None, :]
    return wd, wa, wg, pvec, mul


def _run_layer(x2d, nb, seq, ada, states, pos0, lay, wts, big, alpha):
    d, rw, lslot, lora_w = lay["d"], lay["rw"], lay["lslot"], lay["lora_w"]
    (wd, wa, wg, pvec, mul, gnw, gnb, wba, wbb, wout, ln1w, ln1b, cwt, ln2w, ln2b) = wts
    w_in_t, w_up, w_down = big
    m = nb * seq
    sh1, sc1, g1, sh2, sc2, g2 = jnp.split(ada, 6, axis=-1)
    if seq == 1:
        tm, tpg = m, 1
        shape_mod = lambda t: t[None]
    else:
        tm = _pick_tile(seq, (512, 256, 128, 64, 32, 16, 8))
        tpg = seq // tm
        shape_mod = lambda t: t[:, None, :]
    sh1, sc1, g1, sh2, sc2, g2 = map(shape_mod, (sh1, sc1, g1, sh2, sc2, g2))

    s_wkv, s_shift, s_ret, s_conv = states
    shift_main = s_shift[:, :3 * rw]
    shift_lora = jnp.pad(s_shift[:, 3 * rw:], ((0, 0), (0, lslot - lora_w)))
    if seq == 1:
        proj, w_in_bf = _modmm(x2d, sc1, sh1, w_in_t, lay, m, 1)
        ya, wkv_new = _rwkv_step(proj, nb, lay, (shift_main, shift_lora), pvec, mul, wd, wa, wg, s_wkv)
        yb, ret_new = _ret_step(proj, nb, lay, gnw, gnb, s_ret, pos0)
    else:
        tm_in = _pick_tile(seq, (1024, 512, 256, 128, 64, 32, 16, 8))
        proj = _modmm(x2d, sc1, sh1, w_in_t, lay, tm_in, seq // tm_in)
        first = (shift_main[:, None, :], shift_main[:, None, :], shift_main[:, None, :],
                 shift_lora[:, None, :])
        ya, wkv_new = _rwkv_seq(proj, nb, seq, lay, first, pvec, mul, wd, wa, wg)
        yb, ret_new = _ret_seq(proj, nb, seq, lay, gnw, gnb)
    last = proj.reshape(nb, seq, lay["nt"])[:, -1, :]
    shift_new = jnp.concatenate([last[:, :3 * rw], last[:, lay["off_l"]:lay["off_l"] + lora_w]],
                                axis=-1).astype(F32)

    tm_merge = min(tm, 256)
    x1, u2 = _merge(ya, yb, proj, x2d, g1, sc2, sh2, wba, wbb, wout, ln1w, ln1b, lay, alpha,
                    tm_merge, (seq // tm_merge) if seq > 1 else 1)
    if seq == 1:
        act, conv_new, w_up_halves = _ffn_up(u2, w_up, cwt, nb, seq, s_conv, tm)
        x2, w_down_bf = _ffn_down(act, w_down, x1, g2, ln2w, ln2b, alpha, tm, tpg)
        rounded = (w_in_bf, w_up_halves, w_down_bf)
    else:
        x2, conv_new = _ffn_seq(u2, w_up, cwt, w_down, x1, g2, ln2w, ln2b, alpha, nb, seq, s_conv, tm)
        rounded = None
    return (x2, wkv_new, shift_new, ret_new, conv_new), rounded


def kernel(x_prompt, x_sample, c_prompt, c_sample, state_wkv, state_shift, state_ret, state_conv, w_ada, b_ada, w_in, shift_mu, w0, w_decay_up, a0, w_aaa_up, w_gate_up, k_k, k_a, r_k, lnx_w, lnx_b, ret_gn_w, ret_gn_b, w_branch_a, w_branch_b, w_out, ln1_w, ln1_b, w_up, conv_w, conv_b, w_down, ln2_w, ln2_b):
    depth = w_ada.shape[0]
    nbp, seq_p, d = x_prompt.shape
    nbs, seq_s, _ = x_sample.shape
    assert seq_s == 1
    rw = k_k.shape[-1]
    ret_heads, dk, dv = state_ret.shape[2:]
    lora_w = w_decay_up.shape[1] + w_aaa_up.shape[1] + w_gate_up.shape[1]
    lay = _layout(d, rw, ret_heads * dk, ret_heads * dv, lora_w, ret_heads)
    lay["lora_dims"] = (w_decay_up.shape[1], w_aaa_up.shape[1], w_gate_up.shape[1])
    heads, hn = r_k.shape[1:]
    assert hn == LANES // 2 and heads * hn == rw
    alpha = (2.0 * depth) ** 0.25
    f2 = w_up.shape[-1]

    xp = x_prompt.reshape(nbp * seq_p, d)
    xs = x_sample.reshape(nbs * seq_s, d)
    c_all = jnp.concatenate([c_prompt, c_sample], axis=0)
    pad = (-c_all.shape[0]) % SUBLANES
    c_all = _bf(jnp.pad(c_all, ((0, pad), (0, 0))))

    outs_p, outs_s = [], []
    for l in range(depth):
        wd, wa, wg, pvec, mul = _prep_weights(
            lay, shift_mu[l], w0[l], w_decay_up[l], a0[l], w_aaa_up[l], w_gate_up[l],
            k_k[l], k_a[l], r_k[l], lnx_w[l], lnx_b[l])
        cwt = jnp.concatenate([conv_w[l], conv_b[l][None, :]], axis=0)
        wts = (wd, wa, wg, pvec, mul, ret_gn_w[l][None, :], ret_gn_b[l][None, :],
               _bf(w_branch_a[l]), _bf(w_branch_b[l]), _bf(w_out[l]), ln1_w[l][None, :], ln1_b[l][None, :],
               cwt, ln2_w[l][None, :], ln2_b[l][None, :])
        ada = _mm_bias(c_all, w_ada[l], b_ada[l][None, :])
        (xs, *st_s), rounded = _run_layer(
            xs, nbs, seq_s, ada[nbp:nbp + nbs],
            (state_wkv[l], state_shift[l], state_ret[l], state_conv[l]), float(PAST_LEN), lay, wts,
            (jnp.transpose(w_in[l]), w_up[l], w_down[l]), alpha)
        zero_states = (None, jnp.zeros((nbp, state_shift.shape[-1]), F32), None,
                       jnp.zeros((nbp, state_conv.shape[2], f2), F32))
        (xp, *st_p), _ = _run_layer(xp, nbp, seq_p, ada[:nbp], zero_states, 0.0, lay, wts, rounded, alpha)
        outs_p.append(st_p)
        outs_s.append(st_s)

    def stack(lst, j, ref):
        layers = [s[j].astype(ref.dtype) for s in lst]
        return layers[0][None] if depth == 1 else jnp.stack(layers, axis=0)

    refs = (state_wkv, state_shift, state_ret, state_conv)
    return (xp.reshape(x_prompt.shape), xs.reshape(x_sample.shape),
            *[stack(outs_p, j, refs[j]) for j in range(4)],
            *[stack(outs_s, j, refs[j]) for j in range(4)])
```

```python
import functools
import math

import jax
import jax.numpy as jnp
from jax import lax
from jax.experimental import pallas as pl
from jax.experimental.pallas import tpu as pltpu

F32 = jnp.float32
BF16 = jnp.bfloat16

PAST_LEN = 16384
ROPE_BASE = 10000.0
RWKV_GN_EPS = 64e-5
RET_GN_EPS = 1e-5
LN_EPS = 1e-5
RET_CHUNK = 128

LANES = 128
SUBLANES = 8
MXU_WIDTH = 256
VMEM_LIMIT_BYTES = 50 * 1024 * 1024

RWKV_CHUNK = 64
RWKV_PAIRS_PER_STEP = 8
RWKV_GROUPS_PER_STEP = 1


def _params(*sem):
    return pltpu.CompilerParams(dimension_semantics=sem, vmem_limit_bytes=VMEM_LIMIT_BYTES)


def _dot(a, b):
    return jnp.dot(a, b, preferred_element_type=F32)


def _dot_nt(a, b):
    return lax.dot_general(a, b, (((1,), (1,)), ((), ())), preferred_element_type=F32)


def _dot_tn(a, b):
    return lax.dot_general(a, b, (((0,), (0,)), ((), ())), preferred_element_type=F32)


def _bf(x):
    return x.astype(BF16)


def _split2(x):
    hi = x.astype(BF16)
    lo = (x - hi.astype(F32)).astype(BF16)
    return hi, lo


def _dot3_nn(a, b):
    ah, al = _split2(a)
    bh, bl = _split2(b)
    return _dot(jnp.concatenate([ah, ah, al], axis=1), jnp.concatenate([bh, bl, bh], axis=0))


def _dot3_nt(a, b):
    ah, al = _split2(a)
    bh, bl = _split2(b)
    return _dot_nt(jnp.concatenate([ah, ah, al], axis=1), jnp.concatenate([bh, bl, bh], axis=1))


def _dot_exact_lhs(a_bf, b):
    b1 = b.astype(BF16)
    r1 = b - b1.astype(F32)
    b2 = r1.astype(BF16)
    b3 = (r1 - b2.astype(F32)).astype(BF16)
    return _dot(jnp.concatenate([a_bf, a_bf, a_bf], axis=1), jnp.concatenate([b1, b2, b3], axis=0))


def _layer_norm_rows(t, w, b):
    mu = jnp.mean(t, axis=-1, keepdims=True)
    d = t - mu
    var = jnp.mean(d * d, axis=-1, keepdims=True)
    return d * lax.rsqrt(var + LN_EPS) * w + b


def _pick_tile(n, candidates):
    for c in candidates:
        if n % c == 0:
            return c
    return n


def _mm_bias_kernel(x_ref, w_ref, b_ref, o_ref):
    o_ref[...] = _dot(x_ref[...], _bf(w_ref[...])) + b_ref[...]


def _mm_bias(x_bf, w, b_row):
    m, k = x_bf.shape
    n = w.shape[1]
    tn = _pick_tile(n, (1024, 512, 256, 128))
    return pl.pallas_call(
        _mm_bias_kernel,
        grid=(n // tn,),
        in_specs=[pl.BlockSpec((m, k), lambda j: (0, 0)),
                  pl.BlockSpec((k, tn), lambda j: (0, j)),
                  pl.BlockSpec((1, tn), lambda j: (0, j))],
        out_specs=pl.BlockSpec((m, tn), lambda j: (0, j)),
        out_shape=jax.ShapeDtypeStruct((m, n), F32),
        compiler_params=_params("arbitrary"),
    )(x_bf, w, b_row)


def _modmm_kernel(x_ref, sc_ref, sh_ref, wt_ref, o_ref, u_ref):
    @pl.when(pl.program_id(1) == 0)
    def _():
        u_ref[...] = _bf(x_ref[...] * (1.0 + sc_ref[0]) + sh_ref[0])

    o_ref[...] = _dot_nt(u_ref[...], wt_ref[...]).astype(o_ref.dtype)


def _modmm_cast_kernel(x_ref, sc_ref, sh_ref, wt_ref, o_ref, wbf_ref, u_ref):
    @pl.when(pl.program_id(1) == 0)
    def _():
        u_ref[...] = _bf(x_ref[...] * (1.0 + sc_ref[0]) + sh_ref[0])

    w = _bf(wt_ref[...])
    wbf_ref[...] = w
    o_ref[...] = _dot_nt(u_ref[...], w).astype(o_ref.dtype)


def _modmm(x, sc, sh, wt, lay, tm, tpg):
    m, d = x.shape
    n = lay["nt"]
    tn = _pick_tile(n, (512, 256, 128))
    r = sc.shape[1]
    mod_spec = pl.BlockSpec((1, r, d), lambda i, j: (i // tpg, 0, 0))
    x_spec = pl.BlockSpec((tm, d), lambda i, j: (i, 0))
    o_spec = pl.BlockSpec((tm, tn), lambda i, j: (i, j))
    w_blocked = pl.BlockSpec((tn, d), lambda i, j: (j, 0))
    if wt.dtype == BF16:
        return pl.pallas_call(
            _modmm_kernel,
            grid=(m // tm, n // tn),
            in_specs=[x_spec, mod_spec, mod_spec, w_blocked],
            out_specs=o_spec,
            out_shape=jax.ShapeDtypeStruct((m, n), BF16),
            scratch_shapes=[pltpu.VMEM((tm, d), BF16)],
            compiler_params=_params("arbitrary", "arbitrary"),
        )(x, sc, sh, wt)

    assert m == tm
    rw3, lora_w, lslot = 3 * lay["rw"], lay["lora_w"], lay["lslot"]
    n_main, n_rest = rw3 // tn, (lay["off_l"] - rw3) // tn
    assert rw3 % tn == 0 and (lay["off_l"] - rw3) % tn == 0 and lslot == tn
    assert rw3 + lslot <= wt.shape[0]
    assert tn % SUBLANES == 0 and rw3 % SUBLANES == 0 and lora_w % SUBLANES == 0

    def w_row(j):
        row = jnp.where(j < n_main, j * tn,
                        jnp.where(j < n_main + n_rest, rw3 + lora_w + (j - n_main) * tn, rw3))
        return pl.multiple_of(row, SUBLANES)

    return pl.pallas_call(
        _modmm_cast_kernel,
        grid=(1, n // tn),
        in_specs=[x_spec, mod_spec, mod_spec,
                  pl.BlockSpec((pl.Element(tn), pl.Element(d)), lambda i, j: (w_row(j), 0))],
        out_specs=[o_spec, w_blocked],
        out_shape=[jax.ShapeDtypeStruct((m, n), BF16), jax.ShapeDtypeStruct((n, d), BF16)],
        scratch_shapes=[pltpu.VMEM((tm, d), BF16)],
        compiler_params=_params("arbitrary", "arbitrary"),
    )(x, sc, sh, wt)


_PV_MU_R, _PV_MU_K, _PV_MU_V, _PV_W0, _PV_A0, _PV_KK, _PV_KA, _PV_RK, _PV_LNW, _PV_LNB = range(10)
_PV_ROWS = 16


def _head_half_mask(shape):
    return lax.broadcasted_iota(jnp.int32, shape, 1) < (LANES // 2)


def _head_sums(x):
    h0 = _head_half_mask((x.shape[0], LANES))
    parts = []
    for p in range(x.shape[1] // LANES):
        xs = x[:, p * LANES:(p + 1) * LANES]
        s0 = jnp.sum(jnp.where(h0, xs, 0.0), axis=-1, keepdims=True)
        s1 = jnp.sum(jnp.where(h0, 0.0, xs), axis=-1, keepdims=True)
        parts.append(jnp.where(h0, s0, s1))
    return parts[0] if len(parts) == 1 else jnp.concatenate(parts, axis=1)


def _lora_split(lay):
    dl, al, gl = lay["lora_dims"]
    up = lambda n: -(-n // LANES) * LANES
    a1, g0, g1 = up(dl + al), (dl + al) // LANES * LANES, up(dl + al + gl)
    assert g1 <= lay["lslot"]
    return a1, g0, g1


def _rwkv_lora_inputs(zl, plr, mul, split):
    a1, g0, g1 = split
    ls = zl + mul * (plr - zl)
    return _bf(jnp.tanh(ls[:, :a1])), _bf(ls[:, :a1]), _bf(jax.nn.sigmoid(ls[:, g0:g1]))


def _rwkv_lora(lora_in, wd, wa, wg):
    return _dot(lora_in[0], wd), _dot(lora_in[1], wa), _dot(lora_in[2], wg)


def _rwkv_tokens(zr, zk, zv, pr, pk, pv_, lora, pvec):
    row = lambda i: pvec[i:i + 1, :]
    r = zr + row(_PV_MU_R) * (pr - zr)
    k = zk + row(_PV_MU_K) * (pk - zk)
    v = zv + row(_PV_MU_V) * (pv_ - zv)
    wl, al, g = lora
    logd = -math.exp(-0.5) * jax.nn.sigmoid(row(_PV_W0) + wl)
    a = jax.nn.sigmoid(row(_PV_A0) + al)
    kkr = k * row(_PV_KK)
    kk = kkr * jnp.minimum(lax.rsqrt(_head_sums(kkr * kkr)), 1e12)
    kp = k * (1.0 + (a - 1.0) * row(_PV_KA))
    bonus = _head_sums(r * kp * row(_PV_RK)) * v
    return r, kp, v, kk, a, logd, g, bonus


def _rwkv_finish(o, bonus, g, pvec):
    inv_n = 1.0 / (LANES // 2)
    mu = _head_sums(o) * inv_n
    d = o - mu
    var = _head_sums(d * d) * inv_n
    on = d * lax.rsqrt(var + RWKV_GN_EPS)
    return (on * pvec[_PV_LNW:_PV_LNW + 1, :] + pvec[_PV_LNB:_PV_LNB + 1, :] + bonus) * g


def _shift_rows(z, first_row):
    rolled = pltpu.roll(z, 1, 0)
    rowid = lax.broadcasted_iota(jnp.int32, z.shape, 0)
    return jnp.where(rowid == 0, first_row, rolled)


def _rwkv_seq_kernel(split, zr_ref, zk_ref, zv_ref, zl_ref, fr_ref, fk_ref, fv_ref, fl_ref,
                     pvec_ref, mul_ref, wd_ref, wa_ref, wg_ref, tri_ref,
                     ya_ref, wkv_ref, s_ref, cr_ref, ck_ref, cv_ref, cl_ref):
    i = pl.program_id(2)
    tc, full_width = zr_ref.shape
    c = RWKV_CHUNK
    nch = tc // c
    half = LANES // 2
    ngroups = RWKV_GROUPS_PER_STEP if (full_width // LANES) % RWKV_GROUPS_PER_STEP == 0 else 1
    npp = full_width // LANES // ngroups
    width = npp * LANES

    @pl.when(i == 0)
    def _():
        s_ref[...] = jnp.zeros_like(s_ref)
        cr_ref[0:1, :] = fr_ref[0]
        ck_ref[0:1, :] = fk_ref[0]
        cv_ref[0:1, :] = fv_ref[0]
        cl_ref[0:1, :] = fl_ref[0]

    zl = zl_ref[...].astype(F32)
    lora_in = _rwkv_lora_inputs(zl, _shift_rows(zl, cl_ref[0:1, :]), mul_ref[...], split)
    cl_ref[0:1, :] = zl[tc - 1:tc, :]
    tri = tri_ref[...]
    for gi in range(ngroups):
        _rwkv_seq_group(gi, npp, tc, c, nch, tri, lora_in, zr_ref, zk_ref, zv_ref, pvec_ref,
                        wd_ref, wa_ref, wg_ref, ya_ref, s_ref, cr_ref, ck_ref, cv_ref)

    @pl.when(i == pl.num_programs(2) - 1)
    def _():
        for p in range(s_ref.shape[0]):
            s = jnp.transpose(s_ref[p])
            wkv_ref[0, 2 * p] = s[:half, :half]
            wkv_ref[0, 2 * p + 1] = s[half:, half:]


def _rwkv_seq_group(gi, npp, tc, c, nch, tri, lora_in, zr_ref, zk_ref, zv_ref, pvec_ref,
                    wd_ref, wa_ref, wg_ref, ya_ref, s_ref, cr_ref, ck_ref, cv_ref):
    width = npp * LANES
    gl = slice(gi * width, (gi + 1) * width)
    lora_all = _rwkv_lora(lora_in, wd_ref[:, gl], wa_ref[:, gl], wg_ref[:, gl])

    tok = {}
    for p in range(npp):
        pl_ = slice(gi * width + p * LANES, gi * width + (p + 1) * LANES)
        ls = slice(p * LANES, (p + 1) * LANES)
        zr, zk, zv = (ref[:, pl_].astype(F32) for ref in (zr_ref, zk_ref, zv_ref))
        pvec_p = pvec_ref[:, pl_]
        r, kp, v, kk, a, logd, g, bonus = _rwkv_tokens(
            zr, zk, zv, _shift_rows(zr, cr_ref[0:1, pl_]), _shift_rows(zk, ck_ref[0:1, pl_]),
            _shift_rows(zv, cv_ref[0:1, pl_]), tuple(x[:, ls] for x in lora_all), pvec_p)
        cr_ref[0:1, pl_] = zr[tc - 1:tc, :]
        ck_ref[0:1, pl_] = zk[tc - 1:tc, :]
        cv_ref[0:1, pl_] = zv[tc - 1:tc, :]
        cum_incl = _dot_exact_lhs(tri, logd)
        tot_rows = [cum_incl[(ci + 1) * c - 1:(ci + 1) * c, :] for ci in range(nch)]
        w_tot = jnp.concatenate([jnp.broadcast_to(jnp.exp(t), (c, LANES)) for t in tot_rows], axis=0)
        e_in = jnp.exp(cum_incl)
        e_neg = 1.0 / e_in
        beta = kk * a
        b_t = beta * e_neg
        k_t = kp * e_neg
        tok[p] = dict(r_t=r * e_in, a_t=-kk * jnp.exp(cum_incl - logd), b_t=b_t, k_t=k_t,
                      b_h=b_t * w_tot, k_h=k_t * w_tot, v=v, g=g, bonus=bonus, pvec=pvec_p, tot_rows=tot_rows)

    h0 = _head_half_mask((c, LANES))

    def stack(name, p, ci):
        xb = tok[p][name][ci * c:(ci + 1) * c, :]
        return jnp.concatenate([jnp.where(h0, xb, 0.0), jnp.where(h0, 0.0, xb)], axis=0)

    keep0 = h0[0:1, :].astype(BF16)
    keep1 = 1.0 - keep0

    def stack_bf(name, p, ci):
        xb = _bf(tok[p][name][ci * c:(ci + 1) * c, :])
        return jnp.concatenate([xb * keep0, xb * keep1], axis=0)

    rr = lax.broadcasted_iota(jnp.int32, (2 * c, 2 * c), 0)
    cc = lax.broadcasted_iota(jnp.int32, (2 * c, 2 * c), 1)
    same = (rr >= c) == (cc >= c)
    strict = same & (cc < rr)
    incl = same & (cc <= rr)
    eye = rr == cc
    zeros_blk = jnp.zeros((2 * c, LANES), BF16)

    probs = [(p, ci) for p in range(npp) for ci in range(nch)]
    a_s = {q: stack("a_t", *q) for q in probs}
    r_s = {q: stack("r_t", *q) for q in probs}
    v_bf = {q: stack_bf("v", *q) for q in probs}

    def twice(name, p, ci):
        xb = _bf(tok[p][name][ci * c:(ci + 1) * c, :])
        return jnp.concatenate([xb, xb], axis=0)

    pmat, a_ak, a_r = {}, {}, {}
    for q in probs:
        amat = _dot_nt(_bf(jnp.concatenate([a_s[q], r_s[q]], axis=0)),
                       jnp.concatenate([twice("b_t", *q), twice("k_t", *q)], axis=0))
        pmat[q] = jnp.where(strict, amat[:2 * c, :2 * c], 0.0)
        a_ak[q] = _bf(jnp.where(strict, amat[:2 * c, 2 * c:], 0.0))
        a_r[q] = _bf(jnp.concatenate([jnp.where(incl, amat[2 * c:, :2 * c], 0.0),
                                      jnp.where(incl, amat[2 * c:, 2 * c:], 0.0)], axis=1))

    x = {q: jnp.concatenate([a_s[q], _dot(a_ak[q], v_bf[q])], axis=1) for q in probs}

    nsteps = int(math.log2(c))
    for it in range(nsteps):
        for q in probs:
            p_bf = _bf(pmat[q])
            x[q] = x[q] + _dot(p_bf, _bf(x[q]))
            if it + 1 < nsteps:
                pmat[q] = _dot(p_bf, p_bf)

    r_pair, o_pair, g_t, h_t = {}, {}, {}, {}
    for q in probs:
        p, ci = q
        x_bf = _bf(x[q])
        rhs = jnp.concatenate([x_bf, jnp.concatenate([zeros_blk, v_bf[q]], axis=1)], axis=0)
        y = _dot(a_r[q], rhs)
        r_hat = r_s[q] + y[:, :LANES]
        r_pair[q] = r_hat[:c] + r_hat[c:]
        o_pair[q] = y[:c, LANES:] + y[c:, LANES:]
        z = _dot_tn(stack_bf("b_h", *q), x_bf)
        w_c = jnp.exp(tok[p]["tot_rows"][ci])
        g_t[q] = jnp.where(eye, w_c, 0.0) + z[:, :LANES]
        h_t[q] = z[:, LANES:] + _dot_tn(stack_bf("k_h", *q), v_bf[q])

    outs = {}
    for ci in range(nch):
        for p in range(npp):
            q = (p, ci)
            t0 = s_ref[gi * npp + p]
            outs[q] = _dot3_nn(r_pair[q], t0) + o_pair[q]
            s_ref[gi * npp + p] = _dot3_nn(g_t[q], t0) + h_t[q]

    for p in range(npp):
        o = jnp.concatenate([outs[(p, ci)] for ci in range(nch)], axis=0) if nch > 1 else outs[(p, 0)]
        pl_ = slice(gi * width + p * LANES, gi * width + (p + 1) * LANES)
        ya_ref[:, pl_] = _bf(_rwkv_finish(o, tok[p]["bonus"], tok[p]["g"], tok[p]["pvec"]))


def _rwkv_seq(proj, nb, seq, lay, first, pvec, mul, wd, wa, wg):
    rw, lslot = lay["rw"], lay["lslot"]
    npair = rw // LANES
    npp = RWKV_PAIRS_PER_STEP if npair % RWKV_PAIRS_PER_STEP == 0 else 1
    ngrp = npair // npp
    width = npp * LANES
    c = RWKV_CHUNK
    tc = _pick_tile(seq, (2 * c, c))
    nt = seq // tc
    t_idx = jnp.arange(tc)
    tri = ((t_idx[:, None] // c == t_idx[None, :] // c) & (t_idx[None, :] <= t_idx[:, None])).astype(BF16)
    fr, fk, fv, fl = first
    col = lambda s: pl.BlockSpec((tc, width), lambda b, gp, i, s=s: (b * nt + i, s * ngrp + gp))
    fcol = lambda s: pl.BlockSpec((1, 1, width), lambda b, gp, i, s=s: (b, 0, s * ngrp + gp))
    split = _lora_split(lay)
    wspec = pl.BlockSpec((split[0], width), lambda b, gp, i: (0, gp))
    wgspec = pl.BlockSpec((split[2] - split[1], width), lambda b, gp, i: (0, gp))
    return pl.pallas_call(
        functools.partial(_rwkv_seq_kernel, split),
        grid=(nb, ngrp, nt),
        in_specs=[col(0), col(1), col(2),
                  pl.BlockSpec((tc, lslot), lambda b, gp, i: (b * nt + i, lay["off_l"] // lslot)),
                  fcol(0), fcol(1), fcol(2),
                  pl.BlockSpec((1, 1, lslot), lambda b, gp, i: (b, 0, 0)),
                  pl.BlockSpec((_PV_ROWS, width), lambda b, gp, i: (0, gp)),
                  pl.BlockSpec((1, lslot), lambda b, gp, i: (0, 0)),
                  wspec, wspec, wgspec,
                  pl.BlockSpec((tc, tc), lambda b, gp, i: (0, 0))],
        out_specs=[pl.BlockSpec((tc, width), lambda b, gp, i: (b * nt + i, gp)),
                   pl.BlockSpec((1, 2 * npp, LANES // 2, LANES // 2), lambda b, gp, i: (b, gp, 0, 0))],
        out_shape=[jax.ShapeDtypeStruct((nb * seq, rw), BF16),
                   jax.ShapeDtypeStruct((nb, 2 * npair, LANES // 2, LANES // 2), F32)],
        scratch_shapes=[pltpu.VMEM((npp, LANES, LANES), F32),
                        pltpu.VMEM((SUBLANES, width), F32), pltpu.VMEM((SUBLANES, width), F32),
                        pltpu.VMEM((SUBLANES, width), F32), pltpu.VMEM((SUBLANES, lslot), F32)],
        compiler_params=_params("arbitrary", "arbitrary", "arbitrary"),
    )(proj, proj, proj, proj, fr, fk, fv, fl, pvec, mul, wd, wa, wg, tri)


def _rwkv_step_kernel(split, zr_ref, zk_ref, zv_ref, zl_ref, pr_ref, pk_ref, pv_ref, plr_ref,
                      pvec_ref, mul_ref, wd_ref, wa_ref, wg_ref, s_ref,
                      ya_ref, snew_ref, o_s):
    half = LANES // 2
    pvec = pvec_ref[...]
    r, kp, v, kk, a, logd, g, bonus = _rwkv_tokens(
        zr_ref[...].astype(F32), zk_ref[...].astype(F32), zv_ref[...].astype(F32),
        pr_ref[...], pk_ref[...], pv_ref[...],
        _rwkv_lora(_rwkv_lora_inputs(zl_ref[...].astype(F32), plr_ref[...], mul_ref[...], split),
                   wd_ref[...], wa_ref[...], wg_ref[...]),
        pvec)
    w = jnp.exp(logd)
    nkk_t, wr_t, w_t, beta_t, kp_t, r_t, v_t = (jnp.transpose(x) for x in (-kk, w * r, w, kk * a, kp, r, v))
    for e in range(2):
        ks = slice(e * half, (e + 1) * half)
        nkk_e, wr_e, w_e, beta_e, kp_e = nkk_t[ks], wr_t[ks], w_t[ks], beta_t[ks], kp_t[ks]
        c_beta = jnp.sum(beta_e * r_t[ks], axis=0, keepdims=True)
        c_k = jnp.sum(kp_e * r_t[ks], axis=0, keepdims=True)
        for vi in range(half):
            row = e * half + vi
            s = s_ref[e, vi]
            sa = jnp.sum(s * nkk_e, axis=0, keepdims=True)
            sw = jnp.sum(s * wr_e, axis=0, keepdims=True)
            v_row = v_t[row:row + 1, :]
            snew_ref[e, vi] = s * w_e + sa * beta_e + v_row * kp_e
            o_s[row:row + 1, :] = sw + sa * c_beta + v_row * c_k
    ya_ref[...] = _bf(_rwkv_finish(jnp.transpose(o_s[...]), bonus, g, pvec))


def _rwkv_step(proj, nb, lay, prev, pvec, mul, wd, wa, wg, s_wkv):
    rw, lslot = lay["rw"], lay["lslot"]
    npair = rw // LANES
    pm, plr = prev
    half = LANES // 2
    s_t = jnp.transpose(s_wkv, (1, 2, 3, 0))
    col = lambda off: pl.BlockSpec((nb, LANES), lambda p, off=off: (0, off + p))
    split = _lora_split(lay)
    wspec = pl.BlockSpec((split[0], LANES), lambda p: (0, p))
    wgspec = pl.BlockSpec((split[2] - split[1], LANES), lambda p: (0, p))
    sspec = pl.BlockSpec((2, half, half, nb), lambda p: (p, 0, 0, 0))
    ya, snew_t = pl.pallas_call(
        functools.partial(_rwkv_step_kernel, split),
        grid=(npair,),
        in_specs=[col(0), col(npair), col(2 * npair),
                  pl.BlockSpec((nb, lslot), lambda p: (0, lay["off_l"] // lslot)),
                  col(0), col(npair), col(2 * npair),
                  pl.BlockSpec((nb, lslot), lambda p: (0, 0)),
                  pl.BlockSpec((_PV_ROWS, LANES), lambda p: (0, p)),
                  pl.BlockSpec((1, lslot), lambda p: (0, 0)),
                  wspec, wspec, wgspec, sspec],
        out_specs=[pl.BlockSpec((nb, LANES), lambda p: (0, p)), sspec],
        out_shape=[jax.ShapeDtypeStruct((nb, rw), BF16), jax.ShapeDtypeStruct(s_t.shape, F32)],
        scratch_shapes=[pltpu.VMEM((LANES, nb), F32)],
        compiler_params=_params("arbitrary"),
    )(proj, proj, proj, proj, pm, pm, pm, plr, pvec, mul, wd, wa, wg, s_t)
    return ya, jnp.transpose(snew_t, (3, 0, 1, 2))


def _rope_rows(t, cos2, sin2):
    return t * cos2 + pltpu.roll(t, LANES // 2, 1) * sin2


def _head_norm_rows(o, eps):
    mu = jnp.mean(o, axis=-1, keepdims=True)
    d = o - mu
    var = jnp.mean(d * d, axis=-1, keepdims=True)
    return d * lax.rsqrt(var + eps)


def _ret_seq_kernel(q_ref, k_ref, v_ref, g_ref, cos_ref, sin_ref, intra_ref, qd_ref, kd_ref,
                    blk_ref, gnw_ref, gnb_ref, yb_ref, ret_ref, s_ref):
    i = pl.program_id(1)
    nh = s_ref.shape[0]
    dk = s_ref.shape[1]

    @pl.when(i == 0)
    def _():
        s_ref[...] = jnp.zeros_like(s_ref)

    cos2, sin2 = cos_ref[...], sin_ref[...]
    heads = range(nh)
    hs = [slice(h * LANES, (h + 1) * LANES) for h in heads]
    kh = [_rope_rows(k_ref[:, hs[h]].astype(F32), cos2, sin2) * (dk ** -0.5) for h in heads]
    qb = [_bf(_rope_rows(q_ref[:, hs[h]].astype(F32), cos2, sin2)) for h in heads]
    vb = [_bf(v_ref[:, hs[h]]) for h in heads]
    scores = [_bf(_dot_nt(qb[h], _bf(kh[h])) * intra_ref[h]) for h in heads]
    s0 = [s_ref[h] for h in heads]
    o = [_dot(scores[h], vb[h]) + _dot(qb[h], _bf(s0[h])) * qd_ref[h] for h in heads]
    for h in heads:
        s_ref[h] = s0[h] * blk_ref[h] + _dot_tn(_bf(kh[h] * kd_ref[h]), vb[h])
    for h in heads:
        on = _head_norm_rows(o[h], RET_GN_EPS)
        yb_ref[:, hs[h]] = _bf((on * gnw_ref[:, hs[h]] + gnb_ref[:, hs[h]])
                               * jax.nn.silu(g_ref[:, hs[h]].astype(F32)))

    @pl.when(i == pl.num_programs(1) - 1)
    def _():
        ret_ref[0] = s_ref[...]


def _ret_tables(nh, c):
    log_g = jnp.log1p(-jnp.exp2(-5.0 - jnp.arange(nh, dtype=F32)))
    i = jnp.arange(c, dtype=F32)
    rel = i[:, None] - i[None, :]
    intra = jnp.where(rel >= 0, jnp.exp(log_g[:, None, None] * jnp.maximum(rel, 0.0)), 0.0)
    q_decay = jnp.exp(log_g[:, None] * (i + 1.0))
    k_decay = jnp.exp(log_g[:, None] * (c - 1.0 - i))
    blk_decay = jnp.exp(log_g * c)
    return intra, q_decay, k_decay, blk_decay


def _rope_tables(pos, dk):
    half = dk // 2
    inv = ROPE_BASE ** (-jnp.arange(half, dtype=F32) / half)
    ang = pos[:, None] * inv[None, :]
    cos, sin = jnp.cos(ang), jnp.sin(ang)
    return jnp.concatenate([cos, cos], axis=-1), jnp.concatenate([-sin, sin], axis=-1)


def _ret_seq(proj, nb, seq, lay, gnw, gnb):
    qk, rv, nh = lay["qk"], lay["rv"], lay["ret_heads"]
    dk, dv = qk // nh, rv // nh
    assert dk == LANES and dv == LANES
    c = RET_CHUNK if seq % RET_CHUNK == 0 else seq
    assert c % SUBLANES == 0
    nt = seq // c
    intra, qd, kd, blk = _ret_tables(nh, c)
    qd = jnp.broadcast_to(qd[:, :, None], (nh, c, dv))
    kd = jnp.broadcast_to(kd[:, :, None], (nh, c, dk))
    blk = jnp.broadcast_to(blk[:, None, None], (nh, 1, dv))
    cos2, sin2 = _rope_tables(jnp.arange(seq, dtype=F32), dk)
    seg = lambda off, w: pl.BlockSpec((c, w), lambda b, i, off=off, w=w: (b * nt + i, off // w))
    full3 = lambda a: pl.BlockSpec(a.shape, lambda b, i: (0, 0, 0))
    return pl.pallas_call(
        _ret_seq_kernel,
        grid=(nb, nt),
        in_specs=[seg(lay["off_q"], qk), seg(lay["off_kr"], qk), seg(lay["off_vr"], rv),
                  seg(lay["off_rg"], rv),
                  pl.BlockSpec((c, dk), lambda b, i: (i, 0)), pl.BlockSpec((c, dk), lambda b, i: (i, 0)),
                  full3(intra), full3(qd), full3(kd), full3(blk),
                  pl.BlockSpec((1, rv), lambda b, i: (0, 0)), pl.BlockSpec((1, rv), lambda b, i: (0, 0))],
        out_specs=[pl.BlockSpec((c, rv), lambda b, i: (b * nt + i, 0)),
                   pl.BlockSpec((1, nh, dk, dv), lambda b, i: (b, 0, 0, 0))],
        out_shape=[jax.ShapeDtypeStruct((nb * seq, rv), BF16),
                   jax.ShapeDtypeStruct((nb, nh, dk, dv), F32)],
        scratch_shapes=[pltpu.VMEM((nh, dk, dv), F32)],
        compiler_params=_params("arbitrary", "arbitrary"),
    )(proj, proj, proj, proj, cos2, sin2, intra, qd, kd, blk, gnw, gnb)


_RC_INTRA, _RC_QD, _RC_KD, _RC_BLK = range(4)


def _ret_step_kernel(q_ref, k_ref, v_ref, g_ref, cos_ref, sin_ref, rc_ref, gnw_ref, gnb_ref, s_ref,
                     yb_ref, snew_ref, q_s, k_s, v_s, o_s):
    bb = q_ref.shape[0]
    nh = s_ref.shape[1]
    dk = s_ref.shape[2]
    cos2, sin2 = cos_ref[...], sin_ref[...]
    for h in range(nh):
        hs = slice(h * LANES, (h + 1) * LANES)
        q_s[:, hs] = _rope_rows(q_ref[:, hs].astype(F32), cos2, sin2)
        k_s[:, hs] = _rope_rows(k_ref[:, hs].astype(F32), cos2, sin2) * (dk ** -0.5)
        v_s[:, hs] = v_ref[:, hs].astype(F32)
    eye = (lax.broadcasted_iota(jnp.int32, (LANES, LANES), 0)
           == lax.broadcasted_iota(jnp.int32, (LANES, LANES), 1))
    for b in range(bb):
        for h in range(nh):
            hs = slice(h * LANES, (h + 1) * LANES)
            rc = lambda j: rc_ref[h, j:j + 1, :]
            q_row = q_s[b:b + 1, hs]
            k_row = k_s[b:b + 1, hs]
            v_row = v_s[b:b + 1, hs]
            s0 = s_ref[b, h]
            q_col = jnp.sum(jnp.where(eye, q_row, 0.0), axis=-1, keepdims=True)
            k_col = jnp.sum(jnp.where(eye, k_row, 0.0), axis=-1, keepdims=True)
            score = jnp.sum(q_row * k_row, axis=-1, keepdims=True) * rc(_RC_INTRA)
            o_row = score * v_row + jnp.sum(s0 * q_col, axis=0, keepdims=True) * rc(_RC_QD)
            snew_ref[b, h] = s0 * rc(_RC_BLK) + (k_col * rc(_RC_KD)) * v_row
            o_s[b:b + 1, hs] = o_row
    for h in range(nh):
        hs = slice(h * LANES, (h + 1) * LANES)
        on = _head_norm_rows(o_s[:, hs], RET_GN_EPS)
        yb_ref[:, hs] = _bf((on * gnw_ref[:, hs] + gnb_ref[:, hs]) * jax.nn.silu(g_ref[:, hs].astype(F32)))


def _ret_step(proj, nb, lay, gnw, gnb, s_ret, pos0):
    qk, rv, nh = lay["qk"], lay["rv"], lay["ret_heads"]
    dk, dv = qk // nh, rv // nh
    assert dk == LANES and dv == LANES
    bb = 2 * SUBLANES
    assert nb % bb == 0
    intra, qd, kd, blk = _ret_tables(nh, 1)
    rc = jnp.stack([intra[:, 0, 0], qd[:, 0], kd[:, 0], blk], axis=1)
    rc = jnp.pad(rc, ((0, 0), (0, SUBLANES - 4)))
    rc = jnp.broadcast_to(rc[:, :, None], (nh, SUBLANES, LANES))
    cos2, sin2 = _rope_tables(jnp.asarray([pos0], dtype=F32), dk)
    seg = lambda off, w: pl.BlockSpec((bb, w), lambda j, off=off, w=w: (j, off // w))
    sspec = pl.BlockSpec((bb, nh, dk, dv), lambda j: (j, 0, 0, 0))
    row = lambda w: pl.BlockSpec((1, w), lambda j: (0, 0))
    return pl.pallas_call(
        _ret_step_kernel,
        grid=(nb // bb,),
        in_specs=[seg(lay["off_q"], qk), seg(lay["off_kr"], qk), seg(lay["off_vr"], rv),
                  seg(lay["off_rg"], rv), row(dk), row(dk),
                  pl.BlockSpec(rc.shape, lambda j: (0, 0, 0)), row(rv), row(rv), sspec],
        out_specs=[pl.BlockSpec((bb, rv), lambda j: (j, 0)), sspec],
        out_shape=[jax.ShapeDtypeStruct((nb, rv), BF16), jax.ShapeDtypeStruct(s_ret.shape, F32)],
        scratch_shapes=[pltpu.VMEM((bb, qk), F32), pltpu.VMEM((bb, qk), F32), pltpu.VMEM((bb, rv), F32),
                        pltpu.VMEM((bb, rv), F32)],
        compiler_params=_params("arbitrary"),
    )(proj, proj, proj, proj, cos2, sin2, rc, gnw, gnb, s_ret)


def _merge_kernel(alpha, ng, *refs):
    ya_ref, yb_ref = refs[0], refs[1]
    ga_refs = refs[2:2 + ng]
    gb_refs = refs[2 + ng:2 + 2 * ng]
    (x_ref, g1_ref, sc2_ref, sh2_ref, wa_ref, wb_ref, wo_ref, lnw_ref, lnb_ref,
     x1_ref, u2_ref) = refs[2 + 2 * ng:]
    cat = lambda rs: jnp.concatenate([r[...].astype(F32) for r in rs], axis=1)
    merged = (jax.nn.sigmoid(cat(ga_refs)) * _dot(ya_ref[...], wa_ref[...])
              + jax.nn.sigmoid(cat(gb_refs)) * _dot(yb_ref[...], wb_ref[...]))
    t = alpha * x_ref[...] + g1_ref[0] * _dot(_bf(merged), wo_ref[...])
    x1 = _layer_norm_rows(t, lnw_ref[...], lnb_ref[...])
    x1_ref[...] = x1
    u2_ref[...] = _bf(x1 * (1.0 + sc2_ref[0]) + sh2_ref[0])


def _merge(ya, yb, proj, x, g1, sc2, sh2, wba, wbb, wout, lnw, lnb, lay, alpha, tm, tpg):
    m, d = x.shape
    rw, rv = ya.shape[1], yb.shape[1]
    gw = math.gcd(lay["off_ga"], d)
    ng = d // gw
    r = g1.shape[1]
    rowt = lambda w: pl.BlockSpec((tm, w), lambda i: (i, 0))
    gate = lambda off, q: pl.BlockSpec((tm, gw), lambda i, off=off, q=q: (i, off // gw + q))
    mod = pl.BlockSpec((1, r, d), lambda i: (i // tpg, 0, 0))
    const = lambda a: pl.BlockSpec(a.shape, lambda i: (0, 0), pipeline_mode=pl.Buffered(1))
    return pl.pallas_call(
        functools.partial(_merge_kernel, alpha, ng),
        grid=(m // tm,),
        in_specs=[rowt(rw), rowt(rv)]
                 + [gate(lay["off_ga"], q) for q in range(ng)]
                 + [gate(lay["off_gb"], q) for q in range(ng)]
                 + [rowt(d), mod, mod, mod, const(wba), const(wbb), const(wout), const(lnw), const(lnb)],
        out_specs=[rowt(d), rowt(d)],
        out_shape=[jax.ShapeDtypeStruct((m, d), F32), jax.ShapeDtypeStruct((m, d), BF16)],
        compiler_params=_params("arbitrary"),
    )(ya, yb, *([proj] * (2 * ng)), x, g1, sc2, sh2, wba, wbb, wout, lnw, lnb)


def _ffn_seq_kernel(tps, nj, nsteps, alpha, u_ref, wa_ref, wb_ref, cwa_ref, cwb_ref, ia_ref, ib_ref,
                    wd_ref, x1_ref, g2_ref, lnw_ref, lnb_ref, o_ref, ta_ref, tb_ref,
                    ca_s, cb_s, ha_s, hb_s, act_s):
    t = pl.program_id(0)
    tm = u_ref.shape[0]

    @pl.when(t == 0)
    def _():
        act_s[...] = jnp.zeros_like(act_s)
        ha_s[...] = jnp.zeros_like(ha_s)
        hb_s[...] = jnp.zeros_like(hb_s)

    @pl.when((t < 2) | ((t - 2) % nj == 0))
    def _():
        o_ref[...] = jnp.zeros_like(o_ref)

    u = u_ref[...]
    ha_s[t % 2] = _dot(u, wa_ref[...])
    hb_s[t % 2] = _dot(u, wb_ref[...])

    o_ref[...] += _dot(act_s[t % 2], wd_ref[...])

    te = jnp.clip(t - 1, 0, nsteps - 1)
    i, jc = te // nj, te % nj
    first = i % tps == 0
    halves, tails = [], []
    for h_s, cw_ref, init_ref, c_s in ((ha_s, cwa_ref, ia_ref, ca_s), (hb_s, cwb_ref, ib_ref, cb_s)):
        h = h_s[(t + 1) % 2]
        prev = jnp.where(first, init_ref[0], c_s[jc])
        rid = lax.broadcasted_iota(jnp.int32, h.shape, 0)
        h1 = jnp.where(rid == 0, prev[SUBLANES - 1:SUBLANES, :], pltpu.roll(h, 1, 0))
        h2 = jnp.where(rid == 0, prev[SUBLANES - 2:SUBLANES - 1, :],
                       jnp.where(rid == 1, prev[SUBLANES - 1:SUBLANES, :], pltpu.roll(h, 2, 0)))
        tail = h[tm - SUBLANES:tm, :]
        tails.append(tail)
        cw = cw_ref[...]
        halves.append(cw[3:4, :] + cw[0:1, :] * h2 + cw[1:2, :] * h1 + cw[2:3, :] * h)
    act_s[(t + 1) % 2] = _bf(jax.nn.silu(halves[0]) * halves[1])

    @pl.when(t >= 1)
    def _():
        ca_s[jc] = tails[0]
        cb_s[jc] = tails[1]

    @pl.when((t >= 1) & (t <= nsteps) & (i % tps == tps - 1))
    def _():
        ta_ref[i // tps, jc] = tails[0]
        tb_ref[i // tps, jc] = tails[1]

    @pl.when((t >= 2) & ((t - 1) % nj == 0))
    def _():
        y = alpha * x1_ref[...] + g2_ref[0] * o_ref[...]
        o_ref[...] = _layer_norm_rows(y, lnw_ref[...], lnb_ref[...])


def _ffn_up_step_kernel(u_ref, wa_ref, wb_ref, cwa_ref, cwb_ref, p1a_ref, p2a_ref, p1b_ref, p2b_ref,
                        act_ref, ha_ref, hb_ref, wa_bf_ref, wb_bf_ref):
    u = u_ref[...]
    halves = []
    for w_ref, cw_ref, p1_ref, p2_ref, h_ref, w_bf_ref in (
            (wa_ref, cwa_ref, p1a_ref, p2a_ref, ha_ref, wa_bf_ref),
            (wb_ref, cwb_ref, p1b_ref, p2b_ref, hb_ref, wb_bf_ref)):
        w = _bf(w_ref[...])
        w_bf_ref[...] = w
        h = _dot(u, w)
        h_ref[...] = h
        cw = cw_ref[...]
        halves.append(cw[3:4, :] + cw[0:1, :] * p2_ref[...] + cw[1:2, :] * p1_ref[...] + cw[2:3, :] * h)
    act_ref[...] = _bf(jax.nn.silu(halves[0]) * halves[1])


def _ffn_up(u2, w_up, cwt, nb, seq, s_conv, tm):
    m, d = u2.shape
    assert m == tm
    f2 = w_up.shape[1]
    f = f2 // 2
    tn = _pick_tile(f, (512, 256, 128))
    nj = f // tn
    w_a = pl.BlockSpec((d, tn), lambda i, j: (0, j))
    w_b = pl.BlockSpec((d, tn), lambda i, j: (0, nj + j))
    cw_a = pl.BlockSpec((4, tn), lambda i, j: (0, j))
    cw_b = pl.BlockSpec((4, tn), lambda i, j: (0, nj + j))
    assert seq == 1
    p_a = pl.BlockSpec((tm, tn), lambda i, j: (i, j))
    p_b = pl.BlockSpec((tm, tn), lambda i, j: (i, nj + j))
    prev1, prev2 = s_conv[:, 1, :], s_conv[:, 0, :]
    w_o = pl.BlockSpec((d, tn), lambda i, j: (0, j))
    act, h_a, h_b, w_a_bf, w_b_bf = pl.pallas_call(
        _ffn_up_step_kernel,
        grid=(m // tm, nj),
        in_specs=[pl.BlockSpec((tm, d), lambda i, j: (i, 0)), w_a, w_b, cw_a, cw_b, p_a, p_a, p_b, p_b],
        out_specs=[pl.BlockSpec((tm, tn), lambda i, j: (i, j))] * 3 + [w_o, w_o],
        out_shape=[jax.ShapeDtypeStruct((m, f), BF16), jax.ShapeDtypeStruct((m, f), F32),
                   jax.ShapeDtypeStruct((m, f), F32), jax.ShapeDtypeStruct((d, f), BF16),
                   jax.ShapeDtypeStruct((d, f), BF16)],
        compiler_params=_params("arbitrary", "arbitrary"),
    )(u2, w_up, w_up, cwt, cwt, prev1, prev2, prev1, prev2)
    conv_new = jnp.stack([prev1, jnp.concatenate([h_a, h_b], axis=-1)], axis=1)
    return act, conv_new, (w_a_bf, w_b_bf)


def _ffn_seq(u2, w_up_halves, cwt, w_down_bf, x1, g2, lnw, lnb, alpha, nb, seq, s_conv, tm):
    m, d = u2.shape
    w_up_a, w_up_b = w_up_halves
    f = w_up_a.shape[1]
    tn = _pick_tile(f, (512, 256, 128))
    nj = f // tn
    assert seq % tm == 0 and tm >= SUBLANES
    tps = seq // tm
    nsteps = (m // tm) * nj
    pair = lambda t, s: jnp.clip(t - s, 0, nsteps - 1)
    row_i = lambda t, s: pair(t, s) // nj
    col_j = lambda t, s: pair(t, s) % nj
    w_up = pl.BlockSpec((d, tn), lambda t: (0, col_j(t, 0)))
    cw_a = pl.BlockSpec((4, tn), lambda t: (0, col_j(t, 1)))
    cw_b = pl.BlockSpec((4, tn), lambda t: (0, nj + col_j(t, 1)))
    init = jnp.pad(s_conv, ((0, 0), (SUBLANES - 2, 0), (0, 0)))
    i_a = pl.BlockSpec((1, SUBLANES, tn), lambda t: (row_i(t, 1) // tps, 0, col_j(t, 1)))
    i_b = pl.BlockSpec((1, SUBLANES, tn), lambda t: (row_i(t, 1) // tps, 0, nj + col_j(t, 1)))
    w_d = pl.BlockSpec((tn, d), lambda t: (col_j(t, 2), 0))
    row_up = pl.BlockSpec((tm, d), lambda t: (row_i(t, 0), 0))
    row_dn = pl.BlockSpec((tm, d), lambda t: (row_i(t, 2), 0))
    vec = pl.BlockSpec((1, d), lambda t: (0, 0))
    t_o = pl.BlockSpec((nb, nj, SUBLANES, tn), lambda t: (0, 0, 0, 0))
    tails = jax.ShapeDtypeStruct((nb, nj, SUBLANES, tn), F32)
    x2, t_a, t_b = pl.pallas_call(
        functools.partial(_ffn_seq_kernel, tps, nj, nsteps, alpha),
        grid=(nsteps + 2,),
        in_specs=[row_up, w_up, w_up, cw_a, cw_b, i_a, i_b, w_d, row_dn,
                  pl.BlockSpec((1, 1, d), lambda t: (row_i(t, 2) // tps, 0, 0)), vec, vec],
        out_specs=[row_dn, t_o, t_o],
        out_shape=[jax.ShapeDtypeStruct((m, d), F32), tails, tails],
        scratch_shapes=[pltpu.VMEM((nj, SUBLANES, tn), F32), pltpu.VMEM((nj, SUBLANES, tn), F32),
                        pltpu.VMEM((2, tm, tn), F32), pltpu.VMEM((2, tm, tn), F32),
                        pltpu.VMEM((2, tm, tn), BF16)],
        compiler_params=_params("arbitrary"),
    )(u2, w_up_a, w_up_b, cwt, cwt, init, init, w_down_bf, x1, g2, lnw, lnb)
    rows = lambda t: t[:, :, SUBLANES - 2:, :].transpose(0, 2, 1, 3).reshape(nb, 2, f)
    return x2, jnp.concatenate([rows(t_a), rows(t_b)], axis=-1)


def _ffn_down_kernel(alpha, act_ref, w_ref, x1_ref, g2_ref, lnw_ref, lnb_ref, o_ref, w_bf_ref, acc_ref):
    k = pl.program_id(1)

    @pl.when(k == 0)
    def _():
        acc_ref[...] = jnp.zeros_like(acc_ref)

    w = _bf(w_ref[...])
    w_bf_ref[...] = w
    acc_ref[...] += _dot(act_ref[...], w)

    @pl.when(k == pl.num_programs(1) - 1)
    def _():
        t = alpha * x1_ref[...] + g2_ref[0] * acc_ref[...]
        o_ref[...] = _layer_norm_rows(t, lnw_ref[...], lnb_ref[...])


def _ffn_down(act, w_down, x1, g2, lnw, lnb, alpha, tm, tpg):
    m, f = act.shape
    assert m == tm
    d = w_down.shape[1]
    tk = _pick_tile(f, (512, 256, 128))
    r = g2.shape[1]
    return pl.pallas_call(
        functools.partial(_ffn_down_kernel, alpha),
        grid=(m // tm, f // tk),
        in_specs=[pl.BlockSpec((tm, tk), lambda i, k: (i, k)),
                  pl.BlockSpec((tk, d), lambda i, k: (k, 0)),
                  pl.BlockSpec((tm, d), lambda i, k: (i, 0)),
                  pl.BlockSpec((1, r, d), lambda i, k: (i // tpg, 0, 0)),
                  pl.BlockSpec((1, d), lambda i, k: (0, 0)),
                  pl.BlockSpec((1, d), lambda i, k: (0, 0))],
        out_specs=[pl.BlockSpec((tm, d), lambda i, k: (i, 0)), pl.BlockSpec((tk, d), lambda i, k: (k, 0))],
        out_shape=[jax.ShapeDtypeStruct((m, d), F32), jax.ShapeDtypeStruct((f, d), BF16)],
        scratch_shapes=[pltpu.VMEM((tm, d), F32)],
        compiler_params=_params("arbitrary", "arbitrary"),
    )(act, w_down, x1, g2, lnw, lnb)


def _layout(d, rw, qk, rv, lora_w, ret_heads):
    lslot = next(s for s in (128, 256, 512, 1024, 2048) if s >= lora_w)
    off_q = 3 * rw
    off_kr = off_q + qk
    off_vr = off_kr + qk
    off_rg = off_vr + rv
    off_ga = off_rg + rv
    off_gb = off_ga + d
    off_l = off_gb + d
    assert rw % LANES == 0 and off_l % lslot == 0
    assert off_q % qk == 0 and off_kr % qk == 0 and off_vr % rv == 0 and off_rg % rv == 0
    return dict(d=d, rw=rw, qk=qk, rv=rv, lslot=lslot, lora_w=lora_w, ret_heads=ret_heads,
                off_q=off_q, off_kr=off_kr, off_vr=off_vr, off_rg=off_rg, off_ga=off_ga,
                off_gb=off_gb, off_l=off_l, nt=off_l + lslot)


def _prep_weights(lay, shift_mu, w0, w_decay_up, a0, w_aaa_up, w_gate_up, k_k, k_a, r_k, lnx_w, lnx_b):
    rw, lslot, lora_w = lay["rw"], lay["lslot"], lay["lora_w"]
    dl, al, gl = lay["lora_dims"]
    a1, g0, g1 = _lora_split(lay)
    pad_rows = lambda w, lo, n: jnp.pad(w, ((lo, n - lo - w.shape[0]), (0, 0))).astype(BF16)
    wd = pad_rows(w_decay_up, 0, a1)
    wa = pad_rows(w_aaa_up, dl, a1)
    wg = pad_rows(w_gate_up, dl + al - g0, g1 - g0)
    rows = [shift_mu[:rw], shift_mu[rw:2 * rw], shift_mu[2 * rw:3 * rw], w0, a0, k_k, k_a,
            r_k.reshape(-1), lnx_w, lnx_b]
    pvec = jnp.pad(jnp.stack(rows, axis=0), ((0, _PV_ROWS - len(rows)), (0, 0)))
    mul = jnp.pad(shift_mu[3 * rw:], (0, lslot - lora_w))[None, :]
    return wd, wa, wg, pvec, mul


def _run_layer(x2d, nb, seq, ada, states, pos0, lay, wts, big, alpha):
    d, rw, lslot, lora_w = lay["d"], lay["rw"], lay["lslot"], lay["lora_w"]
    (wd, wa, wg, pvec, mul, gnw, gnb, wba, wbb, wout, ln1w, ln1b, cwt, ln2w, ln2b) = wts
    w_in_t, w_up, w_down = big
    m = nb * seq
    sh1, sc1, g1, sh2, sc2, g2 = jnp.split(ada, 6, axis=-1)
    if seq == 1:
        tm, tpg = m, 1
        shape_mod = lambda t: t[None]
    else:
        tm = _pick_tile(seq, (512, 256, 128, 64, 32, 16, 8))
        tpg = seq // tm
        shape_mod = lambda t: t[:, None, :]
    sh1, sc1, g1, sh2, sc2, g2 = map(shape_mod, (sh1, sc1, g1, sh2, sc2, g2))

    s_wkv, s_shift, s_ret, s_conv = states
    shift_main = s_shift[:, :3 * rw]
    shift_lora = jnp.pad(s_shift[:, 3 * rw:], ((0, 0), (0, lslot - lora_w)))
    if seq == 1:
        proj, w_in_bf = _modmm(x2d, sc1, sh1, w_in_t, lay, m, 1)
        ya, wkv_new = _rwkv_step(proj, nb, lay, (shift_main, shift_lora), pvec, mul, wd, wa, wg, s_wkv)
        yb, ret_new = _ret_step(proj, nb, lay, gnw, gnb, s_ret, pos0)
    else:
        tm_in = _pick_tile(seq, (1024, 512, 256, 128, 64, 32, 16, 8))
        proj = _modmm(x2d, sc1, sh1, w_in_t, lay, tm_in, seq // tm_in)
        first = (shift_main[:, None, :], shift_main[:, None, :], shift_main[:, None, :],
                 shift_lora[:, None, :])
        ya, wkv_new = _rwkv_seq(proj, nb, seq, lay, first, pvec, mul, wd, wa, wg)
        yb, ret_new = _ret_seq(proj, nb, seq, lay, gnw, gnb)
    last = proj.reshape(nb, seq, lay["nt"])[:, -1, :]
    shift_new = jnp.concatenate([last[:, :3 * rw], last[:, lay["off_l"]:lay["off_l"] + lora_w]],
                                axis=-1).astype(F32)

    tm_merge = min(tm, 256)
    x1, u2 = _merge(ya, yb, proj, x2d, g1, sc2, sh2, wba, wbb, wout, ln1w, ln1b, lay, alpha,
                    tm_merge, (seq // tm_merge) if seq > 1 else 1)
    if seq == 1:
        act, conv_new, w_up_halves = _ffn_up(u2, w_up, cwt, nb, seq, s_conv, tm)
        x2, w_down_bf = _ffn_down(act, w_down, x1, g2, ln2w, ln2b, alpha, tm, tpg)
        rounded = (w_in_bf, w_up_halves, w_down_bf)
    else:
        x2, conv_new = _ffn_seq(u2, w_up, cwt, w_down, x1, g2, ln2w, ln2b, alpha, nb, seq, s_conv, tm)
        rounded = None
    return (x2, wkv_new, shift_new, ret_new, conv_new), rounded


def kernel(x_prompt, x_sample, c_prompt, c_sample, state_wkv, state_shift, state_ret, state_conv, w_ada, b_ada, w_in, shift_mu, w0, w_decay_up, a0, w_aaa_up, w_gate_up, k_k, k_a, r_k, lnx_w, lnx_b, ret_gn_w, ret_gn_b, w_branch_a, w_branch_b, w_out, ln1_w, ln1_b, w_up, conv_w, conv_b, w_down, ln2_w, ln2_b):
    depth = w_ada.shape[0]
    nbp, seq_p, d = x_prompt.shape
    nbs, seq_s, _ = x_sample.shape
    assert seq_s == 1
    rw = k_k.shape[-1]
    ret_heads, dk, dv = state_ret.shape[2:]
    lora_w = w_decay_up.shape[1] + w_aaa_up.shape[1] + w_gate_up.shape[1]
    lay = _layout(d, rw, ret_heads * dk, ret_heads * dv, lora_w, ret_heads)
    lay["lora_dims"] = (w_decay_up.shape[1], w_aaa_up.shape[1], w_gate_up.shape[1])
    heads, hn = r_k.shape[1:]
    assert hn == LANES // 2 and heads * hn == rw
    alpha = (2.0 * depth) ** 0.25
    f2 = w_up.shape[-1]

    xp = x_prompt.reshape(nbp * seq_p, d)
    xs = x_sample.reshape(nbs * seq_s, d)
    c_all = jnp.concatenate([c_prompt, c_sample], axis=0)
    pad = (-c_all.shape[0]) % SUBLANES
    c_all = _bf(jnp.pad(c_all, ((0, pad), (0, 0))))

    outs_p, outs_s = [], []
    for l in range(depth):
        wd, wa, wg, pvec, mul = _prep_weights(
            lay, shift_mu[l], w0[l], w_decay_up[l], a0[l], w_aaa_up[l], w_gate_up[l],
            k_k[l], k_a[l], r_k[l], lnx_w[l], lnx_b[l])
        cwt = jnp.concatenate([conv_w[l], conv_b[l][None, :]], axis=0)
        wts = (wd, wa, wg, pvec, mul, ret_gn_w[l][None, :], ret_gn_b[l][None, :],
               _bf(w_branch_a[l]), _bf(w_branch_b[l]), _bf(w_out[l]), ln1_w[l][None, :], ln1_b[l][None, :],
               cwt, ln2_w[l][None, :], ln2_b[l][None, :])
        ada = _mm_bias(c_all, w_ada[l], b_ada[l][None, :])
        (xs, *st_s), rounded = _run_layer(
            xs, nbs, seq_s, ada[nbp:nbp + nbs],
            (state_wkv[l], state_shift[l], state_ret[l], state_conv[l]), float(PAST_LEN), lay, wts,
            (jnp.transpose(w_in[l]), w_up[l], w_down[l]), alpha)
        zero_states = (None, jnp.zeros((nbp, state_shift.shape[-1]), F32), None,
                       jnp.zeros((nbp, state_conv.shape[2], f2), F32))
        (xp, *st_p), _ = _run_layer(xp, nbp, seq_p, ada[:nbp], zero_states, 0.0, lay, wts, rounded, alpha)
        outs_p.append(st_p)
        outs_s.append(st_s)

    def stack(lst, j, ref):
        layers = [s[j].astype(ref.dtype) for s in lst]
        return layers[0][None] if depth == 1 else jnp.stack(layers, axis=0)

    refs = (state_wkv, state_shift, state_ret, state_conv)
    return (xp.reshape(x_prompt.shape), xs.reshape(x_sample.shape),
            *[stack(outs_p, j, refs[j]) for j in range(4)],
            *[stack(outs_s, j, refs[j]) for j in range(4)])
```

```python
import functools
import math

import jax
import jax.numpy as jnp
from jax import lax
from jax.experimental import pallas as pl
from jax.experimental.pallas import tpu as pltpu

F32 = jnp.float32
BF16 = jnp.bfloat16

PAST_LEN = 16384
ROPE_BASE = 10000.0
RWKV_GN_EPS = 64e-5
RET_GN_EPS = 1e-5
LN_EPS = 1e-5
RET_CHUNK = 128

LANES = 128
SUBLANES = 8
VMEM_LIMIT_BYTES = 50 * 1024 * 1024

RWKV_CHUNK = 64
RWKV_PAIRS_PER_STEP = 8
RWKV_GROUPS_PER_STEP = 1


def _params(*sem):
    return pltpu.CompilerParams(dimension_semantics=sem, vmem_limit_bytes=VMEM_LIMIT_BYTES)


def _dot(a, b):
    return jnp.dot(a, b, preferred_element_type=F32)


def _dot_nt(a, b):
    return lax.dot_general(a, b, (((1,), (1,)), ((), ())), preferred_element_type=F32)


def _dot_tn(a, b):
    return lax.dot_general(a, b, (((0,), (0,)), ((), ())), preferred_element_type=F32)


def _bf(x):
    return x.astype(BF16)


def _split2(x):
    hi = x.astype(BF16)
    lo = (x - hi.astype(F32)).astype(BF16)
    return hi, lo


def _dot3_nn(a, b):
    ah, al = _split2(a)
    bh, bl = _split2(b)
    return _dot(jnp.concatenate([ah, ah, al], axis=1), jnp.concatenate([bh, bl, bh], axis=0))


def _dot_exact_lhs(a_bf, b):
    b1 = b.astype(BF16)
    r1 = b - b1.astype(F32)
    b2 = r1.astype(BF16)
    b3 = (r1 - b2.astype(F32)).astype(BF16)
    return _dot(jnp.concatenate([a_bf, a_bf, a_bf], axis=1), jnp.concatenate([b1, b2, b3], axis=0))


def _layer_norm_rows(t, w, b):
    mu = jnp.mean(t, axis=-1, keepdims=True)
    d = t - mu
    var = jnp.mean(d * d, axis=-1, keepdims=True)
    return d * lax.rsqrt(var + LN_EPS) * w + b


def _pick_tile(n, candidates):
    for c in candidates:
        if n % c == 0:
            return c
    return n


def _mm_bias_kernel(x_ref, w_ref, b_ref, o_ref):
    o_ref[...] = _dot(x_ref[...], _bf(w_ref[...])) + b_ref[...]


def _mm_bias(x_bf, w, b_row):
    m, k = x_bf.shape
    n = w.shape[1]
    tn = _pick_tile(n, (1024, 512, 256, 128))
    return pl.pallas_call(
        _mm_bias_kernel,
        grid=(n // tn,),
        in_specs=[pl.BlockSpec((m, k), lambda j: (0, 0)),
                  pl.BlockSpec((k, tn), lambda j: (0, j)),
                  pl.BlockSpec((1, tn), lambda j: (0, j))],
        out_specs=pl.BlockSpec((m, tn), lambda j: (0, j)),
        out_shape=jax.ShapeDtypeStruct((m, n), F32),
        compiler_params=_params("arbitrary"),
    )(x_bf, w, b_row)


def _modmm_kernel(x_ref, sc_ref, sh_ref, wt_ref, o_ref, u_ref):
    @pl.when(pl.program_id(1) == 0)
    def _():
        u_ref[...] = _bf(x_ref[...] * (1.0 + sc_ref[0]) + sh_ref[0])

    o_ref[...] = _dot_nt(u_ref[...], wt_ref[...]).astype(o_ref.dtype)


def _modmm_cast_kernel(x_ref, sc_ref, sh_ref, wt_ref, o_ref, wbf_ref, u_ref):
    @pl.when(pl.program_id(1) == 0)
    def _():
        u_ref[...] = _bf(x_ref[...] * (1.0 + sc_ref[0]) + sh_ref[0])

    w = _bf(wt_ref[...])
    wbf_ref[...] = w
    o_ref[...] = _dot_nt(u_ref[...], w).astype(o_ref.dtype)


def _modmm(x, sc, sh, wt, lay, tm, tpg):
    m, d = x.shape
    n = lay["nt"]
    tn = _pick_tile(n, (512, 256, 128))
    r = sc.shape[1]
    mod_spec = pl.BlockSpec((1, r, d), lambda i, j: (i // tpg, 0, 0))
    x_spec = pl.BlockSpec((tm, d), lambda i, j: (i, 0))
    o_spec = pl.BlockSpec((tm, tn), lambda i, j: (i, j))
    w_blocked = pl.BlockSpec((tn, d), lambda i, j: (j, 0))
    if wt.dtype == BF16:
        return pl.pallas_call(
            _modmm_kernel,
            grid=(m // tm, n // tn),
            in_specs=[x_spec, mod_spec, mod_spec, w_blocked],
            out_specs=o_spec,
            out_shape=jax.ShapeDtypeStruct((m, n), BF16),
            scratch_shapes=[pltpu.VMEM((tm, d), BF16)],
            compiler_params=_params("arbitrary", "arbitrary"),
        )(x, sc, sh, wt)

    assert m == tm
    rw3, lora_w, lslot = 3 * lay["rw"], lay["lora_w"], lay["lslot"]
    n_main, n_rest = rw3 // tn, (lay["off_l"] - rw3) // tn
    assert rw3 % tn == 0 and (lay["off_l"] - rw3) % tn == 0 and lslot == tn
    assert rw3 + lslot <= wt.shape[0]
    assert tn % SUBLANES == 0 and rw3 % SUBLANES == 0 and lora_w % SUBLANES == 0

    def w_row(j):
        row = jnp.where(j < n_main, j * tn,
                        jnp.where(j < n_main + n_rest, rw3 + lora_w + (j - n_main) * tn, rw3))
        return pl.multiple_of(row, SUBLANES)

    return pl.pallas_call(
        _modmm_cast_kernel,
        grid=(1, n // tn),
        in_specs=[x_spec, mod_spec, mod_spec,
                  pl.BlockSpec((pl.Element(tn), pl.Element(d)), lambda i, j: (w_row(j), 0))],
        out_specs=[o_spec, w_blocked],
        out_shape=[jax.ShapeDtypeStruct((m, n), BF16), jax.ShapeDtypeStruct((n, d), BF16)],
        scratch_shapes=[pltpu.VMEM((tm, d), BF16)],
        compiler_params=_params("arbitrary", "arbitrary"),
    )(x, sc, sh, wt)


_PV_MU_R, _PV_MU_K, _PV_MU_V, _PV_W0, _PV_A0, _PV_KK, _PV_KA, _PV_RK, _PV_LNW, _PV_LNB = range(10)
_PV_ROWS = 16


def _head_half_mask(shape):
    return lax.broadcasted_iota(jnp.int32, shape, 1) < (LANES // 2)


def _head_sums(x):
    h0 = _head_half_mask((x.shape[0], LANES))
    parts = []
    for p in range(x.shape[1] // LANES):
        xs = x[:, p * LANES:(p + 1) * LANES]
        s0 = jnp.sum(jnp.where(h0, xs, 0.0), axis=-1, keepdims=True)
        s1 = jnp.sum(jnp.where(h0, 0.0, xs), axis=-1, keepdims=True)
        parts.append(jnp.where(h0, s0, s1))
    return parts[0] if len(parts) == 1 else jnp.concatenate(parts, axis=1)


def _lora_split(lay):
    dl, al, gl = lay["lora_dims"]
    up = lambda n: -(-n // LANES) * LANES
    a1, g0, g1 = up(dl + al), (dl + al) // LANES * LANES, up(dl + al + gl)
    assert g1 <= lay["lslot"]
    return a1, g0, g1


def _rwkv_lora_inputs(zl, plr, mul, split):
    a1, g0, g1 = split
    ls = zl + mul * (plr - zl)
    return _bf(jnp.tanh(ls[:, :a1])), _bf(ls[:, :a1]), _bf(jax.nn.sigmoid(ls[:, g0:g1]))


def _rwkv_lora(lora_in, wd, wa, wg):
    return _dot(lora_in[0], wd), _dot(lora_in[1], wa), _dot(lora_in[2], wg)


def _rwkv_tokens(zr, zk, zv, pr, pk, pv_, lora, pvec):
    row = lambda i: pvec[i:i + 1, :]
    r = zr + row(_PV_MU_R) * (pr - zr)
    k = zk + row(_PV_MU_K) * (pk - zk)
    v = zv + row(_PV_MU_V) * (pv_ - zv)
    wl, al, g = lora
    logd = -math.exp(-0.5) * jax.nn.sigmoid(row(_PV_W0) + wl)
    a = jax.nn.sigmoid(row(_PV_A0) + al)
    kkr = k * row(_PV_KK)
    kk = kkr * jnp.minimum(lax.rsqrt(_head_sums(kkr * kkr)), 1e12)
    kp = k * (1.0 + (a - 1.0) * row(_PV_KA))
    bonus = _head_sums(r * kp * row(_PV_RK)) * v
    return r, kp, v, kk, a, logd, g, bonus


def _rwkv_finish(o, bonus, g, pvec):
    inv_n = 1.0 / (LANES // 2)
    mu = _head_sums(o) * inv_n
    d = o - mu
    var = _head_sums(d * d) * inv_n
    on = d * lax.rsqrt(var + RWKV_GN_EPS)
    return (on * pvec[_PV_LNW:_PV_LNW + 1, :] + pvec[_PV_LNB:_PV_LNB + 1, :] + bonus) * g


def _shift_rows(z, first_row):
    rolled = pltpu.roll(z, 1, 0)
    rowid = lax.broadcasted_iota(jnp.int32, z.shape, 0)
    return jnp.where(rowid == 0, first_row, rolled)


def _rwkv_seq_kernel(split, zr_ref, zk_ref, zv_ref, zl_ref, fr_ref, fk_ref, fv_ref, fl_ref,
                     pvec_ref, mul_ref, wd_ref, wa_ref, wg_ref, tri_ref,
                     ya_ref, wkv_ref, s_ref, cr_ref, ck_ref, cv_ref, cl_ref):
    i = pl.program_id(2)
    tc, full_width = zr_ref.shape
    c = RWKV_CHUNK
    nch = tc // c
    half = LANES // 2
    ngroups = RWKV_GROUPS_PER_STEP if (full_width // LANES) % RWKV_GROUPS_PER_STEP == 0 else 1
    npp = full_width // LANES // ngroups

    @pl.when(i == 0)
    def _():
        s_ref[...] = jnp.zeros_like(s_ref)
        cr_ref[0:1, :] = fr_ref[0]
        ck_ref[0:1, :] = fk_ref[0]
        cv_ref[0:1, :] = fv_ref[0]
        cl_ref[0:1, :] = fl_ref[0]

    zl = zl_ref[...].astype(F32)
    lora_in = _rwkv_lora_inputs(zl, _shift_rows(zl, cl_ref[0:1, :]), mul_ref[...], split)
    cl_ref[0:1, :] = zl[tc - 1:tc, :]
    tri = tri_ref[...]
    for gi in range(ngroups):
        _rwkv_seq_group(gi, npp, tc, c, nch, tri, lora_in, zr_ref, zk_ref, zv_ref, pvec_ref,
                        wd_ref, wa_ref, wg_ref, ya_ref, s_ref, cr_ref, ck_ref, cv_ref)

    @pl.when(i == pl.num_programs(2) - 1)
    def _():
        for p in range(s_ref.shape[0]):
            s = jnp.transpose(s_ref[p])
            wkv_ref[0, 2 * p] = s[:half, :half]
            wkv_ref[0, 2 * p + 1] = s[half:, half:]


def _rwkv_seq_group(gi, npp, tc, c, nch, tri, lora_in, zr_ref, zk_ref, zv_ref, pvec_ref,
                    wd_ref, wa_ref, wg_ref, ya_ref, s_ref, cr_ref, ck_ref, cv_ref):
    width = npp * LANES
    gl = slice(gi * width, (gi + 1) * width)
    lora_all = _rwkv_lora(lora_in, wd_ref[:, gl], wa_ref[:, gl], wg_ref[:, gl])

    tok = {}
    for p in range(npp):
        pl_ = slice(gi * width + p * LANES, gi * width + (p + 1) * LANES)
        ls = slice(p * LANES, (p + 1) * LANES)
        zr, zk, zv = (ref[:, pl_].astype(F32) for ref in (zr_ref, zk_ref, zv_ref))
        pvec_p = pvec_ref[:, pl_]
        r, kp, v, kk, a, logd, g, bonus = _rwkv_tokens(
            zr, zk, zv, _shift_rows(zr, cr_ref[0:1, pl_]), _shift_rows(zk, ck_ref[0:1, pl_]),
            _shift_rows(zv, cv_ref[0:1, pl_]), tuple(x[:, ls] for x in lora_all), pvec_p)
        cr_ref[0:1, pl_] = zr[tc - 1:tc, :]
        ck_ref[0:1, pl_] = zk[tc - 1:tc, :]
        cv_ref[0:1, pl_] = zv[tc - 1:tc, :]
        cum_incl = _dot_exact_lhs(tri, logd)
        tot_rows = [cum_incl[(ci + 1) * c - 1:(ci + 1) * c, :] for ci in range(nch)]
        w_tot = jnp.concatenate([jnp.broadcast_to(jnp.exp(t), (c, LANES)) for t in tot_rows], axis=0)
        e_in = jnp.exp(cum_incl)
        e_neg = 1.0 / e_in
        beta = kk * a
        b_t = beta * e_neg
        k_t = kp * e_neg
        tok[p] = dict(r_t=r * e_in, a_t=-kk * jnp.exp(cum_incl - logd), b_t=b_t, k_t=k_t,
                      b_h=b_t * w_tot, k_h=k_t * w_tot, v=v, g=g, bonus=bonus, pvec=pvec_p, tot_rows=tot_rows)

    h0 = _head_half_mask((c, LANES))

    def stack(name, p, ci):
        xb = tok[p][name][ci * c:(ci + 1) * c, :]
        return jnp.concatenate([jnp.where(h0, xb, 0.0), jnp.where(h0, 0.0, xb)], axis=0)

    keep0 = h0[0:1, :].astype(BF16)
    keep1 = 1.0 - keep0

    def stack_bf(name, p, ci):
        xb = _bf(tok[p][name][ci * c:(ci + 1) * c, :])
        return jnp.concatenate([xb * keep0, xb * keep1], axis=0)

    rr = lax.broadcasted_iota(jnp.int32, (2 * c, 2 * c), 0)
    cc = lax.broadcasted_iota(jnp.int32, (2 * c, 2 * c), 1)
    same = (rr >= c) == (cc >= c)
    strict = same & (cc < rr)
    incl = same & (cc <= rr)
    eye = rr == cc
    zeros_blk = jnp.zeros((2 * c, LANES), BF16)

    probs = [(p, ci) for p in range(npp) for ci in range(nch)]
    a_s = {q: stack("a_t", *q) for q in probs}
    r_s = {q: stack("r_t", *q) for q in probs}
    v_bf = {q: stack_bf("v", *q) for q in probs}

    def twice(name, p, ci):
        xb = _bf(tok[p][name][ci * c:(ci + 1) * c, :])
        return jnp.concatenate([xb, xb], axis=0)

    pmat, a_ak, a_r = {}, {}, {}
    for q in probs:
        amat = _dot_nt(_bf(jnp.concatenate([a_s[q], r_s[q]], axis=0)),
                       jnp.concatenate([twice("b_t", *q), twice("k_t", *q)], axis=0))
        pmat[q] = jnp.where(strict, amat[:2 * c, :2 * c], 0.0)
        a_ak[q] = _bf(jnp.where(strict, amat[:2 * c, 2 * c:], 0.0))
        a_r[q] = _bf(jnp.concatenate([jnp.where(incl, amat[2 * c:, :2 * c], 0.0),
                                      jnp.where(incl, amat[2 * c:, 2 * c:], 0.0)], axis=1))

    x = {q: jnp.concatenate([a_s[q], _dot(a_ak[q], v_bf[q])], axis=1) for q in probs}

    nsteps = int(math.log2(c))
    for it in range(nsteps):
        for q in probs:
            p_bf = _bf(pmat[q])
            x[q] = x[q] + _dot(p_bf, _bf(x[q]))
            if it + 1 < nsteps:
                pmat[q] = _dot(p_bf, p_bf)

    r_pair, o_pair, g_t, h_t = {}, {}, {}, {}
    for q in probs:
        p, ci = q
        x_bf = _bf(x[q])
        rhs = jnp.concatenate([x_bf, jnp.concatenate([zeros_blk, v_bf[q]], axis=1)], axis=0)
        y = _dot(a_r[q], rhs)
        r_hat = r_s[q] + y[:, :LANES]
        r_pair[q] = r_hat[:c] + r_hat[c:]
        o_pair[q] = y[:c, LANES:] + y[c:, LANES:]
        z = _dot_tn(stack_bf("b_h", *q), x_bf)
        w_c = jnp.exp(tok[p]["tot_rows"][ci])
        g_t[q] = jnp.where(eye, w_c, 0.0) + z[:, :LANES]
        h_t[q] = z[:, LANES:] + _dot_tn(stack_bf("k_h", *q), v_bf[q])

    outs = {}
    for ci in range(nch):
        for p in range(npp):
            q = (p, ci)
            t0 = s_ref[gi * npp + p]
            outs[q] = _dot3_nn(r_pair[q], t0) + o_pair[q]
            s_ref[gi * npp + p] = _dot3_nn(g_t[q], t0) + h_t[q]

    for p in range(npp):
        o = jnp.concatenate([outs[(p, ci)] for ci in range(nch)], axis=0) if nch > 1 else outs[(p, 0)]
        pl_ = slice(gi * width + p * LANES, gi * width + (p + 1) * LANES)
        ya_ref[:, pl_] = _bf(_rwkv_finish(o, tok[p]["bonus"], tok[p]["g"], tok[p]["pvec"]))


def _rwkv_seq(proj, nb, seq, lay, first, pvec, mul, wd, wa, wg):
    rw, lslot = lay["rw"], lay["lslot"]
    npair = rw // LANES
    npp = RWKV_PAIRS_PER_STEP if npair % RWKV_PAIRS_PER_STEP == 0 else 1
    ngrp = npair // npp
    width = npp * LANES
    c = RWKV_CHUNK
    tc = _pick_tile(seq, (2 * c, c))
    nt = seq // tc
    t_idx = jnp.arange(tc)
    tri = ((t_idx[:, None] // c == t_idx[None, :] // c) & (t_idx[None, :] <= t_idx[:, None])).astype(BF16)
    fr, fk, fv, fl = first
    col = lambda s: pl.BlockSpec((tc, width), lambda b, gp, i, s=s: (b * nt + i, s * ngrp + gp))
    fcol = lambda s: pl.BlockSpec((1, 1, width), lambda b, gp, i, s=s: (b, 0, s * ngrp + gp))
    split = _lora_split(lay)
    wspec = pl.BlockSpec((split[0], width), lambda b, gp, i: (0, gp))
    wgspec = pl.BlockSpec((split[2] - split[1], width), lambda b, gp, i: (0, gp))
    return pl.pallas_call(
        functools.partial(_rwkv_seq_kernel, split),
        grid=(nb, ngrp, nt),
        in_specs=[col(0), col(1), col(2),
                  pl.BlockSpec((tc, lslot), lambda b, gp, i: (b * nt + i, lay["off_l"] // lslot)),
                  fcol(0), fcol(1), fcol(2),
                  pl.BlockSpec((1, 1, lslot), lambda b, gp, i: (b, 0, 0)),
                  pl.BlockSpec((_PV_ROWS, width), lambda b, gp, i: (0, gp)),
                  pl.BlockSpec((1, lslot), lambda b, gp, i: (0, 0)),
                  wspec, wspec, wgspec,
                  pl.BlockSpec((tc, tc), lambda b, gp, i: (0, 0))],
        out_specs=[pl.BlockSpec((tc, width), lambda b, gp, i: (b * nt + i, gp)),
                   pl.BlockSpec((1, 2 * npp, LANES // 2, LANES // 2), lambda b, gp, i: (b, gp, 0, 0))],
        out_shape=[jax.ShapeDtypeStruct((nb * seq, rw), BF16),
                   jax.ShapeDtypeStruct((nb, 2 * npair, LANES // 2, LANES // 2), F32)],
        scratch_shapes=[pltpu.VMEM((npp, LANES, LANES), F32),
                        pltpu.VMEM((SUBLANES, width), F32), pltpu.VMEM((SUBLANES, width), F32),
                        pltpu.VMEM((SUBLANES, width), F32), pltpu.VMEM((SUBLANES, lslot), F32)],
        compiler_params=_params("arbitrary", "arbitrary", "arbitrary"),
    )(proj, proj, proj, proj, fr, fk, fv, fl, pvec, mul, wd, wa, wg, tri)


def _rwkv_step_kernel(split, zr_ref, zk_ref, zv_ref, zl_ref, pr_ref, pk_ref, pv_ref, plr_ref,
                      pvec_ref, mul_ref, wd_ref, wa_ref, wg_ref, s_ref,
                      ya_ref, snew_ref, o_s):
    half = LANES // 2
    pvec = pvec_ref[...]
    r, kp, v, kk, a, logd, g, bonus = _rwkv_tokens(
        zr_ref[...].astype(F32), zk_ref[...].astype(F32), zv_ref[...].astype(F32),
        pr_ref[...], pk_ref[...], pv_ref[...],
        _rwkv_lora(_rwkv_lora_inputs(zl_ref[...].astype(F32), plr_ref[...], mul_ref[...], split),
                   wd_ref[...], wa_ref[...], wg_ref[...]),
        pvec)
    w = jnp.exp(logd)
    nkk_t, wr_t, w_t, beta_t, kp_t, r_t, v_t = (jnp.transpose(x) for x in (-kk, w * r, w, kk * a, kp, r, v))
    for e in range(2):
        ks = slice(e * half, (e + 1) * half)
        nkk_e, wr_e, w_e, beta_e, kp_e = nkk_t[ks], wr_t[ks], w_t[ks], beta_t[ks], kp_t[ks]
        c_beta = jnp.sum(beta_e * r_t[ks], axis=0, keepdims=True)
        c_k = jnp.sum(kp_e * r_t[ks], axis=0, keepdims=True)
        for vi in range(half):
            row = e * half + vi
            s = s_ref[e, vi]
            sa = jnp.sum(s * nkk_e, axis=0, keepdims=True)
            sw = jnp.sum(s * wr_e, axis=0, keepdims=True)
            v_row = v_t[row:row + 1, :]
            snew_ref[e, vi] = s * w_e + sa * beta_e + v_row * kp_e
            o_s[row:row + 1, :] = sw + sa * c_beta + v_row * c_k
    ya_ref[...] = _bf(_rwkv_finish(jnp.transpose(o_s[...]), bonus, g, pvec))


def _rwkv_step(proj, nb, lay, prev, pvec, mul, wd, wa, wg, s_wkv):
    rw, lslot = lay["rw"], lay["lslot"]
    npair = rw // LANES
    pm, plr = prev
    half = LANES // 2
    s_t = jnp.transpose(s_wkv, (1, 2, 3, 0))
    col = lambda off: pl.BlockSpec((nb, LANES), lambda p, off=off: (0, off + p))
    split = _lora_split(lay)
    wspec = pl.BlockSpec((split[0], LANES), lambda p: (0, p))
    wgspec = pl.BlockSpec((split[2] - split[1], LANES), lambda p: (0, p))
    sspec = pl.BlockSpec((2, half, half, nb), lambda p: (p, 0, 0, 0))
    ya, snew_t = pl.pallas_call(
        functools.partial(_rwkv_step_kernel, split),
        grid=(npair,),
        in_specs=[col(0), col(npair), col(2 * npair),
                  pl.BlockSpec((nb, lslot), lambda p: (0, lay["off_l"] // lslot)),
                  col(0), col(npair), col(2 * npair),
                  pl.BlockSpec((nb, lslot), lambda p: (0, 0)),
                  pl.BlockSpec((_PV_ROWS, LANES), lambda p: (0, p)),
                  pl.BlockSpec((1, lslot), lambda p: (0, 0)),
                  wspec, wspec, wgspec, sspec],
        out_specs=[pl.BlockSpec((nb, LANES), lambda p: (0, p)), sspec],
        out_shape=[jax.ShapeDtypeStruct((nb, rw), BF16), jax.ShapeDtypeStruct(s_t.shape, F32)],
        scratch_shapes=[pltpu.VMEM((LANES, nb), F32)],
        compiler_params=_params("arbitrary"),
    )(proj, proj, proj, proj, pm, pm, pm, plr, pvec, mul, wd, wa, wg, s_t)
    return ya, jnp.transpose(snew_t, (3, 0, 1, 2))


def _rope_rows(t, cos2, sin2):
    return t * cos2 + pltpu.roll(t, LANES // 2, 1) * sin2


def _head_norm_rows(o, eps):
    mu = jnp.mean(o, axis=-1, keepdims=True)
    d = o - mu
    var = jnp.mean(d * d, axis=-1, keepdims=True)
    return d * lax.rsqrt(var + eps)


def _ret_seq_kernel(q_ref, k_ref, v_ref, g_ref, cos_ref, sin_ref, intra_ref, qd_ref, kd_ref,
                    blk_ref, gnw_ref, gnb_ref, yb_ref, ret_ref, s_ref):
    i = pl.program_id(1)
    nh = s_ref.shape[0]
    dk = s_ref.shape[1]

    @pl.when(i == 0)
    def _():
        s_ref[...] = jnp.zeros_like(s_ref)

    cos2, sin2 = cos_ref[...], sin_ref[...]
    heads = range(nh)
    hs = [slice(h * LANES, (h + 1) * LANES) for h in heads]
    kh = [_rope_rows(k_ref[:, hs[h]].astype(F32), cos2, sin2) * (dk ** -0.5) for h in heads]
    qb = [_bf(_rope_rows(q_ref[:, hs[h]].astype(F32), cos2, sin2)) for h in heads]
    vb = [_bf(v_ref[:, hs[h]]) for h in heads]
    scores = [_bf(_dot_nt(qb[h], _bf(kh[h])) * intra_ref[h]) for h in heads]
    s0 = [s_ref[h] for h in heads]
    o = [_dot(scores[h], vb[h]) + _dot(qb[h], _bf(s0[h])) * qd_ref[h] for h in heads]
    for h in heads:
        s_ref[h] = s0[h] * blk_ref[h] + _dot_tn(_bf(kh[h] * kd_ref[h]), vb[h])
    for h in heads:
        on = _head_norm_rows(o[h], RET_GN_EPS)
        yb_ref[:, hs[h]] = _bf((on * gnw_ref[:, hs[h]] + gnb_ref[:, hs[h]])
                               * jax.nn.silu(g_ref[:, hs[h]].astype(F32)))

    @pl.when(i == pl.num_programs(1) - 1)
    def _():
        ret_ref[0] = s_ref[...]


def _ret_tables(nh, c):
    log_g = jnp.log1p(-jnp.exp2(-5.0 - jnp.arange(nh, dtype=F32)))
    i = jnp.arange(c, dtype=F32)
    rel = i[:, None] - i[None, :]
    intra = jnp.where(rel >= 0, jnp.exp(log_g[:, None, None] * jnp.maximum(rel, 0.0)), 0.0)
    q_decay = jnp.exp(log_g[:, None] * (i + 1.0))
    k_decay = jnp.exp(log_g[:, None] * (c - 1.0 - i))
    blk_decay = jnp.exp(log_g * c)
    return intra, q_decay, k_decay, blk_decay


def _rope_tables(pos, dk):
    half = dk // 2
    inv = ROPE_BASE ** (-jnp.arange(half, dtype=F32) / half)
    ang = pos[:, None] * inv[None, :]
    cos, sin = jnp.cos(ang), jnp.sin(ang)
    return jnp.concatenate([cos, cos], axis=-1), jnp.concatenate([-sin, sin], axis=-1)


def _ret_seq(proj, nb, seq, lay, gnw, gnb):
    qk, rv, nh = lay["qk"], lay["rv"], lay["ret_heads"]
    dk, dv = qk // nh, rv // nh
    assert dk == LANES and dv == LANES
    c = RET_CHUNK if seq % RET_CHUNK == 0 else seq
    assert c % SUBLANES == 0
    nt = seq // c
    intra, qd, kd, blk = _ret_tables(nh, c)
    qd = jnp.broadcast_to(qd[:, :, None], (nh, c, dv))
    kd = jnp.broadcast_to(kd[:, :, None], (nh, c, dk))
    blk = jnp.broadcast_to(blk[:, None, None], (nh, 1, dv))
    cos2, sin2 = _rope_tables(jnp.arange(seq, dtype=F32), dk)
    seg = lambda off, w: pl.BlockSpec((c, w), lambda b, i, off=off, w=w: (b * nt + i, off // w))
    full3 = lambda a: pl.BlockSpec(a.shape, lambda b, i: (0, 0, 0))
    return pl.pallas_call(
        _ret_seq_kernel,
        grid=(nb, nt),
        in_specs=[seg(lay["off_q"], qk), seg(lay["off_kr"], qk), seg(lay["off_vr"], rv),
                  seg(lay["off_rg"], rv),
                  pl.BlockSpec((c, dk), lambda b, i: (i, 0)), pl.BlockSpec((c, dk), lambda b, i: (i, 0)),
                  full3(intra), full3(qd), full3(kd), full3(blk),
                  pl.BlockSpec((1, rv), lambda b, i: (0, 0)), pl.BlockSpec((1, rv), lambda b, i: (0, 0))],
        out_specs=[pl.BlockSpec((c, rv), lambda b, i: (b * nt + i, 0)),
                   pl.BlockSpec((1, nh, dk, dv), lambda b, i: (b, 0, 0, 0))],
        out_shape=[jax.ShapeDtypeStruct((nb * seq, rv), BF16),
                   jax.ShapeDtypeStruct((nb, nh, dk, dv), F32)],
        scratch_shapes=[pltpu.VMEM((nh, dk, dv), F32)],
        compiler_params=_params("arbitrary", "arbitrary"),
    )(proj, proj, proj, proj, cos2, sin2, intra, qd, kd, blk, gnw, gnb)


_RC_INTRA, _RC_QD, _RC_KD, _RC_BLK = range(4)


def _ret_step_kernel(q_ref, k_ref, v_ref, g_ref, cos_ref, sin_ref, rc_ref, gnw_ref, gnb_ref, s_ref,
                     yb_ref, snew_ref, q_s, k_s, v_s, o_s):
    bb = q_ref.shape[0]
    nh = s_ref.shape[1]
    dk = s_ref.shape[2]
    cos2, sin2 = cos_ref[...], sin_ref[...]
    for h in range(nh):
        hs = slice(h * LANES, (h + 1) * LANES)
        q_s[:, hs] = _rope_rows(q_ref[:, hs].astype(F32), cos2, sin2)
        k_s[:, hs] = _rope_rows(k_ref[:, hs].astype(F32), cos2, sin2) * (dk ** -0.5)
        v_s[:, hs] = v_ref[:, hs].astype(F32)
    eye = (lax.broadcasted_iota(jnp.int32, (LANES, LANES), 0)
           == lax.broadcasted_iota(jnp.int32, (LANES, LANES), 1))
    for b in range(bb):
        for h in range(nh):
            hs = slice(h * LANES, (h + 1) * LANES)
            rc = lambda j: rc_ref[h, j:j + 1, :]
            q_row = q_s[b:b + 1, hs]
            k_row = k_s[b:b + 1, hs]
            v_row = v_s[b:b + 1, hs]
            s0 = s_ref[b, h]
            q_col = jnp.sum(jnp.where(eye, q_row, 0.0), axis=-1, keepdims=True)
            k_col = jnp.sum(jnp.where(eye, k_row, 0.0), axis=-1, keepdims=True)
            score = jnp.sum(q_row * k_row, axis=-1, keepdims=True) * rc(_RC_INTRA)
            o_row = score * v_row + jnp.sum(s0 * q_col, axis=0, keepdims=True) * rc(_RC_QD)
            snew_ref[b, h] = s0 * rc(_RC_BLK) + (k_col * rc(_RC_KD)) * v_row
            o_s[b:b + 1, hs] = o_row
    for h in range(nh):
        hs = slice(h * LANES, (h + 1) * LANES)
        on = _head_norm_rows(o_s[:, hs], RET_GN_EPS)
        yb_ref[:, hs] = _bf((on * gnw_ref[:, hs] + gnb_ref[:, hs]) * jax.nn.silu(g_ref[:, hs].astype(F32)))


def _ret_step(proj, nb, lay, gnw, gnb, s_ret, pos0):
    qk, rv, nh = lay["qk"], lay["rv"], lay["ret_heads"]
    dk, dv = qk // nh, rv // nh
    assert dk == LANES and dv == LANES
    bb = 2 * SUBLANES
    assert nb % bb == 0
    intra, qd, kd, blk = _ret_tables(nh, 1)
    rc = jnp.stack([intra[:, 0, 0], qd[:, 0], kd[:, 0], blk], axis=1)
    rc = jnp.pad(rc, ((0, 0), (0, SUBLANES - 4)))
    rc = jnp.broadcast_to(rc[:, :, None], (nh, SUBLANES, LANES))
    cos2, sin2 = _rope_tables(jnp.asarray([pos0], dtype=F32), dk)
    seg = lambda off, w: pl.BlockSpec((bb, w), lambda j, off=off, w=w: (j, off // w))
    sspec = pl.BlockSpec((bb, nh, dk, dv), lambda j: (j, 0, 0, 0))
    row = lambda w: pl.BlockSpec((1, w), lambda j: (0, 0))
    return pl.pallas_call(
        _ret_step_kernel,
        grid=(nb // bb,),
        in_specs=[seg(lay["off_q"], qk), seg(lay["off_kr"], qk), seg(lay["off_vr"], rv),
                  seg(lay["off_rg"], rv), row(dk), row(dk),
                  pl.BlockSpec(rc.shape, lambda j: (0, 0, 0)), row(rv), row(rv), sspec],
        out_specs=[pl.BlockSpec((bb, rv), lambda j: (j, 0)), sspec],
        out_shape=[jax.ShapeDtypeStruct((nb, rv), BF16), jax.ShapeDtypeStruct(s_ret.shape, F32)],
        scratch_shapes=[pltpu.VMEM((bb, qk), F32), pltpu.VMEM((bb, qk), F32), pltpu.VMEM((bb, rv), F32),
                        pltpu.VMEM((bb, rv), F32)],
        compiler_params=_params("arbitrary"),
    )(proj, proj, proj, proj, cos2, sin2, rc, gnw, gnb, s_ret)


def _merge_kernel(alpha, ng, *refs):
    ya_ref, yb_ref = refs[0], refs[1]
    ga_refs = refs[2:2 + ng]
    gb_refs = refs[2 + ng:2 + 2 * ng]
    (x_ref, g1_ref, sc2_ref, sh2_ref, wa_ref, wb_ref, wo_ref, lnw_ref, lnb_ref,
     x1_ref, u2_ref) = refs[2 + 2 * ng:]
    cat = lambda rs: jnp.concatenate([r[...].astype(F32) for r in rs], axis=1)
    merged = (jax.nn.sigmoid(cat(ga_refs)) * _dot(ya_ref[...], wa_ref[...])
              + jax.nn.sigmoid(cat(gb_refs)) * _dot(yb_ref[...], wb_ref[...]))
    t = alpha * x_ref[...] + g1_ref[0] * _dot(_bf(merged), wo_ref[...])
    x1 = _layer_norm_rows(t, lnw_ref[...], lnb_ref[...])
    x1_ref[...] = x1
    u2_ref[...] = _bf(x1 * (1.0 + sc2_ref[0]) + sh2_ref[0])


def _merge(ya, yb, proj, x, g1, sc2, sh2, wba, wbb, wout, lnw, lnb, lay, alpha, tm, tpg):
    m, d = x.shape
    rw, rv = ya.shape[1], yb.shape[1]
    gw = math.gcd(lay["off_ga"], d)
    ng = d // gw
    r = g1.shape[1]
    rowt = lambda w: pl.BlockSpec((tm, w), lambda i: (i, 0))
    gate = lambda off, q: pl.BlockSpec((tm, gw), lambda i, off=off, q=q: (i, off // gw + q))
    mod = pl.BlockSpec((1, r, d), lambda i: (i // tpg, 0, 0))
    const = lambda a: pl.BlockSpec(a.shape, lambda i: (0, 0), pipeline_mode=pl.Buffered(1))
    return pl.pallas_call(
        functools.partial(_merge_kernel, alpha, ng),
        grid=(m // tm,),
        in_specs=[rowt(rw), rowt(rv)]
                 + [gate(lay["off_ga"], q) for q in range(ng)]
                 + [gate(lay["off_gb"], q) for q in range(ng)]
                 + [rowt(d), mod, mod, mod, const(wba), const(wbb), const(wout), const(lnw), const(lnb)],
        out_specs=[rowt(d), rowt(d)],
        out_shape=[jax.ShapeDtypeStruct((m, d), F32), jax.ShapeDtypeStruct((m, d), BF16)],
        compiler_params=_params("arbitrary"),
    )(ya, yb, *([proj] * (2 * ng)), x, g1, sc2, sh2, wba, wbb, wout, lnw, lnb)


def _ffn_seq_kernel(tps, nj, nsteps, alpha, u_ref, wa_ref, wb_ref, cwa_ref, cwb_ref, ia_ref, ib_ref,
                    wd_ref, x1_ref, g2_ref, lnw_ref, lnb_ref, o_ref, ta_ref, tb_ref,
                    ca_s, cb_s, ha_s, hb_s, act_s):
    t = pl.program_id(0)
    tm = u_ref.shape[0]

    @pl.when(t == 0)
    def _():
        act_s[...] = jnp.zeros_like(act_s)
        ha_s[...] = jnp.zeros_like(ha_s)
        hb_s[...] = jnp.zeros_like(hb_s)

    @pl.when((t < 2) | ((t - 2) % nj == 0))
    def _():
        o_ref[...] = jnp.zeros_like(o_ref)

    o_ref[...] += _dot(act_s[t % 2], wd_ref[...])

    u = u_ref[...]
    ha_s[t % 2] = _dot(u, wa_ref[...])
    hb_s[t % 2] = _dot(u, wb_ref[...])

    te = jnp.clip(t - 1, 0, nsteps - 1)
    i, jc = te // nj, te % nj
    first = i % tps == 0
    halves, tails = [], []
    for h_s, cw_ref, init_ref, c_s in ((ha_s, cwa_ref, ia_ref, ca_s), (hb_s, cwb_ref, ib_ref, cb_s)):
        h = h_s[(t + 1) % 2]
        prev = jnp.where(first, init_ref[0], c_s[jc])
        rid = lax.broadcasted_iota(jnp.int32, h.shape, 0)
        h1 = jnp.where(rid == 0, prev[SUBLANES - 1:SUBLANES, :], pltpu.roll(h, 1, 0))
        h2 = jnp.where(rid == 0, prev[SUBLANES - 2:SUBLANES - 1, :],
                       jnp.where(rid == 1, prev[SUBLANES - 1:SUBLANES, :], pltpu.roll(h, 2, 0)))
        tail = h[tm - SUBLANES:tm, :]
        tails.append(tail)
        cw = cw_ref[...]
        halves.append(cw[3:4, :] + cw[0:1, :] * h2 + cw[1:2, :] * h1 + cw[2:3, :] * h)
    act_s[(t + 1) % 2] = _bf(jax.nn.silu(halves[0]) * halves[1])

    @pl.when(t >= 1)
    def _():
        ca_s[jc] = tails[0]
        cb_s[jc] = tails[1]

    @pl.when((t >= 1) & (t <= nsteps) & (i % tps == tps - 1))
    def _():
        ta_ref[i // tps, jc] = tails[0]
        tb_ref[i // tps, jc] = tails[1]

    @pl.when((t >= 2) & ((t - 1) % nj == 0))
    def _():
        y = alpha * x1_ref[...] + g2_ref[0] * o_ref[...]
        o_ref[...] = _layer_norm_rows(y, lnw_ref[...], lnb_ref[...])


def _ffn_up_step_kernel(u_ref, wa_ref, wb_ref, cwa_ref, cwb_ref, p1a_ref, p2a_ref, p1b_ref, p2b_ref,
                        act_ref, ha_ref, hb_ref, wa_bf_ref, wb_bf_ref):
    u = u_ref[...]
    halves = []
    for w_ref, cw_ref, p1_ref, p2_ref, h_ref, w_bf_ref in (
            (wa_ref, cwa_ref, p1a_ref, p2a_ref, ha_ref, wa_bf_ref),
            (wb_ref, cwb_ref, p1b_ref, p2b_ref, hb_ref, wb_bf_ref)):
        w = _bf(w_ref[...])
        w_bf_ref[...] = w
        h = _dot(u, w)
        h_ref[...] = h
        cw = cw_ref[...]
        halves.append(cw[3:4, :] + cw[0:1, :] * p2_ref[...] + cw[1:2, :] * p1_ref[...] + cw[2:3, :] * h)
    act_ref[...] = _bf(jax.nn.silu(halves[0]) * halves[1])


def _ffn_up(u2, w_up, cwt, nb, seq, s_conv, tm):
    m, d = u2.shape
    assert m == tm
    f2 = w_up.shape[1]
    f = f2 // 2
    tn = _pick_tile(f, (512, 256, 128))
    nj = f // tn
    w_a = pl.BlockSpec((d, tn), lambda i, j: (0, j))
    w_b = pl.BlockSpec((d, tn), lambda i, j: (0, nj + j))
    cw_a = pl.BlockSpec((4, tn), lambda i, j: (0, j))
    cw_b = pl.BlockSpec((4, tn), lambda i, j: (0, nj + j))
    assert seq == 1
    p_a = pl.BlockSpec((tm, tn), lambda i, j: (i, j))
    p_b = pl.BlockSpec((tm, tn), lambda i, j: (i, nj + j))
    prev1, prev2 = s_conv[:, 1, :], s_conv[:, 0, :]
    w_o = pl.BlockSpec((d, tn), lambda i, j: (0, j))
    act, h_a, h_b, w_a_bf, w_b_bf = pl.pallas_call(
        _ffn_up_step_kernel,
        grid=(m // tm, nj),
        in_specs=[pl.BlockSpec((tm, d), lambda i, j: (i, 0)), w_a, w_b, cw_a, cw_b, p_a, p_a, p_b, p_b],
        out_specs=[pl.BlockSpec((tm, tn), lambda i, j: (i, j))] * 3 + [w_o, w_o],
        out_shape=[jax.ShapeDtypeStruct((m, f), BF16), jax.ShapeDtypeStruct((m, f), F32),
                   jax.ShapeDtypeStruct((m, f), F32), jax.ShapeDtypeStruct((d, f), BF16),
                   jax.ShapeDtypeStruct((d, f), BF16)],
        compiler_params=_params("arbitrary", "arbitrary"),
    )(u2, w_up, w_up, cwt, cwt, prev1, prev2, prev1, prev2)
    conv_new = jnp.stack([prev1, jnp.concatenate([h_a, h_b], axis=-1)], axis=1)
    return act, conv_new, (w_a_bf, w_b_bf)


def _ffn_seq(u2, w_up_halves, cwt, w_down_bf, x1, g2, lnw, lnb, alpha, nb, seq, s_conv, tm):
    m, d = u2.shape
    w_up_a, w_up_b = w_up_halves
    f = w_up_a.shape[1]
    tn = _pick_tile(f, (512, 256, 128))
    nj = f // tn
    assert seq % tm == 0 and tm >= SUBLANES
    tps = seq // tm
    nsteps = (m // tm) * nj
    pair = lambda t, s: jnp.clip(t - s, 0, nsteps - 1)
    row_i = lambda t, s: pair(t, s) // nj
    col_j = lambda t, s: pair(t, s) % nj
    w_up = pl.BlockSpec((d, tn), lambda t: (0, col_j(t, 0)))
    cw_a = pl.BlockSpec((4, tn), lambda t: (0, col_j(t, 1)))
    cw_b = pl.BlockSpec((4, tn), lambda t: (0, nj + col_j(t, 1)))
    init = jnp.pad(s_conv, ((0, 0), (SUBLANES - 2, 0), (0, 0)))
    i_a = pl.BlockSpec((1, SUBLANES, tn), lambda t: (row_i(t, 1) // tps, 0, col_j(t, 1)))
    i_b = pl.BlockSpec((1, SUBLANES, tn), lambda t: (row_i(t, 1) // tps, 0, nj + col_j(t, 1)))
    w_d = pl.BlockSpec((tn, d), lambda t: (col_j(t, 2), 0))
    row_up = pl.BlockSpec((tm, d), lambda t: (row_i(t, 0), 0))
    row_dn = pl.BlockSpec((tm, d), lambda t: (row_i(t, 2), 0))
    vec = pl.BlockSpec((1, d), lambda t: (0, 0))
    t_o = pl.BlockSpec((nb, nj, SUBLANES, tn), lambda t: (0, 0, 0, 0))
    tails = jax.ShapeDtypeStruct((nb, nj, SUBLANES, tn), F32)
    x2, t_a, t_b = pl.pallas_call(
        functools.partial(_ffn_seq_kernel, tps, nj, nsteps, alpha),
        grid=(nsteps + 2,),
        in_specs=[row_up, w_up, w_up, cw_a, cw_b, i_a, i_b, w_d, row_dn,
                  pl.BlockSpec((1, 1, d), lambda t: (row_i(t, 2) // tps, 0, 0)), vec, vec],
        out_specs=[row_dn, t_o, t_o],
        out_shape=[jax.ShapeDtypeStruct((m, d), F32), tails, tails],
        scratch_shapes=[pltpu.VMEM((nj, SUBLANES, tn), F32), pltpu.VMEM((nj, SUBLANES, tn), F32),
                        pltpu.VMEM((2, tm, tn), F32), pltpu.VMEM((2, tm, tn), F32),
                        pltpu.VMEM((2, tm, tn), BF16)],
        compiler_params=_params("arbitrary"),
    )(u2, w_up_a, w_up_b, cwt, cwt, init, init, w_down_bf, x1, g2, lnw, lnb)
    rows = lambda t: t[:, :, SUBLANES - 2:, :].transpose(0, 2, 1, 3).reshape(nb, 2, f)
    return x2, jnp.concatenate([rows(t_a), rows(t_b)], axis=-1)


def _ffn_down_kernel(alpha, act_ref, w_ref, x1_ref, g2_ref, lnw_ref, lnb_ref, o_ref, w_bf_ref, acc_ref):
    k = pl.program_id(1)

    @pl.when(k == 0)
    def _():
        acc_ref[...] = jnp.zeros_like(acc_ref)

    w = _bf(w_ref[...])
    w_bf_ref[...] = w
    acc_ref[...] += _dot(act_ref[...], w)

    @pl.when(k == pl.num_programs(1) - 1)
    def _():
        t = alpha * x1_ref[...] + g2_ref[0] * acc_ref[...]
        o_ref[...] = _layer_norm_rows(t, lnw_ref[...], lnb_ref[...])


def _ffn_down(act, w_down, x1, g2, lnw, lnb, alpha, tm, tpg):
    m, f = act.shape
    assert m == tm
    d = w_down.shape[1]
    tk = _pick_tile(f, (512, 256, 128))
    r = g2.shape[1]
    return pl.pallas_call(
        functools.partial(_ffn_down_kernel, alpha),
        grid=(m // tm, f // tk),
        in_specs=[pl.BlockSpec((tm, tk), lambda i, k: (i, k)),
                  pl.BlockSpec((tk, d), lambda i, k: (k, 0)),
                  pl.BlockSpec((tm, d), lambda i, k: (i, 0)),
                  pl.BlockSpec((1, r, d), lambda i, k: (i // tpg, 0, 0)),
                  pl.BlockSpec((1, d), lambda i, k: (0, 0)),
                  pl.BlockSpec((1, d), lambda i, k: (0, 0))],
        out_specs=[pl.BlockSpec((tm, d), lambda i, k: (i, 0)), pl.BlockSpec((tk, d), lambda i, k: (k, 0))],
        out_shape=[jax.ShapeDtypeStruct((m, d), F32), jax.ShapeDtypeStruct((f, d), BF16)],
        scratch_shapes=[pltpu.VMEM((tm, d), F32)],
        compiler_params=_params("arbitrary", "arbitrary"),
    )(act, w_down, x1, g2, lnw, lnb)


def _layout(d, rw, qk, rv, lora_w, ret_heads):
    lslot = next(s for s in (128, 256, 512, 1024, 2048) if s >= lora_w)
    off_q = 3 * rw
    off_kr = off_q + qk
    off_vr = off_kr + qk
    off_rg = off_vr + rv
    off_ga = off_rg + rv
    off_gb = off_ga + d
    off_l = off_gb + d
    assert rw % LANES == 0 and off_l % lslot == 0
    assert off_q % qk == 0 and off_kr % qk == 0 and off_vr % rv == 0 and off_rg % rv == 0
    return dict(d=d, rw=rw, qk=qk, rv=rv, lslot=lslot, lora_w=lora_w, ret_heads=ret_heads,
                off_q=off_q, off_kr=off_kr, off_vr=off_vr, off_rg=off_rg, off_ga=off_ga,
                off_gb=off_gb, off_l=off_l, nt=off_l + lslot)


def _prep_weights(lay, shift_mu, w0, w_decay_up, a0, w_aaa_up, w_gate_up, k_k, k_a, r_k, lnx_w, lnx_b):
    rw, lslot, lora_w = lay["rw"], lay["lslot"], lay["lora_w"]
    dl, al, gl = lay["lora_dims"]
    a1, g0, g1 = _lora_split(lay)
    pad_rows = lambda w, lo, n: jnp.pad(w, ((lo, n - lo - w.shape[0]), (0, 0))).astype(BF16)
    wd = pad_rows(w_decay_up, 0, a1)
    wa = pad_rows(w_aaa_up, dl, a1)
    wg = pad_rows(w_gate_up, dl + al - g0, g1 - g0)
    rows = [shift_mu[:rw], shift_mu[rw:2 * rw], shift_mu[2 * rw:3 * rw], w0, a0, k_k, k_a,
            r_k.reshape(-1), lnx_w, lnx_b]
    pvec = jnp.pad(jnp.stack(rows, axis=0), ((0, _PV_ROWS - len(rows)), (0, 0)))
    mul = jnp.pad(shift_mu[3 * rw:], (0, lslot - lora_w))[None, :]
    return wd, wa, wg, pvec, mul


def _run_layer(x2d, nb, seq, ada, states, pos0, lay, wts, big, alpha):
    rw, lslot, lora_w = lay["rw"], lay["lslot"], lay["lora_w"]
    (wd, wa, wg, pvec, mul, gnw, gnb, wba, wbb, wout, ln1w, ln1b, cwt, ln2w, ln2b) = wts
    w_in_t, w_up, w_down = big
    m = nb * seq
    sh1, sc1, g1, sh2, sc2, g2 = jnp.split(ada, 6, axis=-1)
    if seq == 1:
        tm, tpg = m, 1
        shape_mod = lambda t: t[None]
    else:
        tm = _pick_tile(seq, (512, 256, 128, 64, 32, 16, 8))
        tpg = seq // tm
        shape_mod = lambda t: t[:, None, :]
    sh1, sc1, g1, sh2, sc2, g2 = map(shape_mod, (sh1, sc1, g1, sh2, sc2, g2))

    s_wkv, s_shift, s_ret, s_conv = states
    shift_main = s_shift[:, :3 * rw]
    shift_lora = jnp.pad(s_shift[:, 3 * rw:], ((0, 0), (0, lslot - lora_w)))
    if seq == 1:
        proj, w_in_bf = _modmm(x2d, sc1, sh1, w_in_t, lay, m, 1)
        ya, wkv_new = _rwkv_step(proj, nb, lay, (shift_main, shift_lora), pvec, mul, wd, wa, wg, s_wkv)
        yb, ret_new = _ret_step(proj, nb, lay, gnw, gnb, s_ret, pos0)
    else:
        tm_in = _pick_tile(seq, (1024, 512, 256, 128, 64, 32, 16, 8))
        proj = _modmm(x2d, sc1, sh1, w_in_t, lay, tm_in, seq // tm_in)
        first = (shift_main[:, None, :], shift_main[:, None, :], shift_main[:, None, :],
                 shift_lora[:, None, :])
        ya, wkv_new = _rwkv_seq(proj, nb, seq, lay, first, pvec, mul, wd, wa, wg)
        yb, ret_new = _ret_seq(proj, nb, seq, lay, gnw, gnb)
    last = proj.reshape(nb, seq, lay["nt"])[:, -1, :]
    shift_new = jnp.concatenate([last[:, :3 * rw], last[:, lay["off_l"]:lay["off_l"] + lora_w]],
                                axis=-1).astype(F32)

    tm_merge = min(tm, 256)
    x1, u2 = _merge(ya, yb, proj, x2d, g1, sc2, sh2, wba, wbb, wout, ln1w, ln1b, lay, alpha,
                    tm_merge, (seq // tm_merge) if seq > 1 else 1)
    if seq == 1:
        act, conv_new, w_up_halves = _ffn_up(u2, w_up, cwt, nb, seq, s_conv, tm)
        x2, w_down_bf = _ffn_down(act, w_down, x1, g2, ln2w, ln2b, alpha, tm, tpg)
        rounded = (w_in_bf, w_up_halves, w_down_bf)
    else:
        x2, conv_new = _ffn_seq(u2, w_up, cwt, w_down, x1, g2, ln2w, ln2b, alpha, nb, seq, s_conv, tm)
        rounded = None
    return (x2, wkv_new, shift_new, ret_new, conv_new), rounded


def kernel(x_prompt, x_sample, c_prompt, c_sample, state_wkv, state_shift, state_ret, state_conv, w_ada, b_ada, w_in, shift_mu, w0, w_decay_up, a0, w_aaa_up, w_gate_up, k_k, k_a, r_k, lnx_w, lnx_b, ret_gn_w, ret_gn_b, w_branch_a, w_branch_b, w_out, ln1_w, ln1_b, w_up, conv_w, conv_b, w_down, ln2_w, ln2_b):
    depth = w_ada.shape[0]
    nbp, seq_p, d = x_prompt.shape
    nbs, seq_s, _ = x_sample.shape
    assert seq_s == 1
    rw = k_k.shape[-1]
    ret_heads, dk, dv = state_ret.shape[2:]
    lora_w = w_decay_up.shape[1] + w_aaa_up.shape[1] + w_gate_up.shape[1]
    lay = _layout(d, rw, ret_heads * dk, ret_heads * dv, lora_w, ret_heads)
    lay["lora_dims"] = (w_decay_up.shape[1], w_aaa_up.shape[1], w_gate_up.shape[1])
    heads, hn = r_k.shape[1:]
    assert hn == LANES // 2 and heads * hn == rw
    alpha = (2.0 * depth) ** 0.25
    f2 = w_up.shape[-1]

    xp = x_prompt.reshape(nbp * seq_p, d)
    xs = x_sample.reshape(nbs * seq_s, d)
    c_all = jnp.concatenate([c_prompt, c_sample], axis=0)
    pad = (-c_all.shape[0]) % SUBLANES
    c_all = _bf(jnp.pad(c_all, ((0, pad), (0, 0))))

    outs_p, outs_s = [], []
    for l in range(depth):
        wd, wa, wg, pvec, mul = _prep_weights(
            lay, shift_mu[l], w0[l], w_decay_up[l], a0[l], w_aaa_up[l], w_gate_up[l],
            k_k[l], k_a[l], r_k[l], lnx_w[l], lnx_b[l])
        cwt = jnp.concatenate([conv_w[l], conv_b[l][None, :]], axis=0)
        wts = (wd, wa, wg, pvec, mul, ret_gn_w[l][None, :], ret_gn_b[l][None, :],
               _bf(w_branch_a[l]), _bf(w_branch_b[l]), _bf(w_out[l]), ln1_w[l][None, :], ln1_b[l][None, :],
               cwt, ln2_w[l][None, :], ln2_b[l][None, :])
        ada = _mm_bias(c_all, w_ada[l], b_ada[l][None, :])
        (xs, *st_s), rounded = _run_layer(
            xs, nbs, seq_s, ada[nbp:nbp + nbs],
            (state_wkv[l], state_shift[l], state_ret[l], state_conv[l]), float(PAST_LEN), lay, wts,
            (jnp.transpose(w_in[l]), w_up[l], w_down[l]), alpha)
        zero_states = (None, jnp.zeros((nbp, state_shift.shape[-1]), F32), None,
                       jnp.zeros((nbp, state_conv.shape[2], f2), F32))
        (xp, *st_p), _ = _run_layer(xp, nbp, seq_p, ada[:nbp], zero_states, 0.0, lay, wts, rounded, alpha)
        outs_p.append(st_p)
        outs_s.append(st_s)

    def stack(lst, j, ref):
        layers = [s[j].astype(ref.dtype) for s in lst]
        return layers[0][None] if depth == 1 else jnp.stack(layers, axis=0)

    refs = (state_wkv, state_shift, state_ret, state_conv)
    return (xp.reshape(x_prompt.shape), xs.reshape(x_sample.shape),
            *[stack(outs_p, j, refs[j]) for j in range(4)],
            *[stack(outs_s, j, refs[j]) for j in range(4)])
```

```python
import functools
import math

import jax
import jax.numpy as jnp
from jax import lax
from jax.experimental import pallas as pl
from jax.experimental.pallas import tpu as pltpu

F32 = jnp.float32
BF16 = jnp.bfloat16

PAST_LEN = 16384
ROPE_BASE = 10000.0
RWKV_GN_EPS = 64e-5
RET_GN_EPS = 1e-5
LN_EPS = 1e-5
RET_CHUNK = 128

LANES = 128
SUBLANES = 8
VMEM_LIMIT_BYTES = 50 * 1024 * 1024

RWKV_CHUNK = 64
RWKV_PAIRS_PER_STEP = 8
RWKV_GROUPS_PER_STEP = 1


def _params(*sem):
    return pltpu.CompilerParams(dimension_semantics=sem, vmem_limit_bytes=VMEM_LIMIT_BYTES)


def _dot(a, b):
    return jnp.dot(a, b, preferred_element_type=F32)


def _dot_nt(a, b):
    return lax.dot_general(a, b, (((1,), (1,)), ((), ())), preferred_element_type=F32)


def _dot_tn(a, b):
    return lax.dot_general(a, b, (((0,), (0,)), ((), ())), preferred_element_type=F32)


def _bf(x):
    return x.astype(BF16)


def _split2(x):
    hi = x.astype(BF16)
    lo = (x - hi.astype(F32)).astype(BF16)
    return hi, lo


def _dot3_nn(a, b):
    ah, al = _split2(a)
    bh, bl = _split2(b)
    return _dot(jnp.concatenate([ah, ah, al], axis=1), jnp.concatenate([bh, bl, bh], axis=0))


def _dot_exact_lhs(a_bf, b):
    b1 = b.astype(BF16)
    r1 = b - b1.astype(F32)
    b2 = r1.astype(BF16)
    b3 = (r1 - b2.astype(F32)).astype(BF16)
    return _dot(jnp.concatenate([a_bf, a_bf, a_bf], axis=1), jnp.concatenate([b1, b2, b3], axis=0))


def _layer_norm_rows(t, w, b):
    mu = jnp.mean(t, axis=-1, keepdims=True)
    d = t - mu
    var = jnp.mean(d * d, axis=-1, keepdims=True)
    return d * lax.rsqrt(var + LN_EPS) * w + b


def _pick_tile(n, candidates):
    for c in candidates:
        if n % c == 0:
            return c
    return n


def _mm_bias_kernel(x_ref, w_ref, b_ref, o_ref):
    o_ref[...] = _dot(x_ref[...], _bf(w_ref[...])) + b_ref[...]


def _mm_bias(x_bf, w, b_row):
    m, k = x_bf.shape
    n = w.shape[1]
    tn = _pick_tile(n, (1024, 512, 256, 128))
    return pl.pallas_call(
        _mm_bias_kernel,
        grid=(n // tn,),
        in_specs=[pl.BlockSpec((m, k), lambda j: (0, 0)),
                  pl.BlockSpec((k, tn), lambda j: (0, j)),
                  pl.BlockSpec((1, tn), lambda j: (0, j))],
        out_specs=pl.BlockSpec((m, tn), lambda j: (0, j)),
        out_shape=jax.ShapeDtypeStruct((m, n), F32),
        compiler_params=_params("arbitrary"),
    )(x_bf, w, b_row)


def _modmm_kernel(x_ref, sc_ref, sh_ref, wt_ref, o_ref, u_ref):
    @pl.when(pl.program_id(1) == 0)
    def _():
        u_ref[...] = _bf(x_ref[...] * (1.0 + sc_ref[0]) + sh_ref[0])

    o_ref[...] = _dot_nt(u_ref[...], wt_ref[...]).astype(o_ref.dtype)


def _modmm_cast_kernel(x_ref, sc_ref, sh_ref, wt_ref, o_ref, wbf_ref, u_ref):
    @pl.when(pl.program_id(1) == 0)
    def _():
        u_ref[...] = _bf(x_ref[...] * (1.0 + sc_ref[0]) + sh_ref[0])

    w = _bf(wt_ref[...])
    wbf_ref[...] = w
    o_ref[...] = _dot_nt(u_ref[...], w).astype(o_ref.dtype)


def _modmm(x, sc, sh, wt, lay, tm, tpg):
    m, d = x.shape
    n = lay["nt"]
    tn = _pick_tile(n, (512, 256, 128))
    r = sc.shape[1]
    mod_spec = pl.BlockSpec((1, r, d), lambda i, j: (i // tpg, 0, 0))
    x_spec = pl.BlockSpec((tm, d), lambda i, j: (i, 0))
    o_spec = pl.BlockSpec((tm, tn), lambda i, j: (i, j))
    w_blocked = pl.BlockSpec((tn, d), lambda i, j: (j, 0))
    if wt.dtype == BF16:
        return pl.pallas_call(
            _modmm_kernel,
            grid=(m // tm, n // tn),
            in_specs=[x_spec, mod_spec, mod_spec, w_blocked],
            out_specs=o_spec,
            out_shape=jax.ShapeDtypeStruct((m, n), BF16),
            scratch_shapes=[pltpu.VMEM((tm, d), BF16)],
            compiler_params=_params("arbitrary", "arbitrary"),
        )(x, sc, sh, wt)

    assert m == tm
    rw3, lora_w, lslot = 3 * lay["rw"], lay["lora_w"], lay["lslot"]
    n_main, n_rest = rw3 // tn, (lay["off_l"] - rw3) // tn
    assert rw3 % tn == 0 and (lay["off_l"] - rw3) % tn == 0 and lslot == tn
    assert rw3 + lslot <= wt.shape[0]
    assert tn % SUBLANES == 0 and rw3 % SUBLANES == 0 and lora_w % SUBLANES == 0

    def w_row(j):
        row = jnp.where(j < n_main, j * tn,
                        jnp.where(j < n_main + n_rest, rw3 + lora_w + (j - n_main) * tn, rw3))
        return pl.multiple_of(row, SUBLANES)

    return pl.pallas_call(
        _modmm_cast_kernel,
        grid=(1, n // tn),
        in_specs=[x_spec, mod_spec, mod_spec,
                  pl.BlockSpec((pl.Element(tn), pl.Element(d)), lambda i, j: (w_row(j), 0))],
        out_specs=[o_spec, w_blocked],
        out_shape=[jax.ShapeDtypeStruct((m, n), BF16), jax.ShapeDtypeStruct((n, d), BF16)],
        scratch_shapes=[pltpu.VMEM((tm, d), BF16)],
        compiler_params=_params("arbitrary", "arbitrary"),
    )(x, sc, sh, wt)


_PV_MU_R, _PV_MU_K, _PV_MU_V, _PV_W0, _PV_A0, _PV_KK, _PV_KA, _PV_RK, _PV_LNW, _PV_LNB = range(10)
_PV_ROWS = 16


def _head_half_mask(shape):
    return lax.broadcasted_iota(jnp.int32, shape, 1) < (LANES // 2)


def _head_sums(x):
    h0 = _head_half_mask((x.shape[0], LANES))
    parts = []
    for p in range(x.shape[1] // LANES):
        xs = x[:, p * LANES:(p + 1) * LANES]
        s0 = jnp.sum(jnp.where(h0, xs, 0.0), axis=-1, keepdims=True)
        s1 = jnp.sum(jnp.where(h0, 0.0, xs), axis=-1, keepdims=True)
        parts.append(jnp.where(h0, s0, s1))
    return parts[0] if len(parts) == 1 else jnp.concatenate(parts, axis=1)


def _lora_split(lay):
    dl, al, gl = lay["lora_dims"]
    up = lambda n: -(-n // LANES) * LANES
    a1, g0, g1 = up(dl + al), (dl + al) // LANES * LANES, up(dl + al + gl)
    assert g1 <= lay["lslot"]
    return a1, g0, g1


def _rwkv_lora_inputs(zl, plr, mul, split):
    a1, g0, g1 = split
    ls = zl + mul * (plr - zl)
    return _bf(jnp.tanh(ls[:, :a1])), _bf(ls[:, :a1]), _bf(jax.nn.sigmoid(ls[:, g0:g1]))


def _rwkv_lora(lora_in, wd, wa, wg):
    return _dot(lora_in[0], wd), _dot(lora_in[1], wa), _dot(lora_in[2], wg)


def _rwkv_tokens(zr, zk, zv, pr, pk, pv_, lora, pvec):
    row = lambda i: pvec[i:i + 1, :]
    r = zr + row(_PV_MU_R) * (pr - zr)
    k = zk + row(_PV_MU_K) * (pk - zk)
    v = zv + row(_PV_MU_V) * (pv_ - zv)
    wl, al, g = lora
    logd = -math.exp(-0.5) * jax.nn.sigmoid(row(_PV_W0) + wl)
    a = jax.nn.sigmoid(row(_PV_A0) + al)
    kkr = k * row(_PV_KK)
    kk = kkr * jnp.minimum(lax.rsqrt(_head_sums(kkr * kkr)), 1e12)
    kp = k * (1.0 + (a - 1.0) * row(_PV_KA))
    bonus = _head_sums(r * kp * row(_PV_RK)) * v
    return r, kp, v, kk, a, logd, g, bonus


def _rwkv_finish(o, bonus, g, pvec):
    inv_n = 1.0 / (LANES // 2)
    mu = _head_sums(o) * inv_n
    d = o - mu
    var = _head_sums(d * d) * inv_n
    on = d * lax.rsqrt(var + RWKV_GN_EPS)
    return (on * pvec[_PV_LNW:_PV_LNW + 1, :] + pvec[_PV_LNB:_PV_LNB + 1, :] + bonus) * g


def _shift_rows(z, first_row):
    rolled = pltpu.roll(z, 1, 0)
    rowid = lax.broadcasted_iota(jnp.int32, z.shape, 0)
    return jnp.where(rowid == 0, first_row, rolled)


def _rwkv_seq_kernel(split, zr_ref, zk_ref, zv_ref, zl_ref, fr_ref, fk_ref, fv_ref, fl_ref,
                     pvec_ref, mul_ref, wd_ref, wa_ref, wg_ref, tri_ref,
                     ya_ref, wkv_ref, s_ref, cr_ref, ck_ref, cv_ref, cl_ref):
    i = pl.program_id(2)
    tc, full_width = zr_ref.shape
    c = RWKV_CHUNK
    nch = tc // c
    half = LANES // 2
    ngroups = RWKV_GROUPS_PER_STEP if (full_width // LANES) % RWKV_GROUPS_PER_STEP == 0 else 1
    npp = full_width // LANES // ngroups

    @pl.when(i == 0)
    def _():
        s_ref[...] = jnp.zeros_like(s_ref)
        cr_ref[0:1, :] = fr_ref[0]
        ck_ref[0:1, :] = fk_ref[0]
        cv_ref[0:1, :] = fv_ref[0]
        cl_ref[0:1, :] = fl_ref[0]

    zl = zl_ref[...].astype(F32)
    lora_in = _rwkv_lora_inputs(zl, _shift_rows(zl, cl_ref[0:1, :]), mul_ref[...], split)
    cl_ref[0:1, :] = zl[tc - 1:tc, :]
    tri = tri_ref[...]
    for gi in range(ngroups):
        _rwkv_seq_group(gi, npp, tc, c, nch, tri, lora_in, zr_ref, zk_ref, zv_ref, pvec_ref,
                        wd_ref, wa_ref, wg_ref, ya_ref, s_ref, cr_ref, ck_ref, cv_ref)

    @pl.when(i == pl.num_programs(2) - 1)
    def _():
        for p in range(s_ref.shape[0]):
            s = jnp.transpose(s_ref[p])
            wkv_ref[0, 2 * p] = s[:half, :half]
            wkv_ref[0, 2 * p + 1] = s[half:, half:]


def _rwkv_seq_group(gi, npp, tc, c, nch, tri, lora_in, zr_ref, zk_ref, zv_ref, pvec_ref,
                    wd_ref, wa_ref, wg_ref, ya_ref, s_ref, cr_ref, ck_ref, cv_ref):
    width = npp * LANES
    gl = slice(gi * width, (gi + 1) * width)
    lora_all = _rwkv_lora(lora_in, wd_ref[:, gl], wa_ref[:, gl], wg_ref[:, gl])

    tok = {}
    for p in range(npp):
        pl_ = slice(gi * width + p * LANES, gi * width + (p + 1) * LANES)
        ls = slice(p * LANES, (p + 1) * LANES)
        zr, zk, zv = (ref[:, pl_].astype(F32) for ref in (zr_ref, zk_ref, zv_ref))
        pvec_p = pvec_ref[:, pl_]
        r, kp, v, kk, a, logd, g, bonus = _rwkv_tokens(
            zr, zk, zv, _shift_rows(zr, cr_ref[0:1, pl_]), _shift_rows(zk, ck_ref[0:1, pl_]),
            _shift_rows(zv, cv_ref[0:1, pl_]), tuple(x[:, ls] for x in lora_all), pvec_p)
        cr_ref[0:1, pl_] = zr[tc - 1:tc, :]
        ck_ref[0:1, pl_] = zk[tc - 1:tc, :]
        cv_ref[0:1, pl_] = zv[tc - 1:tc, :]
        cum_incl = _dot_exact_lhs(tri, logd)
        tot_rows = [cum_incl[(ci + 1) * c - 1:(ci + 1) * c, :] for ci in range(nch)]
        w_tot = jnp.concatenate([jnp.broadcast_to(jnp.exp(t), (c, LANES)) for t in tot_rows], axis=0)
        e_in = jnp.exp(cum_incl)
        e_neg = 1.0 / e_in
        beta = kk * a
        b_t = beta * e_neg
        k_t = kp * e_neg
        tok[p] = dict(r_t=r * e_in, a_t=-kk * jnp.exp(cum_incl - logd), b_t=b_t, k_t=k_t,
                      b_h=b_t * w_tot, k_h=k_t * w_tot, v=v, g=g, bonus=bonus, pvec=pvec_p, tot_rows=tot_rows)

    h0 = _head_half_mask((c, LANES))

    def stack(name, p, ci):
        xb = tok[p][name][ci * c:(ci + 1) * c, :]
        return jnp.concatenate([jnp.where(h0, xb, 0.0), jnp.where(h0, 0.0, xb)], axis=0)

    keep0 = h0[0:1, :].astype(BF16)
    keep1 = 1.0 - keep0

    def stack_bf(name, p, ci):
        xb = _bf(tok[p][name][ci * c:(ci + 1) * c, :])
        return jnp.concatenate([xb * keep0, xb * keep1], axis=0)

    rr = lax.broadcasted_iota(jnp.int32, (2 * c, 2 * c), 0)
    cc = lax.broadcasted_iota(jnp.int32, (2 * c, 2 * c), 1)
    same = (rr >= c) == (cc >= c)
    strict = same & (cc < rr)
    incl = same & (cc <= rr)
    eye = rr == cc
    zeros_blk = jnp.zeros((2 * c, LANES), BF16)

    probs = [(p, ci) for p in range(npp) for ci in range(nch)]
    a_s = {q: stack("a_t", *q) for q in probs}
    r_s = {q: stack("r_t", *q) for q in probs}
    v_bf = {q: stack_bf("v", *q) for q in probs}

    def twice(name, p, ci):
        xb = _bf(tok[p][name][ci * c:(ci + 1) * c, :])
        return jnp.concatenate([xb, xb], axis=0)

    pmat, a_ak, a_r = {}, {}, {}
    for q in probs:
        amat = _dot_nt(_bf(jnp.concatenate([a_s[q], r_s[q]], axis=0)),
                       jnp.concatenate([twice("b_t", *q), twice("k_t", *q)], axis=0))
        pmat[q] = jnp.where(strict, amat[:2 * c, :2 * c], 0.0)
        a_ak[q] = _bf(jnp.where(strict, amat[:2 * c, 2 * c:], 0.0))
        a_r[q] = _bf(jnp.concatenate([jnp.where(incl, amat[2 * c:, :2 * c], 0.0),
                                      jnp.where(incl, amat[2 * c:, 2 * c:], 0.0)], axis=1))

    x = {q: jnp.concatenate([a_s[q], _dot(a_ak[q], v_bf[q])], axis=1) for q in probs}

    nsteps = int(math.log2(c))
    for it in range(nsteps):
        for q in probs:
            p_bf = _bf(pmat[q])
            x[q] = x[q] + _dot(p_bf, _bf(x[q]))
            if it + 1 < nsteps:
                pmat[q] = _dot(p_bf, p_bf)

    r_pair, o_pair, g_t, h_t = {}, {}, {}, {}
    for q in probs:
        p, ci = q
        x_bf = _bf(x[q])
        rhs = jnp.concatenate([x_bf, jnp.concatenate([zeros_blk, v_bf[q]], axis=1)], axis=0)
        y = _dot(a_r[q], rhs)
        r_hat = r_s[q] + y[:, :LANES]
        r_pair[q] = r_hat[:c] + r_hat[c:]
        o_pair[q] = y[:c, LANES:] + y[c:, LANES:]
        z = _dot_tn(stack_bf("b_h", *q), x_bf)
        w_c = jnp.exp(tok[p]["tot_rows"][ci])
        g_t[q] = jnp.where(eye, w_c, 0.0) + z[:, :LANES]
        h_t[q] = z[:, LANES:] + _dot_tn(stack_bf("k_h", *q), v_bf[q])

    outs = {}
    for ci in range(nch):
        for p in range(npp):
            q = (p, ci)
            t0 = s_ref[gi * npp + p]
            outs[q] = _dot3_nn(r_pair[q], t0) + o_pair[q]
            s_ref[gi * npp + p] = _dot3_nn(g_t[q], t0) + h_t[q]

    for p in range(npp):
        o = jnp.concatenate([outs[(p, ci)] for ci in range(nch)], axis=0) if nch > 1 else outs[(p, 0)]
        pl_ = slice(gi * width + p * LANES, gi * width + (p + 1) * LANES)
        ya_ref[:, pl_] = _bf(_rwkv_finish(o, tok[p]["bonus"], tok[p]["g"], tok[p]["pvec"]))


def _rwkv_seq(proj, nb, seq, lay, first, pvec, mul, wd, wa, wg):
    rw, lslot = lay["rw"], lay["lslot"]
    npair = rw // LANES
    npp = RWKV_PAIRS_PER_STEP if npair % RWKV_PAIRS_PER_STEP == 0 else 1
    ngrp = npair // npp
    width = npp * LANES
    c = RWKV_CHUNK
    tc = _pick_tile(seq, (2 * c, c))
    nt = seq // tc
    t_idx = jnp.arange(tc)
    tri = ((t_idx[:, None] // c == t_idx[None, :] // c) & (t_idx[None, :] <= t_idx[:, None])).astype(BF16)
    fr, fk, fv, fl = first
    col = lambda s: pl.BlockSpec((tc, width), lambda b, gp, i, s=s: (b * nt + i, s * ngrp + gp))
    fcol = lambda s: pl.BlockSpec((1, 1, width), lambda b, gp, i, s=s: (b, 0, s * ngrp + gp))
    split = _lora_split(lay)
    wspec = pl.BlockSpec((split[0], width), lambda b, gp, i: (0, gp))
    wgspec = pl.BlockSpec((split[2] - split[1], width), lambda b, gp, i: (0, gp))
    return pl.pallas_call(
        functools.partial(_rwkv_seq_kernel, split),
        grid=(nb, ngrp, nt),
        in_specs=[col(0), col(1), col(2),
                  pl.BlockSpec((tc, lslot), lambda b, gp, i: (b * nt + i, lay["off_l"] // lslot)),
                  fcol(0), fcol(1), fcol(2),
                  pl.BlockSpec((1, 1, lslot), lambda b, gp, i: (b, 0, 0)),
                  pl.BlockSpec((_PV_ROWS, width), lambda b, gp, i: (0, gp)),
                  pl.BlockSpec((1, lslot), lambda b, gp, i: (0, 0)),
                  wspec, wspec, wgspec,
                  pl.BlockSpec((tc, tc), lambda b, gp, i: (0, 0))],
        out_specs=[pl.BlockSpec((tc, width), lambda b, gp, i: (b * nt + i, gp)),
                   pl.BlockSpec((1, 2 * npp, LANES // 2, LANES // 2), lambda b, gp, i: (b, gp, 0, 0))],
        out_shape=[jax.ShapeDtypeStruct((nb * seq, rw), BF16),
                   jax.ShapeDtypeStruct((nb, 2 * npair, LANES // 2, LANES // 2), F32)],
        scratch_shapes=[pltpu.VMEM((npp, LANES, LANES), F32),
                        pltpu.VMEM((SUBLANES, width), F32), pltpu.VMEM((SUBLANES, width), F32),
                        pltpu.VMEM((SUBLANES, width), F32), pltpu.VMEM((SUBLANES, lslot), F32)],
        compiler_params=_params("arbitrary", "arbitrary", "arbitrary"),
    )(proj, proj, proj, proj, fr, fk, fv, fl, pvec, mul, wd, wa, wg, tri)


def _rwkv_step_kernel(split, zr_ref, zk_ref, zv_ref, zl_ref, pr_ref, pk_ref, pv_ref, plr_ref,
                      pvec_ref, mul_ref, wd_ref, wa_ref, wg_ref, s_ref,
                      ya_ref, snew_ref, o_s):
    half = LANES // 2
    pvec = pvec_ref[...]
    r, kp, v, kk, a, logd, g, bonus = _rwkv_tokens(
        zr_ref[...].astype(F32), zk_ref[...].astype(F32), zv_ref[...].astype(F32),
        pr_ref[...], pk_ref[...], pv_ref[...],
        _rwkv_lora(_rwkv_lora_inputs(zl_ref[...].astype(F32), plr_ref[...], mul_ref[...], split),
                   wd_ref[...], wa_ref[...], wg_ref[...]),
        pvec)
    w = jnp.exp(logd)
    nkk_t, wr_t, w_t, beta_t, kp_t, r_t, v_t = (jnp.transpose(x) for x in (-kk, w * r, w, kk * a, kp, r, v))
    for e in range(2):
        ks = slice(e * half, (e + 1) * half)
        nkk_e, wr_e, w_e, beta_e, kp_e = nkk_t[ks], wr_t[ks], w_t[ks], beta_t[ks], kp_t[ks]
        c_beta = jnp.sum(beta_e * r_t[ks], axis=0, keepdims=True)
        c_k = jnp.sum(kp_e * r_t[ks], axis=0, keepdims=True)
        for vi in range(half):
            row = e * half + vi
            s = s_ref[e, vi]
            sa = jnp.sum(s * nkk_e, axis=0, keepdims=True)
            sw = jnp.sum(s * wr_e, axis=0, keepdims=True)
            v_row = v_t[row:row + 1, :]
            snew_ref[e, vi] = s * w_e + sa * beta_e + v_row * kp_e
            o_s[row:row + 1, :] = sw + sa * c_beta + v_row * c_k
    ya_ref[...] = _bf(_rwkv_finish(jnp.transpose(o_s[...]), bonus, g, pvec))


def _rwkv_step(proj, nb, lay, prev, pvec, mul, wd, wa, wg, s_wkv):
    rw, lslot = lay["rw"], lay["lslot"]
    npair = rw // LANES
    pm, plr = prev
    half = LANES // 2
    s_t = jnp.transpose(s_wkv, (1, 2, 3, 0))
    col = lambda off: pl.BlockSpec((nb, LANES), lambda p, off=off: (0, off + p))
    split = _lora_split(lay)
    wspec = pl.BlockSpec((split[0], LANES), lambda p: (0, p))
    wgspec = pl.BlockSpec((split[2] - split[1], LANES), lambda p: (0, p))
    sspec = pl.BlockSpec((2, half, half, nb), lambda p: (p, 0, 0, 0))
    ya, snew_t = pl.pallas_call(
        functools.partial(_rwkv_step_kernel, split),
        grid=(npair,),
        in_specs=[col(0), col(npair), col(2 * npair),
                  pl.BlockSpec((nb, lslot), lambda p: (0, lay["off_l"] // lslot)),
                  col(0), col(npair), col(2 * npair),
                  pl.BlockSpec((nb, lslot), lambda p: (0, 0)),
                  pl.BlockSpec((_PV_ROWS, LANES), lambda p: (0, p)),
                  pl.BlockSpec((1, lslot), lambda p: (0, 0)),
                  wspec, wspec, wgspec, sspec],
        out_specs=[pl.BlockSpec((nb, LANES), lambda p: (0, p)), sspec],
        out_shape=[jax.ShapeDtypeStruct((nb, rw), BF16), jax.ShapeDtypeStruct(s_t.shape, F32)],
        scratch_shapes=[pltpu.VMEM((LANES, nb), F32)],
        compiler_params=_params("arbitrary"),
    )(proj, proj, proj, proj, pm, pm, pm, plr, pvec, mul, wd, wa, wg, s_t)
    return ya, jnp.transpose(snew_t, (3, 0, 1, 2))


def _rope_rows(t, cos2, sin2):
    return t * cos2 + pltpu.roll(t, LANES // 2, 1) * sin2


def _head_norm_rows(o, eps):
    mu = jnp.mean(o, axis=-1, keepdims=True)
    d = o - mu
    var = jnp.mean(d * d, axis=-1, keepdims=True)
    return d * lax.rsqrt(var + eps)


def _ret_seq_kernel(q_ref, k_ref, v_ref, g_ref, cos_ref, sin_ref, intra_ref, qd_ref, kd_ref,
                    blk_ref, gnw_ref, gnb_ref, yb_ref, ret_ref, s_ref):
    i = pl.program_id(1)
    nh = s_ref.shape[0]
    dk = s_ref.shape[1]

    @pl.when(i == 0)
    def _():
        s_ref[...] = jnp.zeros_like(s_ref)

    cos2, sin2 = cos_ref[...], sin_ref[...]
    heads = range(nh)
    hs = [slice(h * LANES, (h + 1) * LANES) for h in heads]
    kh = [_rope_rows(k_ref[:, hs[h]].astype(F32), cos2, sin2) * (dk ** -0.5) for h in heads]
    qb = [_bf(_rope_rows(q_ref[:, hs[h]].astype(F32), cos2, sin2)) for h in heads]
    vb = [_bf(v_ref[:, hs[h]]) for h in heads]
    scores = [_bf(_dot_nt(qb[h], _bf(kh[h])) * intra_ref[h]) for h in heads]
    s0 = [s_ref[h] for h in heads]
    o = [_dot(scores[h], vb[h]) + _dot(qb[h], _bf(s0[h])) * qd_ref[h] for h in heads]
    for h in heads:
        s_ref[h] = s0[h] * blk_ref[h] + _dot_tn(_bf(kh[h] * kd_ref[h]), vb[h])
    for h in heads:
        on = _head_norm_rows(o[h], RET_GN_EPS)
        yb_ref[:, hs[h]] = _bf((on * gnw_ref[:, hs[h]] + gnb_ref[:, hs[h]])
                               * jax.nn.silu(g_ref[:, hs[h]].astype(F32)))

    @pl.when(i == pl.num_programs(1) - 1)
    def _():
        ret_ref[0] = s_ref[...]


def _ret_tables(nh, c):
    log_g = jnp.log1p(-jnp.exp2(-5.0 - jnp.arange(nh, dtype=F32)))
    i = jnp.arange(c, dtype=F32)
    rel = i[:, None] - i[None, :]
    intra = jnp.where(rel >= 0, jnp.exp(log_g[:, None, None] * jnp.maximum(rel, 0.0)), 0.0)
    q_decay = jnp.exp(log_g[:, None] * (i + 1.0))
    k_decay = jnp.exp(log_g[:, None] * (c - 1.0 - i))
    blk_decay = jnp.exp(log_g * c)
    return intra, q_decay, k_decay, blk_decay


def _rope_tables(pos, dk):
    half = dk // 2
    inv = ROPE_BASE ** (-jnp.arange(half, dtype=F32) / half)
    ang = pos[:, None] * inv[None, :]
    cos, sin = jnp.cos(ang), jnp.sin(ang)
    return jnp.concatenate([cos, cos], axis=-1), jnp.concatenate([-sin, sin], axis=-1)


def _ret_seq(proj, nb, seq, lay, gnw, gnb):
    qk, rv, nh = lay["qk"], lay["rv"], lay["ret_heads"]
    dk, dv = qk // nh, rv // nh
    assert dk == LANES and dv == LANES
    c = RET_CHUNK if seq % RET_CHUNK == 0 else seq
    assert c % SUBLANES == 0
    nt = seq // c
    intra, qd, kd, blk = _ret_tables(nh, c)
    qd = jnp.broadcast_to(qd[:, :, None], (nh, c, dv))
    kd = jnp.broadcast_to(kd[:, :, None], (nh, c, dk))
    blk = jnp.broadcast_to(blk[:, None, None], (nh, 1, dv))
    cos2, sin2 = _rope_tables(jnp.arange(seq, dtype=F32), dk)
    seg = lambda off, w: pl.BlockSpec((c, w), lambda b, i, off=off, w=w: (b * nt + i, off // w))
    full3 = lambda a: pl.BlockSpec(a.shape, lambda b, i: (0, 0, 0))
    return pl.pallas_call(
        _ret_seq_kernel,
        grid=(nb, nt),
        in_specs=[seg(lay["off_q"], qk), seg(lay["off_kr"], qk), seg(lay["off_vr"], rv),
                  seg(lay["off_rg"], rv),
                  pl.BlockSpec((c, dk), lambda b, i: (i, 0)), pl.BlockSpec((c, dk), lambda b, i: (i, 0)),
                  full3(intra), full3(qd), full3(kd), full3(blk),
                  pl.BlockSpec((1, rv), lambda b, i: (0, 0)), pl.BlockSpec((1, rv), lambda b, i: (0, 0))],
        out_specs=[pl.BlockSpec((c, rv), lambda b, i: (b * nt + i, 0)),
                   pl.BlockSpec((1, nh, dk, dv), lambda b, i: (b, 0, 0, 0))],
        out_shape=[jax.ShapeDtypeStruct((nb * seq, rv), BF16),
                   jax.ShapeDtypeStruct((nb, nh, dk, dv), F32)],
        scratch_shapes=[pltpu.VMEM((nh, dk, dv), F32)],
        compiler_params=_params("arbitrary", "arbitrary"),
    )(proj, proj, proj, proj, cos2, sin2, intra, qd, kd, blk, gnw, gnb)


_RC_INTRA, _RC_QD, _RC_KD, _RC_BLK = range(4)


def _ret_step_kernel(q_ref, k_ref, v_ref, g_ref, cos_ref, sin_ref, rc_ref, gnw_ref, gnb_ref, s_ref,
                     yb_ref, snew_ref, q_s, k_s, v_s, o_s):
    bb = q_ref.shape[0]
    nh = s_ref.shape[1]
    dk = s_ref.shape[2]
    cos2, sin2 = cos_ref[...], sin_ref[...]
    for h in range(nh):
        hs = slice(h * LANES, (h + 1) * LANES)
        q_s[:, hs] = _rope_rows(q_ref[:, hs].astype(F32), cos2, sin2)
        k_s[:, hs] = _rope_rows(k_ref[:, hs].astype(F32), cos2, sin2) * (dk ** -0.5)
        v_s[:, hs] = v_ref[:, hs].astype(F32)
    eye = (lax.broadcasted_iota(jnp.int32, (LANES, LANES), 0)
           == lax.broadcasted_iota(jnp.int32, (LANES, LANES), 1))
    for b in range(bb):
        for h in range(nh):
            hs = slice(h * LANES, (h + 1) * LANES)
            rc = lambda j: rc_ref[h, j:j + 1, :]
            q_row = q_s[b:b + 1, hs]
            k_row = k_s[b:b + 1, hs]
            v_row = v_s[b:b + 1, hs]
            s0 = s_ref[b, h]
            q_col = jnp.sum(jnp.where(eye, q_row, 0.0), axis=-1, keepdims=True)
            k_col = jnp.sum(jnp.where(eye, k_row, 0.0), axis=-1, keepdims=True)
            score = jnp.sum(q_row * k_row, axis=-1, keepdims=True) * rc(_RC_INTRA)
            o_row = score * v_row + jnp.sum(s0 * q_col, axis=0, keepdims=True) * rc(_RC_QD)
            snew_ref[b, h] = s0 * rc(_RC_BLK) + (k_col * rc(_RC_KD)) * v_row
            o_s[b:b + 1, hs] = o_row
    for h in range(nh):
        hs = slice(h * LANES, (h + 1) * LANES)
        on = _head_norm_rows(o_s[:, hs], RET_GN_EPS)
        yb_ref[:, hs] = _bf((on * gnw_ref[:, hs] + gnb_ref[:, hs]) * jax.nn.silu(g_ref[:, hs].astype(F32)))


def _ret_step(proj, nb, lay, gnw, gnb, s_ret, pos0):
    qk, rv, nh = lay["qk"], lay["rv"], lay["ret_heads"]
    dk, dv = qk // nh, rv // nh
    assert dk == LANES and dv == LANES
    bb = 2 * SUBLANES
    assert nb % bb == 0
    intra, qd, kd, blk = _ret_tables(nh, 1)
    rc = jnp.stack([intra[:, 0, 0], qd[:, 0], kd[:, 0], blk], axis=1)
    rc = jnp.pad(rc, ((0, 0), (0, SUBLANES - 4)))
    rc = jnp.broadcast_to(rc[:, :, None], (nh, SUBLANES, LANES))
    cos2, sin2 = _rope_tables(jnp.asarray([pos0], dtype=F32), dk)
    seg = lambda off, w: pl.BlockSpec((bb, w), lambda j, off=off, w=w: (j, off // w))
    sspec = pl.BlockSpec((bb, nh, dk, dv), lambda j: (j, 0, 0, 0))
    row = lambda w: pl.BlockSpec((1, w), lambda j: (0, 0))
    return pl.pallas_call(
        _ret_step_kernel,
        grid=(nb // bb,),
        in_specs=[seg(lay["off_q"], qk), seg(lay["off_kr"], qk), seg(lay["off_vr"], rv),
                  seg(lay["off_rg"], rv), row(dk), row(dk),
                  pl.BlockSpec(rc.shape, lambda j: (0, 0, 0)), row(rv), row(rv), sspec],
        out_specs=[pl.BlockSpec((bb, rv), lambda j: (j, 0)), sspec],
        out_shape=[jax.ShapeDtypeStruct((nb, rv), BF16), jax.ShapeDtypeStruct(s_ret.shape, F32)],
        scratch_shapes=[pltpu.VMEM((bb, qk), F32), pltpu.VMEM((bb, qk), F32), pltpu.VMEM((bb, rv), F32),
                        pltpu.VMEM((bb, rv), F32)],
        compiler_params=_params("arbitrary"),
    )(proj, proj, proj, proj, cos2, sin2, rc, gnw, gnb, s_ret)


def _merge_kernel(alpha, ng, *refs):
    ya_ref, yb_ref = refs[0], refs[1]
    ga_refs = refs[2:2 + ng]
    gb_refs = refs[2 + ng:2 + 2 * ng]
    (x_ref, g1_ref, sc2_ref, sh2_ref, wa_ref, wb_ref, wo_ref, lnw_ref, lnb_ref,
     x1_ref, u2_ref) = refs[2 + 2 * ng:]
    cat = lambda rs: jnp.concatenate([r[...].astype(F32) for r in rs], axis=1)
    merged = (jax.nn.sigmoid(cat(ga_refs)) * _dot(ya_ref[...], wa_ref[...])
              + jax.nn.sigmoid(cat(gb_refs)) * _dot(yb_ref[...], wb_ref[...]))
    t = alpha * x_ref[...] + g1_ref[0] * _dot(_bf(merged), wo_ref[...])
    x1 = _layer_norm_rows(t, lnw_ref[...], lnb_ref[...])
    x1_ref[...] = x1
    u2_ref[...] = _bf(x1 * (1.0 + sc2_ref[0]) + sh2_ref[0])


def _merge(ya, yb, proj, x, g1, sc2, sh2, wba, wbb, wout, lnw, lnb, lay, alpha, tm, tpg):
    m, d = x.shape
    rw, rv = ya.shape[1], yb.shape[1]
    gw = math.gcd(lay["off_ga"], d)
    ng = d // gw
    r = g1.shape[1]
    rowt = lambda w: pl.BlockSpec((tm, w), lambda i: (i, 0))
    gate = lambda off, q: pl.BlockSpec((tm, gw), lambda i, off=off, q=q: (i, off // gw + q))
    mod = pl.BlockSpec((1, r, d), lambda i: (i // tpg, 0, 0))
    const = lambda a: pl.BlockSpec(a.shape, lambda i: (0, 0), pipeline_mode=pl.Buffered(1))
    return pl.pallas_call(
        functools.partial(_merge_kernel, alpha, ng),
        grid=(m // tm,),
        in_specs=[rowt(rw), rowt(rv)]
                 + [gate(lay["off_ga"], q) for q in range(ng)]
                 + [gate(lay["off_gb"], q) for q in range(ng)]
                 + [rowt(d), mod, mod, mod, const(wba), const(wbb), const(wout), const(lnw), const(lnb)],
        out_specs=[rowt(d), rowt(d)],
        out_shape=[jax.ShapeDtypeStruct((m, d), F32), jax.ShapeDtypeStruct((m, d), BF16)],
        compiler_params=_params("arbitrary"),
    )(ya, yb, *([proj] * (2 * ng)), x, g1, sc2, sh2, wba, wbb, wout, lnw, lnb)


def _ffn_seq_kernel(tps, nj, nsteps, alpha, u_ref, wa_ref, wb_ref, cwa_ref, cwb_ref, ia_ref, ib_ref,
                    wd_ref, x1_ref, g2_ref, lnw_ref, lnb_ref, o_ref, ta_ref, tb_ref,
                    ca_s, cb_s, ha_s, hb_s, act_s):
    t = pl.program_id(0)
    tm = u_ref.shape[0]

    @pl.when(t == 0)
    def _():
        act_s[...] = jnp.zeros_like(act_s)
        ha_s[...] = jnp.zeros_like(ha_s)
        hb_s[...] = jnp.zeros_like(hb_s)

    @pl.when((t < 2) | ((t - 2) % nj == 0))
    def _():
        o_ref[...] = jnp.zeros_like(o_ref)

    o_ref[...] += _dot(act_s[t % 2], wd_ref[...])

    u = u_ref[...]
    ha_s[t % 2] = _dot(u, wa_ref[...])

    te = jnp.clip(t - 1, 0, nsteps - 1)
    i, jc = te // nj, te % nj
    first = i % tps == 0
    halves, tails = [], []
    for h_s, cw_ref, init_ref, c_s in ((ha_s, cwa_ref, ia_ref, ca_s), (hb_s, cwb_ref, ib_ref, cb_s)):
        h = h_s[(t + 1) % 2]
        prev = jnp.where(first, init_ref[0], c_s[jc])
        rid = lax.broadcasted_iota(jnp.int32, h.shape, 0)
        h1 = jnp.where(rid == 0, prev[SUBLANES - 1:SUBLANES, :], pltpu.roll(h, 1, 0))
        h2 = jnp.where(rid == 0, prev[SUBLANES - 2:SUBLANES - 1, :],
                       jnp.where(rid == 1, prev[SUBLANES - 1:SUBLANES, :], pltpu.roll(h, 2, 0)))
        tail = h[tm - SUBLANES:tm, :]
        tails.append(tail)
        cw = cw_ref[...]
        halves.append(cw[3:4, :] + cw[0:1, :] * h2 + cw[1:2, :] * h1 + cw[2:3, :] * h)
    gate = jax.nn.silu(halves[0]) * halves[1]
    act_s[(t + 1) % 2] = _bf(gate)
    zbits = (pltpu.bitcast(gate, jnp.uint32) >> 16) >> 16
    z16 = (zbits[0:16] | zbits[tm // 2:tm // 2 + 16] | zbits[tm - 16:tm]).astype(F32).astype(BF16)
    zrow = jnp.concatenate([z16] * (u.shape[1] // z16.shape[1]), axis=1)
    u_dep = jnp.concatenate([u[0:16] + zrow, u[16:]], axis=0)
    hb_s[t % 2] = _dot(u_dep, wb_ref[...])

    @pl.when(t >= 1)
    def _():
        ca_s[jc] = tails[0]
        cb_s[jc] = tails[1]

    @pl.when((t >= 1) & (t <= nsteps) & (i % tps == tps - 1))
    def _():
        ta_ref[i // tps, jc] = tails[0]
        tb_ref[i // tps, jc] = tails[1]

    @pl.when((t >= 2) & ((t - 1) % nj == 0))
    def _():
        y = alpha * x1_ref[...] + g2_ref[0] * o_ref[...]
        o_ref[...] = _layer_norm_rows(y, lnw_ref[...], lnb_ref[...])


def _ffn_up_step_kernel(u_ref, wa_ref, wb_ref, cwa_ref, cwb_ref, p1a_ref, p2a_ref, p1b_ref, p2b_ref,
                        act_ref, ha_ref, hb_ref, wa_bf_ref, wb_bf_ref):
    u = u_ref[...]
    halves = []
    for w_ref, cw_ref, p1_ref, p2_ref, h_ref, w_bf_ref in (
            (wa_ref, cwa_ref, p1a_ref, p2a_ref, ha_ref, wa_bf_ref),
            (wb_ref, cwb_ref, p1b_ref, p2b_ref, hb_ref, wb_bf_ref)):
        w = _bf(w_ref[...])
        w_bf_ref[...] = w
        h = _dot(u, w)
        h_ref[...] = h
        cw = cw_ref[...]
        halves.append(cw[3:4, :] + cw[0:1, :] * p2_ref[...] + cw[1:2, :] * p1_ref[...] + cw[2:3, :] * h)
    act_ref[...] = _bf(jax.nn.silu(halves[0]) * halves[1])


def _ffn_up(u2, w_up, cwt, nb, seq, s_conv, tm):
    m, d = u2.shape
    assert m == tm
    f2 = w_up.shape[1]
    f = f2 // 2
    tn = _pick_tile(f, (512, 256, 128))
    nj = f // tn
    w_a = pl.BlockSpec((d, tn), lambda i, j: (0, j))
    w_b = pl.BlockSpec((d, tn), lambda i, j: (0, nj + j))
    cw_a = pl.BlockSpec((4, tn), lambda i, j: (0, j))
    cw_b = pl.BlockSpec((4, tn), lambda i, j: (0, nj + j))
    assert seq == 1
    p_a = pl.BlockSpec((tm, tn), lambda i, j: (i, j))
    p_b = pl.BlockSpec((tm, tn), lambda i, j: (i, nj + j))
    prev1, prev2 = s_conv[:, 1, :], s_conv[:, 0, :]
    w_o = pl.BlockSpec((d, tn), lambda i, j: (0, j))
    act, h_a, h_b, w_a_bf, w_b_bf = pl.pallas_call(
        _ffn_up_step_kernel,
        grid=(m // tm, nj),
        in_specs=[pl.BlockSpec((tm, d), lambda i, j: (i, 0)), w_a, w_b, cw_a, cw_b, p_a, p_a, p_b, p_b],
        out_specs=[pl.BlockSpec((tm, tn), lambda i, j: (i, j))] * 3 + [w_o, w_o],
        out_shape=[jax.ShapeDtypeStruct((m, f), BF16), jax.ShapeDtypeStruct((m, f), F32),
                   jax.ShapeDtypeStruct((m, f), F32), jax.ShapeDtypeStruct((d, f), BF16),
                   jax.ShapeDtypeStruct((d, f), BF16)],
        compiler_params=_params("arbitrary", "arbitrary"),
    )(u2, w_up, w_up, cwt, cwt, prev1, prev2, prev1, prev2)
    conv_new = jnp.stack([prev1, jnp.concatenate([h_a, h_b], axis=-1)], axis=1)
    return act, conv_new, (w_a_bf, w_b_bf)


def _ffn_seq(u2, w_up_halves, cwt, w_down_bf, x1, g2, lnw, lnb, alpha, nb, seq, s_conv, tm):
    m, d = u2.shape
    w_up_a, w_up_b = w_up_halves
    f = w_up_a.shape[1]
    tn = _pick_tile(f, (512, 256, 128))
    nj = f // tn
    assert seq % tm == 0 and tm >= SUBLANES
    tps = seq // tm
    nsteps = (m // tm) * nj
    pair = lambda t, s: jnp.clip(t - s, 0, nsteps - 1)
    row_i = lambda t, s: pair(t, s) // nj
    col_j = lambda t, s: pair(t, s) % nj
    w_up = pl.BlockSpec((d, tn), lambda t: (0, col_j(t, 0)))
    cw_a = pl.BlockSpec((4, tn), lambda t: (0, col_j(t, 1)))
    cw_b = pl.BlockSpec((4, tn), lambda t: (0, nj + col_j(t, 1)))
    init = jnp.pad(s_conv, ((0, 0), (SUBLANES - 2, 0), (0, 0)))
    i_a = pl.BlockSpec((1, SUBLANES, tn), lambda t: (row_i(t, 1) // tps, 0, col_j(t, 1)))
    i_b = pl.BlockSpec((1, SUBLANES, tn), lambda t: (row_i(t, 1) // tps, 0, nj + col_j(t, 1)))
    w_d = pl.BlockSpec((tn, d), lambda t: (col_j(t, 2), 0))
    row_up = pl.BlockSpec((tm, d), lambda t: (row_i(t, 0), 0))
    row_dn = pl.BlockSpec((tm, d), lambda t: (row_i(t, 2), 0))
    vec = pl.BlockSpec((1, d), lambda t: (0, 0))
    t_o = pl.BlockSpec((nb, nj, SUBLANES, tn), lambda t: (0, 0, 0, 0))
    tails = jax.ShapeDtypeStruct((nb, nj, SUBLANES, tn), F32)
    x2, t_a, t_b = pl.pallas_call(
        functools.partial(_ffn_seq_kernel, tps, nj, nsteps, alpha),
        grid=(nsteps + 2,),
        in_specs=[row_up, w_up, w_up, cw_a, cw_b, i_a, i_b, w_d, row_dn,
                  pl.BlockSpec((1, 1, d), lambda t: (row_i(t, 2) // tps, 0, 0)), vec, vec],
        out_specs=[row_dn, t_o, t_o],
        out_shape=[jax.ShapeDtypeStruct((m, d), F32), tails, tails],
        scratch_shapes=[pltpu.VMEM((nj, SUBLANES, tn), F32), pltpu.VMEM((nj, SUBLANES, tn), F32),
                        pltpu.VMEM((2, tm, tn), F32), pltpu.VMEM((2, tm, tn), F32),
                        pltpu.VMEM((2, tm, tn), BF16)],
        compiler_params=_params("arbitrary"),
    )(u2, w_up_a, w_up_b, cwt, cwt, init, init, w_down_bf, x1, g2, lnw, lnb)
    rows = lambda t: t[:, :, SUBLANES - 2:, :].transpose(0, 2, 1, 3).reshape(nb, 2, f)
    return x2, jnp.concatenate([rows(t_a), rows(t_b)], axis=-1)


def _ffn_down_kernel(alpha, act_ref, w_ref, x1_ref, g2_ref, lnw_ref, lnb_ref, o_ref, w_bf_ref, acc_ref):
    k = pl.program_id(1)

    @pl.when(k == 0)
    def _():
        acc_ref[...] = jnp.zeros_like(acc_ref)

    w = _bf(w_ref[...])
    w_bf_ref[...] = w
    acc_ref[...] += _dot(act_ref[...], w)

    @pl.when(k == pl.num_programs(1) - 1)
    def _():
        t = alpha * x1_ref[...] + g2_ref[0] * acc_ref[...]
        o_ref[...] = _layer_norm_rows(t, lnw_ref[...], lnb_ref[...])


def _ffn_down(act, w_down, x1, g2, lnw, lnb, alpha, tm, tpg):
    m, f = act.shape
    assert m == tm
    d = w_down.shape[1]
    tk = _pick_tile(f, (512, 256, 128))
    r = g2.shape[1]
    return pl.pallas_call(
        functools.partial(_ffn_down_kernel, alpha),
        grid=(m // tm, f // tk),
        in_specs=[pl.BlockSpec((tm, tk), lambda i, k: (i, k)),
                  pl.BlockSpec((tk, d), lambda i, k: (k, 0)),
                  pl.BlockSpec((tm, d), lambda i, k: (i, 0)),
                  pl.BlockSpec((1, r, d), lambda i, k: (i // tpg, 0, 0)),
                  pl.BlockSpec((1, d), lambda i, k: (0, 0)),
                  pl.BlockSpec((1, d), lambda i, k: (0, 0))],
        out_specs=[pl.BlockSpec((tm, d), lambda i, k: (i, 0)), pl.BlockSpec((tk, d), lambda i, k: (k, 0))],
        out_shape=[jax.ShapeDtypeStruct((m, d), F32), jax.ShapeDtypeStruct((f, d), BF16)],
        scratch_shapes=[pltpu.VMEM((tm, d), F32)],
        compiler_params=_params("arbitrary", "arbitrary"),
    )(act, w_down, x1, g2, lnw, lnb)


def _layout(d, rw, qk, rv, lora_w, ret_heads):
    lslot = next(s for s in (128, 256, 512, 1024, 2048) if s >= lora_w)
    off_q = 3 * rw
    off_kr = off_q + qk
    off_vr = off_kr + qk
    off_rg = off_vr + rv
    off_ga = off_rg + rv
    off_gb = off_ga + d
    off_l = off_gb + d
    assert rw % LANES == 0 and off_l % lslot == 0
    assert off_q % qk == 0 and off_kr % qk == 0 and off_vr % rv == 0 and off_rg % rv == 0
    return dict(d=d, rw=rw, qk=qk, rv=rv, lslot=lslot, lora_w=lora_w, ret_heads=ret_heads,
                off_q=off_q, off_kr=off_kr, off_vr=off_vr, off_rg=off_rg, off_ga=off_ga,
                off_gb=off_gb, off_l=off_l, nt=off_l + lslot)


def _prep_weights(lay, shift_mu, w0, w_decay_up, a0, w_aaa_up, w_gate_up, k_k, k_a, r_k, lnx_w, lnx_b):
    rw, lslot, lora_w = lay["rw"], lay["lslot"], lay["lora_w"]
    dl, al, gl = lay["lora_dims"]
    a1, g0, g1 = _lora_split(lay)
    pad_rows = lambda w, lo, n: jnp.pad(w, ((lo, n - lo - w.shape[0]), (0, 0))).astype(BF16)
    wd = pad_rows(w_decay_up, 0, a1)
    wa = pad_rows(w_aaa_up, dl, a1)
    wg = pad_rows(w_gate_up, dl + al - g0, g1 - g0)
    rows = [shift_mu[:rw], shift_mu[rw:2 * rw], shift_mu[2 * rw:3 * rw], w0, a0, k_k, k_a,
            r_k.reshape(-1), lnx_w, lnx_b]
    pvec = jnp.pad(jnp.stack(rows, axis=0), ((0, _PV_ROWS - len(rows)), (0, 0)))
    mul = jnp.pad(shift_mu[3 * rw:], (0, lslot - lora_w))[None, :]
    return wd, wa, wg, pvec, mul


def _run_layer(x2d, nb, seq, ada, states, pos0, lay, wts, big, alpha):
    rw, lslot, lora_w = lay["rw"], lay["lslot"], lay["lora_w"]
    (wd, wa, wg, pvec, mul, gnw, gnb, wba, wbb, wout, ln1w, ln1b, cwt, ln2w, ln2b) = wts
    w_in_t, w_up, w_down = big
    m = nb * seq
    sh1, sc1, g1, sh2, sc2, g2 = jnp.split(ada, 6, axis=-1)
    if seq == 1:
        tm, tpg = m, 1
        shape_mod = lambda t: t[None]
    else:
        tm = _pick_tile(seq, (512, 256, 128, 64, 32, 16, 8))
        tpg = seq // tm
        shape_mod = lambda t: t[:, None, :]
    sh1, sc1, g1, sh2, sc2, g2 = map(shape_mod, (sh1, sc1, g1, sh2, sc2, g2))

    s_wkv, s_shift, s_ret, s_conv = states
    shift_main = s_shift[:, :3 * rw]
    shift_lora = jnp.pad(s_shift[:, 3 * rw:], ((0, 0), (0, lslot - lora_w)))
    if seq == 1:
        proj, w_in_bf = _modmm(x2d, sc1, sh1, w_in_t, lay, m, 1)
        ya, wkv_new = _rwkv_step(proj, nb, lay, (shift_main, shift_lora), pvec, mul, wd, wa, wg, s_wkv)
        yb, ret_new = _ret_step(proj, nb, lay, gnw, gnb, s_ret, pos0)
    else:
        tm_in = _pick_tile(seq, (1024, 512, 256, 128, 64, 32, 16, 8))
        proj = _modmm(x2d, sc1, sh1, w_in_t, lay, tm_in, seq // tm_in)
        first = (shift_main[:, None, :], shift_main[:, None, :], shift_main[:, None, :],
                 shift_lora[:, None, :])
        ya, wkv_new = _rwkv_seq(proj, nb, seq, lay, first, pvec, mul, wd, wa, wg)
        yb, ret_new = _ret_seq(proj, nb, seq, lay, gnw, gnb)
    last = proj.reshape(nb, seq, lay["nt"])[:, -1, :]
    shift_new = jnp.concatenate([last[:, :3 * rw], last[:, lay["off_l"]:lay["off_l"] + lora_w]],
                                axis=-1).astype(F32)

    tm_merge = min(tm, 256)
    x1, u2 = _merge(ya, yb, proj, x2d, g1, sc2, sh2, wba, wbb, wout, ln1w, ln1b, lay, alpha,
                    tm_merge, (seq // tm_merge) if seq > 1 else 1)
    if seq == 1:
        act, conv_new, w_up_halves = _ffn_up(u2, w_up, cwt, nb, seq, s_conv, tm)
        x2, w_down_bf = _ffn_down(act, w_down, x1, g2, ln2w, ln2b, alpha, tm, tpg)
        rounded = (w_in_bf, w_up_halves, w_down_bf)
    else:
        x2, conv_new = _ffn_seq(u2, w_up, cwt, w_down, x1, g2, ln2w, ln2b, alpha, nb, seq, s_conv, tm)
        rounded = None
    return (x2, wkv_new, shift_new, ret_new, conv_new), rounded


def kernel(x_prompt, x_sample, c_prompt, c_sample, state_wkv, state_shift, state_ret, state_conv, w_ada, b_ada, w_in, shift_mu, w0, w_decay_up, a0, w_aaa_up, w_gate_up, k_k, k_a, r_k, lnx_w, lnx_b, ret_gn_w, ret_gn_b, w_branch_a, w_branch_b, w_out, ln1_w, ln1_b, w_up, conv_w, conv_b, w_down, ln2_w, ln2_b):
    depth = w_ada.shape[0]
    nbp, seq_p, d = x_prompt.shape
    nbs, seq_s, _ = x_sample.shape
    assert seq_s == 1
    rw = k_k.shape[-1]
    ret_heads, dk, dv = state_ret.shape[2:]
    lora_w = w_decay_up.shape[1] + w_aaa_up.shape[1] + w_gate_up.shape[1]
    lay = _layout(d, rw, ret_heads * dk, ret_heads * dv, lora_w, ret_heads)
    lay["lora_dims"] = (w_decay_up.shape[1], w_aaa_up.shape[1], w_gate_up.shape[1])
    heads, hn = r_k.shape[1:]
    assert hn == LANES // 2 and heads * hn == rw
    alpha = (2.0 * depth) ** 0.25
    f2 = w_up.shape[-1]

    xp = x_prompt.reshape(nbp * seq_p, d)
    xs = x_sample.reshape(nbs * seq_s, d)
    c_all = jnp.concatenate([c_prompt, c_sample], axis=0)
    pad = (-c_all.shape[0]) % SUBLANES
    c_all = _bf(jnp.pad(c_all, ((0, pad), (0, 0))))

    outs_p, outs_s = [], []
    for l in range(depth):
        wd, wa, wg, pvec, mul = _prep_weights(
            lay, shift_mu[l], w0[l], w_decay_up[l], a0[l], w_aaa_up[l], w_gate_up[l],
            k_k[l], k_a[l], r_k[l], lnx_w[l], lnx_b[l])
        cwt = jnp.concatenate([conv_w[l], conv_b[l][None, :]], axis=0)
        wts = (wd, wa, wg, pvec, mul, ret_gn_w[l][None, :], ret_gn_b[l][None, :],
               _bf(w_branch_a[l]), _bf(w_branch_b[l]), _bf(w_out[l]), ln1_w[l][None, :], ln1_b[l][None, :],
               cwt, ln2_w[l][None, :], ln2_b[l][None, :])
        ada = _mm_bias(c_all, w_ada[l], b_ada[l][None, :])
        (xs, *st_s), rounded = _run_layer(
            xs, nbs, seq_s, ada[nbp:nbp + nbs],
            (state_wkv[l], state_shift[l], state_ret[l], state_conv[l]), float(PAST_LEN), lay, wts,
            (jnp.transpose(w_in[l]), w_up[l], w_down[l]), alpha)
        zero_states = (None, jnp.zeros((nbp, state_shift.shape[-1]), F32), None,
                       jnp.zeros((nbp, state_conv.shape[2], f2), F32))
        (xp, *st_p), _ = _run_layer(xp, nbp, seq_p, ada[:nbp], zero_states, 0.0, lay, wts, rounded, alpha)
        outs_p.append(st_p)
        outs_s.append(st_s)

    def stack(lst, j, ref):
        layers = [s[j].astype(ref.dtype) for s in lst]
        return layers[0][None] if depth == 1 else jnp.stack(layers, axis=0)

    refs = (state_wkv, state_shift, state_ret, state_conv)
    return (xp.reshape(x_prompt.shape), xs.reshape(x_sample.shape),
            *[stack(outs_p, j, refs[j]) for j in range(4)],
            *[stack(outs_s, j, refs[j]) for j in range(4)])
```
